```python
import math
import jax, jax.numpy as jnp
from jax import lax
import numpy as np

D_MODEL = 1024
BATCH = 8
SEQ = 16384
DEPTH = 4

N_MIXERS = 3
EPS = 1e-6
ATTN_GROUPS = ((128, 1), (512, 4), (2048, 16))
ATTN_N_GROUPS = len(ATTN_GROUPS)
ATTN_HEADS_PER_GROUP = 8
ATTN_HEAD_DIM = 64
ATTN_BLOCK = 128
ATTN_GROUP_WIDTH = ATTN_HEADS_PER_GROUP * ATTN_HEAD_DIM
ATTN_IN_WIDTH = ATTN_N_GROUPS * 3 * ATTN_GROUP_WIDTH
CONV_CHANNELS = D_MODEL
CONV_WIDTH = 31
HGRN_EXPAND = 128
HGRN_HEADS = D_MODEL // HGRN_EXPAND
HGRN_KEY_DIM = HGRN_HEADS * HGRN_EXPAND
HGRN_HEAD_V = D_MODEL // HGRN_HEADS
HGRN_VALUE_DIM = HGRN_HEADS * HGRN_HEAD_V
HGRN_CHUNK = 64
FFN_HIDDEN = 2816
FFN_CONV_WIDTH = 3
N_ATTN_LAYERS = len(range(0, DEPTH, N_MIXERS))
N_CONV_LAYERS = len(range(1, DEPTH, N_MIXERS))
N_HGRN_LAYERS = len(range(2, DEPTH, N_MIXERS))

kernel_name = "hybrid_dilated_attn_conformer_hgrn2_trunk"


def rms_norm(x, gain):
    xf = x.astype(jnp.float32)
    y = xf * lax.rsqrt(jnp.mean(xf * xf, axis=-1, keepdims=True) + EPS)
    return (y * gain.astype(jnp.float32)).astype(x.dtype)


def layer_norm(x, gain, bias):
    xf = x.astype(jnp.float32)
    mu = jnp.mean(xf, axis=-1, keepdims=True)
    var = jnp.mean(jnp.square(xf - mu), axis=-1, keepdims=True)
    y = (xf - mu) * lax.rsqrt(var + EPS)
    return (y * gain.astype(jnp.float32) + bias.astype(jnp.float32)).astype(x.dtype)


def causal_depthwise_conv(x, w, b):
    K, C = w.shape
    y = lax.conv_general_dilated(
        x, w[:, None, :].astype(x.dtype), window_strides=(1,), padding=((K - 1, 0),),
        dimension_numbers=('NWC', 'WIO', 'NWC'), feature_group_count=C)
    return y + b.astype(x.dtype)


def dilated_window_attention(q, k, v, window, dilation):
    B, S, H, Dh = q.shape
    span = window // dilation
    unit = dilation * ATTN_BLOCK
    L = -(-S // unit) * unit
    n = L // dilation
    nb = n // ATTN_BLOCK

    def to_blocks(t):
        t = jnp.pad(t, ((0, 0), (0, L - S), (0, 0), (0, 0)))
        t = t.reshape(B, n, dilation, H, Dh).transpose(0, 2, 3, 1, 4)
        return t.reshape(B, dilation, H, nb, ATTN_BLOCK, Dh)

    def with_prev(t):
        prev = jnp.pad(t, ((0, 0), (0, 0), (0, 0), (1, 0), (0, 0), (0, 0)))[:, :, :, :-1]
        return jnp.concatenate([prev, t], axis=-2)

    qb = to_blocks(q)
    kw = with_prev(to_blocks(k))
    vw = with_prev(to_blocks(v))
    s = jnp.einsum('brhnqd,brhnkd->brhnqk', qb, kw).astype(jnp.float32)
    qi = jnp.arange(ATTN_BLOCK)[:, None]
    kj = jnp.arange(2 * ATTN_BLOCK)[None, :]
    dist = qi + ATTN_BLOCK - kj
    band = (dist >= 0) & (dist <= span)
    kpos = jnp.arange(nb)[:, None, None] * ATTN_BLOCK + kj[None] - ATTN_BLOCK
    mask = band[None] & (kpos >= 0)
    s = jnp.where(mask, s, -jnp.inf)
    lse = jax.nn.logsumexp(s, axis=-1)
    p = jnp.exp(s - lse[..., None])
    o = jnp.einsum('brhnqk,brhnkd->brhnqd', p.astype(v.dtype), vw)
    o = o.reshape(B, dilation, H, n, Dh).transpose(0, 3, 1, 2, 4).reshape(B, L, H, Dh)[:, :S]
    lse = lse.reshape(B, dilation, H, n).transpose(0, 3, 1, 2).reshape(B, L, H)[:, :S]
    return o, lse


def dilated_attention_mixer(h, w_in, q_gain, k_gain, w_out):
    B, S, _ = h.shape
    proj = (h @ w_in).reshape(B, S, ATTN_N_GROUPS, 3, ATTN_HEADS_PER_GROUP, ATTN_HEAD_DIM)
    scale = ATTN_HEAD_DIM ** -0.5
    outs, lses = [], []
    for g, (window, dilation) in enumerate(ATTN_GROUPS):
        q = rms_norm(proj[:, :, g, 0], q_gain[g]) * scale
        k = rms_norm(proj[:, :, g, 1], k_gain[g])
        o, lse = dilated_window_attention(q, k, proj[:, :, g, 2], window, dilation)
        outs.append(o)
        lses.append(lse)
    weights = jax.nn.softmax(jnp.stack(lses), axis=0)
    o = jnp.sum(weights[..., None] * jnp.stack(outs).astype(jnp.float32), axis=0)
    return o.astype(h.dtype).reshape(B, S, ATTN_GROUP_WIDTH) @ w_out


def conformer_conv_mixer(h, w_in, b_in, dw_w, dw_b, ln_g, ln_b, w_out, b_out):
    u = h @ w_in + b_in
    a, gate = jnp.split(u, 2, axis=-1)
    u = a * jax.nn.sigmoid(gate)
    u = causal_depthwise_conv(u, dw_w, dw_b)
    u = jax.nn.silu(layer_norm(u, ln_g, ln_b))
    return u @ w_out + b_out


def hgrn2_chunk_scan(q, k, v, log_f):
    B, S, H, K = q.shape
    V = v.shape[-1]
    C = HGRN_CHUNK
    n = S // C

    def chunks(t):
        return t.reshape(B, n, C, H, t.shape[-1]).transpose(1, 0, 3, 2, 4)

    qc, kc, vc = chunks(q), chunks(k), chunks(v)
    bc = jnp.cumsum(chunks(log_f), axis=-2)
    tri = jnp.tril(jnp.ones((C, C), dtype=bool))[:, :, None]

    def step(state, xs):
        q_, k_, v_, b_ = xs
        o_inter = jnp.einsum('bhtk,bhkv->bhtv', q_ * jnp.exp(b_), state)
        diff = b_[:, :, :, None, :] - b_[:, :, None, :, :]
        decay = jnp.where(tri, jnp.exp(jnp.where(tri, diff, 0.0)), 0.0)
        scores = jnp.einsum('bhtsk,bhsk->bhts', q_[:, :, :, None, :] * decay, k_)
        o_intra = jnp.einsum('bhts,bhsv->bhtv', scores, v_)
        b_last = b_[:, :, -1, :]
        k_dec = k_ * jnp.exp(b_last[:, :, None, :] - b_)
        state = state * jnp.exp(b_last)[..., None] + jnp.einsum('bhsk,bhsv->bhkv', k_dec, v_)
        return state, o_inter + o_intra

    state0 = jnp.zeros((B, H, K, V), jnp.float32)
    _, o = lax.scan(step, state0, (qc, kc, vc, bc))
    return o.transpose(1, 0, 3, 2, 4).reshape(B, S, H, V)


def hgrn2_mixer(h, w_in, lower_bound, norm_gain, w_out):
    B, S, _ = h.shape
    q, f, i, g = jnp.split(h @ w_in, [HGRN_KEY_DIM, 2 * HGRN_KEY_DIM,
                                      2 * HGRN_KEY_DIM + HGRN_VALUE_DIM], axis=-1)
    q = jax.nn.silu(q.astype(jnp.float32))
    lb = lower_bound.astype(jnp.float32)
    log_f = jnp.logaddexp(jnp.log(lb), jnp.log1p(-lb) + jax.nn.log_sigmoid(f.astype(jnp.float32)))
    k = -jnp.expm1(log_f)
    heads = lambda t, dim: t.reshape(B, S, HGRN_HEADS, dim)
    o = hgrn2_chunk_scan(heads(q, HGRN_EXPAND), heads(k, HGRN_EXPAND),
                         heads(i.astype(jnp.float32), HGRN_HEAD_V), heads(log_f, HGRN_EXPAND))
    o = rms_norm(o, norm_gain.reshape(HGRN_HEADS, HGRN_HEAD_V)) * \
        jax.nn.silu(heads(g.astype(jnp.float32), HGRN_HEAD_V))
    return o.reshape(B, S, HGRN_VALUE_DIM).astype(h.dtype) @ w_out


def conv_ffn(h, w_up, conv_w, conv_b, w_down):
    u = causal_depthwise_conv(h @ w_up, conv_w, conv_b)
    gate, up = jnp.split(u, 2, axis=-1)
    return (jax.nn.silu(gate) * up) @ w_down


def _fwd_setup_inputs(seed: int = 0) -> dict:
    key = jax.random.key(seed)
    ks = iter(jax.random.split(key, 32))
    D = D_MODEL
    F2 = 2 * FFN_HIDDEN

    def nrm(shape, scale):
        return scale * jax.random.normal(next(ks), shape, jnp.float32)

    def gain(shape):
        return 1.0 + nrm(shape, 0.02)

    return {
        "x": nrm((BATCH, SEQ, D), 1.0),
        "mixer_norm": gain((DEPTH, D)),
        "ffn_norm": gain((DEPTH, D)),
        "attn_w_in": nrm((N_ATTN_LAYERS, D, ATTN_IN_WIDTH), D ** -0.5),
        "attn_q_gain": gain((N_ATTN_LAYERS, ATTN_N_GROUPS, ATTN_HEAD_DIM)),
        "attn_k_gain": gain((N_ATTN_LAYERS, ATTN_N_GROUPS, ATTN_HEAD_DIM)),
        "attn_w_out": nrm((N_ATTN_LAYERS, ATTN_GROUP_WIDTH, D), ATTN_GROUP_WIDTH ** -0.5),
        "conv_w_in": nrm((N_CONV_LAYERS, D, 2 * CONV_CHANNELS), D ** -0.5),
        "conv_b_in": nrm((N_CONV_LAYERS, 2 * CONV_CHANNELS), 0.02),
        "conv_dw_w": nrm((N_CONV_LAYERS, CONV_WIDTH, CONV_CHANNELS), CONV_WIDTH ** -0.5),
        "conv_dw_b": nrm((N_CONV_LAYERS, CONV_CHANNELS), 0.02),
        "conv_ln_g": gain((N_CONV_LAYERS, CONV_CHANNELS)),
        "conv_ln_b": nrm((N_CONV_LAYERS, CONV_CHANNELS), 0.02),
        "conv_w_out": nrm((N_CONV_LAYERS, CONV_CHANNELS, D), CONV_CHANNELS ** -0.5),
        "conv_b_out": nrm((N_CONV_LAYERS, D), 0.02),
        "hgrn_w_in": nrm((N_HGRN_LAYERS, D, 2 * HGRN_KEY_DIM + 2 * HGRN_VALUE_DIM), D ** -0.5),
        "hgrn_lb_logits": nrm((DEPTH, HGRN_KEY_DIM), 1.0),
        "hgrn_norm_g": gain((N_HGRN_LAYERS, HGRN_VALUE_DIM)),
        "hgrn_w_out": nrm((N_HGRN_LAYERS, HGRN_VALUE_DIM, D), HGRN_VALUE_DIM ** -0.5),
        "ffn_w_up": nrm((DEPTH, D, F2), D ** -0.5),
        "ffn_conv_w": nrm((DEPTH, FFN_CONV_WIDTH, F2), FFN_CONV_WIDTH ** -0.5),
        "ffn_conv_b": nrm((DEPTH, F2), 0.02),
        "ffn_w_down": nrm((DEPTH, FFN_HIDDEN, D), FFN_HIDDEN ** -0.5),
    }


def _fwd_reference(x, mixer_norm, ffn_norm, attn_w_in, attn_q_gain, attn_k_gain, attn_w_out,
              conv_w_in, conv_b_in, conv_dw_w, conv_dw_b, conv_ln_g, conv_ln_b, conv_w_out,
              conv_b_out, hgrn_w_in, hgrn_lb_logits, hgrn_norm_g, hgrn_w_out,
              ffn_w_up, ffn_conv_w, ffn_conv_b, ffn_w_down):
    lb_cum = jnp.cumsum(jax.nn.softmax(hgrn_lb_logits.astype(jnp.float32), axis=0), axis=0)
    lower_bounds = lb_cum - lb_cum[0]
    for layer in range(DEPTH):
        kind = layer % N_MIXERS
        j = layer // N_MIXERS
        h = rms_norm(x, mixer_norm[layer])
        if kind == 0:
            mix = dilated_attention_mixer(h, attn_w_in[j], attn_q_gain[j], attn_k_gain[j],
                                          attn_w_out[j])
        elif kind == 1:
            mix = conformer_conv_mixer(h, conv_w_in[j], conv_b_in[j], conv_dw_w[j], conv_dw_b[j],
                                       conv_ln_g[j], conv_ln_b[j], conv_w_out[j], conv_b_out[j])
        else:
            mix = hgrn2_mixer(h, hgrn_w_in[j], lower_bounds[layer], hgrn_norm_g[j], hgrn_w_out[j])
        x = x + mix
        x = x + conv_ffn(rms_norm(x, ffn_norm[layer]), ffn_w_up[layer], ffn_conv_w[layer],
                         ffn_conv_b[layer], ffn_w_down[layer])
    return x


import jax as _jax
import jax.numpy as _jnp

TWIN_FORMAT = 'train_step'
FWD_PARAMS = ['x', 'mixer_norm', 'ffn_norm', 'attn_w_in', 'attn_q_gain', 'attn_k_gain', 'attn_w_out', 'conv_w_in', 'conv_b_in', 'conv_dw_w', 'conv_dw_b', 'conv_ln_g', 'conv_ln_b', 'conv_w_out', 'conv_b_out', 'hgrn_w_in', 'hgrn_lb_logits', 'hgrn_norm_g', 'hgrn_w_out', 'ffn_w_up', 'ffn_conv_w', 'ffn_conv_b', 'ffn_w_down']
TWIN_WEIGHTS = ['mixer_norm', 'ffn_norm', 'attn_w_in', 'attn_q_gain', 'attn_k_gain', 'attn_w_out', 'conv_w_in', 'conv_b_in', 'conv_dw_w', 'conv_dw_b', 'conv_ln_g', 'conv_ln_b', 'conv_w_out', 'conv_b_out', 'hgrn_w_in', 'hgrn_lb_logits', 'hgrn_norm_g', 'hgrn_w_out', 'ffn_w_up', 'ffn_conv_w', 'ffn_conv_b', 'ffn_w_down']
TWIN_DIFF_INPUT = 'x'
TWIN_INPUTS = ['x', 'mixer_norm', 'ffn_norm', 'attn_w_in', 'attn_q_gain', 'attn_k_gain', 'attn_w_out', 'conv_w_in', 'conv_b_in', 'conv_dw_w', 'conv_dw_b', 'conv_ln_g', 'conv_ln_b', 'conv_w_out', 'conv_b_out', 'hgrn_w_in', 'hgrn_lb_logits', 'hgrn_norm_g', 'hgrn_w_out', 'ffn_w_up', 'ffn_conv_w', 'ffn_conv_b', 'ffn_w_down', 'loss_target', 'm_mixer_norm', 'm_ffn_norm', 'm_attn_w_in', 'm_attn_q_gain', 'm_attn_k_gain', 'm_attn_w_out', 'm_conv_w_in', 'm_conv_b_in', 'm_conv_dw_w', 'm_conv_dw_b', 'm_conv_ln_g', 'm_conv_ln_b', 'm_conv_w_out', 'm_conv_b_out', 'm_hgrn_w_in', 'm_hgrn_lb_logits', 'm_hgrn_norm_g', 'm_hgrn_w_out', 'm_ffn_w_up', 'm_ffn_conv_w', 'm_ffn_conv_b', 'm_ffn_w_down', 'v_mixer_norm', 'v_ffn_norm', 'v_attn_w_in', 'v_attn_q_gain', 'v_attn_k_gain', 'v_attn_w_out', 'v_conv_w_in', 'v_conv_b_in', 'v_conv_dw_w', 'v_conv_dw_b', 'v_conv_ln_g', 'v_conv_ln_b', 'v_conv_w_out', 'v_conv_b_out', 'v_hgrn_w_in', 'v_hgrn_lb_logits', 'v_hgrn_norm_g', 'v_hgrn_w_out', 'v_ffn_w_up', 'v_ffn_conv_w', 'v_ffn_conv_b', 'v_ffn_w_down']
TWIN_OUTPUTS = ['loss', 'grad_x', 'grad_mixer_norm', 'grad_ffn_norm', 'grad_attn_w_in', 'grad_attn_q_gain', 'grad_attn_k_gain', 'grad_attn_w_out', 'grad_conv_w_in', 'grad_conv_b_in', 'grad_conv_dw_w', 'grad_conv_dw_b', 'grad_conv_ln_g', 'grad_conv_ln_b', 'grad_conv_w_out', 'grad_conv_b_out', 'grad_hgrn_w_in', 'grad_hgrn_lb_logits', 'grad_hgrn_norm_g', 'grad_hgrn_w_out', 'grad_ffn_w_up', 'grad_ffn_conv_w', 'grad_ffn_conv_b', 'grad_ffn_w_down', 'delta_mixer_norm', 'delta_ffn_norm', 'delta_attn_w_in', 'delta_attn_q_gain', 'delta_attn_k_gain', 'delta_attn_w_out', 'delta_conv_w_in', 'delta_conv_b_in', 'delta_conv_dw_w', 'delta_conv_dw_b', 'delta_conv_ln_g', 'delta_conv_ln_b', 'delta_conv_w_out', 'delta_conv_b_out', 'delta_hgrn_w_in', 'delta_hgrn_lb_logits', 'delta_hgrn_norm_g', 'delta_hgrn_w_out', 'delta_ffn_w_up', 'delta_ffn_conv_w', 'delta_ffn_conv_b', 'delta_ffn_w_down', 'new_m_mixer_norm', 'new_m_ffn_norm', 'new_m_attn_w_in', 'new_m_attn_q_gain', 'new_m_attn_k_gain', 'new_m_attn_w_out', 'new_m_conv_w_in', 'new_m_conv_b_in', 'new_m_conv_dw_w', 'new_m_conv_dw_b', 'new_m_conv_ln_g', 'new_m_conv_ln_b', 'new_m_conv_w_out', 'new_m_conv_b_out', 'new_m_hgrn_w_in', 'new_m_hgrn_lb_logits', 'new_m_hgrn_norm_g', 'new_m_hgrn_w_out', 'new_m_ffn_w_up', 'new_m_ffn_conv_w', 'new_m_ffn_conv_b', 'new_m_ffn_w_down', 'new_v_mixer_norm', 'new_v_ffn_norm', 'new_v_attn_w_in', 'new_v_attn_q_gain', 'new_v_attn_k_gain', 'new_v_attn_w_out', 'new_v_conv_w_in', 'new_v_conv_b_in', 'new_v_conv_dw_w', 'new_v_conv_dw_b', 'new_v_conv_ln_g', 'new_v_conv_ln_b', 'new_v_conv_w_out', 'new_v_conv_b_out', 'new_v_hgrn_w_in', 'new_v_hgrn_lb_logits', 'new_v_hgrn_norm_g', 'new_v_hgrn_w_out', 'new_v_ffn_w_up', 'new_v_ffn_conv_w', 'new_v_ffn_conv_b', 'new_v_ffn_w_down']
TWIN_LEAF_KINDS = {'loss': 'loss', 'grad_x': 'grad_x', 'grad_mixer_norm': 'grad_w', 'grad_ffn_norm': 'grad_w', 'grad_attn_w_in': 'grad_w', 'grad_attn_q_gain': 'grad_w', 'grad_attn_k_gain': 'grad_w', 'grad_attn_w_out': 'grad_w', 'grad_conv_w_in': 'grad_w', 'grad_conv_b_in': 'grad_w', 'grad_conv_dw_w': 'grad_w', 'grad_conv_dw_b': 'grad_w', 'grad_conv_ln_g': 'grad_w', 'grad_conv_ln_b': 'grad_w', 'grad_conv_w_out': 'grad_w', 'grad_conv_b_out': 'grad_w', 'grad_hgrn_w_in': 'grad_w', 'grad_hgrn_lb_logits': 'grad_w', 'grad_hgrn_norm_g': 'grad_w', 'grad_hgrn_w_out': 'grad_w', 'grad_ffn_w_up': 'grad_w', 'grad_ffn_conv_w': 'grad_w', 'grad_ffn_conv_b': 'grad_w', 'grad_ffn_w_down': 'grad_w', 'delta_mixer_norm': 'delta_w', 'delta_ffn_norm': 'delta_w', 'delta_attn_w_in': 'delta_w', 'delta_attn_q_gain': 'delta_w', 'delta_attn_k_gain': 'delta_w', 'delta_attn_w_out': 'delta_w', 'delta_conv_w_in': 'delta_w', 'delta_conv_b_in': 'delta_w', 'delta_conv_dw_w': 'delta_w', 'delta_conv_dw_b': 'delta_w', 'delta_conv_ln_g': 'delta_w', 'delta_conv_ln_b': 'delta_w', 'delta_conv_w_out': 'delta_w', 'delta_conv_b_out': 'delta_w', 'delta_hgrn_w_in': 'delta_w', 'delta_hgrn_lb_logits': 'delta_w', 'delta_hgrn_norm_g': 'delta_w', 'delta_hgrn_w_out': 'delta_w', 'delta_ffn_w_up': 'delta_w', 'delta_ffn_conv_w': 'delta_w', 'delta_ffn_conv_b': 'delta_w', 'delta_ffn_w_down': 'delta_w', 'new_m_mixer_norm': 'new_m', 'new_m_ffn_norm': 'new_m', 'new_m_attn_w_in': 'new_m', 'new_m_attn_q_gain': 'new_m', 'new_m_attn_k_gain': 'new_m', 'new_m_attn_w_out': 'new_m', 'new_m_conv_w_in': 'new_m', 'new_m_conv_b_in': 'new_m', 'new_m_conv_dw_w': 'new_m', 'new_m_conv_dw_b': 'new_m', 'new_m_conv_ln_g': 'new_m', 'new_m_conv_ln_b': 'new_m', 'new_m_conv_w_out': 'new_m', 'new_m_conv_b_out': 'new_m', 'new_m_hgrn_w_in': 'new_m', 'new_m_hgrn_lb_logits': 'new_m', 'new_m_hgrn_norm_g': 'new_m', 'new_m_hgrn_w_out': 'new_m', 'new_m_ffn_w_up': 'new_m', 'new_m_ffn_conv_w': 'new_m', 'new_m_ffn_conv_b': 'new_m', 'new_m_ffn_w_down': 'new_m', 'new_v_mixer_norm': 'new_v', 'new_v_ffn_norm': 'new_v', 'new_v_attn_w_in': 'new_v', 'new_v_attn_q_gain': 'new_v', 'new_v_attn_k_gain': 'new_v', 'new_v_attn_w_out': 'new_v', 'new_v_conv_w_in': 'new_v', 'new_v_conv_b_in': 'new_v', 'new_v_conv_dw_w': 'new_v', 'new_v_conv_dw_b': 'new_v', 'new_v_conv_ln_g': 'new_v', 'new_v_conv_ln_b': 'new_v', 'new_v_conv_w_out': 'new_v', 'new_v_conv_b_out': 'new_v', 'new_v_hgrn_w_in': 'new_v', 'new_v_hgrn_lb_logits': 'new_v', 'new_v_hgrn_norm_g': 'new_v', 'new_v_hgrn_w_out': 'new_v', 'new_v_ffn_w_up': 'new_v', 'new_v_ffn_conv_w': 'new_v', 'new_v_ffn_conv_b': 'new_v', 'new_v_ffn_w_down': 'new_v'}


def _forward(args):
    return _fwd_reference(*[args[k] for k in FWD_PARAMS])


def _output_shape():
    def fwd():
        inp = _fwd_setup_inputs(0)
        return _fwd_reference(*[inp[k] for k in FWD_PARAMS])
    out = _jax.eval_shape(fwd)
    return out.shape, out.dtype

N_MICROBATCH = 1
ADAM_LR = 0.001
ADAM_B1 = 0.9
ADAM_B2 = 0.999
ADAM_EPS = 1e-08
ADAM_WD = 0.01
ADAM_STEP = 10
PER_EXAMPLE_BATCH_AXIS = {'x': 0, 'loss_target': 0}
SHARED_INPUTS = []
_WEIGHT_DTYPES = {'mixer_norm': _jnp.float32, 'ffn_norm': _jnp.float32, 'attn_w_in': _jnp.float32, 'attn_q_gain': _jnp.float32, 'attn_k_gain': _jnp.float32, 'attn_w_out': _jnp.float32, 'conv_w_in': _jnp.float32, 'conv_b_in': _jnp.float32, 'conv_dw_w': _jnp.float32, 'conv_dw_b': _jnp.float32, 'conv_ln_g': _jnp.float32, 'conv_ln_b': _jnp.float32, 'conv_w_out': _jnp.float32, 'conv_b_out': _jnp.float32, 'hgrn_w_in': _jnp.float32, 'hgrn_lb_logits': _jnp.float32, 'hgrn_norm_g': _jnp.float32, 'hgrn_w_out': _jnp.float32, 'ffn_w_up': _jnp.float32, 'ffn_conv_w': _jnp.float32, 'ffn_conv_b': _jnp.float32, 'ffn_w_down': _jnp.float32}
MOMENT_SCALE = {'mixer_norm': 2.562064e+01, 'ffn_norm': 1.059883e+02, 'attn_w_in': 4.654771e-01, 'attn_q_gain': 2.777469e+00, 'attn_k_gain': 2.768791e+00, 'attn_w_out': 1.197985e+00, 'conv_w_in': 8.818612e-01, 'conv_b_in': 2.411547e+01, 'conv_dw_w': 2.983864e+00, 'conv_dw_b': 6.190009e+01, 'conv_ln_g': 6.346798e+01, 'conv_ln_b': 5.423200e+01, 'conv_w_out': 1.457287e+01, 'conv_b_out': 7.585039e+01, 'hgrn_w_in': 2.863932e+00, 'hgrn_lb_logits': 4.960095e-02, 'hgrn_norm_g': 4.748872e+01, 'hgrn_w_out': 3.506670e+00, 'ffn_w_up': 1.968701e+00, 'ffn_conv_w': 1.491094e+01, 'ffn_conv_b': 1.456110e+01, 'ffn_w_down': 1.638142e+00}


def _to_microbatches(a, axis):
    t = _jnp.moveaxis(a, axis, 0)
    t = t.reshape((N_MICROBATCH, t.shape[0] // N_MICROBATCH) + t.shape[1:])
    return _jnp.moveaxis(t, 1, axis + 1)


def setup_inputs(seed: int = 0) -> dict:
    inp = _fwd_setup_inputs(seed)
    key = _jax.random.fold_in(_jax.random.key(seed), 7919)
    shape, _ = _output_shape()
    out = dict(inp)
    out["loss_target"] = _jax.random.normal(_jax.random.fold_in(key, 0), shape, _jnp.float32)
    for i, name in enumerate(TWIN_WEIGHTS):
        w = inp[name].astype(_jnp.float32)
        if MOMENT_SCALE is None:
            s = _jnp.sqrt(_jnp.mean(_jnp.square(w)) + 1e-30)
        else:
            s = MOMENT_SCALE[name]
        km, kv = _jax.random.split(_jax.random.fold_in(key, i + 1))
        out[name] = w
        out["m_" + name] = s * _jax.random.normal(km, w.shape, _jnp.float32)
        out["v_" + name] = (s * s) * _jax.random.uniform(kv, w.shape, _jnp.float32, 0.5, 1.5)
    if N_MICROBATCH > 1:
        for name, axis in PER_EXAMPLE_BATCH_AXIS.items():
            out[name] = _to_microbatches(out[name], axis)
    return {'x': out['x'], 'mixer_norm': out['mixer_norm'], 'ffn_norm': out['ffn_norm'], 'attn_w_in': out['attn_w_in'], 'attn_q_gain': out['attn_q_gain'], 'attn_k_gain': out['attn_k_gain'], 'attn_w_out': out['attn_w_out'], 'conv_w_in': out['conv_w_in'], 'conv_b_in': out['conv_b_in'], 'conv_dw_w': out['conv_dw_w'], 'conv_dw_b': out['conv_dw_b'], 'conv_ln_g': out['conv_ln_g'], 'conv_ln_b': out['conv_ln_b'], 'conv_w_out': out['conv_w_out'], 'conv_b_out': out['conv_b_out'], 'hgrn_w_in': out['hgrn_w_in'], 'hgrn_lb_logits': out['hgrn_lb_logits'], 'hgrn_norm_g': out['hgrn_norm_g'], 'hgrn_w_out': out['hgrn_w_out'], 'ffn_w_up': out['ffn_w_up'], 'ffn_conv_w': out['ffn_conv_w'], 'ffn_conv_b': out['ffn_conv_b'], 'ffn_w_down': out['ffn_w_down'], 'loss_target': out['loss_target'], 'm_mixer_norm': out['m_mixer_norm'], 'm_ffn_norm': out['m_ffn_norm'], 'm_attn_w_in': out['m_attn_w_in'], 'm_attn_q_gain': out['m_attn_q_gain'], 'm_attn_k_gain': out['m_attn_k_gain'], 'm_attn_w_out': out['m_attn_w_out'], 'm_conv_w_in': out['m_conv_w_in'], 'm_conv_b_in': out['m_conv_b_in'], 'm_conv_dw_w': out['m_conv_dw_w'], 'm_conv_dw_b': out['m_conv_dw_b'], 'm_conv_ln_g': out['m_conv_ln_g'], 'm_conv_ln_b': out['m_conv_ln_b'], 'm_conv_w_out': out['m_conv_w_out'], 'm_conv_b_out': out['m_conv_b_out'], 'm_hgrn_w_in': out['m_hgrn_w_in'], 'm_hgrn_lb_logits': out['m_hgrn_lb_logits'], 'm_hgrn_norm_g': out['m_hgrn_norm_g'], 'm_hgrn_w_out': out['m_hgrn_w_out'], 'm_ffn_w_up': out['m_ffn_w_up'], 'm_ffn_conv_w': out['m_ffn_conv_w'], 'm_ffn_conv_b': out['m_ffn_conv_b'], 'm_ffn_w_down': out['m_ffn_w_down'], 'v_mixer_norm': out['v_mixer_norm'], 'v_ffn_norm': out['v_ffn_norm'], 'v_attn_w_in': out['v_attn_w_in'], 'v_attn_q_gain': out['v_attn_q_gain'], 'v_attn_k_gain': out['v_attn_k_gain'], 'v_attn_w_out': out['v_attn_w_out'], 'v_conv_w_in': out['v_conv_w_in'], 'v_conv_b_in': out['v_conv_b_in'], 'v_conv_dw_w': out['v_conv_dw_w'], 'v_conv_dw_b': out['v_conv_dw_b'], 'v_conv_ln_g': out['v_conv_ln_g'], 'v_conv_ln_b': out['v_conv_ln_b'], 'v_conv_w_out': out['v_conv_w_out'], 'v_conv_b_out': out['v_conv_b_out'], 'v_hgrn_w_in': out['v_hgrn_w_in'], 'v_hgrn_lb_logits': out['v_hgrn_lb_logits'], 'v_hgrn_norm_g': out['v_hgrn_norm_g'], 'v_hgrn_w_out': out['v_hgrn_w_out'], 'v_ffn_w_up': out['v_ffn_w_up'], 'v_ffn_conv_w': out['v_ffn_conv_w'], 'v_ffn_conv_b': out['v_ffn_conv_b'], 'v_ffn_w_down': out['v_ffn_w_down']}


def _loss(weights, diff, rest, loss_target):
    with _jax.named_scope("forward"):
        args = {**rest, TWIN_DIFF_INPUT: diff, **{k: w.astype(_WEIGHT_DTYPES[k]) for k, w in weights.items()}}
        y = _forward(args)
    with _jax.named_scope("loss_head"):
        err = _jnp.square(y.astype(_jnp.float32) - loss_target)
        return 0.5 * _jnp.sum(_jnp.mean(err, axis=-1)) if err.ndim else 0.5 * err


def _adamw(w, g, m, v):
    m = ADAM_B1 * m + (1.0 - ADAM_B1) * g
    v = ADAM_B2 * v + (1.0 - ADAM_B2) * _jnp.square(g)
    m_hat = m / (1.0 - ADAM_B1 ** ADAM_STEP)
    v_hat = v / (1.0 - ADAM_B2 ** ADAM_STEP)
    delta = -ADAM_LR * (m_hat / (_jnp.sqrt(v_hat) + ADAM_EPS) + ADAM_WD * w)
    return delta, m, v


def reference(x, mixer_norm, ffn_norm, attn_w_in, attn_q_gain, attn_k_gain, attn_w_out, conv_w_in, conv_b_in, conv_dw_w, conv_dw_b, conv_ln_g, conv_ln_b, conv_w_out, conv_b_out, hgrn_w_in, hgrn_lb_logits, hgrn_norm_g, hgrn_w_out, ffn_w_up, ffn_conv_w, ffn_conv_b, ffn_w_down, loss_target, m_mixer_norm, m_ffn_norm, m_attn_w_in, m_attn_q_gain, m_attn_k_gain, m_attn_w_out, m_conv_w_in, m_conv_b_in, m_conv_dw_w, m_conv_dw_b, m_conv_ln_g, m_conv_ln_b, m_conv_w_out, m_conv_b_out, m_hgrn_w_in, m_hgrn_lb_logits, m_hgrn_norm_g, m_hgrn_w_out, m_ffn_w_up, m_ffn_conv_w, m_ffn_conv_b, m_ffn_w_down, v_mixer_norm, v_ffn_norm, v_attn_w_in, v_attn_q_gain, v_attn_k_gain, v_attn_w_out, v_conv_w_in, v_conv_b_in, v_conv_dw_w, v_conv_dw_b, v_conv_ln_g, v_conv_ln_b, v_conv_w_out, v_conv_b_out, v_hgrn_w_in, v_hgrn_lb_logits, v_hgrn_norm_g, v_hgrn_w_out, v_ffn_w_up, v_ffn_conv_w, v_ffn_conv_b, v_ffn_w_down):
    given = dict(x=x, mixer_norm=mixer_norm, ffn_norm=ffn_norm, attn_w_in=attn_w_in, attn_q_gain=attn_q_gain, attn_k_gain=attn_k_gain, attn_w_out=attn_w_out, conv_w_in=conv_w_in, conv_b_in=conv_b_in, conv_dw_w=conv_dw_w, conv_dw_b=conv_dw_b, conv_ln_g=conv_ln_g, conv_ln_b=conv_ln_b, conv_w_out=conv_w_out, conv_b_out=conv_b_out, hgrn_w_in=hgrn_w_in, hgrn_lb_logits=hgrn_lb_logits, hgrn_norm_g=hgrn_norm_g, hgrn_w_out=hgrn_w_out, ffn_w_up=ffn_w_up, ffn_conv_w=ffn_conv_w, ffn_conv_b=ffn_conv_b, ffn_w_down=ffn_w_down, loss_target=loss_target, m_mixer_norm=m_mixer_norm, m_ffn_norm=m_ffn_norm, m_attn_w_in=m_attn_w_in, m_attn_q_gain=m_attn_q_gain, m_attn_k_gain=m_attn_k_gain, m_attn_w_out=m_attn_w_out, m_conv_w_in=m_conv_w_in, m_conv_b_in=m_conv_b_in, m_conv_dw_w=m_conv_dw_w, m_conv_dw_b=m_conv_dw_b, m_conv_ln_g=m_conv_ln_g, m_conv_ln_b=m_conv_ln_b, m_conv_w_out=m_conv_w_out, m_conv_b_out=m_conv_b_out, m_hgrn_w_in=m_hgrn_w_in, m_hgrn_lb_logits=m_hgrn_lb_logits, m_hgrn_norm_g=m_hgrn_norm_g, m_hgrn_w_out=m_hgrn_w_out, m_ffn_w_up=m_ffn_w_up, m_ffn_conv_w=m_ffn_conv_w, m_ffn_conv_b=m_ffn_conv_b, m_ffn_w_down=m_ffn_w_down, v_mixer_norm=v_mixer_norm, v_ffn_norm=v_ffn_norm, v_attn_w_in=v_attn_w_in, v_attn_q_gain=v_attn_q_gain, v_attn_k_gain=v_attn_k_gain, v_attn_w_out=v_attn_w_out, v_conv_w_in=v_conv_w_in, v_conv_b_in=v_conv_b_in, v_conv_dw_w=v_conv_dw_w, v_conv_dw_b=v_conv_dw_b, v_conv_ln_g=v_conv_ln_g, v_conv_ln_b=v_conv_ln_b, v_conv_w_out=v_conv_w_out, v_conv_b_out=v_conv_b_out, v_hgrn_w_in=v_hgrn_w_in, v_hgrn_lb_logits=v_hgrn_lb_logits, v_hgrn_norm_g=v_hgrn_norm_g, v_hgrn_w_out=v_hgrn_w_out, v_ffn_w_up=v_ffn_w_up, v_ffn_conv_w=v_ffn_conv_w, v_ffn_conv_b=v_ffn_conv_b, v_ffn_w_down=v_ffn_w_down)
    weights = {n: given[n] for n in TWIN_WEIGHTS}
    shared = {n: given[n] for n in SHARED_INPUTS}
    per_example = {n: given[n] for n in ['x']}
    grad_fn = _jax.value_and_grad(_loss, argnums=(0, 1))

    def one_microbatch(ex, loss_target):
        ex = dict(ex)
        diff = ex.pop(TWIN_DIFF_INPUT)
        return grad_fn(weights, diff, {**shared, **ex}, loss_target)

    if N_MICROBATCH == 1:
        loss, (grad_w, grad_x) = one_microbatch(per_example, given["loss_target"])
    else:
        def body(carry, xs):
            loss_sum, grad_sum = carry
            l_k, (gw_k, gx_k) = one_microbatch(xs[0], xs[1])
            with _jax.named_scope("update"):
                return (loss_sum + l_k, _jax.tree.map(_jnp.add, grad_sum, gw_k)), gx_k

        init = (_jnp.zeros((), _jnp.float32), _jax.tree.map(_jnp.zeros_like, weights))
        (loss, grad_w), grad_x = _jax.lax.scan(body, init, (per_example, given["loss_target"]))
    with _jax.named_scope("update"):
        delta_w, new_m, new_v = {}, {}, {}
        for n in TWIN_WEIGHTS:
            delta_w[n], new_m[n], new_v[n] = _adamw(weights[n], grad_w[n], given["m_" + n], given["v_" + n])
    return (loss, grad_x, *[grad_w[n] for n in TWIN_WEIGHTS], *[delta_w[n] for n in TWIN_WEIGHTS],
            *[new_m[n] for n in TWIN_WEIGHTS], *[new_v[n] for n in TWIN_WEIGHTS])
```

```python
import functools

import jax
import jax.numpy as jnp
from jax import lax
from jax.experimental import pallas as pl
from jax.experimental.pallas import tpu as pltpu

F32 = jnp.float32
BF16 = jnp.bfloat16

EPS = 1e-6
N_MIXERS = 3
ATTN_DILATIONS = (1, 4, 16)
ATTN_BLOCK = 128
ATTN_HEADS = 8
ATTN_HEAD_DIM = 64
ATTN_GW = ATTN_HEADS * ATTN_HEAD_DIM
HGRN_HEAD = 128
HGRN_CHUNK = 16
HGRN_TILE = 256
ADAM_LR, ADAM_B1, ADAM_B2, ADAM_EPS, ADAM_WD, ADAM_STEP = 0.001, 0.9, 0.999, 1e-08, 0.01, 10

LANES = 128
VMEM_LIMIT = 56 * 1024 * 1024
N_CHIPS = 4
MESH = pl.DeviceIdType.MESH

HI = lax.Precision.HIGHEST


def _params(*sem):
    return pltpu.CompilerParams(dimension_semantics=sem, vmem_limit_bytes=VMEM_LIMIT)


def _pick(n, target):
    if n <= target:
        return n
    best = None
    for t in range(LANES, target + 1, LANES):
        if n % t == 0:
            best = t
    assert best is not None, (n, target)
    return best


def _pick_rows(n, target):
    if n <= target:
        return n
    for t in range(target, 15, -16):
        if n % t == 0:
            return t
    return n


def _dot(a, b, dims, precision=None):
    return lax.dot_general(a, b, (dims, ((), ())), precision=precision, preferred_element_type=F32)


def _nn(a, b, precision=None):
    return _dot(a, b, ((1,), (0,)), precision)


def _nt(a, b, precision=None):
    return _dot(a, b, ((1,), (1,)), precision)


def _tn(a, b, precision=None):
    return _dot(a, b, ((0,), (0,)), precision)


def _sigmoid(x):
    return 1.0 / (1.0 + jnp.exp(-x))


def _rowwise(name, fn, rows, pars=(), outs=(), accs=(), *, tc=None, tm=512, rb=16):
    S = rows[0][0].shape[0]
    tm = _pick_rows(S, tm)
    rb = rb if tm % rb == 0 else tm
    width = tc if tc is not None else None
    ncol = 1
    if tc is not None:
        base = outs[0][0] if outs else accs[0][1]
        ncol = base // tc
    n_r, n_p, n_o, n_a = len(rows), len(pars), len(outs), len(accs)

    def body(*refs):
        row_refs, par_refs = refs[:n_r], refs[n_r:n_r + n_p]
        out_refs, acc_refs = refs[n_r + n_p:n_r + n_p + n_o], refs[n_r + n_p + n_o:]
        if n_a:
            @pl.when(pl.program_id(1) == 0)
            def _():
                for a in acc_refs:
                    a[...] = jnp.zeros_like(a)

        def step(s, carry):
            sl = pl.ds(pl.multiple_of(s * rb, rb), rb)
            res = fn(*[r[sl, :] for r in row_refs], *[p[...] for p in par_refs])
            res = res if isinstance(res, tuple) else (res,)
            for o, v in zip(out_refs, res[:n_o]):
                o[sl, :] = v.astype(o.dtype)
            for a, v in zip(acc_refs, res[n_o:]):
                a[...] += v
            return carry

        lax.fori_loop(0, tm // rb, step, 0)

    def row_spec(c, off):
        if tc is None:
            return pl.BlockSpec((tm, c), lambda j, i: (i, 0))
        return pl.BlockSpec((tm, tc), lambda j, i, o=off // tc: (i, j + o))

    def par_spec(shape, off):
        if off is None or tc is None:
            return pl.BlockSpec(shape, lambda j, i: (0, 0))
        return pl.BlockSpec((shape[0], tc), lambda j, i, o=off // tc: (0, j + o))

    in_specs = [row_spec(a.shape[1], off) for a, off in rows]
    in_specs += [par_spec(a.shape, off) for a, off in pars]
    out_specs = [row_spec(c, 0) for c, _ in outs] + [par_spec(s, 0) for s in accs]
    out_shape = [jax.ShapeDtypeStruct((S, c), d) for c, d in outs]
    out_shape += [jax.ShapeDtypeStruct(s, F32) for s in accs]
    res = pl.pallas_call(
        body, name=name, grid=(ncol, S // tm), in_specs=in_specs, out_specs=out_specs, out_shape=out_shape,
        compiler_params=_params("parallel", "arbitrary" if n_a else "parallel"),
    )(*[a for a, _ in rows], *[a for a, _ in pars])
    return res[0] if len(res) == 1 else tuple(res)


def _matmul(name, pairs, *, trans_b=False, bias=None, residual=None, out_dtype=F32, tm=512, tn=512):
    M = pairs[0][0].shape[0]
    N = pairs[0][1].shape[0] if trans_b else pairs[0][1].shape[1]
    tm, tn = _pick_rows(M, tm), _pick(N, tn)
    n = len(pairs)

    def body(*refs):
        acc = None
        for i in range(n):
            a = refs[2 * i][...].astype(BF16)
            b = refs[2 * i + 1][...].astype(BF16)
            d = _nt(a, b) if trans_b else _nn(a, b)
            acc = d if acc is None else acc + d
        k = 2 * n
        if bias is not None:
            acc = acc + refs[k][...]
            k += 1
        if residual is not None:
            acc = acc + refs[k][...]
            k += 1
        refs[k][...] = acc.astype(out_dtype)

    in_specs, args = [], []
    for a, b in pairs:
        K = a.shape[1]
        in_specs.append(pl.BlockSpec((tm, K), lambda i, j: (i, 0)))
        in_specs.append(pl.BlockSpec((tn, K), lambda i, j: (j, 0)) if trans_b
                        else pl.BlockSpec((K, tn), lambda i, j: (0, j)))
        args += [a, b]
    if bias is not None:
        in_specs.append(pl.BlockSpec((1, tn), lambda i, j: (0, j)))
        args.append(bias)
    if residual is not None:
        in_specs.append(pl.BlockSpec((tm, tn), lambda i, j: (i, j)))
        args.append(residual)
    return pl.pallas_call(
        body, name=name, grid=(M // tm, N // tn), in_specs=in_specs,
        out_specs=pl.BlockSpec((tm, tn), lambda i, j: (i, j)),
        out_shape=jax.ShapeDtypeStruct((M, N), out_dtype), compiler_params=_params("parallel", "parallel"),
    )(*args)


def _matmul_tn(name, a, b, *, tm=1024, tn=512, tk=512):
    S, M = a.shape
    N = b.shape[1]
    tm, tn, tk = _pick(M, tm), _pick(N, tn), _pick_rows(S, tk)

    def body(a_ref, b_ref, o_ref):
        @pl.when(pl.program_id(2) == 0)
        def _():
            o_ref[...] = jnp.zeros_like(o_ref)

        o_ref[...] += _tn(a_ref[...].astype(BF16), b_ref[...].astype(BF16))

    return pl.pallas_call(
        body, name=name, grid=(M // tm, N // tn, S // tk),
        in_specs=[pl.BlockSpec((tk, tm), lambda i, j, k: (k, i)), pl.BlockSpec((tk, tn), lambda i, j, k: (k, j))],
        out_specs=pl.BlockSpec((tm, tn), lambda i, j, k: (i, j)),
        out_shape=jax.ShapeDtypeStruct((M, N), F32), compiler_params=_params("parallel", "parallel", "arbitrary"),
    )(a, b)


def _halo_rows(K):
    return 8 if K <= 9 else 32


def _dwconv(name, x, w, b, *, reverse, out_dtype=F32, tm=256, tc=256):
    S, C = x.shape
    K = w.shape[0]
    H = _halo_rows(K)
    tm, tc = _pick_rows(S, tm), _pick(C, tc)
    nrow = S // tm
    RB = 16 if out_dtype == BF16 else 8

    def body(x_ref, h_ref, w_ref, b_ref, o_ref, ext):
        i = pl.program_id(1)
        edge = (i == nrow - 1) if reverse else (i == 0)
        halo = jnp.where(edge, 0.0, h_ref[...].astype(F32))
        if reverse:
            ext[0:tm, :] = x_ref[...].astype(F32)
            ext[tm:tm + H, :] = halo
        else:
            ext[0:H, :] = halo
            ext[H:H + tm, :] = x_ref[...].astype(F32)
        wv = w_ref[...]
        for s in range(tm // RB):
            acc = jnp.broadcast_to(b_ref[...], (RB, tc))
            for k in range(K):
                off = s * RB + ((K - 1 - k) if reverse else (H - (K - 1) + k))
                acc = acc + wv[k:k + 1, :] * ext[off:off + RB, :]
            o_ref[s * RB:(s + 1) * RB, :] = acc.astype(out_dtype)

    r = tm // H
    if reverse:
        halo_map = lambda j, i: (jnp.minimum((i + 1) * r, S // H - 1), j)
    else:
        halo_map = lambda j, i: (jnp.maximum(i * r - 1, 0), j)
    return pl.pallas_call(
        body, name=name, grid=(C // tc, nrow),
        in_specs=[pl.BlockSpec((tm, tc), lambda j, i: (i, j)), pl.BlockSpec((H, tc), halo_map),
                  pl.BlockSpec((K, tc), lambda j, i: (0, j)), pl.BlockSpec((1, tc), lambda j, i: (0, j))],
        out_specs=pl.BlockSpec((tm, tc), lambda j, i: (i, j)),
        out_shape=jax.ShapeDtypeStruct((S, C), out_dtype),
        scratch_shapes=[pltpu.VMEM((tm + H, tc), F32)], compiler_params=_params("parallel", "parallel"),
    )(x, x, w, b)


def _dwconv_wgrad(name, x, dy, K, *, tm=256):
    S, C = x.shape
    H = _halo_rows(K)
    tm, tc = _pick_rows(S, tm), LANES
    RB = 8

    def body(x_ref, h_ref, dy_ref, dw_ref, db_ref, ext):
        i = pl.program_id(1)

        @pl.when(i == 0)
        def _():
            dw_ref[...] = jnp.zeros_like(dw_ref)
            db_ref[...] = jnp.zeros_like(db_ref)

        ext[0:H, :] = jnp.where(i == 0, 0.0, h_ref[...].astype(F32))
        ext[H:H + tm, :] = x_ref[...].astype(F32)
        acc = [jnp.zeros((RB, tc), F32) for _ in range(K)]
        accb = jnp.zeros((RB, tc), F32)
        for s in range(tm // RB):
            d = dy_ref[s * RB:(s + 1) * RB, :].astype(F32)
            accb = accb + d
            for k in range(K):
                off = s * RB + H - (K - 1) + k
                acc[k] = acc[k] + d * ext[off:off + RB, :]
        for k in range(K):
            dw_ref[k:k + 1, :] += jnp.sum(acc[k], axis=0, keepdims=True)
        db_ref[...] += jnp.sum(accb, axis=0, keepdims=True)

    r = tm // H
    return pl.pallas_call(
        body, name=name, grid=(C // tc, S // tm),
        in_specs=[pl.BlockSpec((tm, tc), lambda j, i: (i, j)),
                  pl.BlockSpec((H, tc), lambda j, i: (jnp.maximum(i * r - 1, 0), j)),
                  pl.BlockSpec((tm, tc), lambda j, i: (i, j))],
        out_specs=[pl.BlockSpec((K, tc), lambda j, i: (0, j)), pl.BlockSpec((1, tc), lambda j, i: (0, j))],
        out_shape=[jax.ShapeDtypeStruct((K, C), F32), jax.ShapeDtypeStruct((1, C), F32)],
        scratch_shapes=[pltpu.VMEM((tm + H, tc), F32)], compiler_params=_params("parallel", "arbitrary"),
    )(x, x, dy)


def _colsum(v):
    return jnp.sum(v, axis=0, keepdims=True)


def _rmsnorm_fwd(name, x, gain):
    def fn(x, g):
        r = lax.rsqrt(jnp.mean(x * x, axis=-1, keepdims=True) + EPS)
        return x * r * g
    return _rowwise(name, fn, [(x, 0)], [(gain, None)], [(x.shape[1], BF16)])


def _rmsnorm_bwd(name, x, gain, dh, dres):
    def fn(x, dh, dres, g):
        r = lax.rsqrt(jnp.mean(x * x, axis=-1, keepdims=True) + EPS)
        xh = x * r
        dxh = dh * g
        dx = r * (dxh - xh * jnp.mean(dxh * xh, axis=-1, keepdims=True))
        return dres + dx, _colsum(dh * xh)
    D = x.shape[1]
    return _rowwise(name, fn, [(x, 0), (dh, 0), (dres, 0)], [(gain, None)], [(D, F32)], [(1, D)])


def _silu_gate_fwd(name, gate, up):
    F = gate.shape[1]
    def fn(g, up):
        return g * _sigmoid(g) * up
    return _rowwise(name, fn, [(gate, 0), (up, 0)], [], [(F, BF16)], tc=_pick(F, 512))


def _silu_gate_bwd(name, gate, up, da):
    F = gate.shape[1]
    def fn(g, up, da):
        s = _sigmoid(g)
        return da * up * (s * (1.0 + g * (1.0 - s))), da * (g * s)
    return _rowwise(name, fn, [(gate, 0), (up, 0), (da, 0)], [], [(F, F32), (F, F32)], tc=_pick(F, 512))


def _glu_fwd(name, a, gate):
    C = a.shape[1]
    def fn(a, g):
        return a * _sigmoid(g)
    return _rowwise(name, fn, [(a, 0), (gate, 0)], [], [(C, F32)], tc=_pick(C, 512))


def _glu_bwd(name, a, gate, dglu):
    C = a.shape[1]
    def fn(a, g, d):
        s = _sigmoid(g)
        da, dg = d * s, d * a * s * (1.0 - s)
        return da, dg, _colsum(da), _colsum(dg)
    return _rowwise(name, fn, [(a, 0), (gate, 0), (dglu, 0)], [], [(C, BF16), (C, BF16)], [(1, C), (1, C)],
                    tc=_pick(C, 512))


def _ln_silu_fwd(name, c, g, b):
    def fn(c, g, b):
        mu = jnp.mean(c, axis=-1, keepdims=True)
        d = c - mu
        n = d * lax.rsqrt(jnp.mean(d * d, axis=-1, keepdims=True) + EPS) * g + b
        return n * _sigmoid(n)
    return _rowwise(name, fn, [(c, 0)], [(g, None), (b, None)], [(c.shape[1], BF16)])


def _ln_silu_bwd(name, c, g, b, dsw):
    def fn(c, dsw, g, b):
        mu = jnp.mean(c, axis=-1, keepdims=True)
        d = c - mu
        r = lax.rsqrt(jnp.mean(d * d, axis=-1, keepdims=True) + EPS)
        ch = d * r
        n = ch * g + b
        s = _sigmoid(n)
        dn = dsw * (s * (1.0 + n * (1.0 - s)))
        dch = dn * g
        dc = r * (dch - jnp.mean(dch, axis=-1, keepdims=True) - ch * jnp.mean(dch * ch, axis=-1, keepdims=True))
        return dc, _colsum(dn * ch), _colsum(dn)
    C = c.shape[1]
    return _rowwise(name, fn, [(c, 0), (dsw, 0)], [(g, None), (b, None)], [(C, F32)], [(1, C), (1, C)])


def _column_sums(name, x):
    return _rowwise(name, lambda x: (_colsum(x),), [(x, 0)], [], [], [(1, x.shape[1])])


def _loss_grad(name, y, target):
    D = y.shape[1]
    def fn(y, t):
        e = y - t
        return e * (1.0 / D), _colsum(e * e) * (0.5 / D)
    return _rowwise(name, fn, [(y, 0), (target, 0)], [], [(D, F32)], [(1, D)])


def _add(name, arrays):
    def fn(*xs):
        acc = xs[0]
        for x in xs[1:]:
            acc = acc + x
        return acc
    return _rowwise(name, fn, [(a, 0) for a in arrays], [], [(arrays[0].shape[1], F32)], tm=256)


def _sum_slabs(name, stacked):
    n, R, C = stacked.shape
    tm = _pick_rows(R, 256)

    def body(*refs):
        acc = refs[0][0]
        for r in refs[1:n]:
            acc = acc + r[0]
        refs[n][...] = acc

    return pl.pallas_call(
        body, name=name, grid=(R // tm,),
        in_specs=[pl.BlockSpec((1, tm, C), lambda i, q=q: (q, i, 0)) for q in range(n)],
        out_specs=pl.BlockSpec((tm, C), lambda i: (i, 0)), out_shape=jax.ShapeDtypeStruct((R, C), F32),
        compiler_params=_params("parallel"),
    )(*[stacked] * n)


def _adamw(name, w, g, m, v):
    c1 = 1.0 - ADAM_B1 ** ADAM_STEP
    c2 = 1.0 - ADAM_B2 ** ADAM_STEP
    def fn(w, g, m, v):
        m = ADAM_B1 * m + (1.0 - ADAM_B1) * g
        v = ADAM_B2 * v + (1.0 - ADAM_B2) * (g * g)
        delta = -ADAM_LR * ((m / c1) / (jnp.sqrt(v / c2) + ADAM_EPS) + ADAM_WD * w)
        return delta, m, v
    C = w.shape[1]
    return _rowwise(name, fn, [(w, 0), (g, 0), (m, 0), (v, 0)], [], [(C, F32)] * 3, tm=256)


def _segment_matrix(n, seg):
    i = jnp.arange(n) // seg
    return (i[:, None] == i[None, :]).astype(F32)


def _qknorm_fwd(name, proj, gain_full, is_norm, seg):
    def fn(x, gf, isn, B):
        ms = _nn(x * x, B, HI) * (1.0 / ATTN_HEAD_DIM)
        r = lax.rsqrt(ms + EPS)
        return x * (isn * r + (1.0 - isn)) * gf
    W = proj.shape[1]
    return _rowwise(name, fn, [(proj, 0)], [(gain_full, 0), (is_norm, 0), (seg, None)], [(W, BF16)], tc=ATTN_GW)


def _qknorm_bwd(name, proj, dy, gain_full, is_norm, seg):
    def fn(x, dy, gf, isn, B):
        ms = _nn(x * x, B, HI) * (1.0 / ATTN_HEAD_DIM)
        r = lax.rsqrt(ms + EPS)
        xh = x * r
        dxh = dy * gf
        dn = r * (dxh - xh * (_nn(dxh * xh, B, HI) * (1.0 / ATTN_HEAD_DIM)))
        return isn * dn + (1.0 - isn) * dxh, _colsum(dy * xh)
    W = proj.shape[1]
    return _rowwise(name, fn, [(proj, 0), (dy, 0)], [(gain_full, 0), (is_norm, 0), (seg, None)],
                    [(W, BF16)], [(1, W)], tc=ATTN_GW)


def _attn_masks():
    shape = (ATTN_BLOCK, ATTN_BLOCK)
    row = lax.broadcasted_iota(jnp.int32, shape, 0)
    col = lax.broadcasted_iota(jnp.int32, shape, 1)
    low_lanes = col < ATTN_HEAD_DIM
    return col <= row, col >= row, low_lanes


def _attn_group_fwd(name, qkv, d):
    S = qkv.shape[0]
    n, W, G = S // d, 3 * ATTN_GW, ATTN_GW
    nb = n // ATTN_BLOCK
    view = qkv.reshape(n, d * W)

    def body(cur, prev, o_ref, l_ref):
        b = pl.program_id(1)
        cur_mask, prev_mask, low = _attn_masks()
        prev_mask = jnp.logical_and(prev_mask, b > 0)
        for pr in range(G // LANES):
            c0 = pr * LANES
            q2, kc, vc = cur[:, c0:c0 + LANES], cur[:, G + c0:G + c0 + LANES], cur[:, 2 * G + c0:2 * G + c0 + LANES]
            kp, vp = prev[:, G + c0:G + c0 + LANES], prev[:, 2 * G + c0:2 * G + c0 + LANES]
            res = []
            for hm in (low, jnp.logical_not(low)):
                qm = jnp.where(hm, q2, jnp.zeros_like(q2))
                sc = jnp.where(cur_mask, _nt(qm, kc), -jnp.inf)
                sp = jnp.where(prev_mask, _nt(qm, kp), -jnp.inf)
                m = jnp.maximum(jnp.max(sc, axis=1, keepdims=True), jnp.max(sp, axis=1, keepdims=True))
                pc, pp = jnp.exp(sc - m), jnp.exp(sp - m)
                l = jnp.sum(pc, axis=1, keepdims=True) + jnp.sum(pp, axis=1, keepdims=True)
                o = (_nn(pc.astype(BF16), vc) + _nn(pp.astype(BF16), vp)) / l
                res.append((o, jnp.broadcast_to(m + jnp.log(l), o.shape)))
            o_ref[:, c0:c0 + LANES] = jnp.where(low, res[0][0], res[1][0])
            l_ref[:, c0:c0 + LANES] = jnp.where(low, res[0][1], res[1][1])

    o, l = pl.pallas_call(
        body, name=name, grid=(d, nb),
        in_specs=[pl.BlockSpec((ATTN_BLOCK, W), lambda r, b: (b, r)),
                  pl.BlockSpec((ATTN_BLOCK, W), lambda r, b: (jnp.maximum(b - 1, 0), r))],
        out_specs=[pl.BlockSpec((ATTN_BLOCK, G), lambda r, b: (b, r))] * 2,
        out_shape=[jax.ShapeDtypeStruct((n, d * G), F32)] * 2, compiler_params=_params("parallel", "parallel"),
    )(view, view)
    return o.reshape(S, G), l.reshape(S, G)


def _attn_combine(name, os, ls):
    def fn(o1, o2, o3, l1, l2, l3):
        m = jnp.maximum(jnp.maximum(l1, l2), l3)
        e1, e2, e3 = jnp.exp(l1 - m), jnp.exp(l2 - m), jnp.exp(l3 - m)
        den = e1 + e2 + e3
        return (e1 * o1 + e2 * o2 + e3 * o3) / den, m + jnp.log(den)
    G = os[0].shape[1]
    return _rowwise(name, fn, [(a, 0) for a in (*os, *ls)], [], [(G, F32), (G, F32)])


def _attn_delta(name, do, o, seg):
    def fn(do, o, B):
        return _nn(do * o, B, HI)
    return _rowwise(name, fn, [(do, 0), (o, 0)], [(seg, None)], [(o.shape[1], F32)])


def _attn_group_bwd(name, qkv, do, lse, delta, d):
    S = qkv.shape[0]
    n, W, G = S // d, 3 * ATTN_GW, ATTN_GW
    nb = n // ATTN_BLOCK

    def body(qp, qc, qn, do_c, do_n, l_c, l_n, dl_c, dl_n, out):
        j = pl.program_id(1)
        cur_mask, prev_mask, low = _attn_masks()
        next_mask = jnp.logical_and(prev_mask, j < nb - 1)
        prev_mask = jnp.logical_and(prev_mask, j > 0)
        for pr in range(G // LANES):
            c0 = pr * LANES
            q_c, k_c, v_c = qc[:, c0:c0 + LANES], qc[:, G + c0:G + c0 + LANES], qc[:, 2 * G + c0:2 * G + c0 + LANES]
            q_n = qn[:, c0:c0 + LANES]
            k_p, v_p = qp[:, G + c0:G + c0 + LANES], qp[:, 2 * G + c0:2 * G + c0 + LANES]
            d_c, d_n = do_c[:, c0:c0 + LANES].astype(BF16), do_n[:, c0:c0 + LANES].astype(BF16)
            res = []
            for hh, hm in enumerate((low, jnp.logical_not(low))):
                h0 = c0 + hh * ATTN_HEAD_DIM
                lc, ln = l_c[:, h0:h0 + 1], l_n[:, h0:h0 + 1]
                dlc, dln = dl_c[:, h0:h0 + 1], dl_n[:, h0:h0 + 1]
                zero = jnp.zeros_like(q_c)
                qmc, qmn = jnp.where(hm, q_c, zero), jnp.where(hm, q_n, zero)
                dmc, dmn = jnp.where(hm, d_c, zero), jnp.where(hm, d_n, zero)
                p_a = jnp.where(cur_mask, jnp.exp(_nt(qmc, k_c) - lc), 0.0)
                ds_a = p_a * (_nt(dmc, v_c) - dlc)
                p_b = jnp.where(next_mask, jnp.exp(_nt(qmn, k_c) - ln), 0.0)
                ds_b = p_b * (_nt(dmn, v_c) - dln)
                p_c = jnp.where(prev_mask, jnp.exp(_nt(qmc, k_p) - lc), 0.0)
                ds_c = p_c * (_nt(dmc, v_p) - dlc)
                dq = _nn(ds_a.astype(BF16), k_c) + _nn(ds_c.astype(BF16), k_p)
                dk = _tn(ds_a.astype(BF16), q_c) + _tn(ds_b.astype(BF16), q_n)
                dv = _tn(p_a.astype(BF16), d_c) + _tn(p_b.astype(BF16), d_n)
                res.append((dq, dk, dv))
            for t in range(3):
                out[:, t * G + c0:t * G + c0 + LANES] = jnp.where(low, res[0][t], res[1][t])

    prv = lambda r, j: (jnp.maximum(j - 1, 0), r)
    cur = lambda r, j: (j, r)
    nxt = lambda r, j: (jnp.minimum(j + 1, nb - 1), r)
    wide = lambda m: pl.BlockSpec((ATTN_BLOCK, W), m)
    narrow = lambda m: pl.BlockSpec((ATTN_BLOCK, G), m)
    qv, dv, lv, tv = qkv.reshape(n, d * W), do.reshape(n, d * G), lse.reshape(n, d * G), delta.reshape(n, d * G)
    out = pl.pallas_call(
        body, name=name, grid=(d, nb),
        in_specs=[wide(prv), wide(cur), wide(nxt), narrow(cur), narrow(nxt), narrow(cur), narrow(nxt),
                  narrow(cur), narrow(nxt)],
        out_specs=wide(cur), out_shape=jax.ShapeDtypeStruct((n, d * W), F32),
        compiler_params=_params("parallel", "parallel"),
    )(qv, qv, qv, dv, dv, lv, lv, tv, tv)
    return out.reshape(S, W)


def _chunk_triangle(T, upper):
    i = jnp.arange(T)
    same = (i[:, None] // HGRN_CHUNK) == (i[None, :] // HGRN_CHUNK)
    tri = (i[None, :] >= i[:, None]) if upper else (i[None, :] <= i[:, None])
    return jnp.logical_and(same, tri).astype(F32)


def _hgrn_prologue(qr, fr, lbv, q_s, k_s, b_s, tri_ref, T):
    def pro(s, c):
        sl = pl.ds(pl.multiple_of(s * HGRN_CHUNK, HGRN_CHUNK), HGRN_CHUNK)
        sg = _sigmoid(fr[sl, :])
        qv = qr[sl, :]
        q_s[sl, :] = qv * _sigmoid(qv)
        k_s[sl, :] = (1.0 - lbv) * (1.0 - sg)
        b_s[sl, :] = jnp.log(lbv + (1.0 - lbv) * sg)
        return c
    lax.fori_loop(0, T // HGRN_CHUNK, pro, 0)
    b_s[...] = _nn(tri_ref[...], b_s[...], HI)


def _hgrn_scan_fwd(name, pq, pf, pv, lb):
    S, D = pq.shape
    T = _pick_rows(S, HGRN_TILE)
    NH, NT, C, HD = D // HGRN_HEAD, S // T, HGRN_CHUNK, HGRN_HEAD
    tri = _chunk_triangle(T, upper=False)

    def body(qr, fr, iv, lb_ref, tri_ref, o_ref, ck_ref, st_ref, q_s, k_s, b_s):
        @pl.when(pl.program_id(1) == 0)
        def _():
            st_ref[...] = jnp.zeros_like(st_ref)

        ck_ref[...] = st_ref[...]
        _hgrn_prologue(qr, fr, lb_ref[...], q_s, k_s, b_s, tri_ref, T)
        row = lax.broadcasted_iota(jnp.int32, (C, 1), 0)

        def chunk(c, carry):
            sl = pl.ds(pl.multiple_of(c * C, C), C)
            q, k, b, v = q_s[sl, :], k_s[sl, :], b_s[sl, :], iv[sl, :]
            b_last = b[C - 1:C, :]
            st = st_ref[...]
            o = _nt((q * jnp.exp(b)).astype(BF16), st.astype(BF16))
            for s in range(C):
                e = jnp.exp(jnp.minimum(b - b[s:s + 1, :], 0.0))
                a = jnp.sum(q * e * k[s:s + 1, :], axis=1, keepdims=True)
                o = o + jnp.where(row >= s, a, 0.0) * v[s:s + 1, :]
            o_ref[sl, :] = o
            kd = k * jnp.exp(b_last - b)
            st_ref[...] = st * jnp.exp(b_last) + _tn(v.astype(BF16), kd.astype(BF16))
            return carry

        lax.fori_loop(0, T // C, chunk, 0)

    col = lambda off: pl.BlockSpec((T, HD), lambda h, t, o=off: (t, h + o))
    return pl.pallas_call(
        body, name=name, grid=(NH, NT),
        in_specs=[col(0), col(0), col(0), pl.BlockSpec((1, HD), lambda h, t: (0, h)),
                  pl.BlockSpec((T, T), lambda h, t: (0, 0))],
        out_specs=[col(0), pl.BlockSpec((HD, HD), lambda h, t: (t * NH + h, 0))],
        out_shape=[jax.ShapeDtypeStruct((S, D), F32), jax.ShapeDtypeStruct((NT * NH * HD, HD), F32)],
        scratch_shapes=[pltpu.VMEM((HD, HD), F32)] + [pltpu.VMEM((T, HD), F32)] * 3,
        compiler_params=_params("parallel", "arbitrary"),
    )(pq, pf, pv, lb, tri)


def _hgrn_scan_bwd(name, pq, pf, pv, lb, ckpt, do):
    S, D = pq.shape
    T = _pick_rows(S, HGRN_TILE)
    NH, NT, C, HD = D // HGRN_HEAD, S // T, HGRN_CHUNK, HGRN_HEAD
    NC = T // C
    tri, tri_up = _chunk_triangle(T, upper=False), _chunk_triangle(T, upper=True)

    def body(qr, fr, iv, do_ref, ck_ref, lb_ref, tri_ref, triu_ref, dq_ref, df_ref, dv_ref, dlb_ref,
             dst_ref, save, q_s, k_s, b_s, dq_s, dk_s, db_s):
        @pl.when(pl.program_id(1) == 0)
        def _():
            dst_ref[...] = jnp.zeros_like(dst_ref)
            dlb_ref[...] = jnp.zeros_like(dlb_ref)

        lbv = lb_ref[...]
        _hgrn_prologue(qr, fr, lbv, q_s, k_s, b_s, tri_ref, T)
        row = lax.broadcasted_iota(jnp.int32, (C, 1), 0)

        def replay(c, st):
            sl = pl.ds(pl.multiple_of(c * C, C), C)
            save[pl.ds(pl.multiple_of(c * HD, HD), HD), :] = st
            k, b, v = k_s[sl, :], b_s[sl, :], iv[sl, :]
            b_last = b[C - 1:C, :]
            kd = k * jnp.exp(b_last - b)
            return st * jnp.exp(b_last) + _tn(v.astype(BF16), kd.astype(BF16))

        lax.fori_loop(0, NC, replay, ck_ref[...])

        def chunk(ci, carry):
            c = NC - 1 - ci
            sl = pl.ds(pl.multiple_of(c * C, C), C)
            q, k, b, v, g = q_s[sl, :], k_s[sl, :], b_s[sl, :], iv[sl, :], do_ref[sl, :]
            st0 = save[pl.ds(pl.multiple_of(c * HD, HD), HD), :]
            dst1 = dst_ref[...]
            b_last = b[C - 1:C, :]
            eb, ebl, ek = jnp.exp(b), jnp.exp(b_last), jnp.exp(b_last - b)
            dst1_b = dst1.astype(BF16)
            dq = _nn(g.astype(BF16), st0.astype(BF16)) * eb
            dv = _nt((k * ek).astype(BF16), dst1_b)
            dk = _nn(v.astype(BF16), dst1_b) * ek
            db_last = _colsum(dk * k) + _colsum(dst1 * st0) * ebl
            for s in range(C):
                e = jnp.where(row >= s, jnp.exp(jnp.minimum(b - b[s:s + 1, :], 0.0)), 0.0)
                ks, vs = k[s:s + 1, :], v[s:s + 1, :]
                da = jnp.sum(g * vs, axis=1, keepdims=True)
                a = jnp.sum(q * e * ks, axis=1, keepdims=True)
                dq = dq + da * e * ks
                dk = dk + jnp.where(row == s, _colsum(da * q * e), 0.0)
                dv = dv + jnp.where(row == s, _colsum(a * g), 0.0)
            dq_s[sl, :] = dq
            dk_s[sl, :] = dk
            db_s[sl, :] = q * dq - k * dk + jnp.where(row == C - 1, db_last, 0.0)
            dv_ref[sl, :] = dv.astype(BF16)
            dst_ref[...] = dst1 * ebl + _tn(g.astype(BF16), (q * eb).astype(BF16))
            return carry

        lax.fori_loop(0, NC, chunk, 0)
        db_s[...] = _nn(triu_ref[...], db_s[...], HI)

        def epi(s, carry):
            sl = pl.ds(pl.multiple_of(s * C, C), C)
            qv = qr[sl, :]
            sq = _sigmoid(qv)
            dq_ref[sl, :] = (dq_s[sl, :] * sq * (1.0 + qv * (1.0 - sq))).astype(BF16)
            sg = _sigmoid(fr[sl, :])
            common = db_s[sl, :] / (lbv + (1.0 - lbv) * sg) - dk_s[sl, :]
            df_ref[sl, :] = (common * (1.0 - lbv) * sg * (1.0 - sg)).astype(BF16)
            dlb_ref[...] += _colsum(common * (1.0 - sg))
            return carry

        lax.fori_loop(0, NC, epi, 0)

    rev = lambda h, t: (NT - 1 - t, h)
    col = pl.BlockSpec((T, HD), rev)
    dq, df, dv, dlb = pl.pallas_call(
        body, name=name, grid=(NH, NT),
        in_specs=[col, col, col, col, pl.BlockSpec((HD, HD), lambda h, t: ((NT - 1 - t) * NH + h, 0)),
                  pl.BlockSpec((1, HD), lambda h, t: (0, h)),
                  pl.BlockSpec((T, T), lambda h, t: (0, 0)), pl.BlockSpec((T, T), lambda h, t: (0, 0))],
        out_specs=[col, col, col, pl.BlockSpec((1, HD), lambda h, t: (0, h))],
        out_shape=[jax.ShapeDtypeStruct((S, D), BF16)] * 3 + [jax.ShapeDtypeStruct((1, D), F32)],
        scratch_shapes=[pltpu.VMEM((HD, HD), F32), pltpu.VMEM((NC * HD, HD), F32)] + [pltpu.VMEM((T, HD), F32)] * 6,
        compiler_params=_params("parallel", "arbitrary"),
    )(pq, pf, pv, do, ckpt, lb, tri, tri_up)
    return dq, df, dv, dlb


def _hgrn_out_fwd(name, o, gate, norm_g):
    def fn(o, g, ng):
        parts = []
        for h in range(o.shape[1] // HGRN_HEAD):
            c = slice(h * HGRN_HEAD, (h + 1) * HGRN_HEAD)
            oh, gh = o[:, c], g[:, c]
            r = lax.rsqrt(jnp.mean(oh * oh, axis=-1, keepdims=True) + EPS)
            parts.append(oh * r * ng[:, c] * (gh * _sigmoid(gh)))
        return jnp.concatenate(parts, axis=1)
    return _rowwise(name, fn, [(o, 0), (gate, 0)], [(norm_g, None)], [(o.shape[1], BF16)])


def _hgrn_out_bwd(name, o, gate, norm_g, dy):
    def fn(o, g, dy, ng):
        dos, dgs, dngs = [], [], []
        for h in range(o.shape[1] // HGRN_HEAD):
            c = slice(h * HGRN_HEAD, (h + 1) * HGRN_HEAD)
            oh, gh, dyh, ngh = o[:, c], g[:, c], dy[:, c], ng[:, c]
            r = lax.rsqrt(jnp.mean(oh * oh, axis=-1, keepdims=True) + EPS)
            xh = oh * r
            s = _sigmoid(gh)
            dn = dyh * (gh * s)
            dxh = dn * ngh
            dos.append(r * (dxh - xh * jnp.mean(dxh * xh, axis=-1, keepdims=True)))
            dgs.append(dyh * xh * ngh * (s * (1.0 + gh * (1.0 - s))))
            dngs.append(_colsum(dn * xh))
        return jnp.concatenate(dos, axis=1), jnp.concatenate(dgs, axis=1), jnp.concatenate(dngs, axis=1)
    D = o.shape[1]
    return _rowwise(name, fn, [(o, 0), (gate, 0), (dy, 0)], [(norm_g, None)], [(D, F32), (D, BF16)], [(1, D)])


def _lower_bound_fwd(name, logits, layer):
    n = logits.shape[0]

    def body(x_ref, o_ref):
        rows = [x_ref[i:i + 1, :] for i in range(n)]
        m = functools.reduce(jnp.maximum, rows)
        e = [jnp.exp(r - m) for r in rows]
        den = functools.reduce(jnp.add, e)
        o_ref[...] = functools.reduce(jnp.add, e[1:layer + 1]) / den

    return pl.pallas_call(body, name=name, out_shape=jax.ShapeDtypeStruct((1, logits.shape[1]), F32))(logits)


def _lower_bound_bwd(name, logits, dlb, layer):
    n = logits.shape[0]

    def body(x_ref, d_ref, o_ref):
        rows = [x_ref[i:i + 1, :] for i in range(n)]
        m = functools.reduce(jnp.maximum, rows)
        e = [jnp.exp(r - m) for r in rows]
        den = functools.reduce(jnp.add, e)
        s = [v / den for v in e]
        d = d_ref[...]
        inner = functools.reduce(jnp.add, s[1:layer + 1]) * d
        for i in range(n):
            o_ref[i:i + 1, :] = s[i] * ((d if 1 <= i <= layer else 0.0) - inner)

    return pl.pallas_call(body, name=name, out_shape=jax.ShapeDtypeStruct(logits.shape, F32))(logits, dlb)


def _row(v):
    return v.reshape(1, -1)


def _ffn_fwd(l, x1, w):
    h2 = _rmsnorm_fwd(f"ffn{l}_norm", x1, _row(w["ffn_norm"][l]))
    w_up, cw, cb = w["ffn_w_up"][l], w["ffn_conv_w"][l], _row(w["ffn_conv_b"][l])
    F = w_up.shape[1] // 2
    u0, u = [], []
    for p in range(2):
        c = slice(p * F, (p + 1) * F)
        u0.append(_matmul(f"ffn{l}_up{p}", [(h2, w_up[:, c])]))
        u.append(_dwconv(f"ffn{l}_conv{p}", u0[p], cw[:, c], cb[:, c], reverse=False))
    a = _silu_gate_fwd(f"ffn{l}_gate", u[0], u[1])
    x2 = _matmul(f"ffn{l}_down", [(a, w["ffn_w_down"][l])], residual=x1)
    return x2, (x1, h2, u0, u, a)


def _ffn_bwd(l, dx2, saved, w, grads):
    x1, h2, u0, u, a = saved
    w_up, w_down, cw = w["ffn_w_up"][l], w["ffn_w_down"][l], w["ffn_conv_w"][l]
    F = w_down.shape[0]
    grads["ffn_w_down"][l] = _matmul_tn(f"ffn{l}_dwdown", a, dx2)
    da = _matmul(f"ffn{l}_da", [(dx2, w_down)], trans_b=True)
    du = _silu_gate_bwd(f"ffn{l}_dgate", u[0], u[1], da)
    zero_bias = jnp.zeros((1, F), F32)
    du0, dcw, dcb, dwup = [], [], [], []
    for p in range(2):
        c = slice(p * F, (p + 1) * F)
        du0.append(_dwconv(f"ffn{l}_dconv{p}", du[p], cw[:, c], zero_bias, reverse=True, out_dtype=BF16))
        gw, gb = _dwconv_wgrad(f"ffn{l}_dconvw{p}", u0[p], du[p], cw.shape[0])
        dcw.append(gw)
        dcb.append(gb)
        dwup.append(_matmul_tn(f"ffn{l}_dwup{p}", h2, du0[p]))
    grads["ffn_conv_w"][l] = jnp.concatenate(dcw, axis=1)
    grads["ffn_conv_b"][l] = jnp.concatenate(dcb, axis=1)[0]
    grads["ffn_w_up"][l] = jnp.concatenate(dwup, axis=1)
    dh2 = _matmul(f"ffn{l}_dh", [(du0[0], w_up[:, :F]), (du0[1], w_up[:, F:])], trans_b=True)
    dx1, dg = _rmsnorm_bwd(f"ffn{l}_dnorm", x1, _row(w["ffn_norm"][l]), dh2, dx2)
    grads["ffn_norm"][l] = dg[0]
    return dx1


def _attn_gain_rows(w, j, g):
    scale = ATTN_HEAD_DIM ** -0.5
    qg = jnp.tile(w["attn_q_gain"][j, g] * scale, ATTN_HEADS)
    kg = jnp.tile(w["attn_k_gain"][j, g], ATTN_HEADS)
    gain = jnp.concatenate([qg, kg, jnp.ones((ATTN_GW,), F32)])
    is_norm = jnp.concatenate([jnp.ones((2 * ATTN_GW,), F32), jnp.zeros((ATTN_GW,), F32)])
    return _row(gain), _row(is_norm)


def _attn_fwd(l, j, x, w):
    h = _rmsnorm_fwd(f"mix{l}_norm", x, _row(w["mixer_norm"][l]))
    w_in = w["attn_w_in"][j]
    seg = _segment_matrix(ATTN_GW, ATTN_HEAD_DIM)
    GW3 = 3 * ATTN_GW
    proj, qkv, os, ls = [], [], [], []
    for g, d in enumerate(ATTN_DILATIONS):
        gain, is_norm = _attn_gain_rows(w, j, g)
        proj.append(_matmul(f"attn{l}_in{g}", [(h, w_in[:, g * GW3:(g + 1) * GW3])]))
        qkv.append(_qknorm_fwd(f"attn{l}_qknorm{g}", proj[g], gain, is_norm, seg))
        o, lse = _attn_group_fwd(f"attn{l}_core{g}", qkv[g], d)
        os.append(o)
        ls.append(lse)
    o, lse = _attn_combine(f"attn{l}_combine", os, ls)
    x1 = _matmul(f"attn{l}_out", [(o, w["attn_w_out"][j])], residual=x)
    return x1, (x, h, proj, qkv, o, lse)


def _attn_bwd(l, j, dx1, saved, w, grads):
    x, h, proj, qkv, o, lse = saved
    w_in, w_out = w["attn_w_in"][j], w["attn_w_out"][j]
    seg = _segment_matrix(ATTN_GW, ATTN_HEAD_DIM)
    GW3 = 3 * ATTN_GW
    grads["attn_w_out"][j] = _matmul_tn(f"attn{l}_dwout", o, dx1)
    do = _matmul(f"attn{l}_do", [(dx1, w_out)], trans_b=True)
    delta = _attn_delta(f"attn{l}_delta", do, o, seg)
    dproj, dwin, dqg, dkg = [], [], [], []
    for g, d in enumerate(ATTN_DILATIONS):
        gain, is_norm = _attn_gain_rows(w, j, g)
        dqkv = _attn_group_bwd(f"attn{l}_dcore{g}", qkv[g], do, lse, delta, d)
        dp, dgain = _qknorm_bwd(f"attn{l}_dqknorm{g}", proj[g], dqkv, gain, is_norm, seg)
        dproj.append(dp)
        dwin.append(_matmul_tn(f"attn{l}_dwin{g}", h, dp))
        per_head = dgain.reshape(3, ATTN_HEADS, ATTN_HEAD_DIM).sum(axis=1)
        dqg.append(per_head[0] * ATTN_HEAD_DIM ** -0.5)
        dkg.append(per_head[1])
    grads["attn_w_in"][j] = jnp.concatenate(dwin, axis=1)
    grads["attn_q_gain"][j] = jnp.stack(dqg)
    grads["attn_k_gain"][j] = jnp.stack(dkg)
    dh = _matmul(f"attn{l}_dh", [(dproj[g], w_in[:, g * GW3:(g + 1) * GW3]) for g in range(3)], trans_b=True)
    dx, dg = _rmsnorm_bwd(f"mix{l}_dnorm", x, _row(w["mixer_norm"][l]), dh, dx1)
    grads["mixer_norm"][l] = dg[0]
    return dx


def _conv_fwd(l, j, x, w):
    h = _rmsnorm_fwd(f"mix{l}_norm", x, _row(w["mixer_norm"][l]))
    w_in, b_in = w["conv_w_in"][j], _row(w["conv_b_in"][j])
    C = w_in.shape[1] // 2
    ua = _matmul(f"conv{l}_in0", [(h, w_in[:, :C])], bias=b_in[:, :C])
    ug = _matmul(f"conv{l}_in1", [(h, w_in[:, C:])], bias=b_in[:, C:])
    glu = _glu_fwd(f"conv{l}_glu", ua, ug)
    c = _dwconv(f"conv{l}_dw", glu, w["conv_dw_w"][j], _row(w["conv_dw_b"][j]), reverse=False)
    sw = _ln_silu_fwd(f"conv{l}_ln", c, _row(w["conv_ln_g"][j]), _row(w["conv_ln_b"][j]))
    x1 = _matmul(f"conv{l}_out", [(sw, w["conv_w_out"][j])], bias=_row(w["conv_b_out"][j]), residual=x)
    return x1, (x, h, ua, ug, glu, c, sw)


def _conv_bwd(l, j, dx1, saved, w, grads):
    x, h, ua, ug, glu, c, sw = saved
    w_in, w_out, dw_w = w["conv_w_in"][j], w["conv_w_out"][j], w["conv_dw_w"][j]
    C = w_out.shape[0]
    grads["conv_b_out"][j] = _column_sums(f"conv{l}_dbout", dx1)[0]
    grads["conv_w_out"][j] = _matmul_tn(f"conv{l}_dwout", sw, dx1)
    dsw = _matmul(f"conv{l}_dsw", [(dx1, w_out)], trans_b=True)
    dc, dlg, dlb = _ln_silu_bwd(f"conv{l}_dln", c, _row(w["conv_ln_g"][j]), _row(w["conv_ln_b"][j]), dsw)
    grads["conv_ln_g"][j], grads["conv_ln_b"][j] = dlg[0], dlb[0]
    dglu = _dwconv(f"conv{l}_ddw", dc, dw_w, jnp.zeros((1, C), F32), reverse=True)
    gw, gb = _dwconv_wgrad(f"conv{l}_ddww", glu, dc, dw_w.shape[0])
    grads["conv_dw_w"][j], grads["conv_dw_b"][j] = gw, gb[0]
    da, dgate, sa, sg = _glu_bwd(f"conv{l}_dglu", ua, ug, dglu)
    grads["conv_b_in"][j] = jnp.concatenate([sa, sg], axis=1)[0]
    grads["conv_w_in"][j] = jnp.concatenate(
        [_matmul_tn(f"conv{l}_dwin0", h, da), _matmul_tn(f"conv{l}_dwin1", h, dgate)], axis=1)
    dh = _matmul(f"conv{l}_dh", [(da, w_in[:, :C]), (dgate, w_in[:, C:])], trans_b=True)
    dx, dg = _rmsnorm_bwd(f"mix{l}_dnorm", x, _row(w["mixer_norm"][l]), dh, dx1)
    grads["mixer_norm"][l] = dg[0]
    return dx


def _hgrn_fwd(l, j, x, w):
    h = _rmsnorm_fwd(f"mix{l}_norm", x, _row(w["mixer_norm"][l]))
    w_in = w["hgrn_w_in"][j]
    D = w_in.shape[1] // 4
    pq, pf, pv, pg = [_matmul(f"hgrn{l}_in{s}", [(h, w_in[:, s * D:(s + 1) * D])]) for s in range(4)]
    lb = _lower_bound_fwd(f"hgrn{l}_lb", w["hgrn_lb_logits"], l)
    o, ckpt = _hgrn_scan_fwd(f"hgrn{l}_scan", pq, pf, pv, lb)
    y = _hgrn_out_fwd(f"hgrn{l}_gate", o, pg, _row(w["hgrn_norm_g"][j]))
    x1 = _matmul(f"hgrn{l}_out", [(y, w["hgrn_w_out"][j])], residual=x)
    return x1, (x, h, pq, pf, pv, pg, lb, o, ckpt, y)


def _hgrn_bwd(l, j, dx1, saved, w, grads):
    x, h, pq, pf, pv, pg, lb, o, ckpt, y = saved
    w_in, w_out = w["hgrn_w_in"][j], w["hgrn_w_out"][j]
    D = w_out.shape[0]
    grads["hgrn_w_out"][j] = _matmul_tn(f"hgrn{l}_dwout", y, dx1)
    dy = _matmul(f"hgrn{l}_dy", [(dx1, w_out)], trans_b=True)
    do, dpg, dng = _hgrn_out_bwd(f"hgrn{l}_dgate", o, pg, _row(w["hgrn_norm_g"][j]), dy)
    grads["hgrn_norm_g"][j] = dng[0]
    dpq, dpf, dpv, dlb = _hgrn_scan_bwd(f"hgrn{l}_dscan", pq, pf, pv, lb, ckpt, do)
    grads["hgrn_lb_logits"] = grads["hgrn_lb_logits"] + _lower_bound_bwd(f"hgrn{l}_dlb", w["hgrn_lb_logits"], dlb, l)
    dps = [dpq, dpf, dpv, dpg]
    grads["hgrn_w_in"][j] = jnp.concatenate([_matmul_tn(f"hgrn{l}_dwin{s}", h, dps[s]) for s in range(4)], axis=1)
    dh = _matmul(f"hgrn{l}_dh", [(dps[s], w_in[:, s * D:(s + 1) * D]) for s in range(4)], trans_b=True)
    dx, dg = _rmsnorm_bwd(f"mix{l}_dnorm", x, _row(w["mixer_norm"][l]), dh, dx1)
    grads["mixer_norm"][l] = dg[0]
    return dx


_MIXERS = ((_attn_fwd, _attn_bwd), (_conv_fwd, _conv_bwd), (_hgrn_fwd, _hgrn_bwd))
_PER_MIXER = {"attn": 0, "conv": 1, "hgrn": 2}


def _local_step(x, target, w):
    depth = w["mixer_norm"].shape[0]
    grads = {}
    for name, v in w.items():
        lead = v.shape[0]
        grads[name] = jnp.zeros(v.shape, F32) if name == "hgrn_lb_logits" else [None] * lead
    saved = []
    for l in range(depth):
        fwd, _ = _MIXERS[l % N_MIXERS]
        x, s_mix = fwd(l, l // N_MIXERS, x, w)
        x, s_ffn = _ffn_fwd(l, x, w)
        saved.append((s_mix, s_ffn))
    dx, loss_cols = _loss_grad("loss", x, target)
    for l in reversed(range(depth)):
        _, bwd = _MIXERS[l % N_MIXERS]
        s_mix, s_ffn = saved[l]
        dx = _ffn_bwd(l, dx, s_ffn, w, grads)
        dx = bwd(l, l // N_MIXERS, dx, s_mix, w, grads)
    grads = {k: (v if k == "hgrn_lb_logits" else jnp.stack(v)) for k, v in grads.items()}
    return jnp.sum(loss_cols), dx, grads


_HBM = pl.BlockSpec(memory_space=pltpu.HBM)


def _chip_peers():
    x, y, c = lax.axis_index("x"), lax.axis_index("y"), lax.axis_index("c")
    return 2 * x + y, (x, y, c), [(1 - x, y), (x, 1 - y), (1 - x, 1 - y)]


def _exchange_chips(name, src):
    def body(src_ref, out_ref, send_sems, recv_sems, local_sem):
        p, (x, y, c), peers = _chip_peers()
        mine = pltpu.make_async_copy(src_ref.at[p], out_ref.at[p], local_sem)
        mine.start()

        def copy(k, slab_from, slab_to, peer):
            return pltpu.make_async_remote_copy(
                src_ref=src_ref.at[slab_from], dst_ref=out_ref.at[slab_to], send_sem=send_sems.at[k],
                recv_sem=recv_sems.at[k], device_id=(peer[0], peer[1], c), device_id_type=MESH)

        sends = [copy(k, 2 * px + py, p, (px, py)) for k, (px, py) in enumerate(peers)]
        for s in sends:
            s.start()
        for k, (px, py) in enumerate(peers):
            copy(k, p, 2 * px + py, (px, py)).wait_recv()
        for s in sends:
            s.wait_send()
        mine.wait()

    return pl.pallas_call(
        body, name=name, in_specs=[_HBM], out_specs=_HBM, out_shape=jax.ShapeDtypeStruct(src.shape, src.dtype),
        scratch_shapes=[pltpu.SemaphoreType.DMA((3,)), pltpu.SemaphoreType.DMA((3,)), pltpu.SemaphoreType.DMA],
    )(src)


def _all_gather_chips(name, shard):
    def body(src_ref, out_ref, send_sems, recv_sems, local_sem):
        p, (x, y, c), peers = _chip_peers()
        mine = pltpu.make_async_copy(src_ref, out_ref.at[p], local_sem)
        mine.start()

        def copy(k, slab, peer):
            return pltpu.make_async_remote_copy(
                src_ref=src_ref, dst_ref=out_ref.at[slab], send_sem=send_sems.at[k], recv_sem=recv_sems.at[k],
                device_id=(peer[0], peer[1], c), device_id_type=MESH)

        sends = [copy(k, p, peer) for k, peer in enumerate(peers)]
        for s in sends:
            s.start()
        for k, (px, py) in enumerate(peers):
            copy(k, 2 * px + py, (px, py)).wait_recv()
        for s in sends:
            s.wait_send()
        mine.wait()

    return pl.pallas_call(
        body, name=name, in_specs=[_HBM], out_specs=_HBM,
        out_shape=jax.ShapeDtypeStruct((N_CHIPS,) + shard.shape, shard.dtype),
        scratch_shapes=[pltpu.SemaphoreType.DMA((3,)), pltpu.SemaphoreType.DMA((3,)), pltpu.SemaphoreType.DMA],
    )(shard)


def _swap_cores(name, v):
    def body(v_ref, out_ref, send_sem, recv_sem):
        x, y, c = lax.axis_index("x"), lax.axis_index("y"), lax.axis_index("c")
        cp = pltpu.make_async_remote_copy(src_ref=v_ref, dst_ref=out_ref, send_sem=send_sem, recv_sem=recv_sem,
                                          device_id=(x, y, 1 - c), device_id_type=MESH)
        cp.start()
        cp.wait()

    return pl.pallas_call(
        body, name=name, in_specs=[_HBM], out_specs=_HBM, out_shape=jax.ShapeDtypeStruct(v.shape, v.dtype),
        scratch_shapes=[pltpu.SemaphoreType.DMA, pltpu.SemaphoreType.DMA],
    )(v)


_WEIGHTS = ("mixer_norm", "ffn_norm", "attn_w_in", "attn_q_gain", "attn_k_gain", "attn_w_out", "conv_w_in",
            "conv_b_in", "conv_dw_w", "conv_dw_b", "conv_ln_g", "conv_ln_b", "conv_w_out", "conv_b_out",
            "hgrn_w_in", "hgrn_lb_logits", "hgrn_norm_g", "hgrn_w_out", "ffn_w_up", "ffn_conv_w", "ffn_conv_b",
            "ffn_w_down")
_SHARD_AXIS = {"attn_w_in": 2, "attn_w_out": 2, "conv_w_in": 2, "conv_dw_w": 2, "conv_w_out": 1, "hgrn_w_in": 2,
               "hgrn_norm_g": 1, "hgrn_w_out": 1, "ffn_w_up": 2, "ffn_conv_w": 2, "ffn_w_down": 1}
_MATMUL_WEIGHTS = ("attn_w_in", "attn_w_out", "conv_w_in", "conv_w_out", "hgrn_w_in", "hgrn_w_out", "ffn_w_up",
                   "ffn_w_down")
PACK_COLS = 1024
PACK_ROWS = 16


def _pack(arrays, nlead, dtype):
    lead = arrays[0].shape[:nlead]
    flat = []
    for a in arrays:
        f = a.reshape(lead + (-1,)).astype(dtype)
        flat.append(jnp.pad(f, [(0, 0)] * nlead + [(0, (-f.shape[-1]) % PACK_COLS)]))
    buf = jnp.concatenate(flat, axis=-1)
    buf = jnp.pad(buf, [(0, 0)] * nlead + [(0, (-buf.shape[-1]) % (PACK_COLS * PACK_ROWS))])
    return buf.reshape(lead + (-1, PACK_COLS))


def _unpack(buf, shapes, nlead):
    lead = buf.shape[:nlead]
    flat = buf.reshape(lead + (-1,))
    out, off = [], 0
    for shape in shapes:
        n = 1
        for s in shape:
            n *= s
        out.append(flat[..., off:off + n].reshape(lead + tuple(shape)))
        off += n + (-n) % PACK_COLS
    return out


def _merge_shards(piece, axis):
    moved = jnp.moveaxis(piece, 0, axis)
    shape = moved.shape
    return moved.reshape(shape[:axis] + (shape[axis] * shape[axis + 1],) + shape[axis + 2:])


def _split_shards(full, axis):
    shape = full.shape
    cut = full.reshape(shape[:axis] + (N_CHIPS, shape[axis] // N_CHIPS) + shape[axis + 1:])
    return jnp.moveaxis(cut, axis, 0)


def _gather_weights(local):
    big = [n for n in _WEIGHTS if n in _MATMUL_WEIGHTS]
    small = [n for n in _WEIGHTS if n in _SHARD_AXIS and n not in _MATMUL_WEIGHTS]
    full = {n: local[n] for n in _WEIGHTS if n not in _SHARD_AXIS}
    for names, dtype, tag in ((big, BF16, "comm_gather_matmul_weights"), (small, F32, "comm_gather_small_weights")):
        gathered = _all_gather_chips(tag, _pack([local[n] for n in names], 0, dtype))
        pieces = _unpack(gathered, [local[n].shape for n in names], 1)
        for n, piece in zip(names, pieces):
            full[n] = _merge_shards(piece, _SHARD_AXIS[n])
    return full


def _reduce_gradients(grads, local):
    slabs = []
    for n in _WEIGHTS:
        g = grads[n]
        if n in _SHARD_AXIS:
            slabs.append(_split_shards(g, _SHARD_AXIS[n]))
        else:
            slabs.append(jnp.broadcast_to(g[None], (N_CHIPS,) + g.shape))
    packed = _pack(slabs, 1, F32)
    landed = _exchange_chips("comm_scatter_gradients", packed)
    partial = _sum_slabs("sum_chips", landed)
    other = _swap_cores("comm_swap_partial_sums", partial)
    total = _add("sum_cores", [partial, other])
    return dict(zip(_WEIGHTS, _unpack(total, [local[n].shape for n in _WEIGHTS], 0)))


def kernel(x, mixer_norm, ffn_norm, attn_w_in, attn_q_gain, attn_k_gain, attn_w_out, conv_w_in, conv_b_in, conv_dw_w, conv_dw_b, conv_ln_g, conv_ln_b, conv_w_out, conv_b_out, hgrn_w_in, hgrn_lb_logits, hgrn_norm_g, hgrn_w_out, ffn_w_up, ffn_conv_w, ffn_conv_b, ffn_w_down, loss_target, m_mixer_norm, m_ffn_norm, m_attn_w_in, m_attn_q_gain, m_attn_k_gain, m_attn_w_out, m_conv_w_in, m_conv_b_in, m_conv_dw_w, m_conv_dw_b, m_conv_ln_g, m_conv_ln_b, m_conv_w_out, m_conv_b_out, m_hgrn_w_in, m_hgrn_lb_logits, m_hgrn_norm_g, m_hgrn_w_out, m_ffn_w_up, m_ffn_conv_w, m_ffn_conv_b, m_ffn_w_down, v_mixer_norm, v_ffn_norm, v_attn_w_in, v_attn_q_gain, v_attn_k_gain, v_attn_w_out, v_conv_w_in, v_conv_b_in, v_conv_dw_w, v_conv_dw_b, v_conv_ln_g, v_conv_ln_b, v_conv_w_out, v_conv_b_out, v_hgrn_w_in, v_hgrn_lb_logits, v_hgrn_norm_g, v_hgrn_w_out, v_ffn_w_up, v_ffn_conv_w, v_ffn_conv_b, v_ffn_w_down):
    given = dict(locals())
    local = {n: given[n] for n in _WEIGHTS}
    full = _gather_weights(local)
    loss, dx, grads = _local_step(x[0], loss_target[0], full)
    loss = lax.psum(loss, ("x", "y", "c"))
    grad = _reduce_gradients(grads, local)
    delta, new_m, new_v = {}, {}, {}
    for n in _WEIGHTS:
        shape = local[n].shape
        as2d = lambda a: a.reshape(-1, shape[-1])
        d, m, v = _adamw(f"adamw_{n}", as2d(local[n]), as2d(grad[n]), as2d(given["m_" + n]), as2d(given["v_" + n]))
        delta[n], new_m[n], new_v[n] = d.reshape(shape), m.reshape(shape), v.reshape(shape)
    return (loss, dx[None], *[grad[n] for n in _WEIGHTS], *[delta[n] for n in _WEIGHTS],
            *[new_m[n] for n in _WEIGHTS], *[new_v[n] for n in _WEIGHTS])
```

```python
import functools

import jax
import jax.numpy as jnp
from jax import lax
from jax.experimental import pallas as pl
from jax.experimental.pallas import tpu as pltpu

F32 = jnp.float32
BF16 = jnp.bfloat16

EPS = 1e-6
N_MIXERS = 3
ATTN_DILATIONS = (1, 4, 16)
ATTN_BLOCK = 128
ATTN_HEADS = 8
ATTN_HEAD_DIM = 64
ATTN_GW = ATTN_HEADS * ATTN_HEAD_DIM
HGRN_HEAD = 128
HGRN_CHUNK = 16
HGRN_TILE = 256
ADAM_LR, ADAM_B1, ADAM_B2, ADAM_EPS, ADAM_WD, ADAM_STEP = 0.001, 0.9, 0.999, 1e-08, 0.01, 10

LANES = 128
VMEM_LIMIT = 56 * 1024 * 1024
N_CHIPS = 4
MESH = pl.DeviceIdType.MESH

HI = lax.Precision.HIGHEST


def _params(*sem):
    return pltpu.CompilerParams(dimension_semantics=sem, vmem_limit_bytes=VMEM_LIMIT)


def _pick(n, target):
    if n <= target:
        return n
    best = None
    for t in range(LANES, target + 1, LANES):
        if n % t == 0:
            best = t
    assert best is not None, (n, target)
    return best


def _pick_rows(n, target):
    if n <= target:
        return n
    for t in range(target, 15, -16):
        if n % t == 0:
            return t
    return n


def _dot(a, b, dims, precision=None):
    return lax.dot_general(a, b, (dims, ((), ())), precision=precision, preferred_element_type=F32)


def _nn(a, b, precision=None):
    return _dot(a, b, ((1,), (0,)), precision)


def _nt(a, b, precision=None):
    return _dot(a, b, ((1,), (1,)), precision)


def _tn(a, b, precision=None):
    return _dot(a, b, ((0,), (0,)), precision)


def _sigmoid(x):
    return 1.0 / (1.0 + jnp.exp(-x))


def _rowwise(name, fn, rows, pars=(), outs=(), accs=(), *, tc=None, tm=512, rb=16):
    S = rows[0][0].shape[0]
    tm = _pick_rows(S, tm)
    rb = rb if tm % rb == 0 else tm
    width = tc if tc is not None else None
    ncol = 1
    if tc is not None:
        base = outs[0][0] if outs else accs[0][1]
        ncol = base // tc
    n_r, n_p, n_o, n_a = len(rows), len(pars), len(outs), len(accs)

    def body(*refs):
        row_refs, par_refs = refs[:n_r], refs[n_r:n_r + n_p]
        out_refs, acc_refs = refs[n_r + n_p:n_r + n_p + n_o], refs[n_r + n_p + n_o:]
        if n_a:
            @pl.when(pl.program_id(1) == 0)
            def _():
                for a in acc_refs:
                    a[...] = jnp.zeros_like(a)

        def step(s, carry):
            sl = pl.ds(pl.multiple_of(s * rb, rb), rb)
            res = fn(*[r[sl, :] for r in row_refs], *[p[...] for p in par_refs])
            res = res if isinstance(res, tuple) else (res,)
            for o, v in zip(out_refs, res[:n_o]):
                o[sl, :] = v.astype(o.dtype)
            for a, v in zip(acc_refs, res[n_o:]):
                a[...] += v
            return carry

        lax.fori_loop(0, tm // rb, step, 0)

    def row_spec(c, off):
        if tc is None:
            return pl.BlockSpec((tm, c), lambda j, i: (i, 0))
        return pl.BlockSpec((tm, tc), lambda j, i, o=off // tc: (i, j + o))

    def par_spec(shape, off):
        if off is None or tc is None:
            return pl.BlockSpec(shape, lambda j, i: (0, 0))
        return pl.BlockSpec((shape[0], tc), lambda j, i, o=off // tc: (0, j + o))

    in_specs = [row_spec(a.shape[1], off) for a, off in rows]
    in_specs += [par_spec(a.shape, off) for a, off in pars]
    out_specs = [row_spec(c, 0) for c, _ in outs] + [par_spec(s, 0) for s in accs]
    out_shape = [jax.ShapeDtypeStruct((S, c), d) for c, d in outs]
    out_shape += [jax.ShapeDtypeStruct(s, F32) for s in accs]
    res = pl.pallas_call(
        body, name=name, grid=(ncol, S // tm), in_specs=in_specs, out_specs=out_specs, out_shape=out_shape,
        compiler_params=_params("parallel", "arbitrary" if n_a else "parallel"),
    )(*[a for a, _ in rows], *[a for a, _ in pars])
    return res[0] if len(res) == 1 else tuple(res)


MATMUL_VMEM = 36 * 1024 * 1024


def _matmul_tiles(M, N, pairs, out_dtype, residual):
    tm = _pick_rows(M, 512)
    for tn in sorted({_pick(N, t) for t in range(LANES, 2049, LANES)}, reverse=True):
        step = sum(tm * a.shape[1] * a.dtype.itemsize + a.shape[1] * tn * b.dtype.itemsize for a, b in pairs)
        step += tm * tn * (jnp.dtype(out_dtype).itemsize + (4 if residual is not None else 0))
        if 2 * step <= MATMUL_VMEM:
            return tm, tn
    return tm, LANES


def _matmul(name, pairs, *, trans_b=False, bias=None, residual=None, out_dtype=F32):
    M = pairs[0][0].shape[0]
    N = pairs[0][1].shape[0] if trans_b else pairs[0][1].shape[1]
    tm, tn = _matmul_tiles(M, N, pairs, out_dtype, residual)
    n = len(pairs)

    def body(*refs):
        acc = None
        for i in range(n):
            a = refs[2 * i][...].astype(BF16)
            b = refs[2 * i + 1][...].astype(BF16)
            d = _nt(a, b) if trans_b else _nn(a, b)
            acc = d if acc is None else acc + d
        k = 2 * n
        if bias is not None:
            acc = acc + refs[k][...]
            k += 1
        if residual is not None:
            acc = acc + refs[k][...]
            k += 1
        refs[k][...] = acc.astype(out_dtype)

    in_specs, args = [], []
    for a, b in pairs:
        K = a.shape[1]
        in_specs.append(pl.BlockSpec((tm, K), lambda j, i: (i, 0)))
        in_specs.append(pl.BlockSpec((tn, K), lambda j, i: (j, 0)) if trans_b
                        else pl.BlockSpec((K, tn), lambda j, i: (0, j)))
        args += [a, b]
    if bias is not None:
        in_specs.append(pl.BlockSpec((1, tn), lambda j, i: (0, j)))
        args.append(bias)
    if residual is not None:
        in_specs.append(pl.BlockSpec((tm, tn), lambda j, i: (i, j)))
        args.append(residual)
    return pl.pallas_call(
        body, name=name, grid=(N // tn, M // tm), in_specs=in_specs,
        out_specs=pl.BlockSpec((tm, tn), lambda j, i: (i, j)),
        out_shape=jax.ShapeDtypeStruct((M, N), out_dtype), compiler_params=_params("parallel", "parallel"),
    )(*args)


def _matmul_tn(name, a, b, *, tm=1408, tn=1408, tk=512):
    S, M = a.shape
    N = b.shape[1]
    tm, tn, tk = _pick(M, tm), _pick(N, tn), _pick_rows(S, tk)

    def body(a_ref, b_ref, o_ref):
        @pl.when(pl.program_id(2) == 0)
        def _():
            o_ref[...] = jnp.zeros_like(o_ref)

        o_ref[...] += _tn(a_ref[...].astype(BF16), b_ref[...].astype(BF16))

    return pl.pallas_call(
        body, name=name, grid=(M // tm, N // tn, S // tk),
        in_specs=[pl.BlockSpec((tk, tm), lambda i, j, k: (k, i)), pl.BlockSpec((tk, tn), lambda i, j, k: (k, j))],
        out_specs=pl.BlockSpec((tm, tn), lambda i, j, k: (i, j)),
        out_shape=jax.ShapeDtypeStruct((M, N), F32), compiler_params=_params("parallel", "parallel", "arbitrary"),
    )(a, b)


def _halo_rows(K):
    return 8 if K <= 9 else 32


def _dwconv(name, x, w, b, *, reverse, out_dtype=F32):
    S, C = x.shape
    K = w.shape[0]
    H = _halo_rows(K)
    tm, tc = _pick_rows(S, 512 if K <= 4 else 256), _pick(C, 1408 if K <= 4 else 256)
    nrow = S // tm
    RB = 16 if out_dtype == BF16 else 8

    def body(x_ref, h_ref, w_ref, b_ref, o_ref, ext):
        i = pl.program_id(1)
        edge = (i == nrow - 1) if reverse else (i == 0)
        halo = jnp.where(edge, 0.0, h_ref[...].astype(F32))
        if reverse:
            ext[0:tm, :] = x_ref[...].astype(F32)
            ext[tm:tm + H, :] = halo
        else:
            ext[0:H, :] = halo
            ext[H:H + tm, :] = x_ref[...].astype(F32)
        wv = w_ref[...]
        for s in range(tm // RB):
            acc = jnp.broadcast_to(b_ref[...], (RB, tc))
            for k in range(K):
                off = s * RB + ((K - 1 - k) if reverse else (H - (K - 1) + k))
                acc = acc + wv[k:k + 1, :] * ext[off:off + RB, :]
            o_ref[s * RB:(s + 1) * RB, :] = acc.astype(out_dtype)

    r = tm // H
    if reverse:
        halo_map = lambda j, i: (jnp.minimum((i + 1) * r, S // H - 1), j)
    else:
        halo_map = lambda j, i: (jnp.maximum(i * r - 1, 0), j)
    return pl.pallas_call(
        body, name=name, grid=(C // tc, nrow),
        in_specs=[pl.BlockSpec((tm, tc), lambda j, i: (i, j)), pl.BlockSpec((H, tc), halo_map),
                  pl.BlockSpec((K, tc), lambda j, i: (0, j)), pl.BlockSpec((1, tc), lambda j, i: (0, j))],
        out_specs=pl.BlockSpec((tm, tc), lambda j, i: (i, j)),
        out_shape=jax.ShapeDtypeStruct((S, C), out_dtype),
        scratch_shapes=[pltpu.VMEM((tm + H, tc), F32)], compiler_params=_params("parallel", "parallel"),
    )(x, x, w, b)


def _dwconv_wgrad(name, x, dy, K):
    S, C = x.shape
    H = _halo_rows(K)
    tm, tc = _pick_rows(S, 512 if K <= 4 else 256), _pick(C, 512 if K <= 4 else LANES)
    RB = 8

    def body(x_ref, h_ref, dy_ref, dw_ref, db_ref, ext):
        i = pl.program_id(1)

        @pl.when(i == 0)
        def _():
            dw_ref[...] = jnp.zeros_like(dw_ref)
            db_ref[...] = jnp.zeros_like(db_ref)

        ext[0:H, :] = jnp.where(i == 0, 0.0, h_ref[...].astype(F32))
        ext[H:H + tm, :] = x_ref[...].astype(F32)
        acc = [jnp.zeros((RB, tc), F32) for _ in range(K)]
        accb = jnp.zeros((RB, tc), F32)
        for s in range(tm // RB):
            d = dy_ref[s * RB:(s + 1) * RB, :].astype(F32)
            accb = accb + d
            for k in range(K):
                off = s * RB + H - (K - 1) + k
                acc[k] = acc[k] + d * ext[off:off + RB, :]
        for k in range(K):
            dw_ref[k:k + 1, :] += jnp.sum(acc[k], axis=0, keepdims=True)
        db_ref[...] += jnp.sum(accb, axis=0, keepdims=True)

    r = tm // H
    return pl.pallas_call(
        body, name=name, grid=(C // tc, S // tm),
        in_specs=[pl.BlockSpec((tm, tc), lambda j, i: (i, j)),
                  pl.BlockSpec((H, tc), lambda j, i: (jnp.maximum(i * r - 1, 0), j)),
                  pl.BlockSpec((tm, tc), lambda j, i: (i, j))],
        out_specs=[pl.BlockSpec((K, tc), lambda j, i: (0, j)), pl.BlockSpec((1, tc), lambda j, i: (0, j))],
        out_shape=[jax.ShapeDtypeStruct((K, C), F32), jax.ShapeDtypeStruct((1, C), F32)],
        scratch_shapes=[pltpu.VMEM((tm + H, tc), F32)], compiler_params=_params("parallel", "arbitrary"),
    )(x, x, dy)


def _colsum(v):
    return jnp.sum(v, axis=0, keepdims=True)


def _rmsnorm_fwd(name, x, gain):
    def fn(x, g):
        r = lax.rsqrt(jnp.mean(x * x, axis=-1, keepdims=True) + EPS)
        return x * r * g
    return _rowwise(name, fn, [(x, 0)], [(gain, None)], [(x.shape[1], BF16)])


def _rmsnorm_bwd(name, x, gain, dh, dres):
    def fn(x, dh, dres, g):
        r = lax.rsqrt(jnp.mean(x * x, axis=-1, keepdims=True) + EPS)
        xh = x * r
        dxh = dh * g
        dx = r * (dxh - xh * jnp.mean(dxh * xh, axis=-1, keepdims=True))
        return dres + dx, _colsum(dh * xh)
    D = x.shape[1]
    return _rowwise(name, fn, [(x, 0), (dh, 0), (dres, 0)], [(gain, None)], [(D, F32)], [(1, D)])


def _silu_gate_fwd(name, gate, up):
    F = gate.shape[1]
    def fn(g, up):
        return g * _sigmoid(g) * up
    return _rowwise(name, fn, [(gate, 0), (up, 0)], [], [(F, BF16)], tc=_pick(F, 512))


def _silu_gate_bwd(name, gate, up, da):
    F = gate.shape[1]
    def fn(g, up, da):
        s = _sigmoid(g)
        return da * up * (s * (1.0 + g * (1.0 - s))), da * (g * s)
    return _rowwise(name, fn, [(gate, 0), (up, 0), (da, 0)], [], [(F, F32), (F, F32)], tc=_pick(F, 512))


def _glu_fwd(name, a, gate):
    C = a.shape[1]
    def fn(a, g):
        return a * _sigmoid(g)
    return _rowwise(name, fn, [(a, 0), (gate, 0)], [], [(C, F32)], tc=_pick(C, 512))


def _glu_bwd(name, a, gate, dglu):
    C = a.shape[1]
    def fn(a, g, d):
        s = _sigmoid(g)
        da, dg = d * s, d * a * s * (1.0 - s)
        return da, dg, _colsum(da), _colsum(dg)
    return _rowwise(name, fn, [(a, 0), (gate, 0), (dglu, 0)], [], [(C, BF16), (C, BF16)], [(1, C), (1, C)],
                    tc=_pick(C, 512))


def _ln_silu_fwd(name, c, g, b):
    def fn(c, g, b):
        mu = jnp.mean(c, axis=-1, keepdims=True)
        d = c - mu
        n = d * lax.rsqrt(jnp.mean(d * d, axis=-1, keepdims=True) + EPS) * g + b
        return n * _sigmoid(n)
    return _rowwise(name, fn, [(c, 0)], [(g, None), (b, None)], [(c.shape[1], BF16)])


def _ln_silu_bwd(name, c, g, b, dsw):
    def fn(c, dsw, g, b):
        mu = jnp.mean(c, axis=-1, keepdims=True)
        d = c - mu
        r = lax.rsqrt(jnp.mean(d * d, axis=-1, keepdims=True) + EPS)
        ch = d * r
        n = ch * g + b
        s = _sigmoid(n)
        dn = dsw * (s * (1.0 + n * (1.0 - s)))
        dch = dn * g
        dc = r * (dch - jnp.mean(dch, axis=-1, keepdims=True) - ch * jnp.mean(dch * ch, axis=-1, keepdims=True))
        return dc, _colsum(dn * ch), _colsum(dn)
    C = c.shape[1]
    return _rowwise(name, fn, [(c, 0), (dsw, 0)], [(g, None), (b, None)], [(C, F32)], [(1, C), (1, C)])


def _column_sums(name, x):
    return _rowwise(name, lambda x: (_colsum(x),), [(x, 0)], [], [], [(1, x.shape[1])])


def _loss_grad(name, y, target):
    D = y.shape[1]
    def fn(y, t):
        e = y - t
        return e * (1.0 / D), _colsum(e * e) * (0.5 / D)
    return _rowwise(name, fn, [(y, 0), (target, 0)], [], [(D, F32)], [(1, D)])


def _add(name, arrays):
    def fn(*xs):
        acc = xs[0]
        for x in xs[1:]:
            acc = acc + x
        return acc
    return _rowwise(name, fn, [(a, 0) for a in arrays], [], [(arrays[0].shape[1], F32)], tm=256)


def _sum_slabs(name, stacked):
    n, R, C = stacked.shape
    tm = _pick_rows(R, 256)

    def body(*refs):
        acc = refs[0][0]
        for r in refs[1:n]:
            acc = acc + r[0]
        refs[n][...] = acc

    return pl.pallas_call(
        body, name=name, grid=(R // tm,),
        in_specs=[pl.BlockSpec((1, tm, C), lambda i, q=q: (q, i, 0)) for q in range(n)],
        out_specs=pl.BlockSpec((tm, C), lambda i: (i, 0)), out_shape=jax.ShapeDtypeStruct((R, C), F32),
        compiler_params=_params("parallel"),
    )(*[stacked] * n)


def _adamw(name, w, g, m, v):
    c1 = 1.0 - ADAM_B1 ** ADAM_STEP
    c2 = 1.0 - ADAM_B2 ** ADAM_STEP
    def fn(w, g, m, v):
        m = ADAM_B1 * m + (1.0 - ADAM_B1) * g
        v = ADAM_B2 * v + (1.0 - ADAM_B2) * (g * g)
        delta = -ADAM_LR * ((m / c1) / (jnp.sqrt(v / c2) + ADAM_EPS) + ADAM_WD * w)
        return delta, m, v
    C = w.shape[1]
    return _rowwise(name, fn, [(w, 0), (g, 0), (m, 0), (v, 0)], [], [(C, F32)] * 3, tm=256)


SEG_ROWS = 128


def _segment_matrix(n, seg):
    i = jnp.arange(n) // seg
    return (i[:, None] == i[None, :]).astype(BF16)


def _seg_sum(v, B):
    hi = v.astype(BF16)
    lo = (v - hi.astype(F32)).astype(BF16)
    return _nn(hi, B) + _nn(lo, B)


def _qknorm_fwd(name, proj, gain_full, is_norm, seg):
    def fn(x, gf, isn, B):
        ms = _seg_sum(x * x, B) * (1.0 / ATTN_HEAD_DIM)
        r = lax.rsqrt(ms + EPS)
        return x * (isn * r + (1.0 - isn)) * gf
    W = proj.shape[1]
    return _rowwise(name, fn, [(proj, 0)], [(gain_full, 0), (is_norm, 0), (seg, None)], [(W, BF16)],
                    tc=ATTN_GW, rb=SEG_ROWS)


def _qknorm_bwd(name, proj, dy, gain_full, is_norm, seg):
    def fn(x, dy, gf, isn, B):
        ms = _seg_sum(x * x, B) * (1.0 / ATTN_HEAD_DIM)
        r = lax.rsqrt(ms + EPS)
        xh = x * r
        dxh = dy * gf
        dn = r * (dxh - xh * (_seg_sum(dxh * xh, B) * (1.0 / ATTN_HEAD_DIM)))
        return isn * dn + (1.0 - isn) * dxh, _colsum(dy * xh)
    W = proj.shape[1]
    return _rowwise(name, fn, [(proj, 0), (dy, 0)], [(gain_full, 0), (is_norm, 0), (seg, None)],
                    [(W, BF16)], [(1, W)], tc=ATTN_GW, rb=SEG_ROWS)


def _attn_masks():
    shape = (ATTN_BLOCK, ATTN_BLOCK)
    row = lax.broadcasted_iota(jnp.int32, shape, 0)
    col = lax.broadcasted_iota(jnp.int32, shape, 1)
    low_lanes = col < ATTN_HEAD_DIM
    return col <= row, col >= row, low_lanes


def _attn_group_fwd(name, qkv, d):
    S = qkv.shape[0]
    n, W, G = S // d, 3 * ATTN_GW, ATTN_GW
    nb = n // ATTN_BLOCK
    view = qkv.reshape(n, d * W)

    def body(cur, prev, o_ref, l_ref):
        b = pl.program_id(1)
        cur_mask, prev_mask, low = _attn_masks()
        prev_mask = jnp.logical_and(prev_mask, b > 0)
        for pr in range(G // LANES):
            c0 = pr * LANES
            q2, kc, vc = cur[:, c0:c0 + LANES], cur[:, G + c0:G + c0 + LANES], cur[:, 2 * G + c0:2 * G + c0 + LANES]
            kp, vp = prev[:, G + c0:G + c0 + LANES], prev[:, 2 * G + c0:2 * G + c0 + LANES]
            res = []
            for hm in (low, jnp.logical_not(low)):
                qm = jnp.where(hm, q2, jnp.zeros_like(q2))
                sc = jnp.where(cur_mask, _nt(qm, kc), -jnp.inf)
                sp = jnp.where(prev_mask, _nt(qm, kp), -jnp.inf)
                m = jnp.maximum(jnp.max(sc, axis=1, keepdims=True), jnp.max(sp, axis=1, keepdims=True))
                pc, pp = jnp.exp(sc - m), jnp.exp(sp - m)
                l = jnp.sum(pc, axis=1, keepdims=True) + jnp.sum(pp, axis=1, keepdims=True)
                o = (_nn(pc.astype(BF16), vc) + _nn(pp.astype(BF16), vp)) / l
                res.append((o, jnp.broadcast_to(m + jnp.log(l), o.shape)))
            o_ref[:, c0:c0 + LANES] = jnp.where(low, res[0][0], res[1][0])
            l_ref[:, c0:c0 + LANES] = jnp.where(low, res[0][1], res[1][1])

    o, l = pl.pallas_call(
        body, name=name, grid=(d, nb),
        in_specs=[pl.BlockSpec((ATTN_BLOCK, W), lambda r, b: (b, r)),
                  pl.BlockSpec((ATTN_BLOCK, W), lambda r, b: (jnp.maximum(b - 1, 0), r))],
        out_specs=[pl.BlockSpec((ATTN_BLOCK, G), lambda r, b: (b, r))] * 2,
        out_shape=[jax.ShapeDtypeStruct((n, d * G), F32)] * 2, compiler_params=_params("parallel", "parallel"),
    )(view, view)
    return o.reshape(S, G), l.reshape(S, G)


def _attn_combine(name, os, ls):
    def fn(o1, o2, o3, l1, l2, l3):
        m = jnp.maximum(jnp.maximum(l1, l2), l3)
        e1, e2, e3 = jnp.exp(l1 - m), jnp.exp(l2 - m), jnp.exp(l3 - m)
        den = e1 + e2 + e3
        return (e1 * o1 + e2 * o2 + e3 * o3) / den, m + jnp.log(den)
    G = os[0].shape[1]
    return _rowwise(name, fn, [(a, 0) for a in (*os, *ls)], [], [(G, F32), (G, F32)])


def _attn_delta(name, do, o, seg):
    def fn(do, o, B):
        return _seg_sum(do * o, B)
    return _rowwise(name, fn, [(do, 0), (o, 0)], [(seg, None)], [(o.shape[1], F32)], rb=SEG_ROWS)


def _attn_group_bwd(name, qkv, do, lse, delta, d):
    S = qkv.shape[0]
    n, W, G = S // d, 3 * ATTN_GW, ATTN_GW
    nb = n // ATTN_BLOCK

    def body(qp, qc, qn, do_c, do_n, l_c, l_n, dl_c, dl_n, out):
        j = pl.program_id(1)
        cur_mask, prev_mask, low = _attn_masks()
        next_mask = jnp.logical_and(prev_mask, j < nb - 1)
        prev_mask = jnp.logical_and(prev_mask, j > 0)
        for pr in range(G // LANES):
            c0 = pr * LANES
            q_c, k_c, v_c = qc[:, c0:c0 + LANES], qc[:, G + c0:G + c0 + LANES], qc[:, 2 * G + c0:2 * G + c0 + LANES]
            q_n = qn[:, c0:c0 + LANES]
            k_p, v_p = qp[:, G + c0:G + c0 + LANES], qp[:, 2 * G + c0:2 * G + c0 + LANES]
            d_c, d_n = do_c[:, c0:c0 + LANES].astype(BF16), do_n[:, c0:c0 + LANES].astype(BF16)
            res = []
            for hh, hm in enumerate((low, jnp.logical_not(low))):
                h0 = c0 + hh * ATTN_HEAD_DIM
                lc, ln = l_c[:, h0:h0 + 1], l_n[:, h0:h0 + 1]
                dlc, dln = dl_c[:, h0:h0 + 1], dl_n[:, h0:h0 + 1]
                zero = jnp.zeros_like(q_c)
                qmc, qmn = jnp.where(hm, q_c, zero), jnp.where(hm, q_n, zero)
                dmc, dmn = jnp.where(hm, d_c, zero), jnp.where(hm, d_n, zero)
                p_a = jnp.where(cur_mask, jnp.exp(_nt(qmc, k_c) - lc), 0.0)
                ds_a = p_a * (_nt(dmc, v_c) - dlc)
                p_b = jnp.where(next_mask, jnp.exp(_nt(qmn, k_c) - ln), 0.0)
                ds_b = p_b * (_nt(dmn, v_c) - dln)
                p_c = jnp.where(prev_mask, jnp.exp(_nt(qmc, k_p) - lc), 0.0)
                ds_c = p_c * (_nt(dmc, v_p) - dlc)
                dq = _nn(ds_a.astype(BF16), k_c) + _nn(ds_c.astype(BF16), k_p)
                dk = _tn(ds_a.astype(BF16), q_c) + _tn(ds_b.astype(BF16), q_n)
                dv = _tn(p_a.astype(BF16), d_c) + _tn(p_b.astype(BF16), d_n)
                res.append((dq, dk, dv))
            for t in range(3):
                out[:, t * G + c0:t * G + c0 + LANES] = jnp.where(low, res[0][t], res[1][t])

    prv = lambda r, j: (jnp.maximum(j - 1, 0), r)
    cur = lambda r, j: (j, r)
    nxt = lambda r, j: (jnp.minimum(j + 1, nb - 1), r)
    wide = lambda m: pl.BlockSpec((ATTN_BLOCK, W), m)
    narrow = lambda m: pl.BlockSpec((ATTN_BLOCK, G), m)
    qv, dv, lv, tv = qkv.reshape(n, d * W), do.reshape(n, d * G), lse.reshape(n, d * G), delta.reshape(n, d * G)
    out = pl.pallas_call(
        body, name=name, grid=(d, nb),
        in_specs=[wide(prv), wide(cur), wide(nxt), narrow(cur), narrow(nxt), narrow(cur), narrow(nxt),
                  narrow(cur), narrow(nxt)],
        out_specs=wide(cur), out_shape=jax.ShapeDtypeStruct((n, d * W), F32),
        compiler_params=_params("parallel", "parallel"),
    )(qv, qv, qv, dv, dv, lv, lv, tv, tv)
    return out.reshape(S, W)


def _chunk_triangle(T, upper):
    i = jnp.arange(T)
    same = (i[:, None] // HGRN_CHUNK) == (i[None, :] // HGRN_CHUNK)
    tri = (i[None, :] >= i[:, None]) if upper else (i[None, :] <= i[:, None])
    return jnp.logical_and(same, tri).astype(F32)


def _hgrn_prologue(qr, fr, lbv, q_s, k_s, b_s, tri_ref, T):
    def pro(s, c):
        sl = pl.ds(pl.multiple_of(s * HGRN_CHUNK, HGRN_CHUNK), HGRN_CHUNK)
        sg = _sigmoid(fr[sl, :])
        qv = qr[sl, :]
        q_s[sl, :] = qv * _sigmoid(qv)
        k_s[sl, :] = (1.0 - lbv) * (1.0 - sg)
        b_s[sl, :] = jnp.log(lbv + (1.0 - lbv) * sg)
        return c
    lax.fori_loop(0, T // HGRN_CHUNK, pro, 0)
    b_s[...] = _nn(tri_ref[...], b_s[...], HI)


def _hgrn_scan_fwd(name, pq, pf, pv, lb):
    S, D = pq.shape
    T = _pick_rows(S, HGRN_TILE)
    NH, NT, C, HD = D // HGRN_HEAD, S // T, HGRN_CHUNK, HGRN_HEAD
    tri = _chunk_triangle(T, upper=False)

    def body(qr, fr, iv, lb_ref, tri_ref, o_ref, ck_ref, st_ref, q_s, k_s, b_s):
        @pl.when(pl.program_id(1) == 0)
        def _():
            st_ref[...] = jnp.zeros_like(st_ref)

        ck_ref[...] = st_ref[...]
        _hgrn_prologue(qr, fr, lb_ref[...], q_s, k_s, b_s, tri_ref, T)
        row = lax.broadcasted_iota(jnp.int32, (C, 1), 0)

        def chunk(c, carry):
            sl = pl.ds(pl.multiple_of(c * C, C), C)
            q, k, b, v = q_s[sl, :], k_s[sl, :], b_s[sl, :], iv[sl, :]
            b_last = b[C - 1:C, :]
            st = st_ref[...]
            o = _nt((q * jnp.exp(b)).astype(BF16), st.astype(BF16))
            for s in range(C):
                e = jnp.exp(jnp.minimum(b - b[s:s + 1, :], 0.0))
                a = jnp.sum(q * e * k[s:s + 1, :], axis=1, keepdims=True)
                o = o + jnp.where(row >= s, a, 0.0) * v[s:s + 1, :]
            o_ref[sl, :] = o
            kd = k * jnp.exp(b_last - b)
            st_ref[...] = st * jnp.exp(b_last) + _tn(v.astype(BF16), kd.astype(BF16))
            return carry

        lax.fori_loop(0, T // C, chunk, 0)

    col = lambda off: pl.BlockSpec((T, HD), lambda h, t, o=off: (t, h + o))
    return pl.pallas_call(
        body, name=name, grid=(NH, NT),
        in_specs=[col(0), col(0), col(0), pl.BlockSpec((1, HD), lambda h, t: (0, h)),
                  pl.BlockSpec((T, T), lambda h, t: (0, 0))],
        out_specs=[col(0), pl.BlockSpec((HD, HD), lambda h, t: (t * NH + h, 0))],
        out_shape=[jax.ShapeDtypeStruct((S, D), F32), jax.ShapeDtypeStruct((NT * NH * HD, HD), F32)],
        scratch_shapes=[pltpu.VMEM((HD, HD), F32)] + [pltpu.VMEM((T, HD), F32)] * 3,
        compiler_params=_params("parallel", "arbitrary"),
    )(pq, pf, pv, lb, tri)


def _hgrn_scan_bwd(name, pq, pf, pv, lb, ckpt, do):
    S, D = pq.shape
    T = _pick_rows(S, HGRN_TILE)
    NH, NT, C, HD = D // HGRN_HEAD, S // T, HGRN_CHUNK, HGRN_HEAD
    NC = T // C
    tri, tri_up = _chunk_triangle(T, upper=False), _chunk_triangle(T, upper=True)

    def body(qr, fr, iv, do_ref, ck_ref, lb_ref, tri_ref, triu_ref, dq_ref, df_ref, dv_ref, dlb_ref,
             dst_ref, save, q_s, k_s, b_s, dq_s, dk_s, db_s):
        @pl.when(pl.program_id(1) == 0)
        def _():
            dst_ref[...] = jnp.zeros_like(dst_ref)
            dlb_ref[...] = jnp.zeros_like(dlb_ref)

        lbv = lb_ref[...]
        _hgrn_prologue(qr, fr, lbv, q_s, k_s, b_s, tri_ref, T)
        row = lax.broadcasted_iota(jnp.int32, (C, 1), 0)

        def replay(c, st):
            sl = pl.ds(pl.multiple_of(c * C, C), C)
            save[pl.ds(pl.multiple_of(c * HD, HD), HD), :] = st
            k, b, v = k_s[sl, :], b_s[sl, :], iv[sl, :]
            b_last = b[C - 1:C, :]
            kd = k * jnp.exp(b_last - b)
            return st * jnp.exp(b_last) + _tn(v.astype(BF16), kd.astype(BF16))

        lax.fori_loop(0, NC, replay, ck_ref[...])

        def chunk(ci, carry):
            c = NC - 1 - ci
            sl = pl.ds(pl.multiple_of(c * C, C), C)
            q, k, b, v, g = q_s[sl, :], k_s[sl, :], b_s[sl, :], iv[sl, :], do_ref[sl, :]
            st0 = save[pl.ds(pl.multiple_of(c * HD, HD), HD), :]
            dst1 = dst_ref[...]
            b_last = b[C - 1:C, :]
            eb, ebl, ek = jnp.exp(b), jnp.exp(b_last), jnp.exp(b_last - b)
            dst1_b = dst1.astype(BF16)
            dq = _nn(g.astype(BF16), st0.astype(BF16)) * eb
            dv = _nt((k * ek).astype(BF16), dst1_b)
            dk = _nn(v.astype(BF16), dst1_b) * ek
            db_last = _colsum(dk * k) + _colsum(dst1 * st0) * ebl
            for s in range(C):
                e = jnp.where(row >= s, jnp.exp(jnp.minimum(b - b[s:s + 1, :], 0.0)), 0.0)
                ks, vs = k[s:s + 1, :], v[s:s + 1, :]
                da = jnp.sum(g * vs, axis=1, keepdims=True)
                a = jnp.sum(q * e * ks, axis=1, keepdims=True)
                dq = dq + da * e * ks
                dk = dk + jnp.where(row == s, _colsum(da * q * e), 0.0)
                dv = dv + jnp.where(row == s, _colsum(a * g), 0.0)
            dq_s[sl, :] = dq
            dk_s[sl, :] = dk
            db_s[sl, :] = q * dq - k * dk + jnp.where(row == C - 1, db_last, 0.0)
            dv_ref[sl, :] = dv.astype(BF16)
            dst_ref[...] = dst1 * ebl + _tn(g.astype(BF16), (q * eb).astype(BF16))
            return carry

        lax.fori_loop(0, NC, chunk, 0)
        db_s[...] = _nn(triu_ref[...], db_s[...], HI)

        def epi(s, carry):
            sl = pl.ds(pl.multiple_of(s * C, C), C)
            qv = qr[sl, :]
            sq = _sigmoid(qv)
            dq_ref[sl, :] = (dq_s[sl, :] * sq * (1.0 + qv * (1.0 - sq))).astype(BF16)
            sg = _sigmoid(fr[sl, :])
            common = db_s[sl, :] / (lbv + (1.0 - lbv) * sg) - dk_s[sl, :]
            df_ref[sl, :] = (common * (1.0 - lbv) * sg * (1.0 - sg)).astype(BF16)
            dlb_ref[...] += _colsum(common * (1.0 - sg))
            return carry

        lax.fori_loop(0, NC, epi, 0)

    rev = lambda h, t: (NT - 1 - t, h)
    col = pl.BlockSpec((T, HD), rev)
    dq, df, dv, dlb = pl.pallas_call(
        body, name=name, grid=(NH, NT),
        in_specs=[col, col, col, col, pl.BlockSpec((HD, HD), lambda h, t: ((NT - 1 - t) * NH + h, 0)),
                  pl.BlockSpec((1, HD), lambda h, t: (0, h)),
                  pl.BlockSpec((T, T), lambda h, t: (0, 0)), pl.BlockSpec((T, T), lambda h, t: (0, 0))],
        out_specs=[col, col, col, pl.BlockSpec((1, HD), lambda h, t: (0, h))],
        out_shape=[jax.ShapeDtypeStruct((S, D), BF16)] * 3 + [jax.ShapeDtypeStruct((1, D), F32)],
        scratch_shapes=[pltpu.VMEM((HD, HD), F32), pltpu.VMEM((NC * HD, HD), F32)] + [pltpu.VMEM((T, HD), F32)] * 6,
        compiler_params=_params("parallel", "arbitrary"),
    )(pq, pf, pv, do, ckpt, lb, tri, tri_up)
    return dq, df, dv, dlb


def _hgrn_out_fwd(name, o, gate, norm_g):
    def fn(o, g, ng):
        parts = []
        for h in range(o.shape[1] // HGRN_HEAD):
            c = slice(h * HGRN_HEAD, (h + 1) * HGRN_HEAD)
            oh, gh = o[:, c], g[:, c]
            r = lax.rsqrt(jnp.mean(oh * oh, axis=-1, keepdims=True) + EPS)
            parts.append(oh * r * ng[:, c] * (gh * _sigmoid(gh)))
        return jnp.concatenate(parts, axis=1)
    return _rowwise(name, fn, [(o, 0), (gate, 0)], [(norm_g, None)], [(o.shape[1], BF16)])


def _hgrn_out_bwd(name, o, gate, norm_g, dy):
    def fn(o, g, dy, ng):
        dos, dgs, dngs = [], [], []
        for h in range(o.shape[1] // HGRN_HEAD):
            c = slice(h * HGRN_HEAD, (h + 1) * HGRN_HEAD)
            oh, gh, dyh, ngh = o[:, c], g[:, c], dy[:, c], ng[:, c]
            r = lax.rsqrt(jnp.mean(oh * oh, axis=-1, keepdims=True) + EPS)
            xh = oh * r
            s = _sigmoid(gh)
            dn = dyh * (gh * s)
            dxh = dn * ngh
            dos.append(r * (dxh - xh * jnp.mean(dxh * xh, axis=-1, keepdims=True)))
            dgs.append(dyh * xh * ngh * (s * (1.0 + gh * (1.0 - s))))
            dngs.append(_colsum(dn * xh))
        return jnp.concatenate(dos, axis=1), jnp.concatenate(dgs, axis=1), jnp.concatenate(dngs, axis=1)
    D = o.shape[1]
    return _rowwise(name, fn, [(o, 0), (gate, 0), (dy, 0)], [(norm_g, None)], [(D, F32), (D, BF16)], [(1, D)])


def _lower_bound_fwd(name, logits, layer):
    n = logits.shape[0]

    def body(x_ref, o_ref):
        rows = [x_ref[i:i + 1, :] for i in range(n)]
        m = functools.reduce(jnp.maximum, rows)
        e = [jnp.exp(r - m) for r in rows]
        den = functools.reduce(jnp.add, e)
        o_ref[...] = functools.reduce(jnp.add, e[1:layer + 1]) / den

    return pl.pallas_call(body, name=name, out_shape=jax.ShapeDtypeStruct((1, logits.shape[1]), F32))(logits)


def _lower_bound_bwd(name, logits, dlb, layer):
    n = logits.shape[0]

    def body(x_ref, d_ref, o_ref):
        rows = [x_ref[i:i + 1, :] for i in range(n)]
        m = functools.reduce(jnp.maximum, rows)
        e = [jnp.exp(r - m) for r in rows]
        den = functools.reduce(jnp.add, e)
        s = [v / den for v in e]
        d = d_ref[...]
        inner = functools.reduce(jnp.add, s[1:layer + 1]) * d
        for i in range(n):
            o_ref[i:i + 1, :] = s[i] * ((d if 1 <= i <= layer else 0.0) - inner)

    return pl.pallas_call(body, name=name, out_shape=jax.ShapeDtypeStruct(logits.shape, F32))(logits, dlb)


def _row(v):
    return v.reshape(1, -1)


def _ffn_fwd(l, x1, w):
    h2 = _rmsnorm_fwd(f"ffn{l}_norm", x1, _row(w["ffn_norm"][l]))
    w_up, cw, cb = w["ffn_w_up"][l], w["ffn_conv_w"][l], _row(w["ffn_conv_b"][l])
    F = w_up.shape[1] // 2
    u0, u = [], []
    for p in range(2):
        c = slice(p * F, (p + 1) * F)
        u0.append(_matmul(f"ffn{l}_up{p}", [(h2, w_up[:, c])]))
        u.append(_dwconv(f"ffn{l}_conv{p}", u0[p], cw[:, c], cb[:, c], reverse=False))
    a = _silu_gate_fwd(f"ffn{l}_gate", u[0], u[1])
    x2 = _matmul(f"ffn{l}_down", [(a, w["ffn_w_down"][l])], residual=x1)
    return x2, (x1, h2, u0, u, a)


def _ffn_bwd(l, dx2, saved, w, grads):
    x1, h2, u0, u, a = saved
    w_up, w_down, cw = w["ffn_w_up"][l], w["ffn_w_down"][l], w["ffn_conv_w"][l]
    F = w_down.shape[0]
    grads["ffn_w_down"][l] = _matmul_tn(f"ffn{l}_dwdown", a, dx2)
    da = _matmul(f"ffn{l}_da", [(dx2, w_down)], trans_b=True)
    du = _silu_gate_bwd(f"ffn{l}_dgate", u[0], u[1], da)
    zero_bias = jnp.zeros((1, F), F32)
    du0, dcw, dcb, dwup = [], [], [], []
    for p in range(2):
        c = slice(p * F, (p + 1) * F)
        du0.append(_dwconv(f"ffn{l}_dconv{p}", du[p], cw[:, c], zero_bias, reverse=True, out_dtype=BF16))
        gw, gb = _dwconv_wgrad(f"ffn{l}_dconvw{p}", u0[p], du[p], cw.shape[0])
        dcw.append(gw)
        dcb.append(gb)
        dwup.append(_matmul_tn(f"ffn{l}_dwup{p}", h2, du0[p]))
    grads["ffn_conv_w"][l] = jnp.concatenate(dcw, axis=1)
    grads["ffn_conv_b"][l] = jnp.concatenate(dcb, axis=1)[0]
    grads["ffn_w_up"][l] = jnp.concatenate(dwup, axis=1)
    dh2 = _matmul(f"ffn{l}_dh", [(du0[0], w_up[:, :F]), (du0[1], w_up[:, F:])], trans_b=True)
    dx1, dg = _rmsnorm_bwd(f"ffn{l}_dnorm", x1, _row(w["ffn_norm"][l]), dh2, dx2)
    grads["ffn_norm"][l] = dg[0]
    return dx1


def _attn_gain_rows(w, j, g):
    scale = ATTN_HEAD_DIM ** -0.5
    qg = jnp.tile(w["attn_q_gain"][j, g] * scale, ATTN_HEADS)
    kg = jnp.tile(w["attn_k_gain"][j, g], ATTN_HEADS)
    gain = jnp.concatenate([qg, kg, jnp.ones((ATTN_GW,), F32)])
    is_norm = jnp.concatenate([jnp.ones((2 * ATTN_GW,), F32), jnp.zeros((ATTN_GW,), F32)])
    return _row(gain), _row(is_norm)


def _attn_fwd(l, j, x, w):
    h = _rmsnorm_fwd(f"mix{l}_norm", x, _row(w["mixer_norm"][l]))
    w_in = w["attn_w_in"][j]
    seg = _segment_matrix(ATTN_GW, ATTN_HEAD_DIM)
    GW3 = 3 * ATTN_GW
    proj, qkv, os, ls = [], [], [], []
    for g, d in enumerate(ATTN_DILATIONS):
        gain, is_norm = _attn_gain_rows(w, j, g)
        proj.append(_matmul(f"attn{l}_in{g}", [(h, w_in[:, g * GW3:(g + 1) * GW3])]))
        qkv.append(_qknorm_fwd(f"attn{l}_qknorm{g}", proj[g], gain, is_norm, seg))
        o, lse = _attn_group_fwd(f"attn{l}_core{g}", qkv[g], d)
        os.append(o)
        ls.append(lse)
    o, lse = _attn_combine(f"attn{l}_combine", os, ls)
    x1 = _matmul(f"attn{l}_out", [(o, w["attn_w_out"][j])], residual=x)
    return x1, (x, h, proj, qkv, o, lse)


def _attn_bwd(l, j, dx1, saved, w, grads):
    x, h, proj, qkv, o, lse = saved
    w_in, w_out = w["attn_w_in"][j], w["attn_w_out"][j]
    seg = _segment_matrix(ATTN_GW, ATTN_HEAD_DIM)
    GW3 = 3 * ATTN_GW
    grads["attn_w_out"][j] = _matmul_tn(f"attn{l}_dwout", o, dx1)
    do = _matmul(f"attn{l}_do", [(dx1, w_out)], trans_b=True)
    delta = _attn_delta(f"attn{l}_delta", do, o, seg)
    dproj, dwin, dqg, dkg = [], [], [], []
    for g, d in enumerate(ATTN_DILATIONS):
        gain, is_norm = _attn_gain_rows(w, j, g)
        dqkv = _attn_group_bwd(f"attn{l}_dcore{g}", qkv[g], do, lse, delta, d)
        dp, dgain = _qknorm_bwd(f"attn{l}_dqknorm{g}", proj[g], dqkv, gain, is_norm, seg)
        dproj.append(dp)
        dwin.append(_matmul_tn(f"attn{l}_dwin{g}", h, dp))
        per_head = dgain.reshape(3, ATTN_HEADS, ATTN_HEAD_DIM).sum(axis=1)
        dqg.append(per_head[0] * ATTN_HEAD_DIM ** -0.5)
        dkg.append(per_head[1])
    grads["attn_w_in"][j] = jnp.concatenate(dwin, axis=1)
    grads["attn_q_gain"][j] = jnp.stack(dqg)
    grads["attn_k_gain"][j] = jnp.stack(dkg)
    dh = _matmul(f"attn{l}_dh", [(dproj[g], w_in[:, g * GW3:(g + 1) * GW3]) for g in range(3)], trans_b=True)
    dx, dg = _rmsnorm_bwd(f"mix{l}_dnorm", x, _row(w["mixer_norm"][l]), dh, dx1)
    grads["mixer_norm"][l] = dg[0]
    return dx


def _conv_fwd(l, j, x, w):
    h = _rmsnorm_fwd(f"mix{l}_norm", x, _row(w["mixer_norm"][l]))
    w_in, b_in = w["conv_w_in"][j], _row(w["conv_b_in"][j])
    C = w_in.shape[1] // 2
    ua = _matmul(f"conv{l}_in0", [(h, w_in[:, :C])], bias=b_in[:, :C])
    ug = _matmul(f"conv{l}_in1", [(h, w_in[:, C:])], bias=b_in[:, C:])
    glu = _glu_fwd(f"conv{l}_glu", ua, ug)
    c = _dwconv(f"conv{l}_dw", glu, w["conv_dw_w"][j], _row(w["conv_dw_b"][j]), reverse=False)
    sw = _ln_silu_fwd(f"conv{l}_ln", c, _row(w["conv_ln_g"][j]), _row(w["conv_ln_b"][j]))
    x1 = _matmul(f"conv{l}_out", [(sw, w["conv_w_out"][j])], bias=_row(w["conv_b_out"][j]), residual=x)
    return x1, (x, h, ua, ug, glu, c, sw)


def _conv_bwd(l, j, dx1, saved, w, grads):
    x, h, ua, ug, glu, c, sw = saved
    w_in, w_out, dw_w = w["conv_w_in"][j], w["conv_w_out"][j], w["conv_dw_w"][j]
    C = w_out.shape[0]
    grads["conv_b_out"][j] = _column_sums(f"conv{l}_dbout", dx1)[0]
    grads["conv_w_out"][j] = _matmul_tn(f"conv{l}_dwout", sw, dx1)
    dsw = _matmul(f"conv{l}_dsw", [(dx1, w_out)], trans_b=True)
    dc, dlg, dlb = _ln_silu_bwd(f"conv{l}_dln", c, _row(w["conv_ln_g"][j]), _row(w["conv_ln_b"][j]), dsw)
    grads["conv_ln_g"][j], grads["conv_ln_b"][j] = dlg[0], dlb[0]
    dglu = _dwconv(f"conv{l}_ddw", dc, dw_w, jnp.zeros((1, C), F32), reverse=True)
    gw, gb = _dwconv_wgrad(f"conv{l}_ddww", glu, dc, dw_w.shape[0])
    grads["conv_dw_w"][j], grads["conv_dw_b"][j] = gw, gb[0]
    da, dgate, sa, sg = _glu_bwd(f"conv{l}_dglu", ua, ug, dglu)
    grads["conv_b_in"][j] = jnp.concatenate([sa, sg], axis=1)[0]
    grads["conv_w_in"][j] = jnp.concatenate(
        [_matmul_tn(f"conv{l}_dwin0", h, da), _matmul_tn(f"conv{l}_dwin1", h, dgate)], axis=1)
    dh = _matmul(f"conv{l}_dh", [(da, w_in[:, :C]), (dgate, w_in[:, C:])], trans_b=True)
    dx, dg = _rmsnorm_bwd(f"mix{l}_dnorm", x, _row(w["mixer_norm"][l]), dh, dx1)
    grads["mixer_norm"][l] = dg[0]
    return dx


def _hgrn_fwd(l, j, x, w):
    h = _rmsnorm_fwd(f"mix{l}_norm", x, _row(w["mixer_norm"][l]))
    w_in = w["hgrn_w_in"][j]
    D = w_in.shape[1] // 4
    pq, pf, pv, pg = [_matmul(f"hgrn{l}_in{s}", [(h, w_in[:, s * D:(s + 1) * D])]) for s in range(4)]
    lb = _lower_bound_fwd(f"hgrn{l}_lb", w["hgrn_lb_logits"], l)
    o, ckpt = _hgrn_scan_fwd(f"hgrn{l}_scan", pq, pf, pv, lb)
    y = _hgrn_out_fwd(f"hgrn{l}_gate", o, pg, _row(w["hgrn_norm_g"][j]))
    x1 = _matmul(f"hgrn{l}_out", [(y, w["hgrn_w_out"][j])], residual=x)
    return x1, (x, h, pq, pf, pv, pg, lb, o, ckpt, y)


def _hgrn_bwd(l, j, dx1, saved, w, grads):
    x, h, pq, pf, pv, pg, lb, o, ckpt, y = saved
    w_in, w_out = w["hgrn_w_in"][j], w["hgrn_w_out"][j]
    D = w_out.shape[0]
    grads["hgrn_w_out"][j] = _matmul_tn(f"hgrn{l}_dwout", y, dx1)
    dy = _matmul(f"hgrn{l}_dy", [(dx1, w_out)], trans_b=True)
    do, dpg, dng = _hgrn_out_bwd(f"hgrn{l}_dgate", o, pg, _row(w["hgrn_norm_g"][j]), dy)
    grads["hgrn_norm_g"][j] = dng[0]
    dpq, dpf, dpv, dlb = _hgrn_scan_bwd(f"hgrn{l}_dscan", pq, pf, pv, lb, ckpt, do)
    grads["hgrn_lb_logits"] = grads["hgrn_lb_logits"] + _lower_bound_bwd(f"hgrn{l}_dlb", w["hgrn_lb_logits"], dlb, l)
    dps = [dpq, dpf, dpv, dpg]
    grads["hgrn_w_in"][j] = jnp.concatenate([_matmul_tn(f"hgrn{l}_dwin{s}", h, dps[s]) for s in range(4)], axis=1)
    dh = _matmul(f"hgrn{l}_dh", [(dps[s], w_in[:, s * D:(s + 1) * D]) for s in range(4)], trans_b=True)
    dx, dg = _rmsnorm_bwd(f"mix{l}_dnorm", x, _row(w["mixer_norm"][l]), dh, dx1)
    grads["mixer_norm"][l] = dg[0]
    return dx


_MIXERS = ((_attn_fwd, _attn_bwd), (_conv_fwd, _conv_bwd), (_hgrn_fwd, _hgrn_bwd))
_PER_MIXER = {"attn": 0, "conv": 1, "hgrn": 2}


def _local_step(x, target, w):
    depth = w["mixer_norm"].shape[0]
    grads = {}
    for name, v in w.items():
        lead = v.shape[0]
        grads[name] = jnp.zeros(v.shape, F32) if name == "hgrn_lb_logits" else [None] * lead
    saved = []
    for l in range(depth):
        fwd, _ = _MIXERS[l % N_MIXERS]
        x, s_mix = fwd(l, l // N_MIXERS, x, w)
        x, s_ffn = _ffn_fwd(l, x, w)
        saved.append((s_mix, s_ffn))
    dx, loss_cols = _loss_grad("loss", x, target)
    for l in reversed(range(depth)):
        _, bwd = _MIXERS[l % N_MIXERS]
        s_mix, s_ffn = saved[l]
        dx = _ffn_bwd(l, dx, s_ffn, w, grads)
        dx = bwd(l, l // N_MIXERS, dx, s_mix, w, grads)
    grads = {k: (v if k == "hgrn_lb_logits" else jnp.stack(v)) for k, v in grads.items()}
    return jnp.sum(loss_cols), dx, grads


_HBM = pl.BlockSpec(memory_space=pltpu.HBM)


def _chip_peers():
    x, y, c = lax.axis_index("x"), lax.axis_index("y"), lax.axis_index("c")
    return 2 * x + y, (x, y, c), [(1 - x, y), (x, 1 - y), (1 - x, 1 - y)]


def _exchange_chips(name, src):
    def body(src_ref, out_ref, send_sems, recv_sems, local_sem):
        p, (x, y, c), peers = _chip_peers()
        mine = pltpu.make_async_copy(src_ref.at[p], out_ref.at[p], local_sem)
        mine.start()

        def copy(k, slab_from, slab_to, peer):
            return pltpu.make_async_remote_copy(
                src_ref=src_ref.at[slab_from], dst_ref=out_ref.at[slab_to], send_sem=send_sems.at[k],
                recv_sem=recv_sems.at[k], device_id=(peer[0], peer[1], c), device_id_type=MESH)

        sends = [copy(k, 2 * px + py, p, (px, py)) for k, (px, py) in enumerate(peers)]
        for s in sends:
            s.start()
        for k, (px, py) in enumerate(peers):
            copy(k, p, 2 * px + py, (px, py)).wait_recv()
        for s in sends:
            s.wait_send()
        mine.wait()

    return pl.pallas_call(
        body, name=name, in_specs=[_HBM], out_specs=_HBM, out_shape=jax.ShapeDtypeStruct(src.shape, src.dtype),
        scratch_shapes=[pltpu.SemaphoreType.DMA((3,)), pltpu.SemaphoreType.DMA((3,)), pltpu.SemaphoreType.DMA],
    )(src)


def _all_gather_chips(name, shard):
    def body(src_ref, out_ref, send_sems, recv_sems, local_sem):
        p, (x, y, c), peers = _chip_peers()
        mine = pltpu.make_async_copy(src_ref, out_ref.at[p], local_sem)
        mine.start()

        def copy(k, slab, peer):
            return pltpu.make_async_remote_copy(
                src_ref=src_ref, dst_ref=out_ref.at[slab], send_sem=send_sems.at[k], recv_sem=recv_sems.at[k],
                device_id=(peer[0], peer[1], c), device_id_type=MESH)

        sends = [copy(k, p, peer) for k, peer in enumerate(peers)]
        for s in sends:
            s.start()
        for k, (px, py) in enumerate(peers):
            copy(k, 2 * px + py, (px, py)).wait_recv()
        for s in sends:
            s.wait_send()
        mine.wait()

    return pl.pallas_call(
        body, name=name, in_specs=[_HBM], out_specs=_HBM,
        out_shape=jax.ShapeDtypeStruct((N_CHIPS,) + shard.shape, shard.dtype),
        scratch_shapes=[pltpu.SemaphoreType.DMA((3,)), pltpu.SemaphoreType.DMA((3,)), pltpu.SemaphoreType.DMA],
    )(shard)


def _swap_cores(name, v):
    def body(v_ref, out_ref, send_sem, recv_sem):
        x, y, c = lax.axis_index("x"), lax.axis_index("y"), lax.axis_index("c")
        cp = pltpu.make_async_remote_copy(src_ref=v_ref, dst_ref=out_ref, send_sem=send_sem, recv_sem=recv_sem,
                                          device_id=(x, y, 1 - c), device_id_type=MESH)
        cp.start()
        cp.wait()

    return pl.pallas_call(
        body, name=name, in_specs=[_HBM], out_specs=_HBM, out_shape=jax.ShapeDtypeStruct(v.shape, v.dtype),
        scratch_shapes=[pltpu.SemaphoreType.DMA, pltpu.SemaphoreType.DMA],
    )(v)


_WEIGHTS = ("mixer_norm", "ffn_norm", "attn_w_in", "attn_q_gain", "attn_k_gain", "attn_w_out", "conv_w_in",
            "conv_b_in", "conv_dw_w", "conv_dw_b", "conv_ln_g", "conv_ln_b", "conv_w_out", "conv_b_out",
            "hgrn_w_in", "hgrn_lb_logits", "hgrn_norm_g", "hgrn_w_out", "ffn_w_up", "ffn_conv_w", "ffn_conv_b",
            "ffn_w_down")
_SHARD_AXIS = {"attn_w_in": 2, "attn_w_out": 2, "conv_w_in": 2, "conv_dw_w": 2, "conv_w_out": 1, "hgrn_w_in": 2,
               "hgrn_norm_g": 1, "hgrn_w_out": 1, "ffn_w_up": 2, "ffn_conv_w": 2, "ffn_w_down": 1}
_MATMUL_WEIGHTS = ("attn_w_in", "attn_w_out", "conv_w_in", "conv_w_out", "hgrn_w_in", "hgrn_w_out", "ffn_w_up",
                   "ffn_w_down")
PACK_COLS = 1024
PACK_ROWS = 16


def _pack(arrays, nlead, dtype):
    lead = arrays[0].shape[:nlead]
    flat = []
    for a in arrays:
        f = a.reshape(lead + (-1,)).astype(dtype)
        flat.append(jnp.pad(f, [(0, 0)] * nlead + [(0, (-f.shape[-1]) % PACK_COLS)]))
    buf = jnp.concatenate(flat, axis=-1)
    buf = jnp.pad(buf, [(0, 0)] * nlead + [(0, (-buf.shape[-1]) % (PACK_COLS * PACK_ROWS))])
    return buf.reshape(lead + (-1, PACK_COLS))


def _unpack(buf, shapes, nlead):
    lead = buf.shape[:nlead]
    flat = buf.reshape(lead + (-1,))
    out, off = [], 0
    for shape in shapes:
        n = 1
        for s in shape:
            n *= s
        out.append(flat[..., off:off + n].reshape(lead + tuple(shape)))
        off += n + (-n) % PACK_COLS
    return out


def _merge_shards(piece, axis):
    moved = jnp.moveaxis(piece, 0, axis)
    shape = moved.shape
    return moved.reshape(shape[:axis] + (shape[axis] * shape[axis + 1],) + shape[axis + 2:])


def _split_shards(full, axis):
    shape = full.shape
    cut = full.reshape(shape[:axis] + (N_CHIPS, shape[axis] // N_CHIPS) + shape[axis + 1:])
    return jnp.moveaxis(cut, axis, 0)


def _gather_weights(local):
    big = [n for n in _WEIGHTS if n in _MATMUL_WEIGHTS]
    small = [n for n in _WEIGHTS if n in _SHARD_AXIS and n not in _MATMUL_WEIGHTS]
    full = {n: local[n] for n in _WEIGHTS if n not in _SHARD_AXIS}
    for names, dtype, tag in ((big, BF16, "comm_gather_matmul_weights"), (small, F32, "comm_gather_small_weights")):
        gathered = _all_gather_chips(tag, _pack([local[n] for n in names], 0, dtype))
        pieces = _unpack(gathered, [local[n].shape for n in names], 1)
        for n, piece in zip(names, pieces):
            full[n] = _merge_shards(piece, _SHARD_AXIS[n])
    return full


def _reduce_gradients(grads, local):
    slabs = []
    for n in _WEIGHTS:
        g = grads[n]
        if n in _SHARD_AXIS:
            slabs.append(_split_shards(g, _SHARD_AXIS[n]))
        else:
            slabs.append(jnp.broadcast_to(g[None], (N_CHIPS,) + g.shape))
    packed = _pack(slabs, 1, F32)
    landed = _exchange_chips("comm_scatter_gradients", packed)
    partial = _sum_slabs("sum_chips", landed)
    other = _swap_cores("comm_swap_partial_sums", partial)
    total = _add("sum_cores", [partial, other])
    return dict(zip(_WEIGHTS, _unpack(total, [local[n].shape for n in _WEIGHTS], 0)))


def kernel(x, mixer_norm, ffn_norm, attn_w_in, attn_q_gain, attn_k_gain, attn_w_out, conv_w_in, conv_b_in, conv_dw_w, conv_dw_b, conv_ln_g, conv_ln_b, conv_w_out, conv_b_out, hgrn_w_in, hgrn_lb_logits, hgrn_norm_g, hgrn_w_out, ffn_w_up, ffn_conv_w, ffn_conv_b, ffn_w_down, loss_target, m_mixer_norm, m_ffn_norm, m_attn_w_in, m_attn_q_gain, m_attn_k_gain, m_attn_w_out, m_conv_w_in, m_conv_b_in, m_conv_dw_w, m_conv_dw_b, m_conv_ln_g, m_conv_ln_b, m_conv_w_out, m_conv_b_out, m_hgrn_w_in, m_hgrn_lb_logits, m_hgrn_norm_g, m_hgrn_w_out, m_ffn_w_up, m_ffn_conv_w, m_ffn_conv_b, m_ffn_w_down, v_mixer_norm, v_ffn_norm, v_attn_w_in, v_attn_q_gain, v_attn_k_gain, v_attn_w_out, v_conv_w_in, v_conv_b_in, v_conv_dw_w, v_conv_dw_b, v_conv_ln_g, v_conv_ln_b, v_conv_w_out, v_conv_b_out, v_hgrn_w_in, v_hgrn_lb_logits, v_hgrn_norm_g, v_hgrn_w_out, v_ffn_w_up, v_ffn_conv_w, v_ffn_conv_b, v_ffn_w_down):
    given = dict(locals())
    local = {n: given[n] for n in _WEIGHTS}
    full = _gather_weights(local)
    loss, dx, grads = _local_step(x[0], loss_target[0], full)
    loss = lax.psum(loss, ("x", "y", "c"))
    grad = _reduce_gradients(grads, local)
    delta, new_m, new_v = {}, {}, {}
    for n in _WEIGHTS:
        shape = local[n].shape
        as2d = lambda a: a.reshape(-1, shape[-1])
        d, m, v = _adamw(f"adamw_{n}", as2d(local[n]), as2d(grad[n]), as2d(given["m_" + n]), as2d(given["v_" + n]))
        delta[n], new_m[n], new_v[n] = d.reshape(shape), m.reshape(shape), v.reshape(shape)
    return (loss, dx[None], *[grad[n] for n in _WEIGHTS], *[delta[n] for n in _WEIGHTS],
            *[new_m[n] for n in _WEIGHTS], *[new_v[n] for n in _WEIGHTS])
```

```python
import functools

import jax
import jax.numpy as jnp
from jax import lax
from jax.experimental import pallas as pl
from jax.experimental.pallas import tpu as pltpu

F32 = jnp.float32
BF16 = jnp.bfloat16

EPS = 1e-6
N_MIXERS = 3
ATTN_DILATIONS = (1, 4, 16)
ATTN_BLOCK = 128
ATTN_HEADS = 8
ATTN_HEAD_DIM = 64
ATTN_GW = ATTN_HEADS * ATTN_HEAD_DIM
HGRN_HEAD = 128
HGRN_CHUNK = 16
HGRN_TILE = 256
ADAM_LR, ADAM_B1, ADAM_B2, ADAM_EPS, ADAM_WD, ADAM_STEP = 0.001, 0.9, 0.999, 1e-08, 0.01, 10

LANES = 128
VMEM_LIMIT = 56 * 1024 * 1024
N_CHIPS = 4
MESH = pl.DeviceIdType.MESH

HI = lax.Precision.HIGHEST


def _params(*sem):
    return pltpu.CompilerParams(dimension_semantics=sem, vmem_limit_bytes=VMEM_LIMIT)


def _pick(n, target):
    if n <= target:
        return n
    best = None
    for t in range(LANES, target + 1, LANES):
        if n % t == 0:
            best = t
    assert best is not None, (n, target)
    return best


def _pick_rows(n, target):
    if n <= target:
        return n
    for t in range(target, 15, -16):
        if n % t == 0:
            return t
    return n


def _dot(a, b, dims, precision=None):
    return lax.dot_general(a, b, (dims, ((), ())), precision=precision, preferred_element_type=F32)


def _nn(a, b, precision=None):
    return _dot(a, b, ((1,), (0,)), precision)


def _nt(a, b, precision=None):
    return _dot(a, b, ((1,), (1,)), precision)


def _tn(a, b, precision=None):
    return _dot(a, b, ((0,), (0,)), precision)


def _sigmoid(x):
    return 1.0 / (1.0 + jnp.exp(-x))


def _rowwise(name, fn, rows, pars=(), outs=(), accs=(), *, tc=None, tm=512, rb=16):
    S = rows[0][0].shape[0]
    tm = _pick_rows(S, tm)
    rb = rb if tm % rb == 0 else tm
    width = tc if tc is not None else None
    ncol = 1
    if tc is not None:
        base = outs[0][0] if outs else accs[0][1]
        ncol = base // tc
    n_r, n_p, n_o, n_a = len(rows), len(pars), len(outs), len(accs)

    def body(*refs):
        row_refs, par_refs = refs[:n_r], refs[n_r:n_r + n_p]
        out_refs, acc_refs = refs[n_r + n_p:n_r + n_p + n_o], refs[n_r + n_p + n_o:]
        if n_a:
            @pl.when(pl.program_id(1) == 0)
            def _():
                for a in acc_refs:
                    a[...] = jnp.zeros_like(a)

        def step(s, carry):
            sl = pl.ds(pl.multiple_of(s * rb, rb), rb)
            res = fn(*[r[sl, :] for r in row_refs], *[p[...] for p in par_refs])
            res = res if isinstance(res, tuple) else (res,)
            for o, v in zip(out_refs, res[:n_o]):
                o[sl, :] = v.astype(o.dtype)
            for a, v in zip(acc_refs, res[n_o:]):
                a[...] += v
            return carry

        lax.fori_loop(0, tm // rb, step, 0)

    def row_spec(c, off):
        if tc is None:
            return pl.BlockSpec((tm, c), lambda j, i: (i, 0))
        return pl.BlockSpec((tm, tc), lambda j, i, o=off // tc: (i, j + o))

    def par_spec(shape, off):
        if off is None or tc is None:
            return pl.BlockSpec(shape, lambda j, i: (0, 0))
        return pl.BlockSpec((shape[0], tc), lambda j, i, o=off // tc: (0, j + o))

    in_specs = [row_spec(a.shape[1], off) for a, off in rows]
    in_specs += [par_spec(a.shape, off) for a, off in pars]
    out_specs = [row_spec(c, 0) for c, _ in outs] + [par_spec(s, 0) for s in accs]
    out_shape = [jax.ShapeDtypeStruct((S, c), d) for c, d in outs]
    out_shape += [jax.ShapeDtypeStruct(s, F32) for s in accs]
    res = pl.pallas_call(
        body, name=name, grid=(ncol, S // tm), in_specs=in_specs, out_specs=out_specs, out_shape=out_shape,
        compiler_params=_params("parallel", "arbitrary" if n_a else "parallel"),
    )(*[a for a, _ in rows], *[a for a, _ in pars])
    return res[0] if len(res) == 1 else tuple(res)


MATMUL_VMEM = 36 * 1024 * 1024


def _matmul_tiles(M, N, pairs, out_dtype, residual):
    tm = _pick_rows(M, 512)
    for tn in sorted({_pick(N, t) for t in range(LANES, 2049, LANES)}, reverse=True):
        step = sum(tm * a.shape[1] * a.dtype.itemsize + a.shape[1] * tn * b.dtype.itemsize for a, b in pairs)
        step += tm * tn * (jnp.dtype(out_dtype).itemsize + (4 if residual is not None else 0))
        if 2 * step <= MATMUL_VMEM:
            return tm, tn
    return tm, LANES


def _matmul(name, pairs, *, trans_b=False, bias=None, residual=None, out_dtype=F32):
    M = pairs[0][0].shape[0]
    N = pairs[0][1].shape[0] if trans_b else pairs[0][1].shape[1]
    tm, tn = _matmul_tiles(M, N, pairs, out_dtype, residual)
    n = len(pairs)

    def body(*refs):
        acc = None
        for i in range(n):
            a = refs[2 * i][...].astype(BF16)
            b = refs[2 * i + 1][...].astype(BF16)
            d = _nt(a, b) if trans_b else _nn(a, b)
            acc = d if acc is None else acc + d
        k = 2 * n
        if bias is not None:
            acc = acc + refs[k][...]
            k += 1
        if residual is not None:
            acc = acc + refs[k][...]
            k += 1
        refs[k][...] = acc.astype(out_dtype)

    in_specs, args = [], []
    for a, b in pairs:
        K = a.shape[1]
        in_specs.append(pl.BlockSpec((tm, K), lambda j, i: (i, 0)))
        in_specs.append(pl.BlockSpec((tn, K), lambda j, i: (j, 0)) if trans_b
                        else pl.BlockSpec((K, tn), lambda j, i: (0, j)))
        args += [a, b]
    if bias is not None:
        in_specs.append(pl.BlockSpec((1, tn), lambda j, i: (0, j)))
        args.append(bias)
    if residual is not None:
        in_specs.append(pl.BlockSpec((tm, tn), lambda j, i: (i, j)))
        args.append(residual)
    return pl.pallas_call(
        body, name=name, grid=(N // tn, M // tm), in_specs=in_specs,
        out_specs=pl.BlockSpec((tm, tn), lambda j, i: (i, j)),
        out_shape=jax.ShapeDtypeStruct((M, N), out_dtype), compiler_params=_params("parallel", "parallel"),
    )(*args)


def _matmul_tn(name, a, b, *, tm=1408, tn=1408, tk=512):
    S, M = a.shape
    N = b.shape[1]
    tm, tn, tk = _pick(M, tm), _pick(N, tn), _pick_rows(S, tk)

    def body(a_ref, b_ref, o_ref):
        @pl.when(pl.program_id(2) == 0)
        def _():
            o_ref[...] = jnp.zeros_like(o_ref)

        o_ref[...] += _tn(a_ref[...].astype(BF16), b_ref[...].astype(BF16))

    return pl.pallas_call(
        body, name=name, grid=(M // tm, N // tn, S // tk),
        in_specs=[pl.BlockSpec((tk, tm), lambda i, j, k: (k, i)), pl.BlockSpec((tk, tn), lambda i, j, k: (k, j))],
        out_specs=pl.BlockSpec((tm, tn), lambda i, j, k: (i, j)),
        out_shape=jax.ShapeDtypeStruct((M, N), F32), compiler_params=_params("parallel", "parallel", "arbitrary"),
    )(a, b)


def _halo_rows(K):
    return 8 if K <= 9 else 32


def _dwconv(name, x, w, b, *, reverse, out_dtype=F32):
    S, C = x.shape
    K = w.shape[0]
    H = _halo_rows(K)
    tm, tc = _pick_rows(S, 512 if K <= 4 else 256), _pick(C, 1408 if K <= 4 else 256)
    nrow = S // tm
    RB = 16 if out_dtype == BF16 else 8

    def body(x_ref, h_ref, w_ref, b_ref, o_ref, ext):
        i = pl.program_id(1)
        edge = (i == nrow - 1) if reverse else (i == 0)
        halo = jnp.where(edge, 0.0, h_ref[...].astype(F32))
        if reverse:
            ext[0:tm, :] = x_ref[...].astype(F32)
            ext[tm:tm + H, :] = halo
        else:
            ext[0:H, :] = halo
            ext[H:H + tm, :] = x_ref[...].astype(F32)
        wv = w_ref[...]
        for s in range(tm // RB):
            acc = jnp.broadcast_to(b_ref[...], (RB, tc))
            for k in range(K):
                off = s * RB + ((K - 1 - k) if reverse else (H - (K - 1) + k))
                acc = acc + wv[k:k + 1, :] * ext[off:off + RB, :]
            o_ref[s * RB:(s + 1) * RB, :] = acc.astype(out_dtype)

    r = tm // H
    if reverse:
        halo_map = lambda j, i: (jnp.minimum((i + 1) * r, S // H - 1), j)
    else:
        halo_map = lambda j, i: (jnp.maximum(i * r - 1, 0), j)
    return pl.pallas_call(
        body, name=name, grid=(C // tc, nrow),
        in_specs=[pl.BlockSpec((tm, tc), lambda j, i: (i, j)), pl.BlockSpec((H, tc), halo_map),
                  pl.BlockSpec((K, tc), lambda j, i: (0, j)), pl.BlockSpec((1, tc), lambda j, i: (0, j))],
        out_specs=pl.BlockSpec((tm, tc), lambda j, i: (i, j)),
        out_shape=jax.ShapeDtypeStruct((S, C), out_dtype),
        scratch_shapes=[pltpu.VMEM((tm + H, tc), F32)], compiler_params=_params("parallel", "parallel"),
    )(x, x, w, b)


def _dwconv_wgrad(name, x, dy, K):
    S, C = x.shape
    H = _halo_rows(K)
    tm, tc = _pick_rows(S, 512 if K <= 4 else 256), _pick(C, 512 if K <= 4 else LANES)
    RB = 8

    def body(x_ref, h_ref, dy_ref, dw_ref, db_ref, ext):
        i = pl.program_id(1)

        @pl.when(i == 0)
        def _():
            dw_ref[...] = jnp.zeros_like(dw_ref)
            db_ref[...] = jnp.zeros_like(db_ref)

        ext[0:H, :] = jnp.where(i == 0, 0.0, h_ref[...].astype(F32))
        ext[H:H + tm, :] = x_ref[...].astype(F32)
        acc = [jnp.zeros((RB, tc), F32) for _ in range(K)]
        accb = jnp.zeros((RB, tc), F32)
        for s in range(tm // RB):
            d = dy_ref[s * RB:(s + 1) * RB, :].astype(F32)
            accb = accb + d
            for k in range(K):
                off = s * RB + H - (K - 1) + k
                acc[k] = acc[k] + d * ext[off:off + RB, :]
        for k in range(K):
            dw_ref[k:k + 1, :] += jnp.sum(acc[k], axis=0, keepdims=True)
        db_ref[...] += jnp.sum(accb, axis=0, keepdims=True)

    r = tm // H
    return pl.pallas_call(
        body, name=name, grid=(C // tc, S // tm),
        in_specs=[pl.BlockSpec((tm, tc), lambda j, i: (i, j)),
                  pl.BlockSpec((H, tc), lambda j, i: (jnp.maximum(i * r - 1, 0), j)),
                  pl.BlockSpec((tm, tc), lambda j, i: (i, j))],
        out_specs=[pl.BlockSpec((K, tc), lambda j, i: (0, j)), pl.BlockSpec((1, tc), lambda j, i: (0, j))],
        out_shape=[jax.ShapeDtypeStruct((K, C), F32), jax.ShapeDtypeStruct((1, C), F32)],
        scratch_shapes=[pltpu.VMEM((tm + H, tc), F32)], compiler_params=_params("parallel", "arbitrary"),
    )(x, x, dy)


FFN_HALO = 16
FFN_DY_HALO = 8


def _conv_taps(w, b, ext, r0, rows, cs):
    K = w.shape[0]
    acc = b
    for k in range(K):
        off = r0 - (K - 1) + k
        acc = acc + w[k:k + 1, :] * ext[off:off + rows, cs]
    return acc


def _ffn_up_fused(name, h, w_up, cw, cb):
    S, D = h.shape
    F = w_up.shape[1] // 2
    tm, tn = _pick_rows(S, 512), _pick(F, 1408)
    nj, H, RB = F // tn, FFN_HALO, 16

    def body(h_ref, hh_ref, wg_ref, wu_ref, cwg_ref, cwu_ref, cbg_ref, cbu_ref, a_ref, u0g_ref, u0u_ref, eg, eu):
        first = pl.program_id(1) == 0
        hv, halo = h_ref[...], hh_ref[...]
        for w_ref, u0_ref, e in ((wg_ref, u0g_ref, eg), (wu_ref, u0u_ref, eu)):
            w = w_ref[...]
            u0 = _nn(hv, w)
            u0_ref[...] = u0.astype(BF16)
            e[0:H, :] = jnp.where(first, 0.0, _nn(halo, w))
            e[H:H + tm, :] = u0
        for c in range(tn // LANES):
            cs = slice(c * LANES, (c + 1) * LANES)
            wg, wu, bg, bu = cwg_ref[:, cs], cwu_ref[:, cs], cbg_ref[:, cs], cbu_ref[:, cs]
            for s in range(tm // RB):
                ug = _conv_taps(wg, bg, eg, H + s * RB, RB, cs)
                uu = _conv_taps(wu, bu, eu, H + s * RB, RB, cs)
                a_ref[s * RB:(s + 1) * RB, cs] = (ug * _sigmoid(ug) * uu).astype(BF16)

    r = tm // H
    gate = lambda rows: pl.BlockSpec((rows, tn), lambda j, i: (0, j))
    up = lambda rows: pl.BlockSpec((rows, tn), lambda j, i: (0, j + nj))
    tile = pl.BlockSpec((tm, tn), lambda j, i: (i, j))
    K = cw.shape[0]
    return pl.pallas_call(
        body, name=name, grid=(nj, S // tm),
        in_specs=[pl.BlockSpec((tm, D), lambda j, i: (i, 0)),
                  pl.BlockSpec((H, D), lambda j, i: (jnp.maximum(i * r - 1, 0), 0)),
                  gate(D), up(D), gate(K), up(K), gate(1), up(1)],
        out_specs=[tile, tile, tile], out_shape=[jax.ShapeDtypeStruct((S, F), BF16)] * 3,
        scratch_shapes=[pltpu.VMEM((H + tm, tn), F32)] * 2, compiler_params=_params("parallel", "parallel"),
    )(h, h, w_up, w_up, cw, cw, cb, cb)


def _ffn_gate_bwd_fused(name, dy, w_down, u0g, u0u, cw, cb):
    S, D = dy.shape
    F, K = w_down.shape[0], cw.shape[0]
    tm, tn = _pick_rows(S, 512), _pick(F, 1408)
    nj, nrow, H, HD = F // tn, S // tm, FFN_HALO, FFN_DY_HALO

    def body(dy_ref, dyn_ref, wd_ref, g_ref, gp_ref, gn_ref, u_ref, up_ref, un_ref, cwg_ref, cwu_ref, cbg_ref, cbu_ref,
             dg_ref, du_ref, dcwg_ref, dcwu_ref, dcbg_ref, dcbu_ref, eg, eu, dg_s, du_s, da_s):
        i = pl.program_id(1)
        first, last = i == 0, i == nrow - 1

        @pl.when(first)
        def _():
            for ref in (dcwg_ref, dcwu_ref, dcbg_ref, dcbu_ref):
                ref[...] = jnp.zeros_like(ref)

        wd = wd_ref[...]
        da_s[0:tm, :] = _nt(dy_ref[...].astype(BF16), wd)
        da_s[tm:tm + HD, :] = jnp.where(last, 0.0, _nt(dyn_ref[...].astype(BF16), wd))
        for e, cur, prv, nxt in ((eg, g_ref, gp_ref, gn_ref), (eu, u_ref, up_ref, un_ref)):
            e[0:H, :] = jnp.where(first, 0.0, prv[...].astype(F32))
            e[H:H + tm, :] = cur[...].astype(F32)
            e[H + tm:H + tm + H, :] = jnp.where(last, 0.0, nxt[...].astype(F32))
        for c in range(tn // LANES):
            cs = slice(c * LANES, (c + 1) * LANES)
            wg, wu, bg, bu = cwg_ref[:, cs], cwu_ref[:, cs], cbg_ref[:, cs], cbu_ref[:, cs]
            for s in range(tm // 8 + 1):
                rows = slice(s * 8, (s + 1) * 8)
                ug = _conv_taps(wg, bg, eg, H + s * 8, 8, cs)
                uu = _conv_taps(wu, bu, eu, H + s * 8, 8, cs)
                da = da_s[rows, cs]
                sg = _sigmoid(ug)
                dg_s[rows, cs] = da * uu * (sg * (1.0 + ug * (1.0 - sg)))
                du_s[rows, cs] = da * (ug * sg)
            for d_s, e, w, out_ref, dcw_ref, dcb_ref in ((dg_s, eg, wg, dg_ref, dcwg_ref, dcbg_ref),
                                                        (du_s, eu, wu, du_ref, dcwu_ref, dcbu_ref)):
                acc = [jnp.zeros((8, LANES), F32) for _ in range(K)]
                accb = jnp.zeros((8, LANES), F32)
                for s in range(tm // 16):
                    halves = []
                    for r0 in (s * 16, s * 16 + 8):
                        d = d_s[r0:r0 + 8, cs]
                        accb = accb + d
                        t = None
                        for k in range(K):
                            acc[k] = acc[k] + d * e[H + r0 - (K - 1) + k:H + r0 - (K - 1) + k + 8, cs]
                            term = w[k:k + 1, :] * d_s[r0 + (K - 1) - k:r0 + (K - 1) - k + 8, cs]
                            t = term if t is None else t + term
                        halves.append(t)
                    out_ref[s * 16:(s + 1) * 16, cs] = jnp.concatenate(halves, axis=0).astype(BF16)
                for k in range(K):
                    dcw_ref[k:k + 1, cs] += jnp.sum(acc[k], axis=0, keepdims=True)
                dcb_ref[:, cs] += jnp.sum(accb, axis=0, keepdims=True)

    r, rd = tm // H, tm // HD
    gate = lambda rows: pl.BlockSpec((rows, tn), lambda j, i: (0, j))
    up = lambda rows: pl.BlockSpec((rows, tn), lambda j, i: (0, j + nj))
    tile = pl.BlockSpec((tm, tn), lambda j, i: (i, j))
    prev = pl.BlockSpec((H, tn), lambda j, i: (jnp.maximum(i * r - 1, 0), j))
    nxt = pl.BlockSpec((H, tn), lambda j, i: (jnp.minimum((i + 1) * r, S // H - 1), j))
    acc_w, acc_b = pl.BlockSpec((K, tn), lambda j, i: (0, j)), pl.BlockSpec((1, tn), lambda j, i: (0, j))
    return pl.pallas_call(
        body, name=name, grid=(nj, nrow),
        in_specs=[pl.BlockSpec((tm, D), lambda j, i: (i, 0)),
                  pl.BlockSpec((HD, D), lambda j, i: (jnp.minimum((i + 1) * rd, S // HD - 1), 0)),
                  pl.BlockSpec((tn, D), lambda j, i: (j, 0)),
                  tile, prev, nxt, tile, prev, nxt, gate(K), up(K), gate(1), up(1)],
        out_specs=[tile, tile, acc_w, acc_w, acc_b, acc_b],
        out_shape=[jax.ShapeDtypeStruct((S, F), BF16)] * 2 + [jax.ShapeDtypeStruct((K, F), F32)] * 2
        + [jax.ShapeDtypeStruct((1, F), F32)] * 2,
        scratch_shapes=[pltpu.VMEM((H + tm + H, tn), F32)] * 2 + [pltpu.VMEM((tm + HD, tn), F32)] * 3,
        compiler_params=_params("parallel", "arbitrary"),
    )(dy, dy, w_down, u0g, u0g, u0g, u0u, u0u, u0u, cw, cw, cb, cb)


def _colsum(v):
    return jnp.sum(v, axis=0, keepdims=True)


def _rmsnorm_fwd(name, x, gain):
    def fn(x, g):
        r = lax.rsqrt(jnp.mean(x * x, axis=-1, keepdims=True) + EPS)
        return x * r * g
    return _rowwise(name, fn, [(x, 0)], [(gain, None)], [(x.shape[1], BF16)])


def _rmsnorm_bwd(name, x, gain, dh, dres):
    def fn(x, dh, dres, g):
        r = lax.rsqrt(jnp.mean(x * x, axis=-1, keepdims=True) + EPS)
        xh = x * r
        dxh = dh * g
        dx = r * (dxh - xh * jnp.mean(dxh * xh, axis=-1, keepdims=True))
        return dres + dx, _colsum(dh * xh)
    D = x.shape[1]
    return _rowwise(name, fn, [(x, 0), (dh, 0), (dres, 0)], [(gain, None)], [(D, F32)], [(1, D)])


def _silu_gate_fwd(name, gate, up):
    F = gate.shape[1]
    def fn(g, up):
        return g * _sigmoid(g) * up
    return _rowwise(name, fn, [(gate, 0), (up, 0)], [], [(F, BF16)], tc=_pick(F, 512))


def _silu_gate_bwd(name, gate, up, da):
    F = gate.shape[1]
    def fn(g, up, da):
        s = _sigmoid(g)
        return da * up * (s * (1.0 + g * (1.0 - s))), da * (g * s)
    return _rowwise(name, fn, [(gate, 0), (up, 0), (da, 0)], [], [(F, F32), (F, F32)], tc=_pick(F, 512))


def _glu_fwd(name, a, gate):
    C = a.shape[1]
    def fn(a, g):
        return a * _sigmoid(g)
    return _rowwise(name, fn, [(a, 0), (gate, 0)], [], [(C, F32)], tc=_pick(C, 512))


def _glu_bwd(name, a, gate, dglu):
    C = a.shape[1]
    def fn(a, g, d):
        s = _sigmoid(g)
        da, dg = d * s, d * a * s * (1.0 - s)
        return da, dg, _colsum(da), _colsum(dg)
    return _rowwise(name, fn, [(a, 0), (gate, 0), (dglu, 0)], [], [(C, BF16), (C, BF16)], [(1, C), (1, C)],
                    tc=_pick(C, 512))


def _ln_silu_fwd(name, c, g, b):
    def fn(c, g, b):
        mu = jnp.mean(c, axis=-1, keepdims=True)
        d = c - mu
        n = d * lax.rsqrt(jnp.mean(d * d, axis=-1, keepdims=True) + EPS) * g + b
        return n * _sigmoid(n)
    return _rowwise(name, fn, [(c, 0)], [(g, None), (b, None)], [(c.shape[1], BF16)])


def _ln_silu_bwd(name, c, g, b, dsw):
    def fn(c, dsw, g, b):
        mu = jnp.mean(c, axis=-1, keepdims=True)
        d = c - mu
        r = lax.rsqrt(jnp.mean(d * d, axis=-1, keepdims=True) + EPS)
        ch = d * r
        n = ch * g + b
        s = _sigmoid(n)
        dn = dsw * (s * (1.0 + n * (1.0 - s)))
        dch = dn * g
        dc = r * (dch - jnp.mean(dch, axis=-1, keepdims=True) - ch * jnp.mean(dch * ch, axis=-1, keepdims=True))
        return dc, _colsum(dn * ch), _colsum(dn)
    C = c.shape[1]
    return _rowwise(name, fn, [(c, 0), (dsw, 0)], [(g, None), (b, None)], [(C, F32)], [(1, C), (1, C)])


def _column_sums(name, x):
    return _rowwise(name, lambda x: (_colsum(x),), [(x, 0)], [], [], [(1, x.shape[1])])


def _loss_grad(name, y, target):
    D = y.shape[1]
    def fn(y, t):
        e = y - t
        return e * (1.0 / D), _colsum(e * e) * (0.5 / D)
    return _rowwise(name, fn, [(y, 0), (target, 0)], [], [(D, F32)], [(1, D)])


def _add(name, arrays):
    def fn(*xs):
        acc = xs[0]
        for x in xs[1:]:
            acc = acc + x
        return acc
    return _rowwise(name, fn, [(a, 0) for a in arrays], [], [(arrays[0].shape[1], F32)], tm=256)


def _sum_slabs(name, stacked):
    n, R, C = stacked.shape
    tm = _pick_rows(R, 256)

    def body(*refs):
        acc = refs[0][0]
        for r in refs[1:n]:
            acc = acc + r[0]
        refs[n][...] = acc

    return pl.pallas_call(
        body, name=name, grid=(R // tm,),
        in_specs=[pl.BlockSpec((1, tm, C), lambda i, q=q: (q, i, 0)) for q in range(n)],
        out_specs=pl.BlockSpec((tm, C), lambda i: (i, 0)), out_shape=jax.ShapeDtypeStruct((R, C), F32),
        compiler_params=_params("parallel"),
    )(*[stacked] * n)


def _adamw(name, w, g, m, v):
    c1 = 1.0 - ADAM_B1 ** ADAM_STEP
    c2 = 1.0 - ADAM_B2 ** ADAM_STEP
    def fn(w, g, m, v):
        m = ADAM_B1 * m + (1.0 - ADAM_B1) * g
        v = ADAM_B2 * v + (1.0 - ADAM_B2) * (g * g)
        delta = -ADAM_LR * ((m / c1) / (jnp.sqrt(v / c2) + ADAM_EPS) + ADAM_WD * w)
        return delta, m, v
    C = w.shape[1]
    return _rowwise(name, fn, [(w, 0), (g, 0), (m, 0), (v, 0)], [], [(C, F32)] * 3, tm=256)


SEG_ROWS = 128


def _segment_matrix(n, seg):
    i = jnp.arange(n) // seg
    return (i[:, None] == i[None, :]).astype(BF16)


def _seg_sum(v, B):
    hi = v.astype(BF16)
    lo = (v - hi.astype(F32)).astype(BF16)
    return _nn(hi, B) + _nn(lo, B)


def _qknorm_fwd(name, proj, gain_full, is_norm, seg):
    def fn(x, gf, isn, B):
        ms = _seg_sum(x * x, B) * (1.0 / ATTN_HEAD_DIM)
        r = lax.rsqrt(ms + EPS)
        return x * (isn * r + (1.0 - isn)) * gf
    W = proj.shape[1]
    return _rowwise(name, fn, [(proj, 0)], [(gain_full, 0), (is_norm, 0), (seg, None)], [(W, BF16)],
                    tc=ATTN_GW, rb=SEG_ROWS)


def _qknorm_bwd(name, proj, dy, gain_full, is_norm, seg):
    def fn(x, dy, gf, isn, B):
        ms = _seg_sum(x * x, B) * (1.0 / ATTN_HEAD_DIM)
        r = lax.rsqrt(ms + EPS)
        xh = x * r
        dxh = dy * gf
        dn = r * (dxh - xh * (_seg_sum(dxh * xh, B) * (1.0 / ATTN_HEAD_DIM)))
        return isn * dn + (1.0 - isn) * dxh, _colsum(dy * xh)
    W = proj.shape[1]
    return _rowwise(name, fn, [(proj, 0), (dy, 0)], [(gain_full, 0), (is_norm, 0), (seg, None)],
                    [(W, BF16)], [(1, W)], tc=ATTN_GW, rb=SEG_ROWS)


def _attn_masks():
    shape = (ATTN_BLOCK, ATTN_BLOCK)
    row = lax.broadcasted_iota(jnp.int32, shape, 0)
    col = lax.broadcasted_iota(jnp.int32, shape, 1)
    low_lanes = col < ATTN_HEAD_DIM
    return col <= row, col >= row, low_lanes


def _attn_group_fwd(name, qkv, d):
    S = qkv.shape[0]
    n, W, G = S // d, 3 * ATTN_GW, ATTN_GW
    nb = n // ATTN_BLOCK
    view = qkv.reshape(n, d * W)

    def body(cur, prev, o_ref, l_ref):
        b = pl.program_id(1)
        cur_mask, prev_mask, low = _attn_masks()
        prev_mask = jnp.logical_and(prev_mask, b > 0)
        for pr in range(G // LANES):
            c0 = pr * LANES
            q2, kc, vc = cur[:, c0:c0 + LANES], cur[:, G + c0:G + c0 + LANES], cur[:, 2 * G + c0:2 * G + c0 + LANES]
            kp, vp = prev[:, G + c0:G + c0 + LANES], prev[:, 2 * G + c0:2 * G + c0 + LANES]
            res = []
            for hm in (low, jnp.logical_not(low)):
                qm = jnp.where(hm, q2, jnp.zeros_like(q2))
                sc = jnp.where(cur_mask, _nt(qm, kc), -jnp.inf)
                sp = jnp.where(prev_mask, _nt(qm, kp), -jnp.inf)
                m = jnp.maximum(jnp.max(sc, axis=1, keepdims=True), jnp.max(sp, axis=1, keepdims=True))
                pc, pp = jnp.exp(sc - m), jnp.exp(sp - m)
                l = jnp.sum(pc, axis=1, keepdims=True) + jnp.sum(pp, axis=1, keepdims=True)
                o = (_nn(pc.astype(BF16), vc) + _nn(pp.astype(BF16), vp)) / l
                res.append((o, jnp.broadcast_to(m + jnp.log(l), o.shape)))
            o_ref[:, c0:c0 + LANES] = jnp.where(low, res[0][0], res[1][0])
            l_ref[:, c0:c0 + LANES] = jnp.where(low, res[0][1], res[1][1])

    o, l = pl.pallas_call(
        body, name=name, grid=(d, nb),
        in_specs=[pl.BlockSpec((ATTN_BLOCK, W), lambda r, b: (b, r)),
                  pl.BlockSpec((ATTN_BLOCK, W), lambda r, b: (jnp.maximum(b - 1, 0), r))],
        out_specs=[pl.BlockSpec((ATTN_BLOCK, G), lambda r, b: (b, r))] * 2,
        out_shape=[jax.ShapeDtypeStruct((n, d * G), F32)] * 2, compiler_params=_params("parallel", "parallel"),
    )(view, view)
    return o.reshape(S, G), l.reshape(S, G)


def _attn_combine(name, os, ls):
    def fn(o1, o2, o3, l1, l2, l3):
        m = jnp.maximum(jnp.maximum(l1, l2), l3)
        e1, e2, e3 = jnp.exp(l1 - m), jnp.exp(l2 - m), jnp.exp(l3 - m)
        den = e1 + e2 + e3
        return (e1 * o1 + e2 * o2 + e3 * o3) / den, m + jnp.log(den)
    G = os[0].shape[1]
    return _rowwise(name, fn, [(a, 0) for a in (*os, *ls)], [], [(G, F32), (G, F32)])


def _attn_delta(name, do, o, seg):
    def fn(do, o, B):
        return _seg_sum(do * o, B)
    return _rowwise(name, fn, [(do, 0), (o, 0)], [(seg, None)], [(o.shape[1], F32)], rb=SEG_ROWS)


def _attn_group_bwd(name, qkv, do, lse, delta, d):
    S = qkv.shape[0]
    n, W, G = S // d, 3 * ATTN_GW, ATTN_GW
    nb = n // ATTN_BLOCK

    def body(qp, qc, qn, do_c, do_n, l_c, l_n, dl_c, dl_n, out):
        j = pl.program_id(1)
        cur_mask, prev_mask, low = _attn_masks()
        next_mask = jnp.logical_and(prev_mask, j < nb - 1)
        prev_mask = jnp.logical_and(prev_mask, j > 0)
        for pr in range(G // LANES):
            c0 = pr * LANES
            q_c, k_c, v_c = qc[:, c0:c0 + LANES], qc[:, G + c0:G + c0 + LANES], qc[:, 2 * G + c0:2 * G + c0 + LANES]
            q_n = qn[:, c0:c0 + LANES]
            k_p, v_p = qp[:, G + c0:G + c0 + LANES], qp[:, 2 * G + c0:2 * G + c0 + LANES]
            d_c, d_n = do_c[:, c0:c0 + LANES].astype(BF16), do_n[:, c0:c0 + LANES].astype(BF16)
            res = []
            for hh, hm in enumerate((low, jnp.logical_not(low))):
                h0 = c0 + hh * ATTN_HEAD_DIM
                lc, ln = l_c[:, h0:h0 + 1], l_n[:, h0:h0 + 1]
                dlc, dln = dl_c[:, h0:h0 + 1], dl_n[:, h0:h0 + 1]
                zero = jnp.zeros_like(q_c)
                qmc, qmn = jnp.where(hm, q_c, zero), jnp.where(hm, q_n, zero)
                dmc, dmn = jnp.where(hm, d_c, zero), jnp.where(hm, d_n, zero)
                p_a = jnp.where(cur_mask, jnp.exp(_nt(qmc, k_c) - lc), 0.0)
                ds_a = p_a * (_nt(dmc, v_c) - dlc)
                p_b = jnp.where(next_mask, jnp.exp(_nt(qmn, k_c) - ln), 0.0)
                ds_b = p_b * (_nt(dmn, v_c) - dln)
                p_c = jnp.where(prev_mask, jnp.exp(_nt(qmc, k_p) - lc), 0.0)
                ds_c = p_c * (_nt(dmc, v_p) - dlc)
                dq = _nn(ds_a.astype(BF16), k_c) + _nn(ds_c.astype(BF16), k_p)
                dk = _tn(ds_a.astype(BF16), q_c) + _tn(ds_b.astype(BF16), q_n)
                dv = _tn(p_a.astype(BF16), d_c) + _tn(p_b.astype(BF16), d_n)
                res.append((dq, dk, dv))
            for t in range(3):
                out[:, t * G + c0:t * G + c0 + LANES] = jnp.where(low, res[0][t], res[1][t])

    prv = lambda r, j: (jnp.maximum(j - 1, 0), r)
    cur = lambda r, j: (j, r)
    nxt = lambda r, j: (jnp.minimum(j + 1, nb - 1), r)
    wide = lambda m: pl.BlockSpec((ATTN_BLOCK, W), m)
    narrow = lambda m: pl.BlockSpec((ATTN_BLOCK, G), m)
    qv, dv, lv, tv = qkv.reshape(n, d * W), do.reshape(n, d * G), lse.reshape(n, d * G), delta.reshape(n, d * G)
    out = pl.pallas_call(
        body, name=name, grid=(d, nb),
        in_specs=[wide(prv), wide(cur), wide(nxt), narrow(cur), narrow(nxt), narrow(cur), narrow(nxt),
                  narrow(cur), narrow(nxt)],
        out_specs=wide(cur), out_shape=jax.ShapeDtypeStruct((n, d * W), F32),
        compiler_params=_params("parallel", "parallel"),
    )(qv, qv, qv, dv, dv, lv, lv, tv, tv)
    return out.reshape(S, W)


def _chunk_triangle(T, upper):
    i = jnp.arange(T)
    same = (i[:, None] // HGRN_CHUNK) == (i[None, :] // HGRN_CHUNK)
    tri = (i[None, :] >= i[:, None]) if upper else (i[None, :] <= i[:, None])
    return jnp.logical_and(same, tri).astype(F32)


def _hgrn_prologue(qr, fr, lbv, q_s, k_s, b_s, tri_ref, T):
    def pro(s, c):
        sl = pl.ds(pl.multiple_of(s * HGRN_CHUNK, HGRN_CHUNK), HGRN_CHUNK)
        sg = _sigmoid(fr[sl, :])
        qv = qr[sl, :]
        q_s[sl, :] = qv * _sigmoid(qv)
        k_s[sl, :] = (1.0 - lbv) * (1.0 - sg)
        b_s[sl, :] = jnp.log(lbv + (1.0 - lbv) * sg)
        return c
    lax.fori_loop(0, T // HGRN_CHUNK, pro, 0)
    b_s[...] = _nn(tri_ref[...], b_s[...], HI)


def _hgrn_scan_fwd(name, pq, pf, pv, lb):
    S, D = pq.shape
    T = _pick_rows(S, HGRN_TILE)
    NH, NT, C, HD = D // HGRN_HEAD, S // T, HGRN_CHUNK, HGRN_HEAD
    tri = _chunk_triangle(T, upper=False)

    def body(qr, fr, iv, lb_ref, tri_ref, o_ref, ck_ref, st_ref, q_s, k_s, b_s):
        @pl.when(pl.program_id(1) == 0)
        def _():
            st_ref[...] = jnp.zeros_like(st_ref)

        ck_ref[...] = st_ref[...]
        _hgrn_prologue(qr, fr, lb_ref[...], q_s, k_s, b_s, tri_ref, T)
        row = lax.broadcasted_iota(jnp.int32, (C, 1), 0)

        def chunk(c, carry):
            sl = pl.ds(pl.multiple_of(c * C, C), C)
            q, k, b, v = q_s[sl, :], k_s[sl, :], b_s[sl, :], iv[sl, :]
            b_last = b[C - 1:C, :]
            st = st_ref[...]
            o = _nt((q * jnp.exp(b)).astype(BF16), st.astype(BF16))
            for s in range(C):
                e = jnp.exp(jnp.minimum(b - b[s:s + 1, :], 0.0))
                a = jnp.sum(q * e * k[s:s + 1, :], axis=1, keepdims=True)
                o = o + jnp.where(row >= s, a, 0.0) * v[s:s + 1, :]
            o_ref[sl, :] = o
            kd = k * jnp.exp(b_last - b)
            st_ref[...] = st * jnp.exp(b_last) + _tn(v.astype(BF16), kd.astype(BF16))
            return carry

        lax.fori_loop(0, T // C, chunk, 0)

    col = lambda off: pl.BlockSpec((T, HD), lambda h, t, o=off: (t, h + o))
    return pl.pallas_call(
        body, name=name, grid=(NH, NT),
        in_specs=[col(0), col(0), col(0), pl.BlockSpec((1, HD), lambda h, t: (0, h)),
                  pl.BlockSpec((T, T), lambda h, t: (0, 0))],
        out_specs=[col(0), pl.BlockSpec((HD, HD), lambda h, t: (t * NH + h, 0))],
        out_shape=[jax.ShapeDtypeStruct((S, D), F32), jax.ShapeDtypeStruct((NT * NH * HD, HD), F32)],
        scratch_shapes=[pltpu.VMEM((HD, HD), F32)] + [pltpu.VMEM((T, HD), F32)] * 3,
        compiler_params=_params("parallel", "arbitrary"),
    )(pq, pf, pv, lb, tri)


def _hgrn_scan_bwd(name, pq, pf, pv, lb, ckpt, do):
    S, D = pq.shape
    T = _pick_rows(S, HGRN_TILE)
    NH, NT, C, HD = D // HGRN_HEAD, S // T, HGRN_CHUNK, HGRN_HEAD
    NC = T // C
    tri, tri_up = _chunk_triangle(T, upper=False), _chunk_triangle(T, upper=True)

    def body(qr, fr, iv, do_ref, ck_ref, lb_ref, tri_ref, triu_ref, dq_ref, df_ref, dv_ref, dlb_ref,
             dst_ref, save, q_s, k_s, b_s, dq_s, dk_s, db_s):
        @pl.when(pl.program_id(1) == 0)
        def _():
            dst_ref[...] = jnp.zeros_like(dst_ref)
            dlb_ref[...] = jnp.zeros_like(dlb_ref)

        lbv = lb_ref[...]
        _hgrn_prologue(qr, fr, lbv, q_s, k_s, b_s, tri_ref, T)
        row = lax.broadcasted_iota(jnp.int32, (C, 1), 0)

        def replay(c, st):
            sl = pl.ds(pl.multiple_of(c * C, C), C)
            save[pl.ds(pl.multiple_of(c * HD, HD), HD), :] = st
            k, b, v = k_s[sl, :], b_s[sl, :], iv[sl, :]
            b_last = b[C - 1:C, :]
            kd = k * jnp.exp(b_last - b)
            return st * jnp.exp(b_last) + _tn(v.astype(BF16), kd.astype(BF16))

        lax.fori_loop(0, NC, replay, ck_ref[...])

        def chunk(ci, carry):
            c = NC - 1 - ci
            sl = pl.ds(pl.multiple_of(c * C, C), C)
            q, k, b, v, g = q_s[sl, :], k_s[sl, :], b_s[sl, :], iv[sl, :], do_ref[sl, :]
            st0 = save[pl.ds(pl.multiple_of(c * HD, HD), HD), :]
            dst1 = dst_ref[...]
            b_last = b[C - 1:C, :]
            eb, ebl, ek = jnp.exp(b), jnp.exp(b_last), jnp.exp(b_last - b)
            dst1_b = dst1.astype(BF16)
            dq = _nn(g.astype(BF16), st0.astype(BF16)) * eb
            dv = _nt((k * ek).astype(BF16), dst1_b)
            dk = _nn(v.astype(BF16), dst1_b) * ek
            db_last = _colsum(dk * k) + _colsum(dst1 * st0) * ebl
            for s in range(C):
                e = jnp.where(row >= s, jnp.exp(jnp.minimum(b - b[s:s + 1, :], 0.0)), 0.0)
                ks, vs = k[s:s + 1, :], v[s:s + 1, :]
                da = jnp.sum(g * vs, axis=1, keepdims=True)
                a = jnp.sum(q * e * ks, axis=1, keepdims=True)
                dq = dq + da * e * ks
                dk = dk + jnp.where(row == s, _colsum(da * q * e), 0.0)
                dv = dv + jnp.where(row == s, _colsum(a * g), 0.0)
            dq_s[sl, :] = dq
            dk_s[sl, :] = dk
            db_s[sl, :] = q * dq - k * dk + jnp.where(row == C - 1, db_last, 0.0)
            dv_ref[sl, :] = dv.astype(BF16)
            dst_ref[...] = dst1 * ebl + _tn(g.astype(BF16), (q * eb).astype(BF16))
            return carry

        lax.fori_loop(0, NC, chunk, 0)
        db_s[...] = _nn(triu_ref[...], db_s[...], HI)

        def epi(s, carry):
            sl = pl.ds(pl.multiple_of(s * C, C), C)
            qv = qr[sl, :]
            sq = _sigmoid(qv)
            dq_ref[sl, :] = (dq_s[sl, :] * sq * (1.0 + qv * (1.0 - sq))).astype(BF16)
            sg = _sigmoid(fr[sl, :])
            common = db_s[sl, :] / (lbv + (1.0 - lbv) * sg) - dk_s[sl, :]
            df_ref[sl, :] = (common * (1.0 - lbv) * sg * (1.0 - sg)).astype(BF16)
            dlb_ref[...] += _colsum(common * (1.0 - sg))
            return carry

        lax.fori_loop(0, NC, epi, 0)

    rev = lambda h, t: (NT - 1 - t, h)
    col = pl.BlockSpec((T, HD), rev)
    dq, df, dv, dlb = pl.pallas_call(
        body, name=name, grid=(NH, NT),
        in_specs=[col, col, col, col, pl.BlockSpec((HD, HD), lambda h, t: ((NT - 1 - t) * NH + h, 0)),
                  pl.BlockSpec((1, HD), lambda h, t: (0, h)),
                  pl.BlockSpec((T, T), lambda h, t: (0, 0)), pl.BlockSpec((T, T), lambda h, t: (0, 0))],
        out_specs=[col, col, col, pl.BlockSpec((1, HD), lambda h, t: (0, h))],
        out_shape=[jax.ShapeDtypeStruct((S, D), BF16)] * 3 + [jax.ShapeDtypeStruct((1, D), F32)],
        scratch_shapes=[pltpu.VMEM((HD, HD), F32), pltpu.VMEM((NC * HD, HD), F32)] + [pltpu.VMEM((T, HD), F32)] * 6,
        compiler_params=_params("parallel", "arbitrary"),
    )(pq, pf, pv, do, ckpt, lb, tri, tri_up)
    return dq, df, dv, dlb


def _hgrn_out_fwd(name, o, gate, norm_g):
    def fn(o, g, ng):
        parts = []
        for h in range(o.shape[1] // HGRN_HEAD):
            c = slice(h * HGRN_HEAD, (h + 1) * HGRN_HEAD)
            oh, gh = o[:, c], g[:, c]
            r = lax.rsqrt(jnp.mean(oh * oh, axis=-1, keepdims=True) + EPS)
            parts.append(oh * r * ng[:, c] * (gh * _sigmoid(gh)))
        return jnp.concatenate(parts, axis=1)
    return _rowwise(name, fn, [(o, 0), (gate, 0)], [(norm_g, None)], [(o.shape[1], BF16)])


def _hgrn_out_bwd(name, o, gate, norm_g, dy):
    def fn(o, g, dy, ng):
        dos, dgs, dngs = [], [], []
        for h in range(o.shape[1] // HGRN_HEAD):
            c = slice(h * HGRN_HEAD, (h + 1) * HGRN_HEAD)
            oh, gh, dyh, ngh = o[:, c], g[:, c], dy[:, c], ng[:, c]
            r = lax.rsqrt(jnp.mean(oh * oh, axis=-1, keepdims=True) + EPS)
            xh = oh * r
            s = _sigmoid(gh)
            dn = dyh * (gh * s)
            dxh = dn * ngh
            dos.append(r * (dxh - xh * jnp.mean(dxh * xh, axis=-1, keepdims=True)))
            dgs.append(dyh * xh * ngh * (s * (1.0 + gh * (1.0 - s))))
            dngs.append(_colsum(dn * xh))
        return jnp.concatenate(dos, axis=1), jnp.concatenate(dgs, axis=1), jnp.concatenate(dngs, axis=1)
    D = o.shape[1]
    return _rowwise(name, fn, [(o, 0), (gate, 0), (dy, 0)], [(norm_g, None)], [(D, F32), (D, BF16)], [(1, D)])


def _lower_bound_fwd(name, logits, layer):
    n = logits.shape[0]

    def body(x_ref, o_ref):
        rows = [x_ref[i:i + 1, :] for i in range(n)]
        m = functools.reduce(jnp.maximum, rows)
        e = [jnp.exp(r - m) for r in rows]
        den = functools.reduce(jnp.add, e)
        o_ref[...] = functools.reduce(jnp.add, e[1:layer + 1]) / den

    return pl.pallas_call(body, name=name, out_shape=jax.ShapeDtypeStruct((1, logits.shape[1]), F32))(logits)


def _lower_bound_bwd(name, logits, dlb, layer):
    n = logits.shape[0]

    def body(x_ref, d_ref, o_ref):
        rows = [x_ref[i:i + 1, :] for i in range(n)]
        m = functools.reduce(jnp.maximum, rows)
        e = [jnp.exp(r - m) for r in rows]
        den = functools.reduce(jnp.add, e)
        s = [v / den for v in e]
        d = d_ref[...]
        inner = functools.reduce(jnp.add, s[1:layer + 1]) * d
        for i in range(n):
            o_ref[i:i + 1, :] = s[i] * ((d if 1 <= i <= layer else 0.0) - inner)

    return pl.pallas_call(body, name=name, out_shape=jax.ShapeDtypeStruct(logits.shape, F32))(logits, dlb)


def _row(v):
    return v.reshape(1, -1)


def _ffn_fwd(l, x1, w):
    h2 = _rmsnorm_fwd(f"ffn{l}_norm", x1, _row(w["ffn_norm"][l]))
    a, u0g, u0u = _ffn_up_fused(f"ffn{l}_up", h2, w["ffn_w_up"][l], w["ffn_conv_w"][l], _row(w["ffn_conv_b"][l]))
    x2 = _matmul(f"ffn{l}_down", [(a, w["ffn_w_down"][l])], residual=x1)
    return x2, (x1, h2, u0g, u0u, a)


def _ffn_bwd(l, dx2, saved, w, grads):
    x1, h2, u0g, u0u, a = saved
    w_up, w_down = w["ffn_w_up"][l], w["ffn_w_down"][l]
    F = w_down.shape[0]
    grads["ffn_w_down"][l] = _matmul_tn(f"ffn{l}_dwdown", a, dx2)
    dg, du, dcwg, dcwu, dcbg, dcbu = _ffn_gate_bwd_fused(
        f"ffn{l}_dgate", dx2, w_down, u0g, u0u, w["ffn_conv_w"][l], _row(w["ffn_conv_b"][l]))
    grads["ffn_conv_w"][l] = jnp.concatenate([dcwg, dcwu], axis=1)
    grads["ffn_conv_b"][l] = jnp.concatenate([dcbg, dcbu], axis=1)[0]
    grads["ffn_w_up"][l] = jnp.concatenate(
        [_matmul_tn(f"ffn{l}_dwup0", h2, dg), _matmul_tn(f"ffn{l}_dwup1", h2, du)], axis=1)
    dh2 = _matmul(f"ffn{l}_dh", [(dg, w_up[:, :F]), (du, w_up[:, F:])], trans_b=True)
    dx1, dgain = _rmsnorm_bwd(f"ffn{l}_dnorm", x1, _row(w["ffn_norm"][l]), dh2, dx2)
    grads["ffn_norm"][l] = dgain[0]
    return dx1


def _attn_gain_rows(w, j, g):
    scale = ATTN_HEAD_DIM ** -0.5
    qg = jnp.tile(w["attn_q_gain"][j, g] * scale, ATTN_HEADS)
    kg = jnp.tile(w["attn_k_gain"][j, g], ATTN_HEADS)
    gain = jnp.concatenate([qg, kg, jnp.ones((ATTN_GW,), F32)])
    is_norm = jnp.concatenate([jnp.ones((2 * ATTN_GW,), F32), jnp.zeros((ATTN_GW,), F32)])
    return _row(gain), _row(is_norm)


def _attn_fwd(l, j, x, w):
    h = _rmsnorm_fwd(f"mix{l}_norm", x, _row(w["mixer_norm"][l]))
    w_in = w["attn_w_in"][j]
    seg = _segment_matrix(ATTN_GW, ATTN_HEAD_DIM)
    GW3 = 3 * ATTN_GW
    proj, qkv, os, ls = [], [], [], []
    for g, d in enumerate(ATTN_DILATIONS):
        gain, is_norm = _attn_gain_rows(w, j, g)
        proj.append(_matmul(f"attn{l}_in{g}", [(h, w_in[:, g * GW3:(g + 1) * GW3])]))
        qkv.append(_qknorm_fwd(f"attn{l}_qknorm{g}", proj[g], gain, is_norm, seg))
        o, lse = _attn_group_fwd(f"attn{l}_core{g}", qkv[g], d)
        os.append(o)
        ls.append(lse)
    o, lse = _attn_combine(f"attn{l}_combine", os, ls)
    x1 = _matmul(f"attn{l}_out", [(o, w["attn_w_out"][j])], residual=x)
    return x1, (x, h, proj, qkv, o, lse)


def _attn_bwd(l, j, dx1, saved, w, grads):
    x, h, proj, qkv, o, lse = saved
    w_in, w_out = w["attn_w_in"][j], w["attn_w_out"][j]
    seg = _segment_matrix(ATTN_GW, ATTN_HEAD_DIM)
    GW3 = 3 * ATTN_GW
    grads["attn_w_out"][j] = _matmul_tn(f"attn{l}_dwout", o, dx1)
    do = _matmul(f"attn{l}_do", [(dx1, w_out)], trans_b=True)
    delta = _attn_delta(f"attn{l}_delta", do, o, seg)
    dproj, dwin, dqg, dkg = [], [], [], []
    for g, d in enumerate(ATTN_DILATIONS):
        gain, is_norm = _attn_gain_rows(w, j, g)
        dqkv = _attn_group_bwd(f"attn{l}_dcore{g}", qkv[g], do, lse, delta, d)
        dp, dgain = _qknorm_bwd(f"attn{l}_dqknorm{g}", proj[g], dqkv, gain, is_norm, seg)
        dproj.append(dp)
        dwin.append(_matmul_tn(f"attn{l}_dwin{g}", h, dp))
        per_head = dgain.reshape(3, ATTN_HEADS, ATTN_HEAD_DIM).sum(axis=1)
        dqg.append(per_head[0] * ATTN_HEAD_DIM ** -0.5)
        dkg.append(per_head[1])
    grads["attn_w_in"][j] = jnp.concatenate(dwin, axis=1)
    grads["attn_q_gain"][j] = jnp.stack(dqg)
    grads["attn_k_gain"][j] = jnp.stack(dkg)
    dh = _matmul(f"attn{l}_dh", [(dproj[g], w_in[:, g * GW3:(g + 1) * GW3]) for g in range(3)], trans_b=True)
    dx, dg = _rmsnorm_bwd(f"mix{l}_dnorm", x, _row(w["mixer_norm"][l]), dh, dx1)
    grads["mixer_norm"][l] = dg[0]
    return dx


def _conv_fwd(l, j, x, w):
    h = _rmsnorm_fwd(f"mix{l}_norm", x, _row(w["mixer_norm"][l]))
    w_in, b_in = w["conv_w_in"][j], _row(w["conv_b_in"][j])
    C = w_in.shape[1] // 2
    ua = _matmul(f"conv{l}_in0", [(h, w_in[:, :C])], bias=b_in[:, :C])
    ug = _matmul(f"conv{l}_in1", [(h, w_in[:, C:])], bias=b_in[:, C:])
    glu = _glu_fwd(f"conv{l}_glu", ua, ug)
    c = _dwconv(f"conv{l}_dw", glu, w["conv_dw_w"][j], _row(w["conv_dw_b"][j]), reverse=False)
    sw = _ln_silu_fwd(f"conv{l}_ln", c, _row(w["conv_ln_g"][j]), _row(w["conv_ln_b"][j]))
    x1 = _matmul(f"conv{l}_out", [(sw, w["conv_w_out"][j])], bias=_row(w["conv_b_out"][j]), residual=x)
    return x1, (x, h, ua, ug, glu, c, sw)


def _conv_bwd(l, j, dx1, saved, w, grads):
    x, h, ua, ug, glu, c, sw = saved
    w_in, w_out, dw_w = w["conv_w_in"][j], w["conv_w_out"][j], w["conv_dw_w"][j]
    C = w_out.shape[0]
    grads["conv_b_out"][j] = _column_sums(f"conv{l}_dbout", dx1)[0]
    grads["conv_w_out"][j] = _matmul_tn(f"conv{l}_dwout", sw, dx1)
    dsw = _matmul(f"conv{l}_dsw", [(dx1, w_out)], trans_b=True)
    dc, dlg, dlb = _ln_silu_bwd(f"conv{l}_dln", c, _row(w["conv_ln_g"][j]), _row(w["conv_ln_b"][j]), dsw)
    grads["conv_ln_g"][j], grads["conv_ln_b"][j] = dlg[0], dlb[0]
    dglu = _dwconv(f"conv{l}_ddw", dc, dw_w, jnp.zeros((1, C), F32), reverse=True)
    gw, gb = _dwconv_wgrad(f"conv{l}_ddww", glu, dc, dw_w.shape[0])
    grads["conv_dw_w"][j], grads["conv_dw_b"][j] = gw, gb[0]
    da, dgate, sa, sg = _glu_bwd(f"conv{l}_dglu", ua, ug, dglu)
    grads["conv_b_in"][j] = jnp.concatenate([sa, sg], axis=1)[0]
    grads["conv_w_in"][j] = jnp.concatenate(
        [_matmul_tn(f"conv{l}_dwin0", h, da), _matmul_tn(f"conv{l}_dwin1", h, dgate)], axis=1)
    dh = _matmul(f"conv{l}_dh", [(da, w_in[:, :C]), (dgate, w_in[:, C:])], trans_b=True)
    dx, dg = _rmsnorm_bwd(f"mix{l}_dnorm", x, _row(w["mixer_norm"][l]), dh, dx1)
    grads["mixer_norm"][l] = dg[0]
    return dx


def _hgrn_fwd(l, j, x, w):
    h = _rmsnorm_fwd(f"mix{l}_norm", x, _row(w["mixer_norm"][l]))
    w_in = w["hgrn_w_in"][j]
    D = w_in.shape[1] // 4
    pq, pf, pv, pg = [_matmul(f"hgrn{l}_in{s}", [(h, w_in[:, s * D:(s + 1) * D])]) for s in range(4)]
    lb = _lower_bound_fwd(f"hgrn{l}_lb", w["hgrn_lb_logits"], l)
    o, ckpt = _hgrn_scan_fwd(f"hgrn{l}_scan", pq, pf, pv, lb)
    y = _hgrn_out_fwd(f"hgrn{l}_gate", o, pg, _row(w["hgrn_norm_g"][j]))
    x1 = _matmul(f"hgrn{l}_out", [(y, w["hgrn_w_out"][j])], residual=x)
    return x1, (x, h, pq, pf, pv, pg, lb, o, ckpt, y)


def _hgrn_bwd(l, j, dx1, saved, w, grads):
    x, h, pq, pf, pv, pg, lb, o, ckpt, y = saved
    w_in, w_out = w["hgrn_w_in"][j], w["hgrn_w_out"][j]
    D = w_out.shape[0]
    grads["hgrn_w_out"][j] = _matmul_tn(f"hgrn{l}_dwout", y, dx1)
    dy = _matmul(f"hgrn{l}_dy", [(dx1, w_out)], trans_b=True)
    do, dpg, dng = _hgrn_out_bwd(f"hgrn{l}_dgate", o, pg, _row(w["hgrn_norm_g"][j]), dy)
    grads["hgrn_norm_g"][j] = dng[0]
    dpq, dpf, dpv, dlb = _hgrn_scan_bwd(f"hgrn{l}_dscan", pq, pf, pv, lb, ckpt, do)
    grads["hgrn_lb_logits"] = grads["hgrn_lb_logits"] + _lower_bound_bwd(f"hgrn{l}_dlb", w["hgrn_lb_logits"], dlb, l)
    dps = [dpq, dpf, dpv, dpg]
    grads["hgrn_w_in"][j] = jnp.concatenate([_matmul_tn(f"hgrn{l}_dwin{s}", h, dps[s]) for s in range(4)], axis=1)
    dh = _matmul(f"hgrn{l}_dh", [(dps[s], w_in[:, s * D:(s + 1) * D]) for s in range(4)], trans_b=True)
    dx, dg = _rmsnorm_bwd(f"mix{l}_dnorm", x, _row(w["mixer_norm"][l]), dh, dx1)
    grads["mixer_norm"][l] = dg[0]
    return dx


_MIXERS = ((_attn_fwd, _attn_bwd), (_conv_fwd, _conv_bwd), (_hgrn_fwd, _hgrn_bwd))
_PER_MIXER = {"attn": 0, "conv": 1, "hgrn": 2}


def _local_step(x, target, w):
    depth = w["mixer_norm"].shape[0]
    grads = {}
    for name, v in w.items():
        lead = v.shape[0]
        grads[name] = jnp.zeros(v.shape, F32) if name == "hgrn_lb_logits" else [None] * lead
    saved = []
    for l in range(depth):
        fwd, _ = _MIXERS[l % N_MIXERS]
        x, s_mix = fwd(l, l // N_MIXERS, x, w)
        x, s_ffn = _ffn_fwd(l, x, w)
        saved.append((s_mix, s_ffn))
    dx, loss_cols = _loss_grad("loss", x, target)
    for l in reversed(range(depth)):
        _, bwd = _MIXERS[l % N_MIXERS]
        s_mix, s_ffn = saved[l]
        dx = _ffn_bwd(l, dx, s_ffn, w, grads)
        dx = bwd(l, l // N_MIXERS, dx, s_mix, w, grads)
    grads = {k: (v if k == "hgrn_lb_logits" else jnp.stack(v)) for k, v in grads.items()}
    return jnp.sum(loss_cols), dx, grads


_HBM = pl.BlockSpec(memory_space=pltpu.HBM)


def _chip_peers():
    x, y, c = lax.axis_index("x"), lax.axis_index("y"), lax.axis_index("c")
    return 2 * x + y, (x, y, c), [(1 - x, y), (x, 1 - y), (1 - x, 1 - y)]


def _exchange_chips(name, src):
    def body(src_ref, out_ref, send_sems, recv_sems, local_sem):
        p, (x, y, c), peers = _chip_peers()
        mine = pltpu.make_async_copy(src_ref.at[p], out_ref.at[p], local_sem)
        mine.start()

        def copy(k, slab_from, slab_to, peer):
            return pltpu.make_async_remote_copy(
                src_ref=src_ref.at[slab_from], dst_ref=out_ref.at[slab_to], send_sem=send_sems.at[k],
                recv_sem=recv_sems.at[k], device_id=(peer[0], peer[1], c), device_id_type=MESH)

        sends = [copy(k, 2 * px + py, p, (px, py)) for k, (px, py) in enumerate(peers)]
        for s in sends:
            s.start()
        for k, (px, py) in enumerate(peers):
            copy(k, p, 2 * px + py, (px, py)).wait_recv()
        for s in sends:
            s.wait_send()
        mine.wait()

    return pl.pallas_call(
        body, name=name, in_specs=[_HBM], out_specs=_HBM, out_shape=jax.ShapeDtypeStruct(src.shape, src.dtype),
        scratch_shapes=[pltpu.SemaphoreType.DMA((3,)), pltpu.SemaphoreType.DMA((3,)), pltpu.SemaphoreType.DMA],
    )(src)


def _all_gather_chips(name, shard):
    def body(src_ref, out_ref, send_sems, recv_sems, local_sem):
        p, (x, y, c), peers = _chip_peers()
        mine = pltpu.make_async_copy(src_ref, out_ref.at[p], local_sem)
        mine.start()

        def copy(k, slab, peer):
            return pltpu.make_async_remote_copy(
                src_ref=src_ref, dst_ref=out_ref.at[slab], send_sem=send_sems.at[k], recv_sem=recv_sems.at[k],
                device_id=(peer[0], peer[1], c), device_id_type=MESH)

        sends = [copy(k, p, peer) for k, peer in enumerate(peers)]
        for s in sends:
            s.start()
        for k, (px, py) in enumerate(peers):
            copy(k, 2 * px + py, (px, py)).wait_recv()
        for s in sends:
            s.wait_send()
        mine.wait()

    return pl.pallas_call(
        body, name=name, in_specs=[_HBM], out_specs=_HBM,
        out_shape=jax.ShapeDtypeStruct((N_CHIPS,) + shard.shape, shard.dtype),
        scratch_shapes=[pltpu.SemaphoreType.DMA((3,)), pltpu.SemaphoreType.DMA((3,)), pltpu.SemaphoreType.DMA],
    )(shard)


def _swap_cores(name, v):
    def body(v_ref, out_ref, send_sem, recv_sem):
        x, y, c = lax.axis_index("x"), lax.axis_index("y"), lax.axis_index("c")
        cp = pltpu.make_async_remote_copy(src_ref=v_ref, dst_ref=out_ref, send_sem=send_sem, recv_sem=recv_sem,
                                          device_id=(x, y, 1 - c), device_id_type=MESH)
        cp.start()
        cp.wait()

    return pl.pallas_call(
        body, name=name, in_specs=[_HBM], out_specs=_HBM, out_shape=jax.ShapeDtypeStruct(v.shape, v.dtype),
        scratch_shapes=[pltpu.SemaphoreType.DMA, pltpu.SemaphoreType.DMA],
    )(v)


_WEIGHTS = ("mixer_norm", "ffn_norm", "attn_w_in", "attn_q_gain", "attn_k_gain", "attn_w_out", "conv_w_in",
            "conv_b_in", "conv_dw_w", "conv_dw_b", "conv_ln_g", "conv_ln_b", "conv_w_out", "conv_b_out",
            "hgrn_w_in", "hgrn_lb_logits", "hgrn_norm_g", "hgrn_w_out", "ffn_w_up", "ffn_conv_w", "ffn_conv_b",
            "ffn_w_down")
_SHARD_AXIS = {"attn_w_in": 2, "attn_w_out": 2, "conv_w_in": 2, "conv_dw_w": 2, "conv_w_out": 1, "hgrn_w_in": 2,
               "hgrn_norm_g": 1, "hgrn_w_out": 1, "ffn_w_up": 2, "ffn_conv_w": 2, "ffn_w_down": 1}
_MATMUL_WEIGHTS = ("attn_w_in", "attn_w_out", "conv_w_in", "conv_w_out", "hgrn_w_in", "hgrn_w_out", "ffn_w_up",
                   "ffn_w_down")
PACK_COLS = 1024
PACK_ROWS = 16


def _pack(arrays, nlead, dtype):
    lead = arrays[0].shape[:nlead]
    flat = []
    for a in arrays:
        f = a.reshape(lead + (-1,)).astype(dtype)
        flat.append(jnp.pad(f, [(0, 0)] * nlead + [(0, (-f.shape[-1]) % PACK_COLS)]))
    buf = jnp.concatenate(flat, axis=-1)
    buf = jnp.pad(buf, [(0, 0)] * nlead + [(0, (-buf.shape[-1]) % (PACK_COLS * PACK_ROWS))])
    return buf.reshape(lead + (-1, PACK_COLS))


def _unpack(buf, shapes, nlead):
    lead = buf.shape[:nlead]
    flat = buf.reshape(lead + (-1,))
    out, off = [], 0
    for shape in shapes:
        n = 1
        for s in shape:
            n *= s
        out.append(flat[..., off:off + n].reshape(lead + tuple(shape)))
        off += n + (-n) % PACK_COLS
    return out


def _merge_shards(piece, axis):
    moved = jnp.moveaxis(piece, 0, axis)
    shape = moved.shape
    return moved.reshape(shape[:axis] + (shape[axis] * shape[axis + 1],) + shape[axis + 2:])


def _split_shards(full, axis):
    shape = full.shape
    cut = full.reshape(shape[:axis] + (N_CHIPS, shape[axis] // N_CHIPS) + shape[axis + 1:])
    return jnp.moveaxis(cut, axis, 0)


def _gather_weights(local):
    big = [n for n in _WEIGHTS if n in _MATMUL_WEIGHTS]
    small = [n for n in _WEIGHTS if n in _SHARD_AXIS and n not in _MATMUL_WEIGHTS]
    full = {n: local[n] for n in _WEIGHTS if n not in _SHARD_AXIS}
    for names, dtype, tag in ((big, BF16, "comm_gather_matmul_weights"), (small, F32, "comm_gather_small_weights")):
        gathered = _all_gather_chips(tag, _pack([local[n] for n in names], 0, dtype))
        pieces = _unpack(gathered, [local[n].shape for n in names], 1)
        for n, piece in zip(names, pieces):
            full[n] = _merge_shards(piece, _SHARD_AXIS[n])
    return full


def _reduce_gradients(grads, local):
    slabs = []
    for n in _WEIGHTS:
        g = grads[n]
        if n in _SHARD_AXIS:
            slabs.append(_split_shards(g, _SHARD_AXIS[n]))
        else:
            slabs.append(jnp.broadcast_to(g[None], (N_CHIPS,) + g.shape))
    packed = _pack(slabs, 1, F32)
    landed = _exchange_chips("comm_scatter_gradients", packed)
    partial = _sum_slabs("sum_chips", landed)
    other = _swap_cores("comm_swap_partial_sums", partial)
    total = _add("sum_cores", [partial, other])
    return dict(zip(_WEIGHTS, _unpack(total, [local[n].shape for n in _WEIGHTS], 0)))


def kernel(x, mixer_norm, ffn_norm, attn_w_in, attn_q_gain, attn_k_gain, attn_w_out, conv_w_in, conv_b_in, conv_dw_w, conv_dw_b, conv_ln_g, conv_ln_b, conv_w_out, conv_b_out, hgrn_w_in, hgrn_lb_logits, hgrn_norm_g, hgrn_w_out, ffn_w_up, ffn_conv_w, ffn_conv_b, ffn_w_down, loss_target, m_mixer_norm, m_ffn_norm, m_attn_w_in, m_attn_q_gain, m_attn_k_gain, m_attn_w_out, m_conv_w_in, m_conv_b_in, m_conv_dw_w, m_conv_dw_b, m_conv_ln_g, m_conv_ln_b, m_conv_w_out, m_conv_b_out, m_hgrn_w_in, m_hgrn_lb_logits, m_hgrn_norm_g, m_hgrn_w_out, m_ffn_w_up, m_ffn_conv_w, m_ffn_conv_b, m_ffn_w_down, v_mixer_norm, v_ffn_norm, v_attn_w_in, v_attn_q_gain, v_attn_k_gain, v_attn_w_out, v_conv_w_in, v_conv_b_in, v_conv_dw_w, v_conv_dw_b, v_conv_ln_g, v_conv_ln_b, v_conv_w_out, v_conv_b_out, v_hgrn_w_in, v_hgrn_lb_logits, v_hgrn_norm_g, v_hgrn_w_out, v_ffn_w_up, v_ffn_conv_w, v_ffn_conv_b, v_ffn_w_down):
    given = dict(locals())
    local = {n: given[n] for n in _WEIGHTS}
    full = _gather_weights(local)
    loss, dx, grads = _local_step(x[0], loss_target[0], full)
    loss = lax.psum(loss, ("x", "y", "c"))
    grad = _reduce_gradients(grads, local)
    delta, new_m, new_v = {}, {}, {}
    for n in _WEIGHTS:
        shape = local[n].shape
        as2d = lambda a: a.reshape(-1, shape[-1])
        d, m, v = _adamw(f"adamw_{n}", as2d(local[n]), as2d(grad[n]), as2d(given["m_" + n]), as2d(given["v_" + n]))
        delta[n], new_m[n], new_v[n] = d.reshape(shape), m.reshape(shape), v.reshape(shape)
    return (loss, dx[None], *[grad[n] for n in _WEIGHTS], *[delta[n] for n in _WEIGHTS],
            *[new_m[n] for n in _WEIGHTS], *[new_v[n] for n in _WEIGHTS])
```

```python
import functools

import jax
import jax.numpy as jnp
from jax import lax
from jax.experimental import pallas as pl
from jax.experimental.pallas import tpu as pltpu

F32 = jnp.float32
BF16 = jnp.bfloat16

EPS = 1e-6
N_MIXERS = 3
ATTN_DILATIONS = (1, 4, 16)
ATTN_BLOCK = 128
ATTN_HEADS = 8
ATTN_HEAD_DIM = 64
ATTN_GW = ATTN_HEADS * ATTN_HEAD_DIM
HGRN_HEAD = 128
HGRN_CHUNK = 16
HGRN_TILE = 256
HGRN_GROUP = 4
ADAM_LR, ADAM_B1, ADAM_B2, ADAM_EPS, ADAM_WD, ADAM_STEP = 0.001, 0.9, 0.999, 1e-08, 0.01, 10

LANES = 128
VMEM_LIMIT = 56 * 1024 * 1024
N_CHIPS = 4
MESH = pl.DeviceIdType.MESH

HI = lax.Precision.HIGHEST


def _params(*sem):
    return pltpu.CompilerParams(dimension_semantics=sem, vmem_limit_bytes=VMEM_LIMIT)


def _pick(n, target):
    if n <= target:
        return n
    best = None
    for t in range(LANES, target + 1, LANES):
        if n % t == 0:
            best = t
    assert best is not None, (n, target)
    return best


def _pick_rows(n, target):
    if n <= target:
        return n
    for t in range(target, 15, -16):
        if n % t == 0:
            return t
    return n


def _dot(a, b, dims, precision=None):
    return lax.dot_general(a, b, (dims, ((), ())), precision=precision, preferred_element_type=F32)


def _nn(a, b, precision=None):
    return _dot(a, b, ((1,), (0,)), precision)


def _nt(a, b, precision=None):
    return _dot(a, b, ((1,), (1,)), precision)


def _tn(a, b, precision=None):
    return _dot(a, b, ((0,), (0,)), precision)


def _sigmoid(x):
    return 1.0 / (1.0 + jnp.exp(-x))


ROWWISE_UNROLL_ROWS = 64


def _rowwise(name, fn, rows, pars=(), outs=(), accs=(), *, tc=None, tm=512, rb=16):
    S = rows[0][0].shape[0]
    tm = _pick_rows(S, tm)
    rb = rb if tm % rb == 0 else tm
    width = tc if tc is not None else None
    ncol = 1
    if tc is not None:
        base = outs[0][0] if outs else accs[0][1]
        ncol = base // tc
    n_r, n_p, n_o, n_a = len(rows), len(pars), len(outs), len(accs)

    def body(*refs):
        row_refs, par_refs = refs[:n_r], refs[n_r:n_r + n_p]
        out_refs, acc_refs = refs[n_r + n_p:n_r + n_p + n_o], refs[n_r + n_p + n_o:]
        if n_a:
            @pl.when(pl.program_id(1) == 0)
            def _():
                for a in acc_refs:
                    a[...] = jnp.zeros_like(a)

        def step(s, carry):
            sl = pl.ds(pl.multiple_of(s * rb, rb), rb)
            res = fn(*[r[sl, :] for r in row_refs], *[p[...] for p in par_refs])
            res = res if isinstance(res, tuple) else (res,)
            for o, v in zip(out_refs, res[:n_o]):
                o[sl, :] = v.astype(o.dtype)
            for a, v in zip(acc_refs, res[n_o:]):
                a[...] += v
            return carry

        lax.fori_loop(0, tm // rb, step, 0, unroll=max(1, min(tm // rb, ROWWISE_UNROLL_ROWS // rb)))

    def row_spec(c, off):
        if tc is None:
            return pl.BlockSpec((tm, c), lambda j, i: (i, 0))
        return pl.BlockSpec((tm, tc), lambda j, i, o=off // tc: (i, j + o))

    def par_spec(shape, off):
        if off is None or tc is None:
            return pl.BlockSpec(shape, lambda j, i: (0, 0))
        return pl.BlockSpec((shape[0], tc), lambda j, i, o=off // tc: (0, j + o))

    in_specs = [row_spec(a.shape[1], off) for a, off in rows]
    in_specs += [par_spec(a.shape, off) for a, off in pars]
    out_specs = [row_spec(c, 0) for c, _ in outs] + [par_spec(s, 0) for s in accs]
    out_shape = [jax.ShapeDtypeStruct((S, c), d) for c, d in outs]
    out_shape += [jax.ShapeDtypeStruct(s, F32) for s in accs]
    res = pl.pallas_call(
        body, name=name, grid=(ncol, S // tm), in_specs=in_specs, out_specs=out_specs, out_shape=out_shape,
        compiler_params=_params("parallel", "arbitrary" if n_a else "parallel"),
    )(*[a for a, _ in rows], *[a for a, _ in pars])
    return res[0] if len(res) == 1 else tuple(res)


MATMUL_VMEM = 36 * 1024 * 1024


def _matmul_tiles(M, N, pairs, out_dtype, residual):
    tm = _pick_rows(M, 512)
    for tn in sorted({_pick(N, t) for t in range(LANES, 2049, LANES)}, reverse=True):
        step = sum(tm * a.shape[1] * a.dtype.itemsize + a.shape[1] * tn * b.dtype.itemsize for a, b in pairs)
        step += tm * tn * (jnp.dtype(out_dtype).itemsize + (4 if residual is not None else 0))
        if 2 * step <= MATMUL_VMEM:
            return tm, tn
    return tm, LANES


def _matmul(name, pairs, *, trans_b=False, bias=None, residual=None, out_dtype=F32):
    M = pairs[0][0].shape[0]
    N = pairs[0][1].shape[0] if trans_b else pairs[0][1].shape[1]
    tm, tn = _matmul_tiles(M, N, pairs, out_dtype, residual)
    n = len(pairs)

    def body(*refs):
        acc = None
        for i in range(n):
            a = refs[2 * i][...].astype(BF16)
            b = refs[2 * i + 1][...].astype(BF16)
            d = _nt(a, b) if trans_b else _nn(a, b)
            acc = d if acc is None else acc + d
        k = 2 * n
        if bias is not None:
            acc = acc + refs[k][...]
            k += 1
        if residual is not None:
            acc = acc + refs[k][...]
            k += 1
        refs[k][...] = acc.astype(out_dtype)

    in_specs, args = [], []
    for a, b in pairs:
        K = a.shape[1]
        in_specs.append(pl.BlockSpec((tm, K), lambda j, i: (i, 0)))
        in_specs.append(pl.BlockSpec((tn, K), lambda j, i: (j, 0)) if trans_b
                        else pl.BlockSpec((K, tn), lambda j, i: (0, j)))
        args += [a, b]
    if bias is not None:
        in_specs.append(pl.BlockSpec((1, tn), lambda j, i: (0, j)))
        args.append(bias)
    if residual is not None:
        in_specs.append(pl.BlockSpec((tm, tn), lambda j, i: (i, j)))
        args.append(residual)
    return pl.pallas_call(
        body, name=name, grid=(N // tn, M // tm), in_specs=in_specs,
        out_specs=pl.BlockSpec((tm, tn), lambda j, i: (i, j)),
        out_shape=jax.ShapeDtypeStruct((M, N), out_dtype), compiler_params=_params("parallel", "parallel"),
    )(*args)


def _matmul_tn(name, a, b, *, tm=1408, tn=1408, tk=512):
    S, M = a.shape
    N = b.shape[1]
    tm, tn, tk = _pick(M, tm), _pick(N, tn), _pick_rows(S, tk)

    def body(a_ref, b_ref, o_ref):
        @pl.when(pl.program_id(2) == 0)
        def _():
            o_ref[...] = jnp.zeros_like(o_ref)

        o_ref[...] += _tn(a_ref[...].astype(BF16), b_ref[...].astype(BF16))

    return pl.pallas_call(
        body, name=name, grid=(M // tm, N // tn, S // tk),
        in_specs=[pl.BlockSpec((tk, tm), lambda i, j, k: (k, i)), pl.BlockSpec((tk, tn), lambda i, j, k: (k, j))],
        out_specs=pl.BlockSpec((tm, tn), lambda i, j, k: (i, j)),
        out_shape=jax.ShapeDtypeStruct((M, N), F32), compiler_params=_params("parallel", "parallel", "arbitrary"),
    )(a, b)


def _halo_rows(K):
    return 8 if K <= 9 else 32


def _dwconv(name, x, w, b, *, reverse, out_dtype=F32):
    S, C = x.shape
    K = w.shape[0]
    H = _halo_rows(K)
    tm, tc = _pick_rows(S, 512 if K <= 4 else 256), _pick(C, 1408 if K <= 4 else 256)
    nrow = S // tm
    RB = 16 if out_dtype == BF16 else 8

    def body(x_ref, h_ref, w_ref, b_ref, o_ref, ext):
        i = pl.program_id(1)
        edge = (i == nrow - 1) if reverse else (i == 0)
        halo = jnp.where(edge, 0.0, h_ref[...].astype(F32))
        if reverse:
            ext[0:tm, :] = x_ref[...].astype(F32)
            ext[tm:tm + H, :] = halo
        else:
            ext[0:H, :] = halo
            ext[H:H + tm, :] = x_ref[...].astype(F32)
        wv = w_ref[...]
        for s in range(tm // RB):
            acc = jnp.broadcast_to(b_ref[...], (RB, tc))
            for k in range(K):
                off = s * RB + ((K - 1 - k) if reverse else (H - (K - 1) + k))
                acc = acc + wv[k:k + 1, :] * ext[off:off + RB, :]
            o_ref[s * RB:(s + 1) * RB, :] = acc.astype(out_dtype)

    r = tm // H
    if reverse:
        halo_map = lambda j, i: (jnp.minimum((i + 1) * r, S // H - 1), j)
    else:
        halo_map = lambda j, i: (jnp.maximum(i * r - 1, 0), j)
    return pl.pallas_call(
        body, name=name, grid=(C // tc, nrow),
        in_specs=[pl.BlockSpec((tm, tc), lambda j, i: (i, j)), pl.BlockSpec((H, tc), halo_map),
                  pl.BlockSpec((K, tc), lambda j, i: (0, j)), pl.BlockSpec((1, tc), lambda j, i: (0, j))],
        out_specs=pl.BlockSpec((tm, tc), lambda j, i: (i, j)),
        out_shape=jax.ShapeDtypeStruct((S, C), out_dtype),
        scratch_shapes=[pltpu.VMEM((tm + H, tc), F32)], compiler_params=_params("parallel", "parallel"),
    )(x, x, w, b)


def _dwconv_wgrad(name, x, dy, K):
    S, C = x.shape
    H = _halo_rows(K)
    tm, tc = _pick_rows(S, 512 if K <= 4 else 256), _pick(C, 512 if K <= 4 else LANES)
    RB = 8

    def body(x_ref, h_ref, dy_ref, dw_ref, db_ref, ext):
        i = pl.program_id(1)

        @pl.when(i == 0)
        def _():
            dw_ref[...] = jnp.zeros_like(dw_ref)
            db_ref[...] = jnp.zeros_like(db_ref)

        ext[0:H, :] = jnp.where(i == 0, 0.0, h_ref[...].astype(F32))
        ext[H:H + tm, :] = x_ref[...].astype(F32)
        acc = [jnp.zeros((RB, tc), F32) for _ in range(K)]
        accb = jnp.zeros((RB, tc), F32)
        for s in range(tm // RB):
            d = dy_ref[s * RB:(s + 1) * RB, :].astype(F32)
            accb = accb + d
            for k in range(K):
                off = s * RB + H - (K - 1) + k
                acc[k] = acc[k] + d * ext[off:off + RB, :]
        for k in range(K):
            dw_ref[k:k + 1, :] += jnp.sum(acc[k], axis=0, keepdims=True)
        db_ref[...] += jnp.sum(accb, axis=0, keepdims=True)

    r = tm // H
    return pl.pallas_call(
        body, name=name, grid=(C // tc, S // tm),
        in_specs=[pl.BlockSpec((tm, tc), lambda j, i: (i, j)),
                  pl.BlockSpec((H, tc), lambda j, i: (jnp.maximum(i * r - 1, 0), j)),
                  pl.BlockSpec((tm, tc), lambda j, i: (i, j))],
        out_specs=[pl.BlockSpec((K, tc), lambda j, i: (0, j)), pl.BlockSpec((1, tc), lambda j, i: (0, j))],
        out_shape=[jax.ShapeDtypeStruct((K, C), F32), jax.ShapeDtypeStruct((1, C), F32)],
        scratch_shapes=[pltpu.VMEM((tm + H, tc), F32)], compiler_params=_params("parallel", "arbitrary"),
    )(x, x, dy)


FFN_HALO = 16
FFN_DY_HALO = 8


def _conv_taps(w, b, ext, r0, rows, cs):
    K = w.shape[0]
    acc = b
    for k in range(K):
        off = r0 - (K - 1) + k
        acc = acc + w[k:k + 1, :] * ext[off:off + rows, cs]
    return acc


def _ffn_up_fused(name, h, w_up, cw, cb):
    S, D = h.shape
    F = w_up.shape[1] // 2
    tm, tn = _pick_rows(S, 512), _pick(F, 1408)
    nj, H, RB = F // tn, FFN_HALO, 16

    def body(h_ref, hh_ref, wg_ref, wu_ref, cwg_ref, cwu_ref, cbg_ref, cbu_ref, a_ref, u0g_ref, u0u_ref, eg, eu):
        first = pl.program_id(1) == 0
        hv, halo = h_ref[...], hh_ref[...]
        for w_ref, u0_ref, e in ((wg_ref, u0g_ref, eg), (wu_ref, u0u_ref, eu)):
            w = w_ref[...]
            u0 = _nn(hv, w)
            u0_ref[...] = u0.astype(BF16)
            e[0:H, :] = jnp.where(first, 0.0, _nn(halo, w))
            e[H:H + tm, :] = u0
        for c in range(tn // LANES):
            cs = slice(c * LANES, (c + 1) * LANES)
            wg, wu, bg, bu = cwg_ref[:, cs], cwu_ref[:, cs], cbg_ref[:, cs], cbu_ref[:, cs]
            for s in range(tm // RB):
                ug = _conv_taps(wg, bg, eg, H + s * RB, RB, cs)
                uu = _conv_taps(wu, bu, eu, H + s * RB, RB, cs)
                a_ref[s * RB:(s + 1) * RB, cs] = (ug * _sigmoid(ug) * uu).astype(BF16)

    r = tm // H
    gate = lambda rows: pl.BlockSpec((rows, tn), lambda j, i: (0, j))
    up = lambda rows: pl.BlockSpec((rows, tn), lambda j, i: (0, j + nj))
    tile = pl.BlockSpec((tm, tn), lambda j, i: (i, j))
    K = cw.shape[0]
    return pl.pallas_call(
        body, name=name, grid=(nj, S // tm),
        in_specs=[pl.BlockSpec((tm, D), lambda j, i: (i, 0)),
                  pl.BlockSpec((H, D), lambda j, i: (jnp.maximum(i * r - 1, 0), 0)),
                  gate(D), up(D), gate(K), up(K), gate(1), up(1)],
        out_specs=[tile, tile, tile], out_shape=[jax.ShapeDtypeStruct((S, F), BF16)] * 3,
        scratch_shapes=[pltpu.VMEM((H + tm, tn), F32)] * 2, compiler_params=_params("parallel", "parallel"),
    )(h, h, w_up, w_up, cw, cw, cb, cb)


def _ffn_gate_bwd_fused(name, dy, w_down, u0g, u0u, cw, cb):
    S, D = dy.shape
    F, K = w_down.shape[0], cw.shape[0]
    tm, tn = _pick_rows(S, 512), _pick(F, 1408)
    nj, nrow, H, HD = F // tn, S // tm, FFN_HALO, FFN_DY_HALO

    def body(dy_ref, dyn_ref, wd_ref, g_ref, gp_ref, gn_ref, u_ref, up_ref, un_ref, cwg_ref, cwu_ref, cbg_ref, cbu_ref,
             dg_ref, du_ref, dcwg_ref, dcwu_ref, dcbg_ref, dcbu_ref, eg, eu, dg_s, du_s, da_s):
        i = pl.program_id(1)
        first, last = i == 0, i == nrow - 1

        @pl.when(first)
        def _():
            for ref in (dcwg_ref, dcwu_ref, dcbg_ref, dcbu_ref):
                ref[...] = jnp.zeros_like(ref)

        wd = wd_ref[...]
        da_s[0:tm, :] = _nt(dy_ref[...].astype(BF16), wd)
        da_s[tm:tm + HD, :] = jnp.where(last, 0.0, _nt(dyn_ref[...].astype(BF16), wd))
        for e, cur, prv, nxt in ((eg, g_ref, gp_ref, gn_ref), (eu, u_ref, up_ref, un_ref)):
            e[0:H, :] = jnp.where(first, 0.0, prv[...].astype(F32))
            e[H:H + tm, :] = cur[...].astype(F32)
            e[H + tm:H + tm + H, :] = jnp.where(last, 0.0, nxt[...].astype(F32))
        for c in range(tn // LANES):
            cs = slice(c * LANES, (c + 1) * LANES)
            wg, wu, bg, bu = cwg_ref[:, cs], cwu_ref[:, cs], cbg_ref[:, cs], cbu_ref[:, cs]
            for s in range(tm // 8 + 1):
                rows = slice(s * 8, (s + 1) * 8)
                ug = _conv_taps(wg, bg, eg, H + s * 8, 8, cs)
                uu = _conv_taps(wu, bu, eu, H + s * 8, 8, cs)
                da = da_s[rows, cs]
                sg = _sigmoid(ug)
                dg_s[rows, cs] = da * uu * (sg * (1.0 + ug * (1.0 - sg)))
                du_s[rows, cs] = da * (ug * sg)
            for d_s, e, w, out_ref, dcw_ref, dcb_ref in ((dg_s, eg, wg, dg_ref, dcwg_ref, dcbg_ref),
                                                        (du_s, eu, wu, du_ref, dcwu_ref, dcbu_ref)):
                acc = [jnp.zeros((8, LANES), F32) for _ in range(K)]
                accb = jnp.zeros((8, LANES), F32)
                for s in range(tm // 16):
                    halves = []
                    for r0 in (s * 16, s * 16 + 8):
                        d = d_s[r0:r0 + 8, cs]
                        accb = accb + d
                        t = None
                        for k in range(K):
                            acc[k] = acc[k] + d * e[H + r0 - (K - 1) + k:H + r0 - (K - 1) + k + 8, cs]
                            term = w[k:k + 1, :] * d_s[r0 + (K - 1) - k:r0 + (K - 1) - k + 8, cs]
                            t = term if t is None else t + term
                        halves.append(t)
                    out_ref[s * 16:(s + 1) * 16, cs] = jnp.concatenate(halves, axis=0).astype(BF16)
                for k in range(K):
                    dcw_ref[k:k + 1, cs] += jnp.sum(acc[k], axis=0, keepdims=True)
                dcb_ref[:, cs] += jnp.sum(accb, axis=0, keepdims=True)

    r, rd = tm // H, tm // HD
    gate = lambda rows: pl.BlockSpec((rows, tn), lambda j, i: (0, j))
    up = lambda rows: pl.BlockSpec((rows, tn), lambda j, i: (0, j + nj))
    tile = pl.BlockSpec((tm, tn), lambda j, i: (i, j))
    prev = pl.BlockSpec((H, tn), lambda j, i: (jnp.maximum(i * r - 1, 0), j))
    nxt = pl.BlockSpec((H, tn), lambda j, i: (jnp.minimum((i + 1) * r, S // H - 1), j))
    acc_w, acc_b = pl.BlockSpec((K, tn), lambda j, i: (0, j)), pl.BlockSpec((1, tn), lambda j, i: (0, j))
    return pl.pallas_call(
        body, name=name, grid=(nj, nrow),
        in_specs=[pl.BlockSpec((tm, D), lambda j, i: (i, 0)),
                  pl.BlockSpec((HD, D), lambda j, i: (jnp.minimum((i + 1) * rd, S // HD - 1), 0)),
                  pl.BlockSpec((tn, D), lambda j, i: (j, 0)),
                  tile, prev, nxt, tile, prev, nxt, gate(K), up(K), gate(1), up(1)],
        out_specs=[tile, tile, acc_w, acc_w, acc_b, acc_b],
        out_shape=[jax.ShapeDtypeStruct((S, F), BF16)] * 2 + [jax.ShapeDtypeStruct((K, F), F32)] * 2
        + [jax.ShapeDtypeStruct((1, F), F32)] * 2,
        scratch_shapes=[pltpu.VMEM((H + tm + H, tn), F32)] * 2 + [pltpu.VMEM((tm + HD, tn), F32)] * 3,
        compiler_params=_params("parallel", "arbitrary"),
    )(dy, dy, w_down, u0g, u0g, u0g, u0u, u0u, u0u, cw, cw, cb, cb)


def _colsum(v):
    return jnp.sum(v, axis=0, keepdims=True)


def _rmsnorm_fwd(name, x, gain):
    def fn(x, g):
        r = lax.rsqrt(jnp.mean(x * x, axis=-1, keepdims=True) + EPS)
        return x * r * g
    return _rowwise(name, fn, [(x, 0)], [(gain, None)], [(x.shape[1], BF16)])


def _rmsnorm_bwd(name, x, gain, dh, dres):
    def fn(x, dh, dres, g):
        r = lax.rsqrt(jnp.mean(x * x, axis=-1, keepdims=True) + EPS)
        xh = x * r
        dxh = dh * g
        dx = r * (dxh - xh * jnp.mean(dxh * xh, axis=-1, keepdims=True))
        return dres + dx, _colsum(dh * xh)
    D = x.shape[1]
    return _rowwise(name, fn, [(x, 0), (dh, 0), (dres, 0)], [(gain, None)], [(D, F32)], [(1, D)])


def _silu_gate_fwd(name, gate, up):
    F = gate.shape[1]
    def fn(g, up):
        return g * _sigmoid(g) * up
    return _rowwise(name, fn, [(gate, 0), (up, 0)], [], [(F, BF16)], tc=_pick(F, 512))


def _silu_gate_bwd(name, gate, up, da):
    F = gate.shape[1]
    def fn(g, up, da):
        s = _sigmoid(g)
        return da * up * (s * (1.0 + g * (1.0 - s))), da * (g * s)
    return _rowwise(name, fn, [(gate, 0), (up, 0), (da, 0)], [], [(F, F32), (F, F32)], tc=_pick(F, 512))


def _glu_fwd(name, a, gate):
    C = a.shape[1]
    def fn(a, g):
        return a * _sigmoid(g)
    return _rowwise(name, fn, [(a, 0), (gate, 0)], [], [(C, F32)], tc=_pick(C, 512))


def _glu_bwd(name, a, gate, dglu):
    C = a.shape[1]
    def fn(a, g, d):
        s = _sigmoid(g)
        da, dg = d * s, d * a * s * (1.0 - s)
        return da, dg, _colsum(da), _colsum(dg)
    return _rowwise(name, fn, [(a, 0), (gate, 0), (dglu, 0)], [], [(C, BF16), (C, BF16)], [(1, C), (1, C)],
                    tc=_pick(C, 512))


def _ln_silu_fwd(name, c, g, b):
    def fn(c, g, b):
        mu = jnp.mean(c, axis=-1, keepdims=True)
        d = c - mu
        n = d * lax.rsqrt(jnp.mean(d * d, axis=-1, keepdims=True) + EPS) * g + b
        return n * _sigmoid(n)
    return _rowwise(name, fn, [(c, 0)], [(g, None), (b, None)], [(c.shape[1], BF16)])


def _ln_silu_bwd(name, c, g, b, dsw):
    def fn(c, dsw, g, b):
        mu = jnp.mean(c, axis=-1, keepdims=True)
        d = c - mu
        r = lax.rsqrt(jnp.mean(d * d, axis=-1, keepdims=True) + EPS)
        ch = d * r
        n = ch * g + b
        s = _sigmoid(n)
        dn = dsw * (s * (1.0 + n * (1.0 - s)))
        dch = dn * g
        dc = r * (dch - jnp.mean(dch, axis=-1, keepdims=True) - ch * jnp.mean(dch * ch, axis=-1, keepdims=True))
        return dc, _colsum(dn * ch), _colsum(dn)
    C = c.shape[1]
    return _rowwise(name, fn, [(c, 0), (dsw, 0)], [(g, None), (b, None)], [(C, F32)], [(1, C), (1, C)])


def _column_sums(name, x):
    return _rowwise(name, lambda x: (_colsum(x),), [(x, 0)], [], [], [(1, x.shape[1])])


def _loss_grad(name, y, target):
    D = y.shape[1]
    def fn(y, t):
        e = y - t
        return e * (1.0 / D), _colsum(e * e) * (0.5 / D)
    return _rowwise(name, fn, [(y, 0), (target, 0)], [], [(D, F32)], [(1, D)])


def _add(name, arrays):
    def fn(*xs):
        acc = xs[0]
        for x in xs[1:]:
            acc = acc + x
        return acc
    return _rowwise(name, fn, [(a, 0) for a in arrays], [], [(arrays[0].shape[1], F32)], tm=256)


def _sum_slabs(name, stacked):
    n, R, C = stacked.shape
    tm = _pick_rows(R, 256)

    def body(*refs):
        acc = refs[0][0].astype(F32)
        for r in refs[1:n]:
            acc = acc + r[0].astype(F32)
        refs[n][...] = acc

    return pl.pallas_call(
        body, name=name, grid=(R // tm,),
        in_specs=[pl.BlockSpec((1, tm, C), lambda i, q=q: (q, i, 0)) for q in range(n)],
        out_specs=pl.BlockSpec((tm, C), lambda i: (i, 0)), out_shape=jax.ShapeDtypeStruct((R, C), F32),
        compiler_params=_params("parallel"),
    )(*[stacked] * n)


def _adamw(name, w, g, m, v):
    c1 = 1.0 - ADAM_B1 ** ADAM_STEP
    c2 = 1.0 - ADAM_B2 ** ADAM_STEP
    def fn(w, g, m, v):
        m = ADAM_B1 * m + (1.0 - ADAM_B1) * g
        v = ADAM_B2 * v + (1.0 - ADAM_B2) * (g * g)
        delta = -ADAM_LR * ((m / c1) / (jnp.sqrt(v / c2) + ADAM_EPS) + ADAM_WD * w)
        return delta, m, v
    C = w.shape[1]
    return _rowwise(name, fn, [(w, 0), (g, 0), (m, 0), (v, 0)], [], [(C, F32)] * 3, tm=256)


SEG_ROWS = 128


def _segment_matrix(n, seg):
    i = jnp.arange(n) // seg
    return (i[:, None] == i[None, :]).astype(BF16)


def _seg_sum(v, B):
    hi = v.astype(BF16)
    lo = (v - hi.astype(F32)).astype(BF16)
    return _nn(hi, B) + _nn(lo, B)


def _qknorm_fwd(name, proj, gain_full, is_norm, seg):
    def fn(x, gf, isn, B):
        ms = _seg_sum(x * x, B) * (1.0 / ATTN_HEAD_DIM)
        r = lax.rsqrt(ms + EPS)
        return x * (isn * r + (1.0 - isn)) * gf
    W = proj.shape[1]
    return _rowwise(name, fn, [(proj, 0)], [(gain_full, 0), (is_norm, 0), (seg, None)], [(W, BF16)],
                    tc=ATTN_GW, rb=SEG_ROWS)


def _qknorm_bwd(name, proj, dy, gain_full, is_norm, seg):
    def fn(x, dy, gf, isn, B):
        ms = _seg_sum(x * x, B) * (1.0 / ATTN_HEAD_DIM)
        r = lax.rsqrt(ms + EPS)
        xh = x * r
        dxh = dy * gf
        dn = r * (dxh - xh * (_seg_sum(dxh * xh, B) * (1.0 / ATTN_HEAD_DIM)))
        return isn * dn + (1.0 - isn) * dxh, _colsum(dy * xh)
    W = proj.shape[1]
    return _rowwise(name, fn, [(proj, 0), (dy, 0)], [(gain_full, 0), (is_norm, 0), (seg, None)],
                    [(W, BF16)], [(1, W)], tc=ATTN_GW, rb=SEG_ROWS)


def _attn_masks():
    shape = (ATTN_BLOCK, ATTN_BLOCK)
    row = lax.broadcasted_iota(jnp.int32, shape, 0)
    col = lax.broadcasted_iota(jnp.int32, shape, 1)
    low_lanes = col < ATTN_HEAD_DIM
    return col <= row, col >= row, low_lanes


def _attn_group_fwd(name, qkv, d):
    S = qkv.shape[0]
    n, W, G = S // d, 3 * ATTN_GW, ATTN_GW
    nb = n // ATTN_BLOCK
    view = qkv.reshape(n, d * W)

    def body(cur, prev, o_ref, l_ref):
        b = pl.program_id(1)
        cur_mask, prev_mask, low = _attn_masks()
        prev_mask = jnp.logical_and(prev_mask, b > 0)
        for pr in range(G // LANES):
            c0 = pr * LANES
            q2, kc, vc = cur[:, c0:c0 + LANES], cur[:, G + c0:G + c0 + LANES], cur[:, 2 * G + c0:2 * G + c0 + LANES]
            kp, vp = prev[:, G + c0:G + c0 + LANES], prev[:, 2 * G + c0:2 * G + c0 + LANES]
            res = []
            for hm in (low, jnp.logical_not(low)):
                qm = jnp.where(hm, q2, jnp.zeros_like(q2))
                sc = jnp.where(cur_mask, _nt(qm, kc), -jnp.inf)
                sp = jnp.where(prev_mask, _nt(qm, kp), -jnp.inf)
                m = jnp.maximum(jnp.max(sc, axis=1, keepdims=True), jnp.max(sp, axis=1, keepdims=True))
                pc, pp = jnp.exp(sc - m), jnp.exp(sp - m)
                l = jnp.sum(pc, axis=1, keepdims=True) + jnp.sum(pp, axis=1, keepdims=True)
                o = (_nn(pc.astype(BF16), vc) + _nn(pp.astype(BF16), vp)) / l
                res.append((o, jnp.broadcast_to(m + jnp.log(l), o.shape)))
            o_ref[:, c0:c0 + LANES] = jnp.where(low, res[0][0], res[1][0])
            l_ref[:, c0:c0 + LANES] = jnp.where(low, res[0][1], res[1][1])

    o, l = pl.pallas_call(
        body, name=name, grid=(d, nb),
        in_specs=[pl.BlockSpec((ATTN_BLOCK, W), lambda r, b: (b, r)),
                  pl.BlockSpec((ATTN_BLOCK, W), lambda r, b: (jnp.maximum(b - 1, 0), r))],
        out_specs=[pl.BlockSpec((ATTN_BLOCK, G), lambda r, b: (b, r))] * 2,
        out_shape=[jax.ShapeDtypeStruct((n, d * G), F32)] * 2, compiler_params=_params("parallel", "parallel"),
    )(view, view)
    return o.reshape(S, G), l.reshape(S, G)


def _attn_combine(name, os, ls):
    def fn(o1, o2, o3, l1, l2, l3):
        m = jnp.maximum(jnp.maximum(l1, l2), l3)
        e1, e2, e3 = jnp.exp(l1 - m), jnp.exp(l2 - m), jnp.exp(l3 - m)
        den = e1 + e2 + e3
        return (e1 * o1 + e2 * o2 + e3 * o3) / den, m + jnp.log(den)
    G = os[0].shape[1]
    return _rowwise(name, fn, [(a, 0) for a in (*os, *ls)], [], [(G, F32), (G, F32)])


def _attn_delta(name, do, o, seg):
    def fn(do, o, B):
        return _seg_sum(do * o, B)
    return _rowwise(name, fn, [(do, 0), (o, 0)], [(seg, None)], [(o.shape[1], F32)], rb=SEG_ROWS)


def _attn_group_bwd(name, qkv, do, lse, delta, d):
    S = qkv.shape[0]
    n, W, G = S // d, 3 * ATTN_GW, ATTN_GW
    nb = n // ATTN_BLOCK

    def body(qp, qc, qn, do_c, do_n, l_c, l_n, dl_c, dl_n, out):
        j = pl.program_id(1)
        cur_mask, prev_mask, low = _attn_masks()
        next_mask = jnp.logical_and(prev_mask, j < nb - 1)
        prev_mask = jnp.logical_and(prev_mask, j > 0)
        for pr in range(G // LANES):
            c0 = pr * LANES
            q_c, k_c, v_c = qc[:, c0:c0 + LANES], qc[:, G + c0:G + c0 + LANES], qc[:, 2 * G + c0:2 * G + c0 + LANES]
            q_n = qn[:, c0:c0 + LANES]
            k_p, v_p = qp[:, G + c0:G + c0 + LANES], qp[:, 2 * G + c0:2 * G + c0 + LANES]
            d_c, d_n = do_c[:, c0:c0 + LANES].astype(BF16), do_n[:, c0:c0 + LANES].astype(BF16)
            res = []
            for hh, hm in enumerate((low, jnp.logical_not(low))):
                h0 = c0 + hh * ATTN_HEAD_DIM
                lc, ln = l_c[:, h0:h0 + 1], l_n[:, h0:h0 + 1]
                dlc, dln = dl_c[:, h0:h0 + 1], dl_n[:, h0:h0 + 1]
                zero = jnp.zeros_like(q_c)
                qmc, qmn = jnp.where(hm, q_c, zero), jnp.where(hm, q_n, zero)
                dmc, dmn = jnp.where(hm, d_c, zero), jnp.where(hm, d_n, zero)
                p_a = jnp.where(cur_mask, jnp.exp(_nt(qmc, k_c) - lc), 0.0)
                ds_a = p_a * (_nt(dmc, v_c) - dlc)
                p_b = jnp.where(next_mask, jnp.exp(_nt(qmn, k_c) - ln), 0.0)
                ds_b = p_b * (_nt(dmn, v_c) - dln)
                p_c = jnp.where(prev_mask, jnp.exp(_nt(qmc, k_p) - lc), 0.0)
                ds_c = p_c * (_nt(dmc, v_p) - dlc)
                dq = _nn(ds_a.astype(BF16), k_c) + _nn(ds_c.astype(BF16), k_p)
                dk = _tn(ds_a.astype(BF16), q_c) + _tn(ds_b.astype(BF16), q_n)
                dv = _tn(p_a.astype(BF16), d_c) + _tn(p_b.astype(BF16), d_n)
                res.append((dq, dk, dv))
            for t in range(3):
                out[:, t * G + c0:t * G + c0 + LANES] = jnp.where(low, res[0][t], res[1][t])

    prv = lambda r, j: (jnp.maximum(j - 1, 0), r)
    cur = lambda r, j: (j, r)
    nxt = lambda r, j: (jnp.minimum(j + 1, nb - 1), r)
    wide = lambda m: pl.BlockSpec((ATTN_BLOCK, W), m)
    narrow = lambda m: pl.BlockSpec((ATTN_BLOCK, G), m)
    qv, dv, lv, tv = qkv.reshape(n, d * W), do.reshape(n, d * G), lse.reshape(n, d * G), delta.reshape(n, d * G)
    out = pl.pallas_call(
        body, name=name, grid=(d, nb),
        in_specs=[wide(prv), wide(cur), wide(nxt), narrow(cur), narrow(nxt), narrow(cur), narrow(nxt),
                  narrow(cur), narrow(nxt)],
        out_specs=wide(cur), out_shape=jax.ShapeDtypeStruct((n, d * W), F32),
        compiler_params=_params("parallel", "parallel"),
    )(qv, qv, qv, dv, dv, lv, lv, tv, tv)
    return out.reshape(S, W)


def _chunk_triangle(T, upper):
    i = jnp.arange(T)
    same = (i[:, None] // HGRN_CHUNK) == (i[None, :] // HGRN_CHUNK)
    tri = (i[None, :] >= i[:, None]) if upper else (i[None, :] <= i[:, None])
    return jnp.logical_and(same, tri).astype(F32)


def _hgrn_prologue(qr, fr, lbv, q_s, k_s, b_s, tri_ref, T):
    def pro(s, c):
        sl = pl.ds(pl.multiple_of(s * HGRN_CHUNK, HGRN_CHUNK), HGRN_CHUNK)
        sg = _sigmoid(fr[sl, :])
        qv = qr[sl, :]
        q_s[sl, :] = qv * _sigmoid(qv)
        k_s[sl, :] = (1.0 - lbv) * (1.0 - sg)
        b_s[sl, :] = jnp.log(lbv + (1.0 - lbv) * sg)
        return c
    lax.fori_loop(0, T // HGRN_CHUNK, pro, 0)
    b_s[...] = _nn(tri_ref[...], b_s[...], HI)


def _hgrn_scan_fwd(name, pq, pf, pv, lb):
    S, D = pq.shape
    T = _pick_rows(S, HGRN_TILE)
    NH, NT, C, HD, HB = D // HGRN_HEAD, S // T, HGRN_CHUNK, HGRN_HEAD, HGRN_GROUP
    W = HB * HD
    tri = _chunk_triangle(T, upper=False)

    def body(qr, fr, iv, lb_ref, tri_ref, o_ref, ck_ref, st_ref, q_s, k_s, b_s):
        @pl.when(pl.program_id(1) == 0)
        def _():
            st_ref[...] = jnp.zeros_like(st_ref)

        ck_ref[...] = st_ref[...]
        _hgrn_prologue(qr, fr, lb_ref[...], q_s, k_s, b_s, tri_ref, T)
        row = lax.broadcasted_iota(jnp.int32, (C, 1), 0)

        def chunk(c, carry):
            sl = pl.ds(pl.multiple_of(c * C, C), C)
            for hh in range(HB):
                cs = slice(hh * HD, (hh + 1) * HD)
                q, k, b, v = q_s[sl, cs], k_s[sl, cs], b_s[sl, cs], iv[sl, cs]
                b_last = b[C - 1:C, :]
                st = st_ref[cs, :]
                o = _nt((q * jnp.exp(b)).astype(BF16), st.astype(BF16))
                for s in range(C):
                    e = jnp.exp(jnp.minimum(b - b[s:s + 1, :], 0.0))
                    a = jnp.sum(q * e * k[s:s + 1, :], axis=1, keepdims=True)
                    o = o + jnp.where(row >= s, a, 0.0) * v[s:s + 1, :]
                o_ref[sl, cs] = o
                kd = k * jnp.exp(b_last - b)
                st_ref[cs, :] = st * jnp.exp(b_last) + _tn(v.astype(BF16), kd.astype(BF16))
            return carry

        lax.fori_loop(0, T // C, chunk, 0)

    NG = NH // HB
    col = pl.BlockSpec((T, W), lambda h, t: (t, h))
    return pl.pallas_call(
        body, name=name, grid=(NG, NT),
        in_specs=[col, col, col, pl.BlockSpec((1, W), lambda h, t: (0, h)), pl.BlockSpec((T, T), lambda h, t: (0, 0))],
        out_specs=[col, pl.BlockSpec((W, HD), lambda h, t: (t * NG + h, 0))],
        out_shape=[jax.ShapeDtypeStruct((S, D), F32), jax.ShapeDtypeStruct((NT * NH * HD, HD), F32)],
        scratch_shapes=[pltpu.VMEM((W, HD), F32)] + [pltpu.VMEM((T, W), F32)] * 3,
        compiler_params=_params("parallel", "arbitrary"),
    )(pq, pf, pv, lb, tri)


def _hgrn_scan_bwd(name, pq, pf, pv, lb, ckpt, do):
    S, D = pq.shape
    T = _pick_rows(S, HGRN_TILE)
    NH, NT, C, HD, HB = D // HGRN_HEAD, S // T, HGRN_CHUNK, HGRN_HEAD, HGRN_GROUP
    NC, W, NG = T // C, HB * HD, NH // HB
    tri, tri_up = _chunk_triangle(T, upper=False), _chunk_triangle(T, upper=True)

    def body(qr, fr, iv, do_ref, ck_ref, lb_ref, tri_ref, triu_ref, dq_ref, df_ref, dv_ref, dlb_ref,
             dst_ref, run, save, q_s, k_s, b_s, dq_s, dk_s, db_s):
        @pl.when(pl.program_id(1) == 0)
        def _():
            dst_ref[...] = jnp.zeros_like(dst_ref)
            dlb_ref[...] = jnp.zeros_like(dlb_ref)

        lbv = lb_ref[...]
        _hgrn_prologue(qr, fr, lbv, q_s, k_s, b_s, tri_ref, T)
        row = lax.broadcasted_iota(jnp.int32, (C, 1), 0)
        run[...] = ck_ref[...]

        def replay(c, carry):
            sl = pl.ds(pl.multiple_of(c * C, C), C)
            for hh in range(HB):
                cs = slice(hh * HD, (hh + 1) * HD)
                st = run[cs, :]
                save[pl.ds(pl.multiple_of((hh * NC + c) * HD, HD), HD), :] = st
                k, b, v = k_s[sl, cs], b_s[sl, cs], iv[sl, cs]
                b_last = b[C - 1:C, :]
                kd = k * jnp.exp(b_last - b)
                run[cs, :] = st * jnp.exp(b_last) + _tn(v.astype(BF16), kd.astype(BF16))
            return carry

        lax.fori_loop(0, NC, replay, 0)

        def chunk(ci, carry):
            c = NC - 1 - ci
            sl = pl.ds(pl.multiple_of(c * C, C), C)
            for hh in range(HB):
                cs = slice(hh * HD, (hh + 1) * HD)
                q, k, b, v, g = q_s[sl, cs], k_s[sl, cs], b_s[sl, cs], iv[sl, cs], do_ref[sl, cs]
                st0 = save[pl.ds(pl.multiple_of((hh * NC + c) * HD, HD), HD), :]
                dst1 = dst_ref[cs, :]
                b_last = b[C - 1:C, :]
                eb, ebl, ek = jnp.exp(b), jnp.exp(b_last), jnp.exp(b_last - b)
                dst1_b = dst1.astype(BF16)
                dq = _nn(g.astype(BF16), st0.astype(BF16)) * eb
                dv = _nt((k * ek).astype(BF16), dst1_b)
                dk = _nn(v.astype(BF16), dst1_b) * ek
                db_last = _colsum(dk * k) + _colsum(dst1 * st0) * ebl
                for s in range(C):
                    e = jnp.where(row >= s, jnp.exp(jnp.minimum(b - b[s:s + 1, :], 0.0)), 0.0)
                    ks, vs = k[s:s + 1, :], v[s:s + 1, :]
                    da = jnp.sum(g * vs, axis=1, keepdims=True)
                    a = jnp.sum(q * e * ks, axis=1, keepdims=True)
                    dq = dq + da * e * ks
                    dk = dk + jnp.where(row == s, _colsum(da * q * e), 0.0)
                    dv = dv + jnp.where(row == s, _colsum(a * g), 0.0)
                dq_s[sl, cs] = dq
                dk_s[sl, cs] = dk
                db_s[sl, cs] = q * dq - k * dk + jnp.where(row == C - 1, db_last, 0.0)
                dv_ref[sl, cs] = dv.astype(BF16)
                dst_ref[cs, :] = dst1 * ebl + _tn(g.astype(BF16), (q * eb).astype(BF16))
            return carry

        lax.fori_loop(0, NC, chunk, 0)
        db_s[...] = _nn(triu_ref[...], db_s[...], HI)

        def epi(s, carry):
            sl = pl.ds(pl.multiple_of(s * C, C), C)
            qv = qr[sl, :]
            sq = _sigmoid(qv)
            dq_ref[sl, :] = (dq_s[sl, :] * sq * (1.0 + qv * (1.0 - sq))).astype(BF16)
            sg = _sigmoid(fr[sl, :])
            common = db_s[sl, :] / (lbv + (1.0 - lbv) * sg) - dk_s[sl, :]
            df_ref[sl, :] = (common * (1.0 - lbv) * sg * (1.0 - sg)).astype(BF16)
            dlb_ref[...] += _colsum(common * (1.0 - sg))
            return carry

        lax.fori_loop(0, NC, epi, 0)

    col = pl.BlockSpec((T, W), lambda h, t: (NT - 1 - t, h))
    dq, df, dv, dlb = pl.pallas_call(
        body, name=name, grid=(NG, NT),
        in_specs=[col, col, col, col, pl.BlockSpec((W, HD), lambda h, t: ((NT - 1 - t) * NG + h, 0)),
                  pl.BlockSpec((1, W), lambda h, t: (0, h)),
                  pl.BlockSpec((T, T), lambda h, t: (0, 0)), pl.BlockSpec((T, T), lambda h, t: (0, 0))],
        out_specs=[col, col, col, pl.BlockSpec((1, W), lambda h, t: (0, h))],
        out_shape=[jax.ShapeDtypeStruct((S, D), BF16)] * 3 + [jax.ShapeDtypeStruct((1, D), F32)],
        scratch_shapes=[pltpu.VMEM((W, HD), F32)] * 2 + [pltpu.VMEM((HB * NC * HD, HD), F32)]
        + [pltpu.VMEM((T, W), F32)] * 6,
        compiler_params=_params("parallel", "arbitrary"),
    )(pq, pf, pv, do, ckpt, lb, tri, tri_up)
    return dq, df, dv, dlb


def _hgrn_out_fwd(name, o, gate, norm_g):
    def fn(o, g, ng):
        parts = []
        for h in range(o.shape[1] // HGRN_HEAD):
            c = slice(h * HGRN_HEAD, (h + 1) * HGRN_HEAD)
            oh, gh = o[:, c], g[:, c]
            r = lax.rsqrt(jnp.mean(oh * oh, axis=-1, keepdims=True) + EPS)
            parts.append(oh * r * ng[:, c] * (gh * _sigmoid(gh)))
        return jnp.concatenate(parts, axis=1)
    return _rowwise(name, fn, [(o, 0), (gate, 0)], [(norm_g, None)], [(o.shape[1], BF16)])


def _hgrn_out_bwd(name, o, gate, norm_g, dy):
    def fn(o, g, dy, ng):
        dos, dgs, dngs = [], [], []
        for h in range(o.shape[1] // HGRN_HEAD):
            c = slice(h * HGRN_HEAD, (h + 1) * HGRN_HEAD)
            oh, gh, dyh, ngh = o[:, c], g[:, c], dy[:, c], ng[:, c]
            r = lax.rsqrt(jnp.mean(oh * oh, axis=-1, keepdims=True) + EPS)
            xh = oh * r
            s = _sigmoid(gh)
            dn = dyh * (gh * s)
            dxh = dn * ngh
            dos.append(r * (dxh - xh * jnp.mean(dxh * xh, axis=-1, keepdims=True)))
            dgs.append(dyh * xh * ngh * (s * (1.0 + gh * (1.0 - s))))
            dngs.append(_colsum(dn * xh))
        return jnp.concatenate(dos, axis=1), jnp.concatenate(dgs, axis=1), jnp.concatenate(dngs, axis=1)
    D = o.shape[1]
    return _rowwise(name, fn, [(o, 0), (gate, 0), (dy, 0)], [(norm_g, None)], [(D, F32), (D, BF16)], [(1, D)])


def _lower_bound_fwd(name, logits, layer):
    n = logits.shape[0]

    def body(x_ref, o_ref):
        rows = [x_ref[i:i + 1, :] for i in range(n)]
        m = functools.reduce(jnp.maximum, rows)
        e = [jnp.exp(r - m) for r in rows]
        den = functools.reduce(jnp.add, e)
        o_ref[...] = functools.reduce(jnp.add, e[1:layer + 1]) / den

    return pl.pallas_call(body, name=name, out_shape=jax.ShapeDtypeStruct((1, logits.shape[1]), F32))(logits)


def _lower_bound_bwd(name, logits, dlb, layer):
    n = logits.shape[0]

    def body(x_ref, d_ref, o_ref):
        rows = [x_ref[i:i + 1, :] for i in range(n)]
        m = functools.reduce(jnp.maximum, rows)
        e = [jnp.exp(r - m) for r in rows]
        den = functools.reduce(jnp.add, e)
        s = [v / den for v in e]
        d = d_ref[...]
        inner = functools.reduce(jnp.add, s[1:layer + 1]) * d
        for i in range(n):
            o_ref[i:i + 1, :] = s[i] * ((d if 1 <= i <= layer else 0.0) - inner)

    return pl.pallas_call(body, name=name, out_shape=jax.ShapeDtypeStruct(logits.shape, F32))(logits, dlb)


def _row(v):
    return v.reshape(1, -1)


def _ffn_fwd(l, x1, w):
    h2 = _rmsnorm_fwd(f"ffn{l}_norm", x1, _row(w["ffn_norm"][l]))
    a, u0g, u0u = _ffn_up_fused(f"ffn{l}_up", h2, w["ffn_w_up"][l], w["ffn_conv_w"][l], _row(w["ffn_conv_b"][l]))
    x2 = _matmul(f"ffn{l}_down", [(a, w["ffn_w_down"][l])], residual=x1)
    return x2, (x1, h2, u0g, u0u, a)


def _ffn_bwd(l, dx2, saved, w, grads):
    x1, h2, u0g, u0u, a = saved
    w_up, w_down = w["ffn_w_up"][l], w["ffn_w_down"][l]
    F = w_down.shape[0]
    grads["ffn_w_down"][l] = _matmul_tn(f"ffn{l}_dwdown", a, dx2)
    dg, du, dcwg, dcwu, dcbg, dcbu = _ffn_gate_bwd_fused(
        f"ffn{l}_dgate", dx2, w_down, u0g, u0u, w["ffn_conv_w"][l], _row(w["ffn_conv_b"][l]))
    grads["ffn_conv_w"][l] = jnp.concatenate([dcwg, dcwu], axis=1)
    grads["ffn_conv_b"][l] = jnp.concatenate([dcbg, dcbu], axis=1)[0]
    grads["ffn_w_up"][l] = jnp.concatenate(
        [_matmul_tn(f"ffn{l}_dwup0", h2, dg), _matmul_tn(f"ffn{l}_dwup1", h2, du)], axis=1)
    dh2 = _matmul(f"ffn{l}_dh", [(dg, w_up[:, :F]), (du, w_up[:, F:])], trans_b=True)
    dx1, dgain = _rmsnorm_bwd(f"ffn{l}_dnorm", x1, _row(w["ffn_norm"][l]), dh2, dx2)
    grads["ffn_norm"][l] = dgain[0]
    return dx1


def _attn_gain_rows(w, j, g):
    scale = ATTN_HEAD_DIM ** -0.5
    qg = jnp.tile(w["attn_q_gain"][j, g] * scale, ATTN_HEADS)
    kg = jnp.tile(w["attn_k_gain"][j, g], ATTN_HEADS)
    gain = jnp.concatenate([qg, kg, jnp.ones((ATTN_GW,), F32)])
    is_norm = jnp.concatenate([jnp.ones((2 * ATTN_GW,), F32), jnp.zeros((ATTN_GW,), F32)])
    return _row(gain), _row(is_norm)


def _attn_fwd(l, j, x, w):
    h = _rmsnorm_fwd(f"mix{l}_norm", x, _row(w["mixer_norm"][l]))
    w_in = w["attn_w_in"][j]
    seg = _segment_matrix(ATTN_GW, ATTN_HEAD_DIM)
    GW3 = 3 * ATTN_GW
    proj, qkv, os, ls = [], [], [], []
    for g, d in enumerate(ATTN_DILATIONS):
        gain, is_norm = _attn_gain_rows(w, j, g)
        proj.append(_matmul(f"attn{l}_in{g}", [(h, w_in[:, g * GW3:(g + 1) * GW3])]))
        qkv.append(_qknorm_fwd(f"attn{l}_qknorm{g}", proj[g], gain, is_norm, seg))
        o, lse = _attn_group_fwd(f"attn{l}_core{g}", qkv[g], d)
        os.append(o)
        ls.append(lse)
    o, lse = _attn_combine(f"attn{l}_combine", os, ls)
    x1 = _matmul(f"attn{l}_out", [(o, w["attn_w_out"][j])], residual=x)
    return x1, (x, h, proj, qkv, o, lse)


def _attn_bwd(l, j, dx1, saved, w, grads):
    x, h, proj, qkv, o, lse = saved
    w_in, w_out = w["attn_w_in"][j], w["attn_w_out"][j]
    seg = _segment_matrix(ATTN_GW, ATTN_HEAD_DIM)
    GW3 = 3 * ATTN_GW
    grads["attn_w_out"][j] = _matmul_tn(f"attn{l}_dwout", o, dx1)
    do = _matmul(f"attn{l}_do", [(dx1, w_out)], trans_b=True)
    delta = _attn_delta(f"attn{l}_delta", do, o, seg)
    dproj, dwin, dqg, dkg = [], [], [], []
    for g, d in enumerate(ATTN_DILATIONS):
        gain, is_norm = _attn_gain_rows(w, j, g)
        dqkv = _attn_group_bwd(f"attn{l}_dcore{g}", qkv[g], do, lse, delta, d)
        dp, dgain = _qknorm_bwd(f"attn{l}_dqknorm{g}", proj[g], dqkv, gain, is_norm, seg)
        dproj.append(dp)
        dwin.append(_matmul_tn(f"attn{l}_dwin{g}", h, dp))
        per_head = dgain.reshape(3, ATTN_HEADS, ATTN_HEAD_DIM).sum(axis=1)
        dqg.append(per_head[0] * ATTN_HEAD_DIM ** -0.5)
        dkg.append(per_head[1])
    grads["attn_w_in"][j] = jnp.concatenate(dwin, axis=1)
    grads["attn_q_gain"][j] = jnp.stack(dqg)
    grads["attn_k_gain"][j] = jnp.stack(dkg)
    dh = _matmul(f"attn{l}_dh", [(dproj[g], w_in[:, g * GW3:(g + 1) * GW3]) for g in range(3)], trans_b=True)
    dx, dg = _rmsnorm_bwd(f"mix{l}_dnorm", x, _row(w["mixer_norm"][l]), dh, dx1)
    grads["mixer_norm"][l] = dg[0]
    return dx


def _conv_fwd(l, j, x, w):
    h = _rmsnorm_fwd(f"mix{l}_norm", x, _row(w["mixer_norm"][l]))
    w_in, b_in = w["conv_w_in"][j], _row(w["conv_b_in"][j])
    C = w_in.shape[1] // 2
    ua = _matmul(f"conv{l}_in0", [(h, w_in[:, :C])], bias=b_in[:, :C])
    ug = _matmul(f"conv{l}_in1", [(h, w_in[:, C:])], bias=b_in[:, C:])
    glu = _glu_fwd(f"conv{l}_glu", ua, ug)
    c = _dwconv(f"conv{l}_dw", glu, w["conv_dw_w"][j], _row(w["conv_dw_b"][j]), reverse=False)
    sw = _ln_silu_fwd(f"conv{l}_ln", c, _row(w["conv_ln_g"][j]), _row(w["conv_ln_b"][j]))
    x1 = _matmul(f"conv{l}_out", [(sw, w["conv_w_out"][j])], bias=_row(w["conv_b_out"][j]), residual=x)
    return x1, (x, h, ua, ug, glu, c, sw)


def _conv_bwd(l, j, dx1, saved, w, grads):
    x, h, ua, ug, glu, c, sw = saved
    w_in, w_out, dw_w = w["conv_w_in"][j], w["conv_w_out"][j], w["conv_dw_w"][j]
    C = w_out.shape[0]
    grads["conv_b_out"][j] = _column_sums(f"conv{l}_dbout", dx1)[0]
    grads["conv_w_out"][j] = _matmul_tn(f"conv{l}_dwout", sw, dx1)
    dsw = _matmul(f"conv{l}_dsw", [(dx1, w_out)], trans_b=True)
    dc, dlg, dlb = _ln_silu_bwd(f"conv{l}_dln", c, _row(w["conv_ln_g"][j]), _row(w["conv_ln_b"][j]), dsw)
    grads["conv_ln_g"][j], grads["conv_ln_b"][j] = dlg[0], dlb[0]
    dglu = _dwconv(f"conv{l}_ddw", dc, dw_w, jnp.zeros((1, C), F32), reverse=True)
    gw, gb = _dwconv_wgrad(f"conv{l}_ddww", glu, dc, dw_w.shape[0])
    grads["conv_dw_w"][j], grads["conv_dw_b"][j] = gw, gb[0]
    da, dgate, sa, sg = _glu_bwd(f"conv{l}_dglu", ua, ug, dglu)
    grads["conv_b_in"][j] = jnp.concatenate([sa, sg], axis=1)[0]
    grads["conv_w_in"][j] = jnp.concatenate(
        [_matmul_tn(f"conv{l}_dwin0", h, da), _matmul_tn(f"conv{l}_dwin1", h, dgate)], axis=1)
    dh = _matmul(f"conv{l}_dh", [(da, w_in[:, :C]), (dgate, w_in[:, C:])], trans_b=True)
    dx, dg = _rmsnorm_bwd(f"mix{l}_dnorm", x, _row(w["mixer_norm"][l]), dh, dx1)
    grads["mixer_norm"][l] = dg[0]
    return dx


def _hgrn_fwd(l, j, x, w):
    h = _rmsnorm_fwd(f"mix{l}_norm", x, _row(w["mixer_norm"][l]))
    w_in = w["hgrn_w_in"][j]
    D = w_in.shape[1] // 4
    pq, pf, pv, pg = [_matmul(f"hgrn{l}_in{s}", [(h, w_in[:, s * D:(s + 1) * D])]) for s in range(4)]
    lb = _lower_bound_fwd(f"hgrn{l}_lb", w["hgrn_lb_logits"], l)
    o, ckpt = _hgrn_scan_fwd(f"hgrn{l}_scan", pq, pf, pv, lb)
    y = _hgrn_out_fwd(f"hgrn{l}_gate", o, pg, _row(w["hgrn_norm_g"][j]))
    x1 = _matmul(f"hgrn{l}_out", [(y, w["hgrn_w_out"][j])], residual=x)
    return x1, (x, h, pq, pf, pv, pg, lb, o, ckpt, y)


def _hgrn_bwd(l, j, dx1, saved, w, grads):
    x, h, pq, pf, pv, pg, lb, o, ckpt, y = saved
    w_in, w_out = w["hgrn_w_in"][j], w["hgrn_w_out"][j]
    D = w_out.shape[0]
    grads["hgrn_w_out"][j] = _matmul_tn(f"hgrn{l}_dwout", y, dx1)
    dy = _matmul(f"hgrn{l}_dy", [(dx1, w_out)], trans_b=True)
    do, dpg, dng = _hgrn_out_bwd(f"hgrn{l}_dgate", o, pg, _row(w["hgrn_norm_g"][j]), dy)
    grads["hgrn_norm_g"][j] = dng[0]
    dpq, dpf, dpv, dlb = _hgrn_scan_bwd(f"hgrn{l}_dscan", pq, pf, pv, lb, ckpt, do)
    grads["hgrn_lb_logits"] = grads["hgrn_lb_logits"] + _lower_bound_bwd(f"hgrn{l}_dlb", w["hgrn_lb_logits"], dlb, l)
    dps = [dpq, dpf, dpv, dpg]
    grads["hgrn_w_in"][j] = jnp.concatenate([_matmul_tn(f"hgrn{l}_dwin{s}", h, dps[s]) for s in range(4)], axis=1)
    dh = _matmul(f"hgrn{l}_dh", [(dps[s], w_in[:, s * D:(s + 1) * D]) for s in range(4)], trans_b=True)
    dx, dg = _rmsnorm_bwd(f"mix{l}_dnorm", x, _row(w["mixer_norm"][l]), dh, dx1)
    grads["mixer_norm"][l] = dg[0]
    return dx


_MIXERS = ((_attn_fwd, _attn_bwd), (_conv_fwd, _conv_bwd), (_hgrn_fwd, _hgrn_bwd))
_PER_MIXER = {"attn": 0, "conv": 1, "hgrn": 2}


def _local_step(x, target, w):
    depth = w["mixer_norm"].shape[0]
    grads = {}
    for name, v in w.items():
        lead = v.shape[0]
        grads[name] = jnp.zeros(v.shape, F32) if name == "hgrn_lb_logits" else [None] * lead
    saved = []
    for l in range(depth):
        fwd, _ = _MIXERS[l % N_MIXERS]
        x, s_mix = fwd(l, l // N_MIXERS, x, w)
        x, s_ffn = _ffn_fwd(l, x, w)
        saved.append((s_mix, s_ffn))
    dx, loss_cols = _loss_grad("loss", x, target)
    for l in reversed(range(depth)):
        _, bwd = _MIXERS[l % N_MIXERS]
        s_mix, s_ffn = saved[l]
        dx = _ffn_bwd(l, dx, s_ffn, w, grads)
        dx = bwd(l, l // N_MIXERS, dx, s_mix, w, grads)
    grads = {k: (v if k == "hgrn_lb_logits" else jnp.stack(v)) for k, v in grads.items()}
    return jnp.sum(loss_cols), dx, grads


_HBM = pl.BlockSpec(memory_space=pltpu.HBM)


def _chip_peers():
    x, y, c = lax.axis_index("x"), lax.axis_index("y"), lax.axis_index("c")
    return 2 * x + y, (x, y, c), [(1 - x, y), (x, 1 - y), (1 - x, 1 - y)]


def _exchange_chips(name, src):
    def body(src_ref, out_ref, send_sems, recv_sems, local_sem):
        p, (x, y, c), peers = _chip_peers()
        mine = pltpu.make_async_copy(src_ref.at[p], out_ref.at[p], local_sem)
        mine.start()

        def copy(k, slab_from, slab_to, peer):
            return pltpu.make_async_remote_copy(
                src_ref=src_ref.at[slab_from], dst_ref=out_ref.at[slab_to], send_sem=send_sems.at[k],
                recv_sem=recv_sems.at[k], device_id=(peer[0], peer[1], c), device_id_type=MESH)

        sends = [copy(k, 2 * px + py, p, (px, py)) for k, (px, py) in enumerate(peers)]
        for s in sends:
            s.start()
        for k, (px, py) in enumerate(peers):
            copy(k, p, 2 * px + py, (px, py)).wait_recv()
        for s in sends:
            s.wait_send()
        mine.wait()

    return pl.pallas_call(
        body, name=name, in_specs=[_HBM], out_specs=_HBM, out_shape=jax.ShapeDtypeStruct(src.shape, src.dtype),
        scratch_shapes=[pltpu.SemaphoreType.DMA((3,)), pltpu.SemaphoreType.DMA((3,)), pltpu.SemaphoreType.DMA],
    )(src)


def _all_gather_chips(name, shard):
    def body(src_ref, out_ref, send_sems, recv_sems, local_sem):
        p, (x, y, c), peers = _chip_peers()
        mine = pltpu.make_async_copy(src_ref, out_ref.at[p], local_sem)
        mine.start()

        def copy(k, slab, peer):
            return pltpu.make_async_remote_copy(
                src_ref=src_ref, dst_ref=out_ref.at[slab], send_sem=send_sems.at[k], recv_sem=recv_sems.at[k],
                device_id=(peer[0], peer[1], c), device_id_type=MESH)

        sends = [copy(k, p, peer) for k, peer in enumerate(peers)]
        for s in sends:
            s.start()
        for k, (px, py) in enumerate(peers):
            copy(k, 2 * px + py, (px, py)).wait_recv()
        for s in sends:
            s.wait_send()
        mine.wait()

    return pl.pallas_call(
        body, name=name, in_specs=[_HBM], out_specs=_HBM,
        out_shape=jax.ShapeDtypeStruct((N_CHIPS,) + shard.shape, shard.dtype),
        scratch_shapes=[pltpu.SemaphoreType.DMA((3,)), pltpu.SemaphoreType.DMA((3,)), pltpu.SemaphoreType.DMA],
    )(shard)


def _swap_cores(name, v):
    def body(v_ref, out_ref, send_sem, recv_sem):
        x, y, c = lax.axis_index("x"), lax.axis_index("y"), lax.axis_index("c")
        cp = pltpu.make_async_remote_copy(src_ref=v_ref, dst_ref=out_ref, send_sem=send_sem, recv_sem=recv_sem,
                                          device_id=(x, y, 1 - c), device_id_type=MESH)
        cp.start()
        cp.wait()

    return pl.pallas_call(
        body, name=name, in_specs=[_HBM], out_specs=_HBM, out_shape=jax.ShapeDtypeStruct(v.shape, v.dtype),
        scratch_shapes=[pltpu.SemaphoreType.DMA, pltpu.SemaphoreType.DMA],
    )(v)


_WEIGHTS = ("mixer_norm", "ffn_norm", "attn_w_in", "attn_q_gain", "attn_k_gain", "attn_w_out", "conv_w_in",
            "conv_b_in", "conv_dw_w", "conv_dw_b", "conv_ln_g", "conv_ln_b", "conv_w_out", "conv_b_out",
            "hgrn_w_in", "hgrn_lb_logits", "hgrn_norm_g", "hgrn_w_out", "ffn_w_up", "ffn_conv_w", "ffn_conv_b",
            "ffn_w_down")
_SHARD_AXIS = {"attn_w_in": 2, "attn_w_out": 2, "conv_w_in": 2, "conv_dw_w": 2, "conv_w_out": 1, "hgrn_w_in": 2,
               "hgrn_norm_g": 1, "hgrn_w_out": 1, "ffn_w_up": 2, "ffn_conv_w": 2, "ffn_w_down": 1}
_MATMUL_WEIGHTS = ("attn_w_in", "attn_w_out", "conv_w_in", "conv_w_out", "hgrn_w_in", "hgrn_w_out", "ffn_w_up",
                   "ffn_w_down")
PACK_COLS = 1024
PACK_ROWS = 512


def _pack(arrays, nlead, dtype):
    lead = arrays[0].shape[:nlead]
    flat = []
    for a in arrays:
        f = a.reshape(lead + (-1,)).astype(dtype)
        flat.append(jnp.pad(f, [(0, 0)] * nlead + [(0, (-f.shape[-1]) % PACK_COLS)]))
    buf = jnp.concatenate(flat, axis=-1)
    buf = jnp.pad(buf, [(0, 0)] * nlead + [(0, (-buf.shape[-1]) % (PACK_COLS * PACK_ROWS))])
    return buf.reshape(lead + (-1, PACK_COLS))


def _unpack(buf, shapes, nlead):
    lead = buf.shape[:nlead]
    flat = buf.reshape(lead + (-1,))
    out, off = [], 0
    for shape in shapes:
        n = 1
        for s in shape:
            n *= s
        out.append(flat[..., off:off + n].reshape(lead + tuple(shape)))
        off += n + (-n) % PACK_COLS
    return out


def _merge_shards(piece, axis):
    moved = jnp.moveaxis(piece, 0, axis)
    shape = moved.shape
    return moved.reshape(shape[:axis] + (shape[axis] * shape[axis + 1],) + shape[axis + 2:])


def _split_shards(full, axis):
    shape = full.shape
    cut = full.reshape(shape[:axis] + (N_CHIPS, shape[axis] // N_CHIPS) + shape[axis + 1:])
    return jnp.moveaxis(cut, axis, 0)


def _gather_weights(local):
    big = [n for n in _WEIGHTS if n in _MATMUL_WEIGHTS]
    small = [n for n in _WEIGHTS if n in _SHARD_AXIS and n not in _MATMUL_WEIGHTS]
    full = {n: local[n] for n in _WEIGHTS if n not in _SHARD_AXIS}
    for names, dtype, tag in ((big, BF16, "comm_gather_matmul_weights"), (small, F32, "comm_gather_small_weights")):
        gathered = _all_gather_chips(tag, _pack([local[n] for n in names], 0, dtype))
        pieces = _unpack(gathered, [local[n].shape for n in names], 1)
        for n, piece in zip(names, pieces):
            full[n] = _merge_shards(piece, _SHARD_AXIS[n])
    return full


def _reduce_gradients(grads, local):
    out = {}
    big = [n for n in _WEIGHTS if n in _MATMUL_WEIGHTS]
    rest = [n for n in _WEIGHTS if n not in _MATMUL_WEIGHTS]
    for names, dtype, tag in ((big, BF16, "matmul"), (rest, F32, "small")):
        slabs = []
        for n in names:
            g = grads[n]
            if n in _SHARD_AXIS:
                slabs.append(_split_shards(g, _SHARD_AXIS[n]))
            else:
                slabs.append(jnp.broadcast_to(g[None], (N_CHIPS,) + g.shape))
        packed = _pack(slabs, 1, dtype)
        landed = _exchange_chips(f"comm_scatter_{tag}_gradients", packed)
        partial = _sum_slabs(f"sum_chips_{tag}", landed)
        other = _swap_cores(f"comm_swap_{tag}_sums", partial)
        total = _add(f"sum_cores_{tag}", [partial, other])
        out.update(zip(names, _unpack(total, [local[n].shape for n in names], 0)))
    return out


def kernel(x, mixer_norm, ffn_norm, attn_w_in, attn_q_gain, attn_k_gain, attn_w_out, conv_w_in, conv_b_in, conv_dw_w, conv_dw_b, conv_ln_g, conv_ln_b, conv_w_out, conv_b_out, hgrn_w_in, hgrn_lb_logits, hgrn_norm_g, hgrn_w_out, ffn_w_up, ffn_conv_w, ffn_conv_b, ffn_w_down, loss_target, m_mixer_norm, m_ffn_norm, m_attn_w_in, m_attn_q_gain, m_attn_k_gain, m_attn_w_out, m_conv_w_in, m_conv_b_in, m_conv_dw_w, m_conv_dw_b, m_conv_ln_g, m_conv_ln_b, m_conv_w_out, m_conv_b_out, m_hgrn_w_in, m_hgrn_lb_logits, m_hgrn_norm_g, m_hgrn_w_out, m_ffn_w_up, m_ffn_conv_w, m_ffn_conv_b, m_ffn_w_down, v_mixer_norm, v_ffn_norm, v_attn_w_in, v_attn_q_gain, v_attn_k_gain, v_attn_w_out, v_conv_w_in, v_conv_b_in, v_conv_dw_w, v_conv_dw_b, v_conv_ln_g, v_conv_ln_b, v_conv_w_out, v_conv_b_out, v_hgrn_w_in, v_hgrn_lb_logits, v_hgrn_norm_g, v_hgrn_w_out, v_ffn_w_up, v_ffn_conv_w, v_ffn_conv_b, v_ffn_w_down):
    given = dict(locals())
    local = {n: given[n] for n in _WEIGHTS}
    full = _gather_weights(local)
    loss, dx, grads = _local_step(x[0], loss_target[0], full)
    loss = lax.psum(loss, ("x", "y", "c"))
    grad = _reduce_gradients(grads, local)
    delta, new_m, new_v = {}, {}, {}
    for n in _WEIGHTS:
        shape = local[n].shape
        as2d = lambda a: a.reshape(-1, shape[-1])
        d, m, v = _adamw(f"adamw_{n}", as2d(local[n]), as2d(grad[n]), as2d(given["m_" + n]), as2d(given["v_" + n]))
        delta[n], new_m[n], new_v[n] = d.reshape(shape), m.reshape(shape), v.reshape(shape)
    return (loss, dx[None], *[grad[n] for n in _WEIGHTS], *[delta[n] for n in _WEIGHTS],
            *[new_m[n] for n in _WEIGHTS], *[new_v[n] for n in _WEIGHTS])
```

```python
import functools

import jax
import jax.numpy as jnp
from jax import lax
from jax.experimental import pallas as pl
from jax.experimental.pallas import tpu as pltpu

F32 = jnp.float32
BF16 = jnp.bfloat16

EPS = 1e-6
N_MIXERS = 3
ATTN_DILATIONS = (1, 4, 16)
ATTN_BLOCK = 128
ATTN_HEADS = 8
ATTN_HEAD_DIM = 64
ATTN_GW = ATTN_HEADS * ATTN_HEAD_DIM
HGRN_HEAD = 128
HGRN_CHUNK = 16
HGRN_TILE = 256
HGRN_GROUP = 4
ADAM_LR, ADAM_B1, ADAM_B2, ADAM_EPS, ADAM_WD, ADAM_STEP = 0.001, 0.9, 0.999, 1e-08, 0.01, 10

LANES = 128
SUBLANES = 8
VMEM_LIMIT = 56 * 1024 * 1024
N_CHIPS = 4
MESH = pl.DeviceIdType.MESH

HI = lax.Precision.HIGHEST


def _params(*sem):
    return pltpu.CompilerParams(dimension_semantics=sem, vmem_limit_bytes=VMEM_LIMIT)


def _pick(n, target):
    if n <= target:
        return n
    best = None
    for t in range(LANES, target + 1, LANES):
        if n % t == 0:
            best = t
    assert best is not None, (n, target)
    return best


def _pick_rows(n, target):
    if n <= target:
        return n
    for t in range(target, 15, -16):
        if n % t == 0:
            return t
    return n


def _dot(a, b, dims, precision=None):
    return lax.dot_general(a, b, (dims, ((), ())), precision=precision, preferred_element_type=F32)


def _nn(a, b, precision=None):
    return _dot(a, b, ((1,), (0,)), precision)


def _nt(a, b, precision=None):
    return _dot(a, b, ((1,), (1,)), precision)


def _tn(a, b, precision=None):
    return _dot(a, b, ((0,), (0,)), precision)


def _sigmoid(x):
    return 1.0 / (1.0 + jnp.exp(-x))


ROWWISE_UNROLL_ROWS = 64


def _rowwise(name, fn, rows, pars=(), outs=(), accs=(), *, tc=None, tm=512, rb=16):
    S = rows[0][0].shape[0]
    tm = _pick_rows(S, tm)
    rb = rb if tm % rb == 0 else tm
    width = tc if tc is not None else None
    ncol = 1
    if tc is not None:
        base = outs[0][0] if outs else accs[0][1]
        ncol = base // tc
    n_r, n_p, n_o, n_a = len(rows), len(pars), len(outs), len(accs)

    def body(*refs):
        row_refs, par_refs = refs[:n_r], refs[n_r:n_r + n_p]
        out_refs, acc_refs = refs[n_r + n_p:n_r + n_p + n_o], refs[n_r + n_p + n_o:]
        if n_a:
            @pl.when(pl.program_id(1) == 0)
            def _():
                for a in acc_refs:
                    a[...] = jnp.zeros_like(a)

        def step(s, carry):
            sl = pl.ds(pl.multiple_of(s * rb, rb), rb)
            res = fn(*[r[sl, :] for r in row_refs], *[p[...] for p in par_refs])
            res = res if isinstance(res, tuple) else (res,)
            for o, v in zip(out_refs, res[:n_o]):
                o[sl, :] = v.astype(o.dtype)
            for a, v in zip(acc_refs, res[n_o:]):
                a[...] += v
            return carry

        lax.fori_loop(0, tm // rb, step, 0, unroll=min(tm // rb, max(2, ROWWISE_UNROLL_ROWS // rb)))

    def row_spec(c, off):
        if tc is None:
            return pl.BlockSpec((tm, c), lambda j, i: (i, 0))
        return pl.BlockSpec((tm, tc), lambda j, i, o=off // tc: (i, j + o))

    def par_spec(shape, off):
        if off is None or tc is None:
            return pl.BlockSpec(shape, lambda j, i: (0, 0))
        return pl.BlockSpec((shape[0], tc), lambda j, i, o=off // tc: (0, j + o))

    in_specs = [row_spec(a.shape[1], off) for a, off in rows]
    in_specs += [par_spec(a.shape, off) for a, off in pars]
    out_specs = [row_spec(c, 0) for c, _ in outs] + [par_spec(s, 0) for s in accs]
    out_shape = [jax.ShapeDtypeStruct((S, c), d) for c, d in outs]
    out_shape += [jax.ShapeDtypeStruct(s, F32) for s in accs]
    res = pl.pallas_call(
        body, name=name, grid=(ncol, S // tm), in_specs=in_specs, out_specs=out_specs, out_shape=out_shape,
        compiler_params=_params("parallel", "arbitrary" if n_a else "parallel"),
    )(*[a for a, _ in rows], *[a for a, _ in pars])
    return res[0] if len(res) == 1 else tuple(res)


MATMUL_VMEM = 36 * 1024 * 1024


def _matmul_tiles(M, N, pairs, out_dtype, residual):
    tm = _pick_rows(M, 512)
    for tn in sorted({_pick(N, t) for t in range(LANES, 2049, LANES)}, reverse=True):
        step = sum(tm * a.shape[1] * a.dtype.itemsize + a.shape[1] * tn * b.dtype.itemsize for a, b in pairs)
        step += tm * tn * (jnp.dtype(out_dtype).itemsize + (4 if residual is not None else 0))
        if 2 * step <= MATMUL_VMEM:
            return tm, tn
    return tm, LANES


def _matmul(name, pairs, *, trans_b=False, bias=None, residual=None, out_dtype=F32):
    M = pairs[0][0].shape[0]
    N = pairs[0][1].shape[0] if trans_b else pairs[0][1].shape[1]
    tm, tn = _matmul_tiles(M, N, pairs, out_dtype, residual)
    n = len(pairs)

    def body(*refs):
        acc = None
        for i in range(n):
            a = refs[2 * i][...].astype(BF16)
            b = refs[2 * i + 1][...].astype(BF16)
            d = _nt(a, b) if trans_b else _nn(a, b)
            acc = d if acc is None else acc + d
        k = 2 * n
        if bias is not None:
            acc = acc + refs[k][...]
            k += 1
        if residual is not None:
            acc = acc + refs[k][...]
            k += 1
        refs[k][...] = acc.astype(out_dtype)

    in_specs, args = [], []
    for a, b in pairs:
        K = a.shape[1]
        in_specs.append(pl.BlockSpec((tm, K), lambda j, i: (i, 0)))
        in_specs.append(pl.BlockSpec((tn, K), lambda j, i: (j, 0)) if trans_b
                        else pl.BlockSpec((K, tn), lambda j, i: (0, j)))
        args += [a, b]
    if bias is not None:
        in_specs.append(pl.BlockSpec((1, tn), lambda j, i: (0, j)))
        args.append(bias)
    if residual is not None:
        in_specs.append(pl.BlockSpec((tm, tn), lambda j, i: (i, j)))
        args.append(residual)
    return pl.pallas_call(
        body, name=name, grid=(N // tn, M // tm), in_specs=in_specs,
        out_specs=pl.BlockSpec((tm, tn), lambda j, i: (i, j)),
        out_shape=jax.ShapeDtypeStruct((M, N), out_dtype), compiler_params=_params("parallel", "parallel"),
    )(*args)


def _matmul_tn(name, a, b, *, tm=1408, tn=1408, tk=1024):
    S, M = a.shape
    N = b.shape[1]
    tm, tn, tk = _pick(M, tm), _pick(N, tn), _pick_rows(S, tk)

    def body(a_ref, b_ref, o_ref):
        @pl.when(pl.program_id(2) == 0)
        def _():
            o_ref[...] = jnp.zeros_like(o_ref)

        o_ref[...] += _tn(a_ref[...].astype(BF16), b_ref[...].astype(BF16))

    return pl.pallas_call(
        body, name=name, grid=(M // tm, N // tn, S // tk),
        in_specs=[pl.BlockSpec((tk, tm), lambda i, j, k: (k, i)), pl.BlockSpec((tk, tn), lambda i, j, k: (k, j))],
        out_specs=pl.BlockSpec((tm, tn), lambda i, j, k: (i, j)),
        out_shape=jax.ShapeDtypeStruct((M, N), F32), compiler_params=_params("parallel", "parallel", "arbitrary"),
    )(a, b)


def _halo_rows(K):
    return 8 if K <= 9 else 32


def _shifted_copies(ext, shifted, K):
    if K <= SUBLANES:
        return
    n = shifted.shape[1]
    for s in range(1, SUBLANES):
        shifted[s, 0:n, :] = ext[s:s + n, :]


def _window(ext, shifted, K, off, rows):
    s = off % SUBLANES
    if K <= SUBLANES or s == 0:
        return ext[off:off + rows, :]
    return shifted[s, off - s:off - s + rows, :]


def _dwconv(name, x, w, b, *, reverse, out_dtype=F32):
    S, C = x.shape
    K = w.shape[0]
    H = _halo_rows(K)
    tm, tc = _pick_rows(S, 512 if K <= 4 else 256), _pick(C, 1408 if K <= 4 else 256)
    nrow = S // tm
    RB = 16 if out_dtype == BF16 else 8

    def body(x_ref, h_ref, w_ref, b_ref, o_ref, ext, shifted):
        i = pl.program_id(1)
        edge = (i == nrow - 1) if reverse else (i == 0)
        halo = jnp.where(edge, 0.0, h_ref[...].astype(F32))
        if reverse:
            ext[0:tm, :] = x_ref[...].astype(F32)
            ext[tm:tm + H, :] = halo
        else:
            ext[0:H, :] = halo
            ext[H:H + tm, :] = x_ref[...].astype(F32)
        _shifted_copies(ext, shifted, K)
        wv = w_ref[...]
        for s in range(tm // RB):
            acc = jnp.broadcast_to(b_ref[...], (RB, tc))
            for k in range(K):
                off = s * RB + ((K - 1 - k) if reverse else (H - (K - 1) + k))
                acc = acc + wv[k:k + 1, :] * _window(ext, shifted, K, off, RB)
            o_ref[s * RB:(s + 1) * RB, :] = acc.astype(out_dtype)

    r = tm // H
    if reverse:
        halo_map = lambda j, i: (jnp.minimum((i + 1) * r, S // H - 1), j)
    else:
        halo_map = lambda j, i: (jnp.maximum(i * r - 1, 0), j)
    return pl.pallas_call(
        body, name=name, grid=(C // tc, nrow),
        in_specs=[pl.BlockSpec((tm, tc), lambda j, i: (i, j)), pl.BlockSpec((H, tc), halo_map),
                  pl.BlockSpec((K, tc), lambda j, i: (0, j)), pl.BlockSpec((1, tc), lambda j, i: (0, j))],
        out_specs=pl.BlockSpec((tm, tc), lambda j, i: (i, j)),
        out_shape=jax.ShapeDtypeStruct((S, C), out_dtype),
        scratch_shapes=[pltpu.VMEM((tm + H, tc), F32), pltpu.VMEM((SUBLANES, tm + H - SUBLANES, tc), F32)],
        compiler_params=_params("parallel", "parallel"),
    )(x, x, w, b)


def _dwconv_wgrad(name, x, dy, K):
    S, C = x.shape
    H = _halo_rows(K)
    tm, tc = _pick_rows(S, 512 if K <= 4 else 256), _pick(C, 512 if K <= 4 else LANES)
    RB = 8

    def body(x_ref, h_ref, dy_ref, dw_ref, db_ref, ext, shifted):
        i = pl.program_id(1)

        @pl.when(i == 0)
        def _():
            dw_ref[...] = jnp.zeros_like(dw_ref)
            db_ref[...] = jnp.zeros_like(db_ref)

        ext[0:H, :] = jnp.where(i == 0, 0.0, h_ref[...].astype(F32))
        ext[H:H + tm, :] = x_ref[...].astype(F32)
        _shifted_copies(ext, shifted, K)
        acc = [jnp.zeros((RB, tc), F32) for _ in range(K)]
        accb = jnp.zeros((RB, tc), F32)
        for s in range(tm // RB):
            d = dy_ref[s * RB:(s + 1) * RB, :].astype(F32)
            accb = accb + d
            for k in range(K):
                off = s * RB + H - (K - 1) + k
                acc[k] = acc[k] + d * _window(ext, shifted, K, off, RB)
        for k in range(K):
            dw_ref[k:k + 1, :] += jnp.sum(acc[k], axis=0, keepdims=True)
        db_ref[...] += jnp.sum(accb, axis=0, keepdims=True)

    r = tm // H
    return pl.pallas_call(
        body, name=name, grid=(C // tc, S // tm),
        in_specs=[pl.BlockSpec((tm, tc), lambda j, i: (i, j)),
                  pl.BlockSpec((H, tc), lambda j, i: (jnp.maximum(i * r - 1, 0), j)),
                  pl.BlockSpec((tm, tc), lambda j, i: (i, j))],
        out_specs=[pl.BlockSpec((K, tc), lambda j, i: (0, j)), pl.BlockSpec((1, tc), lambda j, i: (0, j))],
        out_shape=[jax.ShapeDtypeStruct((K, C), F32), jax.ShapeDtypeStruct((1, C), F32)],
        scratch_shapes=[pltpu.VMEM((tm + H, tc), F32), pltpu.VMEM((SUBLANES, tm + H - SUBLANES, tc), F32)],
        compiler_params=_params("parallel", "arbitrary"),
    )(x, x, dy)


FFN_HALO = 16


def _conv_taps(w, b, ext, r0, rows, cs):
    K = w.shape[0]
    acc = b
    for k in range(K):
        off = r0 - (K - 1) + k
        acc = acc + w[k:k + 1, :] * ext[off:off + rows, cs]
    return acc


def _ffn_up_fused(name, h, w_up, cw, cb):
    S, D = h.shape
    F = w_up.shape[1] // 2
    tm, tn = _pick_rows(S, 512), _pick(F, 1408)
    nj, H, RB = F // tn, FFN_HALO, 16

    def body(h_ref, hh_ref, wg_ref, wu_ref, cwg_ref, cwu_ref, cbg_ref, cbu_ref,
             a_ref, u0g_ref, u0u_ref, ug_ref, uu_ref, eg, eu):
        first = pl.program_id(1) == 0
        hv, halo = h_ref[...], hh_ref[...]
        for w_ref, u0_ref, e in ((wg_ref, u0g_ref, eg), (wu_ref, u0u_ref, eu)):
            w = w_ref[...]
            u0 = _nn(hv, w)
            u0_ref[...] = u0.astype(BF16)
            e[0:H, :] = jnp.where(first, 0.0, _nn(halo, w))
            e[H:H + tm, :] = u0
        for c in range(tn // LANES):
            cs = slice(c * LANES, (c + 1) * LANES)
            wg, wu, bg, bu = cwg_ref[:, cs], cwu_ref[:, cs], cbg_ref[:, cs], cbu_ref[:, cs]
            for s in range(tm // RB):
                rows = slice(s * RB, (s + 1) * RB)
                ug = _conv_taps(wg, bg, eg, H + s * RB, RB, cs)
                uu = _conv_taps(wu, bu, eu, H + s * RB, RB, cs)
                ug_ref[rows, cs] = ug.astype(BF16)
                uu_ref[rows, cs] = uu.astype(BF16)
                a_ref[rows, cs] = (ug * _sigmoid(ug) * uu).astype(BF16)

    r = tm // H
    gate = lambda rows: pl.BlockSpec((rows, tn), lambda j, i: (0, j))
    up = lambda rows: pl.BlockSpec((rows, tn), lambda j, i: (0, j + nj))
    tile = pl.BlockSpec((tm, tn), lambda j, i: (i, j))
    K = cw.shape[0]
    return pl.pallas_call(
        body, name=name, grid=(nj, S // tm),
        in_specs=[pl.BlockSpec((tm, D), lambda j, i: (i, 0)),
                  pl.BlockSpec((H, D), lambda j, i: (jnp.maximum(i * r - 1, 0), 0)),
                  gate(D), up(D), gate(K), up(K), gate(1), up(1)],
        out_specs=[tile] * 5, out_shape=[jax.ShapeDtypeStruct((S, F), BF16)] * 5,
        scratch_shapes=[pltpu.VMEM((H + tm, tn), F32)] * 2, compiler_params=_params("parallel", "parallel"),
    )(h, h, w_up, w_up, cw, cw, cb, cb)


def _ffn_gate_bwd_fused(name, dy, w_down, u0g, u0u, ug, uu, cw):
    S, D = dy.shape
    F, K = w_down.shape[0], cw.shape[0]
    tm, tn = _pick_rows(S, 512), _pick(F, 1408)
    nj, nrow, H, RB = F // tn, S // tm, FFN_HALO, 16

    def body(dy_ref, dyn_ref, wd_ref, u0g_ref, u0u_ref, g_ref, gn_ref, u_ref, un_ref, cwg_ref, cwu_ref,
             dg_ref, du_ref, dcwg_ref, dcwu_ref, dcbg_ref, dcbu_ref, dg_s, du_s, da_s):
        i = pl.program_id(1)
        last = i == nrow - 1

        @pl.when(i == 0)
        def _():
            for ref in (dcwg_ref, dcwu_ref, dcbg_ref, dcbu_ref):
                ref[...] = jnp.zeros_like(ref)

        wd = wd_ref[...]
        da_s[0:tm, :] = _nt(dy_ref[...].astype(BF16), wd)
        da_s[tm:tm + H, :] = jnp.where(last, 0.0, _nt(dyn_ref[...].astype(BF16), wd))
        for c in range(tn // LANES):
            cs = slice(c * LANES, (c + 1) * LANES)
            for s in range(tm // RB + 1):
                rows = slice(s * RB, (s + 1) * RB)
                src_g, src_u, src_rows = (g_ref, u_ref, rows) if s < tm // RB else (gn_ref, un_ref, slice(0, RB))
                gv, uv = src_g[src_rows, cs].astype(F32), src_u[src_rows, cs].astype(F32)
                da = da_s[rows, cs]
                sg = _sigmoid(gv)
                dg_s[rows, cs] = da * uv * (sg * (1.0 + gv * (1.0 - sg)))
                du_s[rows, cs] = da * (gv * sg)
            for d_s, u0_ref, cw_ref, out_ref, dcw_ref, dcb_ref in (
                    (dg_s, u0g_ref, cwg_ref, dg_ref, dcwg_ref, dcbg_ref),
                    (du_s, u0u_ref, cwu_ref, du_ref, dcwu_ref, dcbu_ref)):
                w = cw_ref[:, cs]
                acc = [jnp.zeros((RB, LANES), F32) for _ in range(K)]
                accb = jnp.zeros((RB, LANES), F32)
                for s in range(tm // RB):
                    r0 = s * RB
                    u0 = u0_ref[r0:r0 + RB, cs].astype(F32)
                    t = None
                    for k in range(K):
                        m = K - 1 - k
                        win = d_s[r0 + m:r0 + m + RB, cs]
                        if m == 0:
                            accb = accb + win
                        acc[k] = acc[k] + win * u0
                        term = w[k:k + 1, :] * win
                        t = term if t is None else t + term
                    out_ref[r0:r0 + RB, cs] = t.astype(BF16)
                for k in range(K):
                    dcw_ref[k:k + 1, cs] += jnp.sum(acc[k], axis=0, keepdims=True)
                dcb_ref[:, cs] += jnp.sum(accb, axis=0, keepdims=True)

    r = tm // H
    gate = lambda rows: pl.BlockSpec((rows, tn), lambda j, i: (0, j))
    up = lambda rows: pl.BlockSpec((rows, tn), lambda j, i: (0, j + nj))
    tile = pl.BlockSpec((tm, tn), lambda j, i: (i, j))
    nxt = pl.BlockSpec((H, tn), lambda j, i: (jnp.minimum((i + 1) * r, S // H - 1), j))
    acc_w, acc_b = pl.BlockSpec((K, tn), lambda j, i: (0, j)), pl.BlockSpec((1, tn), lambda j, i: (0, j))
    return pl.pallas_call(
        body, name=name, grid=(nj, nrow),
        in_specs=[pl.BlockSpec((tm, D), lambda j, i: (i, 0)),
                  pl.BlockSpec((H, D), lambda j, i: (jnp.minimum((i + 1) * r, S // H - 1), 0)),
                  pl.BlockSpec((tn, D), lambda j, i: (j, 0)),
                  tile, tile, tile, nxt, tile, nxt, gate(K), up(K)],
        out_specs=[tile, tile, acc_w, acc_w, acc_b, acc_b],
        out_shape=[jax.ShapeDtypeStruct((S, F), BF16)] * 2 + [jax.ShapeDtypeStruct((K, F), F32)] * 2
        + [jax.ShapeDtypeStruct((1, F), F32)] * 2,
        scratch_shapes=[pltpu.VMEM((tm + H, tn), F32)] * 3, compiler_params=_params("parallel", "arbitrary"),
    )(dy, dy, w_down, u0g, u0u, ug, ug, uu, uu, cw, cw)


def _colsum(v):
    return jnp.sum(v, axis=0, keepdims=True)


def _rmsnorm_fwd(name, x, gain):
    def fn(x, g):
        r = lax.rsqrt(jnp.mean(x * x, axis=-1, keepdims=True) + EPS)
        return x * r * g
    return _rowwise(name, fn, [(x, 0)], [(gain, None)], [(x.shape[1], BF16)])


def _rmsnorm_bwd(name, x, gain, dh, dres):
    def fn(x, dh, dres, g):
        r = lax.rsqrt(jnp.mean(x * x, axis=-1, keepdims=True) + EPS)
        xh = x * r
        dxh = dh * g
        dx = r * (dxh - xh * jnp.mean(dxh * xh, axis=-1, keepdims=True))
        return dres + dx, _colsum(dh * xh)
    D = x.shape[1]
    return _rowwise(name, fn, [(x, 0), (dh, 0), (dres, 0)], [(gain, None)], [(D, F32)], [(1, D)])


def _silu_gate_fwd(name, gate, up):
    F = gate.shape[1]
    def fn(g, up):
        return g * _sigmoid(g) * up
    return _rowwise(name, fn, [(gate, 0), (up, 0)], [], [(F, BF16)], tc=_pick(F, 512))


def _silu_gate_bwd(name, gate, up, da):
    F = gate.shape[1]
    def fn(g, up, da):
        s = _sigmoid(g)
        return da * up * (s * (1.0 + g * (1.0 - s))), da * (g * s)
    return _rowwise(name, fn, [(gate, 0), (up, 0), (da, 0)], [], [(F, F32), (F, F32)], tc=_pick(F, 512))


def _glu_fwd(name, a, gate):
    C = a.shape[1]
    def fn(a, g):
        return a * _sigmoid(g)
    return _rowwise(name, fn, [(a, 0), (gate, 0)], [], [(C, F32)], tc=_pick(C, 512))


def _glu_bwd(name, a, gate, dglu):
    C = a.shape[1]
    def fn(a, g, d):
        s = _sigmoid(g)
        da, dg = d * s, d * a * s * (1.0 - s)
        return da, dg, _colsum(da), _colsum(dg)
    return _rowwise(name, fn, [(a, 0), (gate, 0), (dglu, 0)], [], [(C, BF16), (C, BF16)], [(1, C), (1, C)],
                    tc=_pick(C, 512))


def _ln_silu_fwd(name, c, g, b):
    def fn(c, g, b):
        mu = jnp.mean(c, axis=-1, keepdims=True)
        d = c - mu
        n = d * lax.rsqrt(jnp.mean(d * d, axis=-1, keepdims=True) + EPS) * g + b
        return n * _sigmoid(n)
    return _rowwise(name, fn, [(c, 0)], [(g, None), (b, None)], [(c.shape[1], BF16)])


def _ln_silu_bwd(name, c, g, b, dsw):
    def fn(c, dsw, g, b):
        mu = jnp.mean(c, axis=-1, keepdims=True)
        d = c - mu
        r = lax.rsqrt(jnp.mean(d * d, axis=-1, keepdims=True) + EPS)
        ch = d * r
        n = ch * g + b
        s = _sigmoid(n)
        dn = dsw * (s * (1.0 + n * (1.0 - s)))
        dch = dn * g
        dc = r * (dch - jnp.mean(dch, axis=-1, keepdims=True) - ch * jnp.mean(dch * ch, axis=-1, keepdims=True))
        return dc, _colsum(dn * ch), _colsum(dn)
    C = c.shape[1]
    return _rowwise(name, fn, [(c, 0), (dsw, 0)], [(g, None), (b, None)], [(C, F32)], [(1, C), (1, C)])


def _column_sums(name, x):
    return _rowwise(name, lambda x: (_colsum(x),), [(x, 0)], [], [], [(1, x.shape[1])])


def _loss_grad(name, y, target):
    D = y.shape[1]
    def fn(y, t):
        e = y - t
        return e * (1.0 / D), _colsum(e * e) * (0.5 / D)
    return _rowwise(name, fn, [(y, 0), (target, 0)], [], [(D, F32)], [(1, D)])


def _add(name, arrays):
    def fn(*xs):
        acc = xs[0]
        for x in xs[1:]:
            acc = acc + x
        return acc
    return _rowwise(name, fn, [(a, 0) for a in arrays], [], [(arrays[0].shape[1], F32)], tm=256)


def _sum_slabs(name, stacked):
    n, R, C = stacked.shape
    tm = _pick_rows(R, 256)

    def body(*refs):
        acc = refs[0][0].astype(F32)
        for r in refs[1:n]:
            acc = acc + r[0].astype(F32)
        refs[n][...] = acc

    return pl.pallas_call(
        body, name=name, grid=(R // tm,),
        in_specs=[pl.BlockSpec((1, tm, C), lambda i, q=q: (q, i, 0)) for q in range(n)],
        out_specs=pl.BlockSpec((tm, C), lambda i: (i, 0)), out_shape=jax.ShapeDtypeStruct((R, C), F32),
        compiler_params=_params("parallel"),
    )(*[stacked] * n)


def _adamw(name, w, g, m, v):
    c1 = 1.0 - ADAM_B1 ** ADAM_STEP
    c2 = 1.0 - ADAM_B2 ** ADAM_STEP
    def fn(w, g, m, v):
        m = ADAM_B1 * m + (1.0 - ADAM_B1) * g
        v = ADAM_B2 * v + (1.0 - ADAM_B2) * (g * g)
        delta = -ADAM_LR * ((m / c1) / (jnp.sqrt(v / c2) + ADAM_EPS) + ADAM_WD * w)
        return delta, m, v
    C = w.shape[1]
    return _rowwise(name, fn, [(w, 0), (g, 0), (m, 0), (v, 0)], [], [(C, F32)] * 3, tm=256)


SEG_ROWS = 128


def _segment_matrix(n, seg):
    i = jnp.arange(n) // seg
    return (i[:, None] == i[None, :]).astype(BF16)


def _seg_sum(v, B):
    hi = v.astype(BF16)
    lo = (v - hi.astype(F32)).astype(BF16)
    return _nn(hi, B) + _nn(lo, B)


def _qknorm_fwd(name, proj, gain_full, is_norm, seg):
    def fn(x, gf, isn, B):
        ms = _seg_sum(x * x, B) * (1.0 / ATTN_HEAD_DIM)
        r = lax.rsqrt(ms + EPS)
        return x * (isn * r + (1.0 - isn)) * gf
    W = proj.shape[1]
    return _rowwise(name, fn, [(proj, 0)], [(gain_full, 0), (is_norm, 0), (seg, None)], [(W, BF16)],
                    tc=ATTN_GW, rb=SEG_ROWS)


def _qknorm_bwd(name, proj, dy, gain_full, is_norm, seg):
    def fn(x, dy, gf, isn, B):
        ms = _seg_sum(x * x, B) * (1.0 / ATTN_HEAD_DIM)
        r = lax.rsqrt(ms + EPS)
        xh = x * r
        dxh = dy * gf
        dn = r * (dxh - xh * (_seg_sum(dxh * xh, B) * (1.0 / ATTN_HEAD_DIM)))
        return isn * dn + (1.0 - isn) * dxh, _colsum(dy * xh)
    W = proj.shape[1]
    return _rowwise(name, fn, [(proj, 0), (dy, 0)], [(gain_full, 0), (is_norm, 0), (seg, None)],
                    [(W, BF16)], [(1, W)], tc=ATTN_GW, rb=SEG_ROWS)


def _attn_masks():
    shape = (ATTN_BLOCK, ATTN_BLOCK)
    row = lax.broadcasted_iota(jnp.int32, shape, 0)
    col = lax.broadcasted_iota(jnp.int32, shape, 1)
    low_lanes = col < ATTN_HEAD_DIM
    return col <= row, col >= row, low_lanes


def _attn_group_fwd(name, qkv, d):
    S = qkv.shape[0]
    n, W, G = S // d, 3 * ATTN_GW, ATTN_GW
    nb = n // ATTN_BLOCK
    view = qkv.reshape(n, d * W)

    def body(cur, prev, o_ref, l_ref):
        b = pl.program_id(1)
        cur_mask, prev_mask, low = _attn_masks()
        prev_mask = jnp.logical_and(prev_mask, b > 0)
        for pr in range(G // LANES):
            c0 = pr * LANES
            q2, kc, vc = cur[:, c0:c0 + LANES], cur[:, G + c0:G + c0 + LANES], cur[:, 2 * G + c0:2 * G + c0 + LANES]
            kp, vp = prev[:, G + c0:G + c0 + LANES], prev[:, 2 * G + c0:2 * G + c0 + LANES]
            res = []
            for hm in (low, jnp.logical_not(low)):
                qm = jnp.where(hm, q2, jnp.zeros_like(q2))
                sc = jnp.where(cur_mask, _nt(qm, kc), -jnp.inf)
                sp = jnp.where(prev_mask, _nt(qm, kp), -jnp.inf)
                m = jnp.maximum(jnp.max(sc, axis=1, keepdims=True), jnp.max(sp, axis=1, keepdims=True))
                pc, pp = jnp.exp(sc - m), jnp.exp(sp - m)
                l = jnp.sum(pc, axis=1, keepdims=True) + jnp.sum(pp, axis=1, keepdims=True)
                o = (_nn(pc.astype(BF16), vc) + _nn(pp.astype(BF16), vp)) / l
                res.append((o, jnp.broadcast_to(m + jnp.log(l), o.shape)))
            o_ref[:, c0:c0 + LANES] = jnp.where(low, res[0][0], res[1][0])
            l_ref[:, c0:c0 + LANES] = jnp.where(low, res[0][1], res[1][1])

    o, l = pl.pallas_call(
        body, name=name, grid=(d, nb),
        in_specs=[pl.BlockSpec((ATTN_BLOCK, W), lambda r, b: (b, r)),
                  pl.BlockSpec((ATTN_BLOCK, W), lambda r, b: (jnp.maximum(b - 1, 0), r))],
        out_specs=[pl.BlockSpec((ATTN_BLOCK, G), lambda r, b: (b, r))] * 2,
        out_shape=[jax.ShapeDtypeStruct((n, d * G), F32)] * 2, compiler_params=_params("parallel", "parallel"),
    )(view, view)
    return o.reshape(S, G), l.reshape(S, G)


def _attn_combine(name, os, ls):
    def fn(o1, o2, o3, l1, l2, l3):
        m = jnp.maximum(jnp.maximum(l1, l2), l3)
        e1, e2, e3 = jnp.exp(l1 - m), jnp.exp(l2 - m), jnp.exp(l3 - m)
        den = e1 + e2 + e3
        return (e1 * o1 + e2 * o2 + e3 * o3) / den, m + jnp.log(den)
    G = os[0].shape[1]
    return _rowwise(name, fn, [(a, 0) for a in (*os, *ls)], [], [(G, F32), (G, F32)])


def _attn_delta(name, do, o, seg):
    def fn(do, o, B):
        return _seg_sum(do * o, B)
    return _rowwise(name, fn, [(do, 0), (o, 0)], [(seg, None)], [(o.shape[1], F32)], rb=SEG_ROWS)


def _attn_group_bwd(name, qkv, do, lse, delta, d):
    S = qkv.shape[0]
    n, W, G = S // d, 3 * ATTN_GW, ATTN_GW
    nb = n // ATTN_BLOCK

    def body(qp, qc, qn, do_c, do_n, l_c, l_n, dl_c, dl_n, out):
        j = pl.program_id(1)
        cur_mask, prev_mask, low = _attn_masks()
        next_mask = jnp.logical_and(prev_mask, j < nb - 1)
        prev_mask = jnp.logical_and(prev_mask, j > 0)
        for pr in range(G // LANES):
            c0 = pr * LANES
            q_c, k_c, v_c = qc[:, c0:c0 + LANES], qc[:, G + c0:G + c0 + LANES], qc[:, 2 * G + c0:2 * G + c0 + LANES]
            q_n = qn[:, c0:c0 + LANES]
            k_p, v_p = qp[:, G + c0:G + c0 + LANES], qp[:, 2 * G + c0:2 * G + c0 + LANES]
            d_c, d_n = do_c[:, c0:c0 + LANES].astype(BF16), do_n[:, c0:c0 + LANES].astype(BF16)
            res = []
            for hh, hm in enumerate((low, jnp.logical_not(low))):
                h0 = c0 + hh * ATTN_HEAD_DIM
                lc, ln = l_c[:, h0:h0 + 1], l_n[:, h0:h0 + 1]
                dlc, dln = dl_c[:, h0:h0 + 1], dl_n[:, h0:h0 + 1]
                zero = jnp.zeros_like(q_c)
                qmc, qmn = jnp.where(hm, q_c, zero), jnp.where(hm, q_n, zero)
                dmc, dmn = jnp.where(hm, d_c, zero), jnp.where(hm, d_n, zero)
                p_a = jnp.where(cur_mask, jnp.exp(_nt(qmc, k_c) - lc), 0.0)
                ds_a = p_a * (_nt(dmc, v_c) - dlc)
                p_b = jnp.where(next_mask, jnp.exp(_nt(qmn, k_c) - ln), 0.0)
                ds_b = p_b * (_nt(dmn, v_c) - dln)
                p_c = jnp.where(prev_mask, jnp.exp(_nt(qmc, k_p) - lc), 0.0)
                ds_c = p_c * (_nt(dmc, v_p) - dlc)
                dq = _nn(ds_a.astype(BF16), k_c) + _nn(ds_c.astype(BF16), k_p)
                dk = _tn(ds_a.astype(BF16), q_c) + _tn(ds_b.astype(BF16), q_n)
                dv = _tn(p_a.astype(BF16), d_c) + _tn(p_b.astype(BF16), d_n)
                res.append((dq, dk, dv))
            for t in range(3):
                out[:, t * G + c0:t * G + c0 + LANES] = jnp.where(low, res[0][t], res[1][t])

    prv = lambda r, j: (jnp.maximum(j - 1, 0), r)
    cur = lambda r, j: (j, r)
    nxt = lambda r, j: (jnp.minimum(j + 1, nb - 1), r)
    wide = lambda m: pl.BlockSpec((ATTN_BLOCK, W), m)
    narrow = lambda m: pl.BlockSpec((ATTN_BLOCK, G), m)
    qv, dv, lv, tv = qkv.reshape(n, d * W), do.reshape(n, d * G), lse.reshape(n, d * G), delta.reshape(n, d * G)
    out = pl.pallas_call(
        body, name=name, grid=(d, nb),
        in_specs=[wide(prv), wide(cur), wide(nxt), narrow(cur), narrow(nxt), narrow(cur), narrow(nxt),
                  narrow(cur), narrow(nxt)],
        out_specs=wide(cur), out_shape=jax.ShapeDtypeStruct((n, d * W), F32),
        compiler_params=_params("parallel", "parallel"),
    )(qv, qv, qv, dv, dv, lv, lv, tv, tv)
    return out.reshape(S, W)


def _chunk_triangle(T, upper):
    i = jnp.arange(T)
    same = (i[:, None] // HGRN_CHUNK) == (i[None, :] // HGRN_CHUNK)
    tri = (i[None, :] >= i[:, None]) if upper else (i[None, :] <= i[:, None])
    return jnp.logical_and(same, tri).astype(F32)


def _hgrn_prologue(qr, fr, lbv, q_s, k_s, b_s, tri_ref, T):
    def pro(s, c):
        sl = pl.ds(pl.multiple_of(s * HGRN_CHUNK, HGRN_CHUNK), HGRN_CHUNK)
        sg = _sigmoid(fr[sl, :])
        qv = qr[sl, :]
        q_s[sl, :] = qv * _sigmoid(qv)
        k_s[sl, :] = (1.0 - lbv) * (1.0 - sg)
        b_s[sl, :] = jnp.log(lbv + (1.0 - lbv) * sg)
        return c
    lax.fori_loop(0, T // HGRN_CHUNK, pro, 0)
    b_s[...] = _nn(tri_ref[...], b_s[...], HI)


def _hgrn_scan_fwd(name, pq, pf, pv, lb):
    S, D = pq.shape
    T = _pick_rows(S, HGRN_TILE)
    NH, NT, C, HD, HB = D // HGRN_HEAD, S // T, HGRN_CHUNK, HGRN_HEAD, HGRN_GROUP
    W = HB * HD
    tri = _chunk_triangle(T, upper=False)

    def body(qr, fr, iv, lb_ref, tri_ref, o_ref, ck_ref, st_ref, q_s, k_s, b_s):
        @pl.when(pl.program_id(1) == 0)
        def _():
            st_ref[...] = jnp.zeros_like(st_ref)

        ck_ref[...] = st_ref[...]
        _hgrn_prologue(qr, fr, lb_ref[...], q_s, k_s, b_s, tri_ref, T)
        row = lax.broadcasted_iota(jnp.int32, (C, 1), 0)

        def chunk(c, carry):
            sl = pl.ds(pl.multiple_of(c * C, C), C)
            for hh in range(HB):
                cs = slice(hh * HD, (hh + 1) * HD)
                q, k, b, v = q_s[sl, cs], k_s[sl, cs], b_s[sl, cs], iv[sl, cs]
                b_last = b[C - 1:C, :]
                st = st_ref[cs, :]
                o = _nt((q * jnp.exp(b)).astype(BF16), st.astype(BF16))
                for s in range(C):
                    e = jnp.exp(jnp.minimum(b - b[s:s + 1, :], 0.0))
                    a = jnp.sum(q * e * k[s:s + 1, :], axis=1, keepdims=True)
                    o = o + jnp.where(row >= s, a, 0.0) * v[s:s + 1, :]
                o_ref[sl, cs] = o
                kd = k * jnp.exp(b_last - b)
                st_ref[cs, :] = st * jnp.exp(b_last) + _tn(v.astype(BF16), kd.astype(BF16))
            return carry

        lax.fori_loop(0, T // C, chunk, 0)

    NG = NH // HB
    col = pl.BlockSpec((T, W), lambda h, t: (t, h))
    return pl.pallas_call(
        body, name=name, grid=(NG, NT),
        in_specs=[col, col, col, pl.BlockSpec((1, W), lambda h, t: (0, h)), pl.BlockSpec((T, T), lambda h, t: (0, 0))],
        out_specs=[col, pl.BlockSpec((W, HD), lambda h, t: (t * NG + h, 0))],
        out_shape=[jax.ShapeDtypeStruct((S, D), F32), jax.ShapeDtypeStruct((NT * NH * HD, HD), F32)],
        scratch_shapes=[pltpu.VMEM((W, HD), F32)] + [pltpu.VMEM((T, W), F32)] * 3,
        compiler_params=_params("parallel", "arbitrary"),
    )(pq, pf, pv, lb, tri)


def _hgrn_scan_bwd(name, pq, pf, pv, lb, ckpt, do):
    S, D = pq.shape
    T = _pick_rows(S, HGRN_TILE)
    NH, NT, C, HD, HB = D // HGRN_HEAD, S // T, HGRN_CHUNK, HGRN_HEAD, HGRN_GROUP
    NC, W, NG = T // C, HB * HD, NH // HB
    tri, tri_up = _chunk_triangle(T, upper=False), _chunk_triangle(T, upper=True)

    def body(qr, fr, iv, do_ref, ck_ref, lb_ref, tri_ref, triu_ref, dq_ref, df_ref, dv_ref, dlb_ref,
             dst_ref, run, save, q_s, k_s, b_s, dq_s, dk_s, db_s):
        @pl.when(pl.program_id(1) == 0)
        def _():
            dst_ref[...] = jnp.zeros_like(dst_ref)
            dlb_ref[...] = jnp.zeros_like(dlb_ref)

        lbv = lb_ref[...]
        _hgrn_prologue(qr, fr, lbv, q_s, k_s, b_s, tri_ref, T)
        row = lax.broadcasted_iota(jnp.int32, (C, 1), 0)
        run[...] = ck_ref[...]

        def replay(c, carry):
            sl = pl.ds(pl.multiple_of(c * C, C), C)
            for hh in range(HB):
                cs = slice(hh * HD, (hh + 1) * HD)
                st = run[cs, :]
                save[pl.ds(pl.multiple_of((hh * NC + c) * HD, HD), HD), :] = st
                k, b, v = k_s[sl, cs], b_s[sl, cs], iv[sl, cs]
                b_last = b[C - 1:C, :]
                kd = k * jnp.exp(b_last - b)
                run[cs, :] = st * jnp.exp(b_last) + _tn(v.astype(BF16), kd.astype(BF16))
            return carry

        lax.fori_loop(0, NC, replay, 0)

        def chunk(ci, carry):
            c = NC - 1 - ci
            sl = pl.ds(pl.multiple_of(c * C, C), C)
            for hh in range(HB):
                cs = slice(hh * HD, (hh + 1) * HD)
                q, k, b, v, g = q_s[sl, cs], k_s[sl, cs], b_s[sl, cs], iv[sl, cs], do_ref[sl, cs]
                st0 = save[pl.ds(pl.multiple_of((hh * NC + c) * HD, HD), HD), :]
                dst1 = dst_ref[cs, :]
                b_last = b[C - 1:C, :]
                eb, ebl, ek = jnp.exp(b), jnp.exp(b_last), jnp.exp(b_last - b)
                dst1_b = dst1.astype(BF16)
                dq = _nn(g.astype(BF16), st0.astype(BF16)) * eb
                dv = _nt((k * ek).astype(BF16), dst1_b)
                dk = _nn(v.astype(BF16), dst1_b) * ek
                db_last = _colsum(dk * k) + _colsum(dst1 * st0) * ebl
                for s in range(C):
                    e = jnp.where(row >= s, jnp.exp(jnp.minimum(b - b[s:s + 1, :], 0.0)), 0.0)
                    ks, vs = k[s:s + 1, :], v[s:s + 1, :]
                    da = jnp.sum(g * vs, axis=1, keepdims=True)
                    a = jnp.sum(q * e * ks, axis=1, keepdims=True)
                    dq = dq + da * e * ks
                    dk = dk + jnp.where(row == s, _colsum(da * q * e), 0.0)
                    dv = dv + jnp.where(row == s, _colsum(a * g), 0.0)
                dq_s[sl, cs] = dq
                dk_s[sl, cs] = dk
                db_s[sl, cs] = q * dq - k * dk + jnp.where(row == C - 1, db_last, 0.0)
                dv_ref[sl, cs] = dv.astype(BF16)
                dst_ref[cs, :] = dst1 * ebl + _tn(g.astype(BF16), (q * eb).astype(BF16))
            return carry

        lax.fori_loop(0, NC, chunk, 0)
        db_s[...] = _nn(triu_ref[...], db_s[...], HI)

        def epi(s, carry):
            sl = pl.ds(pl.multiple_of(s * C, C), C)
            qv = qr[sl, :]
            sq = _sigmoid(qv)
            dq_ref[sl, :] = (dq_s[sl, :] * sq * (1.0 + qv * (1.0 - sq))).astype(BF16)
            sg = _sigmoid(fr[sl, :])
            common = db_s[sl, :] / (lbv + (1.0 - lbv) * sg) - dk_s[sl, :]
            df_ref[sl, :] = (common * (1.0 - lbv) * sg * (1.0 - sg)).astype(BF16)
            dlb_ref[...] += _colsum(common * (1.0 - sg))
            return carry

        lax.fori_loop(0, NC, epi, 0)

    col = pl.BlockSpec((T, W), lambda h, t: (NT - 1 - t, h))
    dq, df, dv, dlb = pl.pallas_call(
        body, name=name, grid=(NG, NT),
        in_specs=[col, col, col, col, pl.BlockSpec((W, HD), lambda h, t: ((NT - 1 - t) * NG + h, 0)),
                  pl.BlockSpec((1, W), lambda h, t: (0, h)),
                  pl.BlockSpec((T, T), lambda h, t: (0, 0)), pl.BlockSpec((T, T), lambda h, t: (0, 0))],
        out_specs=[col, col, col, pl.BlockSpec((1, W), lambda h, t: (0, h))],
        out_shape=[jax.ShapeDtypeStruct((S, D), BF16)] * 3 + [jax.ShapeDtypeStruct((1, D), F32)],
        scratch_shapes=[pltpu.VMEM((W, HD), F32)] * 2 + [pltpu.VMEM((HB * NC * HD, HD), F32)]
        + [pltpu.VMEM((T, W), F32)] * 6,
        compiler_params=_params("parallel", "arbitrary"),
    )(pq, pf, pv, do, ckpt, lb, tri, tri_up)
    return dq, df, dv, dlb


def _hgrn_out_fwd(name, o, gate, norm_g):
    def fn(o, g, ng):
        parts = []
        for h in range(o.shape[1] // HGRN_HEAD):
            c = slice(h * HGRN_HEAD, (h + 1) * HGRN_HEAD)
            oh, gh = o[:, c], g[:, c]
            r = lax.rsqrt(jnp.mean(oh * oh, axis=-1, keepdims=True) + EPS)
            parts.append(oh * r * ng[:, c] * (gh * _sigmoid(gh)))
        return jnp.concatenate(parts, axis=1)
    return _rowwise(name, fn, [(o, 0), (gate, 0)], [(norm_g, None)], [(o.shape[1], BF16)])


def _hgrn_out_bwd(name, o, gate, norm_g, dy):
    def fn(o, g, dy, ng):
        dos, dgs, dngs = [], [], []
        for h in range(o.shape[1] // HGRN_HEAD):
            c = slice(h * HGRN_HEAD, (h + 1) * HGRN_HEAD)
            oh, gh, dyh, ngh = o[:, c], g[:, c], dy[:, c], ng[:, c]
            r = lax.rsqrt(jnp.mean(oh * oh, axis=-1, keepdims=True) + EPS)
            xh = oh * r
            s = _sigmoid(gh)
            dn = dyh * (gh * s)
            dxh = dn * ngh
            dos.append(r * (dxh - xh * jnp.mean(dxh * xh, axis=-1, keepdims=True)))
            dgs.append(dyh * xh * ngh * (s * (1.0 + gh * (1.0 - s))))
            dngs.append(_colsum(dn * xh))
        return jnp.concatenate(dos, axis=1), jnp.concatenate(dgs, axis=1), jnp.concatenate(dngs, axis=1)
    D = o.shape[1]
    return _rowwise(name, fn, [(o, 0), (gate, 0), (dy, 0)], [(norm_g, None)], [(D, F32), (D, BF16)], [(1, D)])


def _lower_bound_fwd(name, logits, layer):
    n = logits.shape[0]

    def body(x_ref, o_ref):
        rows = [x_ref[i:i + 1, :] for i in range(n)]
        m = functools.reduce(jnp.maximum, rows)
        e = [jnp.exp(r - m) for r in rows]
        den = functools.reduce(jnp.add, e)
        o_ref[...] = functools.reduce(jnp.add, e[1:layer + 1]) / den

    return pl.pallas_call(body, name=name, out_shape=jax.ShapeDtypeStruct((1, logits.shape[1]), F32))(logits)


def _lower_bound_bwd(name, logits, dlb, layer):
    n = logits.shape[0]

    def body(x_ref, d_ref, o_ref):
        rows = [x_ref[i:i + 1, :] for i in range(n)]
        m = functools.reduce(jnp.maximum, rows)
        e = [jnp.exp(r - m) for r in rows]
        den = functools.reduce(jnp.add, e)
        s = [v / den for v in e]
        d = d_ref[...]
        inner = functools.reduce(jnp.add, s[1:layer + 1]) * d
        for i in range(n):
            o_ref[i:i + 1, :] = s[i] * ((d if 1 <= i <= layer else 0.0) - inner)

    return pl.pallas_call(body, name=name, out_shape=jax.ShapeDtypeStruct(logits.shape, F32))(logits, dlb)


def _row(v):
    return v.reshape(1, -1)


def _ffn_fwd(l, x1, w):
    h2 = _rmsnorm_fwd(f"ffn{l}_norm", x1, _row(w["ffn_norm"][l]))
    a, *u = _ffn_up_fused(f"ffn{l}_up", h2, w["ffn_w_up"][l], w["ffn_conv_w"][l], _row(w["ffn_conv_b"][l]))
    x2 = _matmul(f"ffn{l}_down", [(a, w["ffn_w_down"][l])], residual=x1)
    return x2, (x1, h2, u, a)


def _ffn_bwd(l, dx2, saved, w, grads):
    x1, h2, (u0g, u0u, ug, uu), a = saved
    w_up, w_down = w["ffn_w_up"][l], w["ffn_w_down"][l]
    F = w_down.shape[0]
    grads["ffn_w_down"][l] = _matmul_tn(f"ffn{l}_dwdown", a, dx2)
    dg, du, dcwg, dcwu, dcbg, dcbu = _ffn_gate_bwd_fused(
        f"ffn{l}_dgate", dx2, w_down, u0g, u0u, ug, uu, w["ffn_conv_w"][l])
    grads["ffn_conv_w"][l] = jnp.concatenate([dcwg, dcwu], axis=1)
    grads["ffn_conv_b"][l] = jnp.concatenate([dcbg, dcbu], axis=1)[0]
    grads["ffn_w_up"][l] = jnp.concatenate(
        [_matmul_tn(f"ffn{l}_dwup0", h2, dg), _matmul_tn(f"ffn{l}_dwup1", h2, du)], axis=1)
    dh2 = _matmul(f"ffn{l}_dh", [(dg, w_up[:, :F]), (du, w_up[:, F:])], trans_b=True)
    dx1, dgain = _rmsnorm_bwd(f"ffn{l}_dnorm", x1, _row(w["ffn_norm"][l]), dh2, dx2)
    grads["ffn_norm"][l] = dgain[0]
    return dx1


def _attn_gain_rows(w, j, g):
    scale = ATTN_HEAD_DIM ** -0.5
    qg = jnp.tile(w["attn_q_gain"][j, g] * scale, ATTN_HEADS)
    kg = jnp.tile(w["attn_k_gain"][j, g], ATTN_HEADS)
    gain = jnp.concatenate([qg, kg, jnp.ones((ATTN_GW,), F32)])
    is_norm = jnp.concatenate([jnp.ones((2 * ATTN_GW,), F32), jnp.zeros((ATTN_GW,), F32)])
    return _row(gain), _row(is_norm)


def _attn_fwd(l, j, x, w):
    h = _rmsnorm_fwd(f"mix{l}_norm", x, _row(w["mixer_norm"][l]))
    w_in = w["attn_w_in"][j]
    seg = _segment_matrix(ATTN_GW, ATTN_HEAD_DIM)
    GW3 = 3 * ATTN_GW
    proj, qkv, os, ls = [], [], [], []
    for g, d in enumerate(ATTN_DILATIONS):
        gain, is_norm = _attn_gain_rows(w, j, g)
        proj.append(_matmul(f"attn{l}_in{g}", [(h, w_in[:, g * GW3:(g + 1) * GW3])]))
        qkv.append(_qknorm_fwd(f"attn{l}_qknorm{g}", proj[g], gain, is_norm, seg))
        o, lse = _attn_group_fwd(f"attn{l}_core{g}", qkv[g], d)
        os.append(o)
        ls.append(lse)
    o, lse = _attn_combine(f"attn{l}_combine", os, ls)
    x1 = _matmul(f"attn{l}_out", [(o, w["attn_w_out"][j])], residual=x)
    return x1, (x, h, proj, qkv, o, lse)


def _attn_bwd(l, j, dx1, saved, w, grads):
    x, h, proj, qkv, o, lse = saved
    w_in, w_out = w["attn_w_in"][j], w["attn_w_out"][j]
    seg = _segment_matrix(ATTN_GW, ATTN_HEAD_DIM)
    GW3 = 3 * ATTN_GW
    grads["attn_w_out"][j] = _matmul_tn(f"attn{l}_dwout", o, dx1)
    do = _matmul(f"attn{l}_do", [(dx1, w_out)], trans_b=True)
    delta = _attn_delta(f"attn{l}_delta", do, o, seg)
    dproj, dwin, dqg, dkg = [], [], [], []
    for g, d in enumerate(ATTN_DILATIONS):
        gain, is_norm = _attn_gain_rows(w, j, g)
        dqkv = _attn_group_bwd(f"attn{l}_dcore{g}", qkv[g], do, lse, delta, d)
        dp, dgain = _qknorm_bwd(f"attn{l}_dqknorm{g}", proj[g], dqkv, gain, is_norm, seg)
        dproj.append(dp)
        dwin.append(_matmul_tn(f"attn{l}_dwin{g}", h, dp))
        per_head = dgain.reshape(3, ATTN_HEADS, ATTN_HEAD_DIM).sum(axis=1)
        dqg.append(per_head[0] * ATTN_HEAD_DIM ** -0.5)
        dkg.append(per_head[1])
    grads["attn_w_in"][j] = jnp.concatenate(dwin, axis=1)
    grads["attn_q_gain"][j] = jnp.stack(dqg)
    grads["attn_k_gain"][j] = jnp.stack(dkg)
    dh = _matmul(f"attn{l}_dh", [(dproj[g], w_in[:, g * GW3:(g + 1) * GW3]) for g in range(3)], trans_b=True)
    dx, dg = _rmsnorm_bwd(f"mix{l}_dnorm", x, _row(w["mixer_norm"][l]), dh, dx1)
    grads["mixer_norm"][l] = dg[0]
    return dx


def _conv_fwd(l, j, x, w):
    h = _rmsnorm_fwd(f"mix{l}_norm", x, _row(w["mixer_norm"][l]))
    w_in, b_in = w["conv_w_in"][j], _row(w["conv_b_in"][j])
    C = w_in.shape[1] // 2
    ua = _matmul(f"conv{l}_in0", [(h, w_in[:, :C])], bias=b_in[:, :C])
    ug = _matmul(f"conv{l}_in1", [(h, w_in[:, C:])], bias=b_in[:, C:])
    glu = _glu_fwd(f"conv{l}_glu", ua, ug)
    c = _dwconv(f"conv{l}_dw", glu, w["conv_dw_w"][j], _row(w["conv_dw_b"][j]), reverse=False)
    sw = _ln_silu_fwd(f"conv{l}_ln", c, _row(w["conv_ln_g"][j]), _row(w["conv_ln_b"][j]))
    x1 = _matmul(f"conv{l}_out", [(sw, w["conv_w_out"][j])], bias=_row(w["conv_b_out"][j]), residual=x)
    return x1, (x, h, ua, ug, glu, c, sw)


def _conv_bwd(l, j, dx1, saved, w, grads):
    x, h, ua, ug, glu, c, sw = saved
    w_in, w_out, dw_w = w["conv_w_in"][j], w["conv_w_out"][j], w["conv_dw_w"][j]
    C = w_out.shape[0]
    grads["conv_b_out"][j] = _column_sums(f"conv{l}_dbout", dx1)[0]
    grads["conv_w_out"][j] = _matmul_tn(f"conv{l}_dwout", sw, dx1)
    dsw = _matmul(f"conv{l}_dsw", [(dx1, w_out)], trans_b=True)
    dc, dlg, dlb = _ln_silu_bwd(f"conv{l}_dln", c, _row(w["conv_ln_g"][j]), _row(w["conv_ln_b"][j]), dsw)
    grads["conv_ln_g"][j], grads["conv_ln_b"][j] = dlg[0], dlb[0]
    dglu = _dwconv(f"conv{l}_ddw", dc, dw_w, jnp.zeros((1, C), F32), reverse=True)
    gw, gb = _dwconv_wgrad(f"conv{l}_ddww", glu, dc, dw_w.shape[0])
    grads["conv_dw_w"][j], grads["conv_dw_b"][j] = gw, gb[0]
    da, dgate, sa, sg = _glu_bwd(f"conv{l}_dglu", ua, ug, dglu)
    grads["conv_b_in"][j] = jnp.concatenate([sa, sg], axis=1)[0]
    grads["conv_w_in"][j] = jnp.concatenate(
        [_matmul_tn(f"conv{l}_dwin0", h, da), _matmul_tn(f"conv{l}_dwin1", h, dgate)], axis=1)
    dh = _matmul(f"conv{l}_dh", [(da, w_in[:, :C]), (dgate, w_in[:, C:])], trans_b=True)
    dx, dg = _rmsnorm_bwd(f"mix{l}_dnorm", x, _row(w["mixer_norm"][l]), dh, dx1)
    grads["mixer_norm"][l] = dg[0]
    return dx


def _hgrn_fwd(l, j, x, w):
    h = _rmsnorm_fwd(f"mix{l}_norm", x, _row(w["mixer_norm"][l]))
    w_in = w["hgrn_w_in"][j]
    D = w_in.shape[1] // 4
    pq, pf, pv, pg = [_matmul(f"hgrn{l}_in{s}", [(h, w_in[:, s * D:(s + 1) * D])]) for s in range(4)]
    lb = _lower_bound_fwd(f"hgrn{l}_lb", w["hgrn_lb_logits"], l)
    o, ckpt = _hgrn_scan_fwd(f"hgrn{l}_scan", pq, pf, pv, lb)
    y = _hgrn_out_fwd(f"hgrn{l}_gate", o, pg, _row(w["hgrn_norm_g"][j]))
    x1 = _matmul(f"hgrn{l}_out", [(y, w["hgrn_w_out"][j])], residual=x)
    return x1, (x, h, pq, pf, pv, pg, lb, o, ckpt, y)


def _hgrn_bwd(l, j, dx1, saved, w, grads):
    x, h, pq, pf, pv, pg, lb, o, ckpt, y = saved
    w_in, w_out = w["hgrn_w_in"][j], w["hgrn_w_out"][j]
    D = w_out.shape[0]
    grads["hgrn_w_out"][j] = _matmul_tn(f"hgrn{l}_dwout", y, dx1)
    dy = _matmul(f"hgrn{l}_dy", [(dx1, w_out)], trans_b=True)
    do, dpg, dng = _hgrn_out_bwd(f"hgrn{l}_dgate", o, pg, _row(w["hgrn_norm_g"][j]), dy)
    grads["hgrn_norm_g"][j] = dng[0]
    dpq, dpf, dpv, dlb = _hgrn_scan_bwd(f"hgrn{l}_dscan", pq, pf, pv, lb, ckpt, do)
    grads["hgrn_lb_logits"] = grads["hgrn_lb_logits"] + _lower_bound_bwd(f"hgrn{l}_dlb", w["hgrn_lb_logits"], dlb, l)
    dps = [dpq, dpf, dpv, dpg]
    grads["hgrn_w_in"][j] = jnp.concatenate([_matmul_tn(f"hgrn{l}_dwin{s}", h, dps[s]) for s in range(4)], axis=1)
    dh = _matmul(f"hgrn{l}_dh", [(dps[s], w_in[:, s * D:(s + 1) * D]) for s in range(4)], trans_b=True)
    dx, dg = _rmsnorm_bwd(f"mix{l}_dnorm", x, _row(w["mixer_norm"][l]), dh, dx1)
    grads["mixer_norm"][l] = dg[0]
    return dx


_MIXERS = ((_attn_fwd, _attn_bwd), (_conv_fwd, _conv_bwd), (_hgrn_fwd, _hgrn_bwd))
_PER_MIXER = {"attn": 0, "conv": 1, "hgrn": 2}


def _local_step(x, target, w):
    depth = w["mixer_norm"].shape[0]
    grads = {}
    for name, v in w.items():
        lead = v.shape[0]
        grads[name] = jnp.zeros(v.shape, F32) if name == "hgrn_lb_logits" else [None] * lead
    saved = []
    for l in range(depth):
        fwd, _ = _MIXERS[l % N_MIXERS]
        x, s_mix = fwd(l, l // N_MIXERS, x, w)
        x, s_ffn = _ffn_fwd(l, x, w)
        saved.append((s_mix, s_ffn))
    dx, loss_cols = _loss_grad("loss", x, target)
    for l in reversed(range(depth)):
        _, bwd = _MIXERS[l % N_MIXERS]
        s_mix, s_ffn = saved[l]
        dx = _ffn_bwd(l, dx, s_ffn, w, grads)
        dx = bwd(l, l // N_MIXERS, dx, s_mix, w, grads)
    grads = {k: (v if k == "hgrn_lb_logits" else jnp.stack(v)) for k, v in grads.items()}
    return jnp.sum(loss_cols), dx, grads


_HBM = pl.BlockSpec(memory_space=pltpu.HBM)


def _chip_peers():
    x, y, c = lax.axis_index("x"), lax.axis_index("y"), lax.axis_index("c")
    return 2 * x + y, (x, y, c), [(1 - x, y), (x, 1 - y), (1 - x, 1 - y)]


def _exchange_chips(name, src):
    def body(src_ref, out_ref, send_sems, recv_sems, local_sem):
        p, (x, y, c), peers = _chip_peers()
        mine = pltpu.make_async_copy(src_ref.at[p], out_ref.at[p], local_sem)
        mine.start()

        def copy(k, slab_from, slab_to, peer):
            return pltpu.make_async_remote_copy(
                src_ref=src_ref.at[slab_from], dst_ref=out_ref.at[slab_to], send_sem=send_sems.at[k],
                recv_sem=recv_sems.at[k], device_id=(peer[0], peer[1], c), device_id_type=MESH)

        sends = [copy(k, 2 * px + py, p, (px, py)) for k, (px, py) in enumerate(peers)]
        for s in sends:
            s.start()
        for k, (px, py) in enumerate(peers):
            copy(k, p, 2 * px + py, (px, py)).wait_recv()
        for s in sends:
            s.wait_send()
        mine.wait()

    return pl.pallas_call(
        body, name=name, in_specs=[_HBM], out_specs=_HBM, out_shape=jax.ShapeDtypeStruct(src.shape, src.dtype),
        scratch_shapes=[pltpu.SemaphoreType.DMA((3,)), pltpu.SemaphoreType.DMA((3,)), pltpu.SemaphoreType.DMA],
    )(src)


def _all_gather_chips(name, shard):
    R = shard.shape[0]
    half = R // 2

    def body(src_ref, out_ref, send_sems, recv_sems, local_sem):
        p, (x, y, c), peers = _chip_peers()
        mine = pltpu.make_async_copy(src_ref, out_ref.at[p], local_sem)
        mine.start()

        def rows(slab, core):
            return out_ref.at[slab, pl.ds(core * half, half), :]

        def over_ici(k, slab, peer):
            src = src_ref.at[pl.ds(c * half, half), :] if slab is None else rows(slab, c)
            return pltpu.make_async_remote_copy(
                src_ref=src, dst_ref=rows(p if slab is None else slab, c), send_sem=send_sems.at[k],
                recv_sem=recv_sems.at[k], device_id=(peer[0], peer[1], c), device_id_type=MESH)

        def to_sibling(k, slab, core):
            return pltpu.make_async_remote_copy(
                src_ref=rows(slab, core), dst_ref=rows(slab, core), send_sem=send_sems.at[3 + k],
                recv_sem=recv_sems.at[3 + k], device_id=(x, y, 1 - c), device_id_type=MESH)

        sends = [over_ici(k, None, peer) for k, peer in enumerate(peers)]
        for s in sends:
            s.start()
        passed = []
        for k, (px, py) in enumerate(peers):
            over_ici(k, 2 * px + py, (px, py)).wait_recv()
            passed.append(to_sibling(k, 2 * px + py, c))
            passed[k].start()
        for k, (px, py) in enumerate(peers):
            to_sibling(k, 2 * px + py, 1 - c).wait_recv()
        for s in sends + passed:
            s.wait_send()
        mine.wait()

    return pl.pallas_call(
        body, name=name, in_specs=[_HBM], out_specs=_HBM,
        out_shape=jax.ShapeDtypeStruct((N_CHIPS,) + shard.shape, shard.dtype),
        scratch_shapes=[pltpu.SemaphoreType.DMA((6,)), pltpu.SemaphoreType.DMA((6,)), pltpu.SemaphoreType.DMA],
    )(shard)


def _swap_cores(name, v):
    def body(v_ref, out_ref, send_sem, recv_sem):
        x, y, c = lax.axis_index("x"), lax.axis_index("y"), lax.axis_index("c")
        cp = pltpu.make_async_remote_copy(src_ref=v_ref, dst_ref=out_ref, send_sem=send_sem, recv_sem=recv_sem,
                                          device_id=(x, y, 1 - c), device_id_type=MESH)
        cp.start()
        cp.wait()

    return pl.pallas_call(
        body, name=name, in_specs=[_HBM], out_specs=_HBM, out_shape=jax.ShapeDtypeStruct(v.shape, v.dtype),
        scratch_shapes=[pltpu.SemaphoreType.DMA, pltpu.SemaphoreType.DMA],
    )(v)


_WEIGHTS = ("mixer_norm", "ffn_norm", "attn_w_in", "attn_q_gain", "attn_k_gain", "attn_w_out", "conv_w_in",
            "conv_b_in", "conv_dw_w", "conv_dw_b", "conv_ln_g", "conv_ln_b", "conv_w_out", "conv_b_out",
            "hgrn_w_in", "hgrn_lb_logits", "hgrn_norm_g", "hgrn_w_out", "ffn_w_up", "ffn_conv_w", "ffn_conv_b",
            "ffn_w_down")
_SHARD_AXIS = {"attn_w_in": 2, "attn_w_out": 2, "conv_w_in": 2, "conv_dw_w": 2, "conv_w_out": 1, "hgrn_w_in": 2,
               "hgrn_norm_g": 1, "hgrn_w_out": 1, "ffn_w_up": 2, "ffn_conv_w": 2, "ffn_w_down": 1}
_MATMUL_WEIGHTS = ("attn_w_in", "attn_w_out", "conv_w_in", "conv_w_out", "hgrn_w_in", "hgrn_w_out", "ffn_w_up",
                   "ffn_w_down")
PACK_COLS = 1024
PACK_ROWS = 512


def _pack(arrays, nlead, dtype):
    lead = arrays[0].shape[:nlead]
    flat = []
    for a in arrays:
        f = a.reshape(lead + (-1,)).astype(dtype)
        flat.append(jnp.pad(f, [(0, 0)] * nlead + [(0, (-f.shape[-1]) % PACK_COLS)]))
    buf = jnp.concatenate(flat, axis=-1)
    buf = jnp.pad(buf, [(0, 0)] * nlead + [(0, (-buf.shape[-1]) % (PACK_COLS * PACK_ROWS))])
    return buf.reshape(lead + (-1, PACK_COLS))


def _unpack(buf, shapes, nlead):
    lead = buf.shape[:nlead]
    flat = buf.reshape(lead + (-1,))
    out, off = [], 0
    for shape in shapes:
        n = 1
        for s in shape:
            n *= s
        out.append(flat[..., off:off + n].reshape(lead + tuple(shape)))
        off += n + (-n) % PACK_COLS
    return out


def _merge_shards(piece, axis):
    moved = jnp.moveaxis(piece, 0, axis)
    shape = moved.shape
    return moved.reshape(shape[:axis] + (shape[axis] * shape[axis + 1],) + shape[axis + 2:])


def _split_shards(full, axis):
    shape = full.shape
    cut = full.reshape(shape[:axis] + (N_CHIPS, shape[axis] // N_CHIPS) + shape[axis + 1:])
    return jnp.moveaxis(cut, axis, 0)


def _gather_weights(local):
    big = [n for n in _WEIGHTS if n in _MATMUL_WEIGHTS]
    small = [n for n in _WEIGHTS if n in _SHARD_AXIS and n not in _MATMUL_WEIGHTS]
    full = {n: local[n] for n in _WEIGHTS if n not in _SHARD_AXIS}
    for names, dtype, tag in ((big, BF16, "comm_gather_matmul_weights"), (small, F32, "comm_gather_small_weights")):
        gathered = _all_gather_chips(tag, _pack([local[n] for n in names], 0, dtype))
        pieces = _unpack(gathered, [local[n].shape for n in names], 1)
        for n, piece in zip(names, pieces):
            full[n] = _merge_shards(piece, _SHARD_AXIS[n])
    return full


def _reduce_gradients(grads, local):
    out = {}
    big = [n for n in _WEIGHTS if n in _MATMUL_WEIGHTS]
    rest = [n for n in _WEIGHTS if n not in _MATMUL_WEIGHTS]
    for names, dtype, tag in ((big, BF16, "matmul"), (rest, F32, "small")):
        slabs = []
        for n in names:
            g = grads[n]
            if n in _SHARD_AXIS:
                slabs.append(_split_shards(g, _SHARD_AXIS[n]))
            else:
                slabs.append(jnp.broadcast_to(g[None], (N_CHIPS,) + g.shape))
        packed = _pack(slabs, 1, dtype)
        landed = _exchange_chips(f"comm_scatter_{tag}_gradients", packed)
        partial = _sum_slabs(f"sum_chips_{tag}", landed)
        other = _swap_cores(f"comm_swap_{tag}_sums", partial)
        total = _add(f"sum_cores_{tag}", [partial, other])
        out.update(zip(names, _unpack(total, [local[n].shape for n in names], 0)))
    return out


def kernel(x, mixer_norm, ffn_norm, attn_w_in, attn_q_gain, attn_k_gain, attn_w_out, conv_w_in, conv_b_in, conv_dw_w, conv_dw_b, conv_ln_g, conv_ln_b, conv_w_out, conv_b_out, hgrn_w_in, hgrn_lb_logits, hgrn_norm_g, hgrn_w_out, ffn_w_up, ffn_conv_w, ffn_conv_b, ffn_w_down, loss_target, m_mixer_norm, m_ffn_norm, m_attn_w_in, m_attn_q_gain, m_attn_k_gain, m_attn_w_out, m_conv_w_in, m_conv_b_in, m_conv_dw_w, m_conv_dw_b, m_conv_ln_g, m_conv_ln_b, m_conv_w_out, m_conv_b_out, m_hgrn_w_in, m_hgrn_lb_logits, m_hgrn_norm_g, m_hgrn_w_out, m_ffn_w_up, m_ffn_conv_w, m_ffn_conv_b, m_ffn_w_down, v_mixer_norm, v_ffn_norm, v_attn_w_in, v_attn_q_gain, v_attn_k_gain, v_attn_w_out, v_conv_w_in, v_conv_b_in, v_conv_dw_w, v_conv_dw_b, v_conv_ln_g, v_conv_ln_b, v_conv_w_out, v_conv_b_out, v_hgrn_w_in, v_hgrn_lb_logits, v_hgrn_norm_g, v_hgrn_w_out, v_ffn_w_up, v_ffn_conv_w, v_ffn_conv_b, v_ffn_w_down):
    given = dict(locals())
    local = {n: given[n] for n in _WEIGHTS}
    full = _gather_weights(local)
    loss, dx, grads = _local_step(x[0], loss_target[0], full)
    loss = lax.psum(loss, ("x", "y", "c"))
    grad = _reduce_gradients(grads, local)
    delta, new_m, new_v = {}, {}, {}
    for n in _WEIGHTS:
        shape = local[n].shape
        as2d = lambda a: a.reshape(-1, shape[-1])
        d, m, v = _adamw(f"adamw_{n}", as2d(local[n]), as2d(grad[n]), as2d(given["m_" + n]), as2d(given["v_" + n]))
        delta[n], new_m[n], new_v[n] = d.reshape(shape), m.reshape(shape), v.reshape(shape)
    return (loss, dx[None], *[grad[n] for n in _WEIGHTS], *[delta[n] for n in _WEIGHTS],
            *[new_m[n] for n in _WEIGHTS], *[new_v[n] for n in _WEIGHTS])
```

```python
import functools

import jax
import jax.numpy as jnp
from jax import lax
from jax.experimental import pallas as pl
from jax.experimental.pallas import tpu as pltpu

F32 = jnp.float32
BF16 = jnp.bfloat16

EPS = 1e-6
N_MIXERS = 3
ATTN_DILATIONS = (1, 4, 16)
ATTN_BLOCK = 128
ATTN_HEADS = 8
ATTN_HEAD_DIM = 64
ATTN_GW = ATTN_HEADS * ATTN_HEAD_DIM
ATTN_ROWS = 32
HGRN_HEAD = 128
HGRN_CHUNK = 16
HGRN_TILE = 256
HGRN_GROUP = 4
ADAM_LR, ADAM_B1, ADAM_B2, ADAM_EPS, ADAM_WD, ADAM_STEP = 0.001, 0.9, 0.999, 1e-08, 0.01, 10

LANES = 128
SUBLANES = 8
VMEM_LIMIT = 56 * 1024 * 1024
N_CHIPS = 4
MESH = pl.DeviceIdType.MESH

HI = lax.Precision.HIGHEST


def _params(*sem):
    return pltpu.CompilerParams(dimension_semantics=sem, vmem_limit_bytes=VMEM_LIMIT)


def _pick(n, target):
    if n <= target:
        return n
    best = None
    for t in range(LANES, target + 1, LANES):
        if n % t == 0:
            best = t
    assert best is not None, (n, target)
    return best


def _pick_rows(n, target):
    if n <= target:
        return n
    for t in range(target, 15, -16):
        if n % t == 0:
            return t
    return n


def _dot(a, b, dims, precision=None):
    return lax.dot_general(a, b, (dims, ((), ())), precision=precision, preferred_element_type=F32)


def _nn(a, b, precision=None):
    return _dot(a, b, ((1,), (0,)), precision)


def _nt(a, b, precision=None):
    return _dot(a, b, ((1,), (1,)), precision)


def _tn(a, b, precision=None):
    return _dot(a, b, ((0,), (0,)), precision)


def _sigmoid(x):
    return 1.0 / (1.0 + jnp.exp(-x))


ROWWISE_UNROLL_ROWS = 64


def _rowwise(name, fn, rows, pars=(), outs=(), accs=(), *, tc=None, tm=512, rb=16):
    S = rows[0][0].shape[0]
    tm = _pick_rows(S, tm)
    rb = rb if tm % rb == 0 else tm
    width = tc if tc is not None else None
    ncol = 1
    if tc is not None:
        base = outs[0][0] if outs else accs[0][1]
        ncol = base // tc
    n_r, n_p, n_o, n_a = len(rows), len(pars), len(outs), len(accs)

    def body(*refs):
        row_refs, par_refs = refs[:n_r], refs[n_r:n_r + n_p]
        out_refs, acc_refs = refs[n_r + n_p:n_r + n_p + n_o], refs[n_r + n_p + n_o:]
        if n_a:
            @pl.when(pl.program_id(1) == 0)
            def _():
                for a in acc_refs:
                    a[...] = jnp.zeros_like(a)

        def step(s, carry):
            sl = pl.ds(pl.multiple_of(s * rb, rb), rb)
            res = fn(*[r[sl, :] for r in row_refs], *[p[...] for p in par_refs])
            res = res if isinstance(res, tuple) else (res,)
            for o, v in zip(out_refs, res[:n_o]):
                o[sl, :] = v.astype(o.dtype)
            for a, v in zip(acc_refs, res[n_o:]):
                a[...] += v
            return carry

        lax.fori_loop(0, tm // rb, step, 0, unroll=min(tm // rb, max(2, ROWWISE_UNROLL_ROWS // rb)))

    def row_spec(c, off):
        if tc is None:
            return pl.BlockSpec((tm, c), lambda j, i: (i, 0))
        return pl.BlockSpec((tm, tc), lambda j, i, o=off // tc: (i, j + o))

    def par_spec(shape, off):
        if off is None or tc is None:
            return pl.BlockSpec(shape, lambda j, i: (0, 0))
        return pl.BlockSpec((shape[0], tc), lambda j, i, o=off // tc: (0, j + o))

    in_specs = [row_spec(a.shape[1], off) for a, off in rows]
    in_specs += [par_spec(a.shape, off) for a, off in pars]
    out_specs = [row_spec(c, 0) for c, _ in outs] + [par_spec(s, 0) for s in accs]
    out_shape = [jax.ShapeDtypeStruct((S, c), d) for c, d in outs]
    out_shape += [jax.ShapeDtypeStruct(s, F32) for s in accs]
    res = pl.pallas_call(
        body, name=name, grid=(ncol, S // tm), in_specs=in_specs, out_specs=out_specs, out_shape=out_shape,
        compiler_params=_params("parallel", "arbitrary" if n_a else "parallel"),
    )(*[a for a, _ in rows], *[a for a, _ in pars])
    return res[0] if len(res) == 1 else tuple(res)


MATMUL_VMEM = 36 * 1024 * 1024


def _matmul_tiles(M, N, pairs, out_dtype, residual):
    tm = _pick_rows(M, 512)
    for tn in sorted({_pick(N, t) for t in range(LANES, 2049, LANES)}, reverse=True):
        step = sum(tm * a.shape[1] * a.dtype.itemsize + a.shape[1] * tn * b.dtype.itemsize for a, b in pairs)
        step += tm * tn * (jnp.dtype(out_dtype).itemsize + (4 if residual is not None else 0))
        if 2 * step <= MATMUL_VMEM:
            return tm, tn
    return tm, LANES


def _matmul(name, pairs, *, trans_b=False, bias=None, residual=None, out_dtype=F32):
    M = pairs[0][0].shape[0]
    N = pairs[0][1].shape[0] if trans_b else pairs[0][1].shape[1]
    tm, tn = _matmul_tiles(M, N, pairs, out_dtype, residual)
    n = len(pairs)

    def body(*refs):
        acc = None
        for i in range(n):
            a = refs[2 * i][...].astype(BF16)
            b = refs[2 * i + 1][...].astype(BF16)
            d = _nt(a, b) if trans_b else _nn(a, b)
            acc = d if acc is None else acc + d
        k = 2 * n
        if bias is not None:
            acc = acc + refs[k][...]
            k += 1
        if residual is not None:
            acc = acc + refs[k][...]
            k += 1
        refs[k][...] = acc.astype(out_dtype)

    in_specs, args = [], []
    for a, b in pairs:
        K = a.shape[1]
        in_specs.append(pl.BlockSpec((tm, K), lambda j, i: (i, 0)))
        in_specs.append(pl.BlockSpec((tn, K), lambda j, i: (j, 0)) if trans_b
                        else pl.BlockSpec((K, tn), lambda j, i: (0, j)))
        args += [a, b]
    if bias is not None:
        in_specs.append(pl.BlockSpec((1, tn), lambda j, i: (0, j)))
        args.append(bias)
    if residual is not None:
        in_specs.append(pl.BlockSpec((tm, tn), lambda j, i: (i, j)))
        args.append(residual)
    return pl.pallas_call(
        body, name=name, grid=(N // tn, M // tm), in_specs=in_specs,
        out_specs=pl.BlockSpec((tm, tn), lambda j, i: (i, j)),
        out_shape=jax.ShapeDtypeStruct((M, N), out_dtype), compiler_params=_params("parallel", "parallel"),
    )(*args)


def _matmul_tn(name, a, b, *, tm=1408, tn=1408, tk=1024):
    S, M = a.shape
    N = b.shape[1]
    tm, tn, tk = _pick(M, tm), _pick(N, tn), _pick_rows(S, tk)

    def body(a_ref, b_ref, o_ref):
        @pl.when(pl.program_id(2) == 0)
        def _():
            o_ref[...] = jnp.zeros_like(o_ref)

        o_ref[...] += _tn(a_ref[...].astype(BF16), b_ref[...].astype(BF16))

    return pl.pallas_call(
        body, name=name, grid=(M // tm, N // tn, S // tk),
        in_specs=[pl.BlockSpec((tk, tm), lambda i, j, k: (k, i)), pl.BlockSpec((tk, tn), lambda i, j, k: (k, j))],
        out_specs=pl.BlockSpec((tm, tn), lambda i, j, k: (i, j)),
        out_shape=jax.ShapeDtypeStruct((M, N), F32), compiler_params=_params("parallel", "parallel", "arbitrary"),
    )(a, b)


def _halo_rows(K):
    return 8 if K <= 9 else 32


def _shifted_copies(ext, shifted, K):
    if K <= SUBLANES:
        return
    n = shifted.shape[1]
    for s in range(1, SUBLANES):
        shifted[s, 0:n, :] = ext[s:s + n, :]


def _window(ext, shifted, K, off, rows):
    s = off % SUBLANES
    if K <= SUBLANES or s == 0:
        return ext[off:off + rows, :]
    return shifted[s, off - s:off - s + rows, :]


def _dwconv(name, x, w, b, *, reverse, out_dtype=F32):
    S, C = x.shape
    K = w.shape[0]
    H = _halo_rows(K)
    tm, tc = _pick_rows(S, 512 if K <= 4 else 256), _pick(C, 1408 if K <= 4 else 256)
    nrow = S // tm
    RB = 16 if out_dtype == BF16 else 8

    def body(x_ref, h_ref, w_ref, b_ref, o_ref, ext, shifted):
        i = pl.program_id(1)
        edge = (i == nrow - 1) if reverse else (i == 0)
        halo = jnp.where(edge, 0.0, h_ref[...].astype(F32))
        if reverse:
            ext[0:tm, :] = x_ref[...].astype(F32)
            ext[tm:tm + H, :] = halo
        else:
            ext[0:H, :] = halo
            ext[H:H + tm, :] = x_ref[...].astype(F32)
        _shifted_copies(ext, shifted, K)
        wv = w_ref[...]
        for s in range(tm // RB):
            acc = jnp.broadcast_to(b_ref[...], (RB, tc))
            for k in range(K):
                off = s * RB + ((K - 1 - k) if reverse else (H - (K - 1) + k))
                acc = acc + wv[k:k + 1, :] * _window(ext, shifted, K, off, RB)
            o_ref[s * RB:(s + 1) * RB, :] = acc.astype(out_dtype)

    r = tm // H
    if reverse:
        halo_map = lambda j, i: (jnp.minimum((i + 1) * r, S // H - 1), j)
    else:
        halo_map = lambda j, i: (jnp.maximum(i * r - 1, 0), j)
    return pl.pallas_call(
        body, name=name, grid=(C // tc, nrow),
        in_specs=[pl.BlockSpec((tm, tc), lambda j, i: (i, j)), pl.BlockSpec((H, tc), halo_map),
                  pl.BlockSpec((K, tc), lambda j, i: (0, j)), pl.BlockSpec((1, tc), lambda j, i: (0, j))],
        out_specs=pl.BlockSpec((tm, tc), lambda j, i: (i, j)),
        out_shape=jax.ShapeDtypeStruct((S, C), out_dtype),
        scratch_shapes=[pltpu.VMEM((tm + H, tc), F32), pltpu.VMEM((SUBLANES, tm + H - SUBLANES, tc), F32)],
        compiler_params=_params("parallel", "parallel"),
    )(x, x, w, b)


def _dwconv_wgrad(name, x, dy, K):
    S, C = x.shape
    H = _halo_rows(K)
    tm, tc = _pick_rows(S, 512 if K <= 4 else 256), _pick(C, 512 if K <= 4 else LANES)
    RB = 8

    def body(x_ref, h_ref, dy_ref, dw_ref, db_ref, ext, shifted):
        i = pl.program_id(1)

        @pl.when(i == 0)
        def _():
            dw_ref[...] = jnp.zeros_like(dw_ref)
            db_ref[...] = jnp.zeros_like(db_ref)

        ext[0:H, :] = jnp.where(i == 0, 0.0, h_ref[...].astype(F32))
        ext[H:H + tm, :] = x_ref[...].astype(F32)
        _shifted_copies(ext, shifted, K)
        acc = [jnp.zeros((RB, tc), F32) for _ in range(K)]
        accb = jnp.zeros((RB, tc), F32)
        for s in range(tm // RB):
            d = dy_ref[s * RB:(s + 1) * RB, :].astype(F32)
            accb = accb + d
            for k in range(K):
                off = s * RB + H - (K - 1) + k
                acc[k] = acc[k] + d * _window(ext, shifted, K, off, RB)
        for k in range(K):
            dw_ref[k:k + 1, :] += jnp.sum(acc[k], axis=0, keepdims=True)
        db_ref[...] += jnp.sum(accb, axis=0, keepdims=True)

    r = tm // H
    return pl.pallas_call(
        body, name=name, grid=(C // tc, S // tm),
        in_specs=[pl.BlockSpec((tm, tc), lambda j, i: (i, j)),
                  pl.BlockSpec((H, tc), lambda j, i: (jnp.maximum(i * r - 1, 0), j)),
                  pl.BlockSpec((tm, tc), lambda j, i: (i, j))],
        out_specs=[pl.BlockSpec((K, tc), lambda j, i: (0, j)), pl.BlockSpec((1, tc), lambda j, i: (0, j))],
        out_shape=[jax.ShapeDtypeStruct((K, C), F32), jax.ShapeDtypeStruct((1, C), F32)],
        scratch_shapes=[pltpu.VMEM((tm + H, tc), F32), pltpu.VMEM((SUBLANES, tm + H - SUBLANES, tc), F32)],
        compiler_params=_params("parallel", "arbitrary"),
    )(x, x, dy)


FFN_HALO = 16


def _conv_taps(w, b, ext, r0, rows, cs):
    K = w.shape[0]
    acc = b
    for k in range(K):
        off = r0 - (K - 1) + k
        acc = acc + w[k:k + 1, :] * ext[off:off + rows, cs]
    return acc


def _ffn_up_fused(name, h, w_up, cw, cb):
    S, D = h.shape
    F = w_up.shape[1] // 2
    tm, tn = _pick_rows(S, 512), _pick(F, 1408)
    nj, H, RB = F // tn, FFN_HALO, 16

    def body(h_ref, hh_ref, wg_ref, wu_ref, cwg_ref, cwu_ref, cbg_ref, cbu_ref,
             a_ref, u0g_ref, u0u_ref, ug_ref, uu_ref, eg, eu):
        first = pl.program_id(1) == 0
        hv, halo = h_ref[...], hh_ref[...]
        for w_ref, u0_ref, e in ((wg_ref, u0g_ref, eg), (wu_ref, u0u_ref, eu)):
            w = w_ref[...]
            u0 = _nn(hv, w)
            u0_ref[...] = u0.astype(BF16)
            e[0:H, :] = jnp.where(first, 0.0, _nn(halo, w))
            e[H:H + tm, :] = u0
        for c in range(tn // LANES):
            cs = slice(c * LANES, (c + 1) * LANES)
            wg, wu, bg, bu = cwg_ref[:, cs], cwu_ref[:, cs], cbg_ref[:, cs], cbu_ref[:, cs]
            for s in range(tm // RB):
                rows = slice(s * RB, (s + 1) * RB)
                ug = _conv_taps(wg, bg, eg, H + s * RB, RB, cs)
                uu = _conv_taps(wu, bu, eu, H + s * RB, RB, cs)
                ug_ref[rows, cs] = ug.astype(BF16)
                uu_ref[rows, cs] = uu.astype(BF16)
                a_ref[rows, cs] = (ug * _sigmoid(ug) * uu).astype(BF16)

    r = tm // H
    gate = lambda rows: pl.BlockSpec((rows, tn), lambda j, i: (0, j))
    up = lambda rows: pl.BlockSpec((rows, tn), lambda j, i: (0, j + nj))
    tile = pl.BlockSpec((tm, tn), lambda j, i: (i, j))
    K = cw.shape[0]
    return pl.pallas_call(
        body, name=name, grid=(nj, S // tm),
        in_specs=[pl.BlockSpec((tm, D), lambda j, i: (i, 0)),
                  pl.BlockSpec((H, D), lambda j, i: (jnp.maximum(i * r - 1, 0), 0)),
                  gate(D), up(D), gate(K), up(K), gate(1), up(1)],
        out_specs=[tile] * 5, out_shape=[jax.ShapeDtypeStruct((S, F), BF16)] * 5,
        scratch_shapes=[pltpu.VMEM((H + tm, tn), F32)] * 2, compiler_params=_params("parallel", "parallel"),
    )(h, h, w_up, w_up, cw, cw, cb, cb)


def _ffn_gate_bwd_fused(name, dy, w_down, u0g, u0u, ug, uu, cw):
    S, D = dy.shape
    F, K = w_down.shape[0], cw.shape[0]
    tm, tn = _pick_rows(S, 512), _pick(F, 1408)
    nj, nrow, H, RB = F // tn, S // tm, FFN_HALO, 16

    def body(dy_ref, dyn_ref, wd_ref, u0g_ref, u0u_ref, g_ref, gn_ref, u_ref, un_ref, cwg_ref, cwu_ref,
             dg_ref, du_ref, dcwg_ref, dcwu_ref, dcbg_ref, dcbu_ref, dg_s, du_s, da_s):
        i = pl.program_id(1)
        last = i == nrow - 1

        @pl.when(i == 0)
        def _():
            for ref in (dcwg_ref, dcwu_ref, dcbg_ref, dcbu_ref):
                ref[...] = jnp.zeros_like(ref)

        wd = wd_ref[...]
        da_s[0:tm, :] = _nt(dy_ref[...].astype(BF16), wd)
        da_s[tm:tm + H, :] = jnp.where(last, 0.0, _nt(dyn_ref[...].astype(BF16), wd))
        for c in range(tn // LANES):
            cs = slice(c * LANES, (c + 1) * LANES)
            for s in range(tm // RB + 1):
                rows = slice(s * RB, (s + 1) * RB)
                src_g, src_u, src_rows = (g_ref, u_ref, rows) if s < tm // RB else (gn_ref, un_ref, slice(0, RB))
                gv, uv = src_g[src_rows, cs].astype(F32), src_u[src_rows, cs].astype(F32)
                da = da_s[rows, cs]
                sg = _sigmoid(gv)
                dg_s[rows, cs] = da * uv * (sg * (1.0 + gv * (1.0 - sg)))
                du_s[rows, cs] = da * (gv * sg)
            for d_s, u0_ref, cw_ref, out_ref, dcw_ref, dcb_ref in (
                    (dg_s, u0g_ref, cwg_ref, dg_ref, dcwg_ref, dcbg_ref),
                    (du_s, u0u_ref, cwu_ref, du_ref, dcwu_ref, dcbu_ref)):
                w = cw_ref[:, cs]
                acc = [jnp.zeros((RB, LANES), F32) for _ in range(K)]
                accb = jnp.zeros((RB, LANES), F32)
                for s in range(tm // RB):
                    r0 = s * RB
                    u0 = u0_ref[r0:r0 + RB, cs].astype(F32)
                    t = None
                    for k in range(K):
                        m = K - 1 - k
                        win = d_s[r0 + m:r0 + m + RB, cs]
                        if m == 0:
                            accb = accb + win
                        acc[k] = acc[k] + win * u0
                        term = w[k:k + 1, :] * win
                        t = term if t is None else t + term
                    out_ref[r0:r0 + RB, cs] = t.astype(BF16)
                for k in range(K):
                    dcw_ref[k:k + 1, cs] += jnp.sum(acc[k], axis=0, keepdims=True)
                dcb_ref[:, cs] += jnp.sum(accb, axis=0, keepdims=True)

    r = tm // H
    gate = lambda rows: pl.BlockSpec((rows, tn), lambda j, i: (0, j))
    up = lambda rows: pl.BlockSpec((rows, tn), lambda j, i: (0, j + nj))
    tile = pl.BlockSpec((tm, tn), lambda j, i: (i, j))
    nxt = pl.BlockSpec((H, tn), lambda j, i: (jnp.minimum((i + 1) * r, S // H - 1), j))
    acc_w, acc_b = pl.BlockSpec((K, tn), lambda j, i: (0, j)), pl.BlockSpec((1, tn), lambda j, i: (0, j))
    return pl.pallas_call(
        body, name=name, grid=(nj, nrow),
        in_specs=[pl.BlockSpec((tm, D), lambda j, i: (i, 0)),
                  pl.BlockSpec((H, D), lambda j, i: (jnp.minimum((i + 1) * r, S // H - 1), 0)),
                  pl.BlockSpec((tn, D), lambda j, i: (j, 0)),
                  tile, tile, tile, nxt, tile, nxt, gate(K), up(K)],
        out_specs=[tile, tile, acc_w, acc_w, acc_b, acc_b],
        out_shape=[jax.ShapeDtypeStruct((S, F), BF16)] * 2 + [jax.ShapeDtypeStruct((K, F), F32)] * 2
        + [jax.ShapeDtypeStruct((1, F), F32)] * 2,
        scratch_shapes=[pltpu.VMEM((tm + H, tn), F32)] * 3, compiler_params=_params("parallel", "arbitrary"),
    )(dy, dy, w_down, u0g, u0u, ug, ug, uu, uu, cw, cw)


def _colsum(v):
    return jnp.sum(v, axis=0, keepdims=True)


def _rmsnorm_fwd(name, x, gain):
    def fn(x, g):
        r = lax.rsqrt(jnp.mean(x * x, axis=-1, keepdims=True) + EPS)
        return x * r * g
    return _rowwise(name, fn, [(x, 0)], [(gain, None)], [(x.shape[1], BF16)])


def _rmsnorm_bwd(name, x, gain, dh, dres):
    def fn(x, dh, dres, g):
        r = lax.rsqrt(jnp.mean(x * x, axis=-1, keepdims=True) + EPS)
        xh = x * r
        dxh = dh * g
        dx = r * (dxh - xh * jnp.mean(dxh * xh, axis=-1, keepdims=True))
        return dres + dx, _colsum(dh * xh)
    D = x.shape[1]
    return _rowwise(name, fn, [(x, 0), (dh, 0), (dres, 0)], [(gain, None)], [(D, F32)], [(1, D)])


def _silu_gate_fwd(name, gate, up):
    F = gate.shape[1]
    def fn(g, up):
        return g * _sigmoid(g) * up
    return _rowwise(name, fn, [(gate, 0), (up, 0)], [], [(F, BF16)], tc=_pick(F, 512))


def _silu_gate_bwd(name, gate, up, da):
    F = gate.shape[1]
    def fn(g, up, da):
        s = _sigmoid(g)
        return da * up * (s * (1.0 + g * (1.0 - s))), da * (g * s)
    return _rowwise(name, fn, [(gate, 0), (up, 0), (da, 0)], [], [(F, F32), (F, F32)], tc=_pick(F, 512))


def _glu_fwd(name, a, gate):
    C = a.shape[1]
    def fn(a, g):
        return a * _sigmoid(g)
    return _rowwise(name, fn, [(a, 0), (gate, 0)], [], [(C, F32)], tc=_pick(C, 512))


def _glu_bwd(name, a, gate, dglu):
    C = a.shape[1]
    def fn(a, g, d):
        s = _sigmoid(g)
        da, dg = d * s, d * a * s * (1.0 - s)
        return da, dg, _colsum(da), _colsum(dg)
    return _rowwise(name, fn, [(a, 0), (gate, 0), (dglu, 0)], [], [(C, BF16), (C, BF16)], [(1, C), (1, C)],
                    tc=_pick(C, 512))


def _ln_silu_fwd(name, c, g, b):
    def fn(c, g, b):
        mu = jnp.mean(c, axis=-1, keepdims=True)
        d = c - mu
        n = d * lax.rsqrt(jnp.mean(d * d, axis=-1, keepdims=True) + EPS) * g + b
        return n * _sigmoid(n)
    return _rowwise(name, fn, [(c, 0)], [(g, None), (b, None)], [(c.shape[1], BF16)])


def _ln_silu_bwd(name, c, g, b, dsw):
    def fn(c, dsw, g, b):
        mu = jnp.mean(c, axis=-1, keepdims=True)
        d = c - mu
        r = lax.rsqrt(jnp.mean(d * d, axis=-1, keepdims=True) + EPS)
        ch = d * r
        n = ch * g + b
        s = _sigmoid(n)
        dn = dsw * (s * (1.0 + n * (1.0 - s)))
        dch = dn * g
        dc = r * (dch - jnp.mean(dch, axis=-1, keepdims=True) - ch * jnp.mean(dch * ch, axis=-1, keepdims=True))
        return dc, _colsum(dn * ch), _colsum(dn)
    C = c.shape[1]
    return _rowwise(name, fn, [(c, 0), (dsw, 0)], [(g, None), (b, None)], [(C, F32)], [(1, C), (1, C)])


def _column_sums(name, x):
    return _rowwise(name, lambda x: (_colsum(x),), [(x, 0)], [], [], [(1, x.shape[1])])


def _loss_grad(name, y, target):
    D = y.shape[1]
    def fn(y, t):
        e = y - t
        return e * (1.0 / D), _colsum(e * e) * (0.5 / D)
    return _rowwise(name, fn, [(y, 0), (target, 0)], [], [(D, F32)], [(1, D)])


def _add(name, arrays):
    def fn(*xs):
        acc = xs[0]
        for x in xs[1:]:
            acc = acc + x
        return acc
    return _rowwise(name, fn, [(a, 0) for a in arrays], [], [(arrays[0].shape[1], F32)], tm=256)


def _sum_slabs(name, stacked):
    n, R, C = stacked.shape
    tm = _pick_rows(R, 256)

    def body(*refs):
        acc = refs[0][0].astype(F32)
        for r in refs[1:n]:
            acc = acc + r[0].astype(F32)
        refs[n][...] = acc

    return pl.pallas_call(
        body, name=name, grid=(R // tm,),
        in_specs=[pl.BlockSpec((1, tm, C), lambda i, q=q: (q, i, 0)) for q in range(n)],
        out_specs=pl.BlockSpec((tm, C), lambda i: (i, 0)), out_shape=jax.ShapeDtypeStruct((R, C), F32),
        compiler_params=_params("parallel"),
    )(*[stacked] * n)


def _adamw(name, w, g, m, v):
    c1 = 1.0 - ADAM_B1 ** ADAM_STEP
    c2 = 1.0 - ADAM_B2 ** ADAM_STEP
    def fn(w, g, m, v):
        m = ADAM_B1 * m + (1.0 - ADAM_B1) * g
        v = ADAM_B2 * v + (1.0 - ADAM_B2) * (g * g)
        delta = -ADAM_LR * ((m / c1) / (jnp.sqrt(v / c2) + ADAM_EPS) + ADAM_WD * w)
        return delta, m, v
    C = w.shape[1]
    return _rowwise(name, fn, [(w, 0), (g, 0), (m, 0), (v, 0)], [], [(C, F32)] * 3, tm=256)


SEG_ROWS = 128


def _segment_matrix(n, seg):
    i = jnp.arange(n) // seg
    return (i[:, None] == i[None, :]).astype(BF16)


def _seg_sum(v, B):
    hi = v.astype(BF16)
    lo = (v - hi.astype(F32)).astype(BF16)
    n = B.shape[0]
    slabs = [slice(c, c + n) for c in range(0, v.shape[1], n)]
    return jnp.concatenate([_nn(hi[:, c], B) + _nn(lo[:, c], B) for c in slabs], axis=1)


def _qknorm_fwd(name, proj, gain_full, is_norm, seg):
    def fn(x, gf, isn, B):
        ms = _seg_sum(x * x, B) * (1.0 / ATTN_HEAD_DIM)
        r = lax.rsqrt(ms + EPS)
        return x * (isn * r + (1.0 - isn)) * gf
    W = proj.shape[1]
    return _rowwise(name, fn, [(proj, 0)], [(gain_full, 0), (is_norm, 0), (seg, None)], [(W, BF16)],
                    tc=ATTN_GW, rb=SEG_ROWS)


def _qknorm_bwd(name, proj, dy, gain_full, is_norm, seg):
    def fn(x, dy, gf, isn, B):
        ms = _seg_sum(x * x, B) * (1.0 / ATTN_HEAD_DIM)
        r = lax.rsqrt(ms + EPS)
        xh = x * r
        dxh = dy * gf
        dn = r * (dxh - xh * (_seg_sum(dxh * xh, B) * (1.0 / ATTN_HEAD_DIM)))
        return isn * dn + (1.0 - isn) * dxh, _colsum(dy * xh)
    W = proj.shape[1]
    return _rowwise(name, fn, [(proj, 0), (dy, 0)], [(gain_full, 0), (is_norm, 0), (seg, None)],
                    [(W, BF16)], [(1, W)], tc=ATTN_GW, rb=SEG_ROWS)


def _attn_masks(r0=0, rows=ATTN_BLOCK):
    shape = (rows, ATTN_BLOCK)
    row = lax.broadcasted_iota(jnp.int32, shape, 0) + r0
    col = lax.broadcasted_iota(jnp.int32, shape, 1)
    return col <= row, col >= row, col < ATTN_HEAD_DIM


def _attn_group_fwd(name, qkv, d):
    S = qkv.shape[0]
    n, W, G = S // d, 3 * ATTN_GW, ATTN_GW
    nb = n // ATTN_BLOCK
    view = qkv.reshape(n, d * W)

    B, RQ = ATTN_BLOCK, ATTN_ROWS

    def body(cur, prev, o_ref, l_ref, s_scr, p_scr, lse_scr, inv_scr):
        b = pl.program_id(1)
        cur_mask, prev_mask, low = _attn_masks()
        prev_mask = jnp.logical_and(prev_mask, b > 0)
        for h in range(ATTN_HEADS):
            c0 = (h // 2) * LANES
            hm = low if h % 2 == 0 else jnp.logical_not(low)
            q2 = cur[:, c0:c0 + LANES]
            qm = jnp.where(hm, q2, jnp.zeros_like(q2))
            s_scr[h, :, 0:B] = jnp.where(cur_mask, _nt(qm, cur[:, G + c0:G + c0 + LANES]), -jnp.inf)
            s_scr[h, :, B:2 * B] = jnp.where(prev_mask, _nt(qm, prev[:, G + c0:G + c0 + LANES]), -jnp.inf)
        for h in range(ATTN_HEADS):
            for r0 in range(0, B, RQ):
                s = s_scr[h, r0:r0 + RQ, :]
                m = jnp.max(s, axis=1, keepdims=True)
                p = jnp.exp(s - m)
                l = jnp.sum(p, axis=1, keepdims=True)
                p_scr[h, r0:r0 + RQ, :] = p.astype(BF16)
                lse_scr[h, r0:r0 + RQ, :] = jnp.broadcast_to(m + jnp.log(l), (RQ, LANES))
                inv_scr[h, r0:r0 + RQ, :] = jnp.broadcast_to(1.0 / l, (RQ, LANES))
        for pr in range(G // LANES):
            c0 = pr * LANES
            vc, vp = cur[:, 2 * G + c0:2 * G + c0 + LANES], prev[:, 2 * G + c0:2 * G + c0 + LANES]
            o = [(_nn(p_scr[h, :, 0:B], vc) + _nn(p_scr[h, :, B:2 * B], vp)) * inv_scr[h] for h in (2 * pr, 2 * pr + 1)]
            o_ref[:, c0:c0 + LANES] = jnp.where(low, o[0], o[1])
            l_ref[:, c0:c0 + LANES] = jnp.where(low, lse_scr[2 * pr], lse_scr[2 * pr + 1])

    o, l = pl.pallas_call(
        body, name=name, grid=(d, nb),
        in_specs=[pl.BlockSpec((B, W), lambda r, b: (b, r)),
                  pl.BlockSpec((B, W), lambda r, b: (jnp.maximum(b - 1, 0), r))],
        out_specs=[pl.BlockSpec((B, G), lambda r, b: (b, r))] * 2,
        out_shape=[jax.ShapeDtypeStruct((n, d * G), F32)] * 2,
        scratch_shapes=[pltpu.VMEM((ATTN_HEADS, B, 2 * B), F32), pltpu.VMEM((ATTN_HEADS, B, 2 * B), BF16),
                        pltpu.VMEM((ATTN_HEADS, B, LANES), F32), pltpu.VMEM((ATTN_HEADS, B, LANES), F32)],
        compiler_params=_params("parallel", "parallel"),
    )(view, view)
    return o.reshape(S, G), l.reshape(S, G)


def _attn_combine(name, os, ls):
    def fn(o1, o2, o3, l1, l2, l3):
        m = jnp.maximum(jnp.maximum(l1, l2), l3)
        e1, e2, e3 = jnp.exp(l1 - m), jnp.exp(l2 - m), jnp.exp(l3 - m)
        den = e1 + e2 + e3
        return (e1 * o1 + e2 * o2 + e3 * o3) / den, m + jnp.log(den)
    G = os[0].shape[1]
    return _rowwise(name, fn, [(a, 0) for a in (*os, *ls)], [], [(G, F32), (G, F32)])


def _attn_delta(name, do, o, seg):
    def fn(do, o, B):
        return _seg_sum(do * o, B)
    return _rowwise(name, fn, [(do, 0), (o, 0)], [(seg, None)], [(o.shape[1], F32)], rb=SEG_ROWS)


def _attn_group_bwd(name, qkv, do, lse, delta, d):
    S = qkv.shape[0]
    n, W, G = S // d, 3 * ATTN_GW, ATTN_GW
    nb = n // ATTN_BLOCK

    B, RQ = ATTN_BLOCK, ATTN_ROWS
    S_A, DP_A, S_B, DP_B, S_C, DP_C = range(6)
    P_A, DS_A, P_B, DS_B, DS_C = range(5)

    def body(qp, qc, qn, do_c, do_n, l_c, l_n, dl_c, dl_n, out, f_scr, b_scr):
        j = pl.program_id(1)
        low = _attn_masks()[2]
        for h in range(ATTN_HEADS):
            c0 = (h // 2) * LANES
            hm = low if h % 2 == 0 else jnp.logical_not(low)
            k_c, v_c = qc[:, G + c0:G + c0 + LANES], qc[:, 2 * G + c0:2 * G + c0 + LANES]
            k_p, v_p = qp[:, G + c0:G + c0 + LANES], qp[:, 2 * G + c0:2 * G + c0 + LANES]
            zero = jnp.zeros((B, LANES), BF16)
            qmc, qmn = jnp.where(hm, qc[:, c0:c0 + LANES], zero), jnp.where(hm, qn[:, c0:c0 + LANES], zero)
            dmc = jnp.where(hm, do_c[:, c0:c0 + LANES].astype(BF16), zero)
            dmn = jnp.where(hm, do_n[:, c0:c0 + LANES].astype(BF16), zero)
            f_scr[h, S_A], f_scr[h, DP_A] = _nt(qmc, k_c), _nt(dmc, v_c)
            f_scr[h, S_B], f_scr[h, DP_B] = _nt(qmn, k_c), _nt(dmn, v_c)
            f_scr[h, S_C], f_scr[h, DP_C] = _nt(qmc, k_p), _nt(dmc, v_p)
        for h in range(ATTN_HEADS):
            h0 = h * ATTN_HEAD_DIM
            for r0 in range(0, B, RQ):
                rows = slice(r0, r0 + RQ)
                cur_mask, band, _ = _attn_masks(r0, RQ)
                next_mask, prev_mask = jnp.logical_and(band, j < nb - 1), jnp.logical_and(band, j > 0)
                lc, ln = l_c[rows, h0:h0 + 1], l_n[rows, h0:h0 + 1]
                dlc, dln = dl_c[rows, h0:h0 + 1], dl_n[rows, h0:h0 + 1]
                p_a = jnp.where(cur_mask, jnp.exp(f_scr[h, S_A, rows, :] - lc), 0.0)
                p_b = jnp.where(next_mask, jnp.exp(f_scr[h, S_B, rows, :] - ln), 0.0)
                p_c = jnp.where(prev_mask, jnp.exp(f_scr[h, S_C, rows, :] - lc), 0.0)
                b_scr[h, P_A, rows, :] = p_a.astype(BF16)
                b_scr[h, P_B, rows, :] = p_b.astype(BF16)
                b_scr[h, DS_A, rows, :] = (p_a * (f_scr[h, DP_A, rows, :] - dlc)).astype(BF16)
                b_scr[h, DS_B, rows, :] = (p_b * (f_scr[h, DP_B, rows, :] - dln)).astype(BF16)
                b_scr[h, DS_C, rows, :] = (p_c * (f_scr[h, DP_C, rows, :] - dlc)).astype(BF16)
        for pr in range(G // LANES):
            c0 = pr * LANES
            q_c, k_c, q_n = qc[:, c0:c0 + LANES], qc[:, G + c0:G + c0 + LANES], qn[:, c0:c0 + LANES]
            k_p = qp[:, G + c0:G + c0 + LANES]
            d_c, d_n = do_c[:, c0:c0 + LANES].astype(BF16), do_n[:, c0:c0 + LANES].astype(BF16)
            res = []
            for h in (2 * pr, 2 * pr + 1):
                dq = _nn(b_scr[h, DS_A], k_c) + _nn(b_scr[h, DS_C], k_p)
                dk = _tn(b_scr[h, DS_A], q_c) + _tn(b_scr[h, DS_B], q_n)
                dv = _tn(b_scr[h, P_A], d_c) + _tn(b_scr[h, P_B], d_n)
                res.append((dq, dk, dv))
            for t in range(3):
                out[:, t * G + c0:t * G + c0 + LANES] = jnp.where(low, res[0][t], res[1][t])

    prv = lambda r, j: (jnp.maximum(j - 1, 0), r)
    cur = lambda r, j: (j, r)
    nxt = lambda r, j: (jnp.minimum(j + 1, nb - 1), r)
    wide = lambda m: pl.BlockSpec((ATTN_BLOCK, W), m)
    narrow = lambda m: pl.BlockSpec((ATTN_BLOCK, G), m)
    qv, dv, lv, tv = qkv.reshape(n, d * W), do.reshape(n, d * G), lse.reshape(n, d * G), delta.reshape(n, d * G)
    out = pl.pallas_call(
        body, name=name, grid=(d, nb),
        in_specs=[wide(prv), wide(cur), wide(nxt), narrow(cur), narrow(nxt), narrow(cur), narrow(nxt),
                  narrow(cur), narrow(nxt)],
        out_specs=wide(cur), out_shape=jax.ShapeDtypeStruct((n, d * W), F32),
        scratch_shapes=[pltpu.VMEM((ATTN_HEADS, 6, B, B), F32), pltpu.VMEM((ATTN_HEADS, 5, B, B), BF16)],
        compiler_params=_params("parallel", "parallel"),
    )(qv, qv, qv, dv, dv, lv, lv, tv, tv)
    return out.reshape(S, W)


def _chunk_triangle(T, upper):
    i = jnp.arange(T)
    same = (i[:, None] // HGRN_CHUNK) == (i[None, :] // HGRN_CHUNK)
    tri = (i[None, :] >= i[:, None]) if upper else (i[None, :] <= i[:, None])
    return jnp.logical_and(same, tri).astype(F32)


def _hgrn_prologue(qr, fr, lbv, q_s, k_s, b_s, tri_ref, T):
    def pro(s, c):
        sl = pl.ds(pl.multiple_of(s * HGRN_CHUNK, HGRN_CHUNK), HGRN_CHUNK)
        sg = _sigmoid(fr[sl, :])
        qv = qr[sl, :]
        q_s[sl, :] = qv * _sigmoid(qv)
        k_s[sl, :] = (1.0 - lbv) * (1.0 - sg)
        b_s[sl, :] = jnp.log(lbv + (1.0 - lbv) * sg)
        return c
    lax.fori_loop(0, T // HGRN_CHUNK, pro, 0)
    b_s[...] = _nn(tri_ref[...], b_s[...], HI)


def _hgrn_scan_fwd(name, pq, pf, pv, lb):
    S, D = pq.shape
    T = _pick_rows(S, HGRN_TILE)
    NH, NT, C, HD, HB = D // HGRN_HEAD, S // T, HGRN_CHUNK, HGRN_HEAD, HGRN_GROUP
    W = HB * HD
    tri = _chunk_triangle(T, upper=False)

    def body(qr, fr, iv, lb_ref, tri_ref, o_ref, ck_ref, st_ref, q_s, k_s, b_s):
        @pl.when(pl.program_id(1) == 0)
        def _():
            st_ref[...] = jnp.zeros_like(st_ref)

        ck_ref[...] = st_ref[...]
        _hgrn_prologue(qr, fr, lb_ref[...], q_s, k_s, b_s, tri_ref, T)
        row = lax.broadcasted_iota(jnp.int32, (C, 1), 0)

        def chunk(c, carry):
            sl = pl.ds(pl.multiple_of(c * C, C), C)
            for hh in range(HB):
                cs = slice(hh * HD, (hh + 1) * HD)
                q, k, b, v = q_s[sl, cs], k_s[sl, cs], b_s[sl, cs], iv[sl, cs]
                b_last = b[C - 1:C, :]
                st = st_ref[cs, :]
                o = _nt((q * jnp.exp(b)).astype(BF16), st.astype(BF16))
                for s in range(C):
                    e = jnp.exp(jnp.minimum(b - b[s:s + 1, :], 0.0))
                    a = jnp.sum(q * e * k[s:s + 1, :], axis=1, keepdims=True)
                    o = o + jnp.where(row >= s, a, 0.0) * v[s:s + 1, :]
                o_ref[sl, cs] = o
                kd = k * jnp.exp(b_last - b)
                st_ref[cs, :] = st * jnp.exp(b_last) + _tn(v.astype(BF16), kd.astype(BF16))
            return carry

        lax.fori_loop(0, T // C, chunk, 0)

    NG = NH // HB
    col = pl.BlockSpec((T, W), lambda h, t: (t, h))
    return pl.pallas_call(
        body, name=name, grid=(NG, NT),
        in_specs=[col, col, col, pl.BlockSpec((1, W), lambda h, t: (0, h)), pl.BlockSpec((T, T), lambda h, t: (0, 0))],
        out_specs=[col, pl.BlockSpec((W, HD), lambda h, t: (t * NG + h, 0))],
        out_shape=[jax.ShapeDtypeStruct((S, D), F32), jax.ShapeDtypeStruct((NT * NH * HD, HD), F32)],
        scratch_shapes=[pltpu.VMEM((W, HD), F32)] + [pltpu.VMEM((T, W), F32)] * 3,
        compiler_params=_params("parallel", "arbitrary"),
    )(pq, pf, pv, lb, tri)


def _hgrn_scan_bwd(name, pq, pf, pv, lb, ckpt, do):
    S, D = pq.shape
    T = _pick_rows(S, HGRN_TILE)
    NH, NT, C, HD, HB = D // HGRN_HEAD, S // T, HGRN_CHUNK, HGRN_HEAD, HGRN_GROUP
    NC, W, NG = T // C, HB * HD, NH // HB
    tri, tri_up = _chunk_triangle(T, upper=False), _chunk_triangle(T, upper=True)

    def body(qr, fr, iv, do_ref, ck_ref, lb_ref, tri_ref, triu_ref, dq_ref, df_ref, dv_ref, dlb_ref,
             dst_ref, run, save, q_s, k_s, b_s, dq_s, dk_s, db_s):
        @pl.when(pl.program_id(1) == 0)
        def _():
            dst_ref[...] = jnp.zeros_like(dst_ref)
            dlb_ref[...] = jnp.zeros_like(dlb_ref)

        lbv = lb_ref[...]
        _hgrn_prologue(qr, fr, lbv, q_s, k_s, b_s, tri_ref, T)
        row = lax.broadcasted_iota(jnp.int32, (C, 1), 0)
        run[...] = ck_ref[...]

        def replay(c, carry):
            sl = pl.ds(pl.multiple_of(c * C, C), C)
            for hh in range(HB):
                cs = slice(hh * HD, (hh + 1) * HD)
                st = run[cs, :]
                save[pl.ds(pl.multiple_of((hh * NC + c) * HD, HD), HD), :] = st
                k, b, v = k_s[sl, cs], b_s[sl, cs], iv[sl, cs]
                b_last = b[C - 1:C, :]
                kd = k * jnp.exp(b_last - b)
                run[cs, :] = st * jnp.exp(b_last) + _tn(v.astype(BF16), kd.astype(BF16))
            return carry

        lax.fori_loop(0, NC, replay, 0)

        def chunk(ci, carry):
            c = NC - 1 - ci
            sl = pl.ds(pl.multiple_of(c * C, C), C)
            for hh in range(HB):
                cs = slice(hh * HD, (hh + 1) * HD)
                q, k, b, v, g = q_s[sl, cs], k_s[sl, cs], b_s[sl, cs], iv[sl, cs], do_ref[sl, cs]
                st0 = save[pl.ds(pl.multiple_of((hh * NC + c) * HD, HD), HD), :]
                dst1 = dst_ref[cs, :]
                b_last = b[C - 1:C, :]
                eb, ebl, ek = jnp.exp(b), jnp.exp(b_last), jnp.exp(b_last - b)
                dst1_b = dst1.astype(BF16)
                dq = _nn(g.astype(BF16), st0.astype(BF16)) * eb
                dv = _nt((k * ek).astype(BF16), dst1_b)
                dk = _nn(v.astype(BF16), dst1_b) * ek
                db_last = _colsum(dk * k) + _colsum(dst1 * st0) * ebl
                for s in range(C):
                    e = jnp.where(row >= s, jnp.exp(jnp.minimum(b - b[s:s + 1, :], 0.0)), 0.0)
                    ks, vs = k[s:s + 1, :], v[s:s + 1, :]
                    da = jnp.sum(g * vs, axis=1, keepdims=True)
                    a = jnp.sum(q * e * ks, axis=1, keepdims=True)
                    dq = dq + da * e * ks
                    dk = dk + jnp.where(row == s, _colsum(da * q * e), 0.0)
                    dv = dv + jnp.where(row == s, _colsum(a * g), 0.0)
                dq_s[sl, cs] = dq
                dk_s[sl, cs] = dk
                db_s[sl, cs] = q * dq - k * dk + jnp.where(row == C - 1, db_last, 0.0)
                dv_ref[sl, cs] = dv.astype(BF16)
                dst_ref[cs, :] = dst1 * ebl + _tn(g.astype(BF16), (q * eb).astype(BF16))
            return carry

        lax.fori_loop(0, NC, chunk, 0)
        db_s[...] = _nn(triu_ref[...], db_s[...], HI)

        def epi(s, carry):
            sl = pl.ds(pl.multiple_of(s * C, C), C)
            qv = qr[sl, :]
            sq = _sigmoid(qv)
            dq_ref[sl, :] = (dq_s[sl, :] * sq * (1.0 + qv * (1.0 - sq))).astype(BF16)
            sg = _sigmoid(fr[sl, :])
            common = db_s[sl, :] / (lbv + (1.0 - lbv) * sg) - dk_s[sl, :]
            df_ref[sl, :] = (common * (1.0 - lbv) * sg * (1.0 - sg)).astype(BF16)
            dlb_ref[...] += _colsum(common * (1.0 - sg))
            return carry

        lax.fori_loop(0, NC, epi, 0)

    col = pl.BlockSpec((T, W), lambda h, t: (NT - 1 - t, h))
    dq, df, dv, dlb = pl.pallas_call(
        body, name=name, grid=(NG, NT),
        in_specs=[col, col, col, col, pl.BlockSpec((W, HD), lambda h, t: ((NT - 1 - t) * NG + h, 0)),
                  pl.BlockSpec((1, W), lambda h, t: (0, h)),
                  pl.BlockSpec((T, T), lambda h, t: (0, 0)), pl.BlockSpec((T, T), lambda h, t: (0, 0))],
        out_specs=[col, col, col, pl.BlockSpec((1, W), lambda h, t: (0, h))],
        out_shape=[jax.ShapeDtypeStruct((S, D), BF16)] * 3 + [jax.ShapeDtypeStruct((1, D), F32)],
        scratch_shapes=[pltpu.VMEM((W, HD), F32)] * 2 + [pltpu.VMEM((HB * NC * HD, HD), F32)]
        + [pltpu.VMEM((T, W), F32)] * 6,
        compiler_params=_params("parallel", "arbitrary"),
    )(pq, pf, pv, do, ckpt, lb, tri, tri_up)
    return dq, df, dv, dlb


def _hgrn_out_fwd(name, o, gate, norm_g):
    def fn(o, g, ng):
        parts = []
        for h in range(o.shape[1] // HGRN_HEAD):
            c = slice(h * HGRN_HEAD, (h + 1) * HGRN_HEAD)
            oh, gh = o[:, c], g[:, c]
            r = lax.rsqrt(jnp.mean(oh * oh, axis=-1, keepdims=True) + EPS)
            parts.append(oh * r * ng[:, c] * (gh * _sigmoid(gh)))
        return jnp.concatenate(parts, axis=1)
    return _rowwise(name, fn, [(o, 0), (gate, 0)], [(norm_g, None)], [(o.shape[1], BF16)])


def _hgrn_out_bwd(name, o, gate, norm_g, dy):
    def fn(o, g, dy, ng):
        dos, dgs, dngs = [], [], []
        for h in range(o.shape[1] // HGRN_HEAD):
            c = slice(h * HGRN_HEAD, (h + 1) * HGRN_HEAD)
            oh, gh, dyh, ngh = o[:, c], g[:, c], dy[:, c], ng[:, c]
            r = lax.rsqrt(jnp.mean(oh * oh, axis=-1, keepdims=True) + EPS)
            xh = oh * r
            s = _sigmoid(gh)
            dn = dyh * (gh * s)
            dxh = dn * ngh
            dos.append(r * (dxh - xh * jnp.mean(dxh * xh, axis=-1, keepdims=True)))
            dgs.append(dyh * xh * ngh * (s * (1.0 + gh * (1.0 - s))))
            dngs.append(_colsum(dn * xh))
        return jnp.concatenate(dos, axis=1), jnp.concatenate(dgs, axis=1), jnp.concatenate(dngs, axis=1)
    D = o.shape[1]
    return _rowwise(name, fn, [(o, 0), (gate, 0), (dy, 0)], [(norm_g, None)], [(D, F32), (D, BF16)], [(1, D)])


def _lower_bound_fwd(name, logits, layer):
    n = logits.shape[0]

    def body(x_ref, o_ref):
        rows = [x_ref[i:i + 1, :] for i in range(n)]
        m = functools.reduce(jnp.maximum, rows)
        e = [jnp.exp(r - m) for r in rows]
        den = functools.reduce(jnp.add, e)
        o_ref[...] = functools.reduce(jnp.add, e[1:layer + 1]) / den

    return pl.pallas_call(body, name=name, out_shape=jax.ShapeDtypeStruct((1, logits.shape[1]), F32))(logits)


def _lower_bound_bwd(name, logits, dlb, layer):
    n = logits.shape[0]

    def body(x_ref, d_ref, o_ref):
        rows = [x_ref[i:i + 1, :] for i in range(n)]
        m = functools.reduce(jnp.maximum, rows)
        e = [jnp.exp(r - m) for r in rows]
        den = functools.reduce(jnp.add, e)
        s = [v / den for v in e]
        d = d_ref[...]
        inner = functools.reduce(jnp.add, s[1:layer + 1]) * d
        for i in range(n):
            o_ref[i:i + 1, :] = s[i] * ((d if 1 <= i <= layer else 0.0) - inner)

    return pl.pallas_call(body, name=name, out_shape=jax.ShapeDtypeStruct(logits.shape, F32))(logits, dlb)


def _row(v):
    return v.reshape(1, -1)


def _ffn_fwd(l, x1, w):
    h2 = _rmsnorm_fwd(f"ffn{l}_norm", x1, _row(w["ffn_norm"][l]))
    a, *u = _ffn_up_fused(f"ffn{l}_up", h2, w["ffn_w_up"][l], w["ffn_conv_w"][l], _row(w["ffn_conv_b"][l]))
    x2 = _matmul(f"ffn{l}_down", [(a, w["ffn_w_down"][l])], residual=x1)
    return x2, (x1, h2, u, a)


def _ffn_bwd(l, dx2, saved, w, grads):
    x1, h2, (u0g, u0u, ug, uu), a = saved
    w_up, w_down = w["ffn_w_up"][l], w["ffn_w_down"][l]
    F = w_down.shape[0]
    grads["ffn_w_down"][l] = _matmul_tn(f"ffn{l}_dwdown", a, dx2)
    dg, du, dcwg, dcwu, dcbg, dcbu = _ffn_gate_bwd_fused(
        f"ffn{l}_dgate", dx2, w_down, u0g, u0u, ug, uu, w["ffn_conv_w"][l])
    grads["ffn_conv_w"][l] = jnp.concatenate([dcwg, dcwu], axis=1)
    grads["ffn_conv_b"][l] = jnp.concatenate([dcbg, dcbu], axis=1)[0]
    grads["ffn_w_up"][l] = jnp.concatenate(
        [_matmul_tn(f"ffn{l}_dwup0", h2, dg), _matmul_tn(f"ffn{l}_dwup1", h2, du)], axis=1)
    dh2 = _matmul(f"ffn{l}_dh", [(dg, w_up[:, :F]), (du, w_up[:, F:])], trans_b=True)
    dx1, dgain = _rmsnorm_bwd(f"ffn{l}_dnorm", x1, _row(w["ffn_norm"][l]), dh2, dx2)
    grads["ffn_norm"][l] = dgain[0]
    return dx1


def _attn_gain_rows(w, j, g):
    scale = ATTN_HEAD_DIM ** -0.5
    qg = jnp.tile(w["attn_q_gain"][j, g] * scale, ATTN_HEADS)
    kg = jnp.tile(w["attn_k_gain"][j, g], ATTN_HEADS)
    gain = jnp.concatenate([qg, kg, jnp.ones((ATTN_GW,), F32)])
    is_norm = jnp.concatenate([jnp.ones((2 * ATTN_GW,), F32), jnp.zeros((ATTN_GW,), F32)])
    return _row(gain), _row(is_norm)


def _attn_fwd(l, j, x, w):
    h = _rmsnorm_fwd(f"mix{l}_norm", x, _row(w["mixer_norm"][l]))
    w_in = w["attn_w_in"][j]
    seg = _segment_matrix(LANES, ATTN_HEAD_DIM)
    GW3 = 3 * ATTN_GW
    proj, qkv, os, ls = [], [], [], []
    for g, d in enumerate(ATTN_DILATIONS):
        gain, is_norm = _attn_gain_rows(w, j, g)
        proj.append(_matmul(f"attn{l}_in{g}", [(h, w_in[:, g * GW3:(g + 1) * GW3])]))
        qkv.append(_qknorm_fwd(f"attn{l}_qknorm{g}", proj[g], gain, is_norm, seg))
        o, lse = _attn_group_fwd(f"attn{l}_core{g}", qkv[g], d)
        os.append(o)
        ls.append(lse)
    o, lse = _attn_combine(f"attn{l}_combine", os, ls)
    x1 = _matmul(f"attn{l}_out", [(o, w["attn_w_out"][j])], residual=x)
    return x1, (x, h, proj, qkv, o, lse)


def _attn_bwd(l, j, dx1, saved, w, grads):
    x, h, proj, qkv, o, lse = saved
    w_in, w_out = w["attn_w_in"][j], w["attn_w_out"][j]
    seg = _segment_matrix(LANES, ATTN_HEAD_DIM)
    GW3 = 3 * ATTN_GW
    grads["attn_w_out"][j] = _matmul_tn(f"attn{l}_dwout", o, dx1)
    do = _matmul(f"attn{l}_do", [(dx1, w_out)], trans_b=True)
    delta = _attn_delta(f"attn{l}_delta", do, o, seg)
    dproj, dwin, dqg, dkg = [], [], [], []
    for g, d in enumerate(ATTN_DILATIONS):
        gain, is_norm = _attn_gain_rows(w, j, g)
        dqkv = _attn_group_bwd(f"attn{l}_dcore{g}", qkv[g], do, lse, delta, d)
        dp, dgain = _qknorm_bwd(f"attn{l}_dqknorm{g}", proj[g], dqkv, gain, is_norm, seg)
        dproj.append(dp)
        dwin.append(_matmul_tn(f"attn{l}_dwin{g}", h, dp))
        per_head = dgain.reshape(3, ATTN_HEADS, ATTN_HEAD_DIM).sum(axis=1)
        dqg.append(per_head[0] * ATTN_HEAD_DIM ** -0.5)
        dkg.append(per_head[1])
    grads["attn_w_in"][j] = jnp.concatenate(dwin, axis=1)
    grads["attn_q_gain"][j] = jnp.stack(dqg)
    grads["attn_k_gain"][j] = jnp.stack(dkg)
    dh = _matmul(f"attn{l}_dh", [(dproj[g], w_in[:, g * GW3:(g + 1) * GW3]) for g in range(3)], trans_b=True)
    dx, dg = _rmsnorm_bwd(f"mix{l}_dnorm", x, _row(w["mixer_norm"][l]), dh, dx1)
    grads["mixer_norm"][l] = dg[0]
    return dx


def _conv_fwd(l, j, x, w):
    h = _rmsnorm_fwd(f"mix{l}_norm", x, _row(w["mixer_norm"][l]))
    w_in, b_in = w["conv_w_in"][j], _row(w["conv_b_in"][j])
    C = w_in.shape[1] // 2
    ua = _matmul(f"conv{l}_in0", [(h, w_in[:, :C])], bias=b_in[:, :C])
    ug = _matmul(f"conv{l}_in1", [(h, w_in[:, C:])], bias=b_in[:, C:])
    glu = _glu_fwd(f"conv{l}_glu", ua, ug)
    c = _dwconv(f"conv{l}_dw", glu, w["conv_dw_w"][j], _row(w["conv_dw_b"][j]), reverse=False)
    sw = _ln_silu_fwd(f"conv{l}_ln", c, _row(w["conv_ln_g"][j]), _row(w["conv_ln_b"][j]))
    x1 = _matmul(f"conv{l}_out", [(sw, w["conv_w_out"][j])], bias=_row(w["conv_b_out"][j]), residual=x)
    return x1, (x, h, ua, ug, glu, c, sw)


def _conv_bwd(l, j, dx1, saved, w, grads):
    x, h, ua, ug, glu, c, sw = saved
    w_in, w_out, dw_w = w["conv_w_in"][j], w["conv_w_out"][j], w["conv_dw_w"][j]
    C = w_out.shape[0]
    grads["conv_b_out"][j] = _column_sums(f"conv{l}_dbout", dx1)[0]
    grads["conv_w_out"][j] = _matmul_tn(f"conv{l}_dwout", sw, dx1)
    dsw = _matmul(f"conv{l}_dsw", [(dx1, w_out)], trans_b=True)
    dc, dlg, dlb = _ln_silu_bwd(f"conv{l}_dln", c, _row(w["conv_ln_g"][j]), _row(w["conv_ln_b"][j]), dsw)
    grads["conv_ln_g"][j], grads["conv_ln_b"][j] = dlg[0], dlb[0]
    dglu = _dwconv(f"conv{l}_ddw", dc, dw_w, jnp.zeros((1, C), F32), reverse=True)
    gw, gb = _dwconv_wgrad(f"conv{l}_ddww", glu, dc, dw_w.shape[0])
    grads["conv_dw_w"][j], grads["conv_dw_b"][j] = gw, gb[0]
    da, dgate, sa, sg = _glu_bwd(f"conv{l}_dglu", ua, ug, dglu)
    grads["conv_b_in"][j] = jnp.concatenate([sa, sg], axis=1)[0]
    grads["conv_w_in"][j] = jnp.concatenate(
        [_matmul_tn(f"conv{l}_dwin0", h, da), _matmul_tn(f"conv{l}_dwin1", h, dgate)], axis=1)
    dh = _matmul(f"conv{l}_dh", [(da, w_in[:, :C]), (dgate, w_in[:, C:])], trans_b=True)
    dx, dg = _rmsnorm_bwd(f"mix{l}_dnorm", x, _row(w["mixer_norm"][l]), dh, dx1)
    grads["mixer_norm"][l] = dg[0]
    return dx


def _hgrn_fwd(l, j, x, w):
    h = _rmsnorm_fwd(f"mix{l}_norm", x, _row(w["mixer_norm"][l]))
    w_in = w["hgrn_w_in"][j]
    D = w_in.shape[1] // 4
    pq, pf, pv, pg = [_matmul(f"hgrn{l}_in{s}", [(h, w_in[:, s * D:(s + 1) * D])]) for s in range(4)]
    lb = _lower_bound_fwd(f"hgrn{l}_lb", w["hgrn_lb_logits"], l)
    o, ckpt = _hgrn_scan_fwd(f"hgrn{l}_scan", pq, pf, pv, lb)
    y = _hgrn_out_fwd(f"hgrn{l}_gate", o, pg, _row(w["hgrn_norm_g"][j]))
    x1 = _matmul(f"hgrn{l}_out", [(y, w["hgrn_w_out"][j])], residual=x)
    return x1, (x, h, pq, pf, pv, pg, lb, o, ckpt, y)


def _hgrn_bwd(l, j, dx1, saved, w, grads):
    x, h, pq, pf, pv, pg, lb, o, ckpt, y = saved
    w_in, w_out = w["hgrn_w_in"][j], w["hgrn_w_out"][j]
    D = w_out.shape[0]
    grads["hgrn_w_out"][j] = _matmul_tn(f"hgrn{l}_dwout", y, dx1)
    dy = _matmul(f"hgrn{l}_dy", [(dx1, w_out)], trans_b=True)
    do, dpg, dng = _hgrn_out_bwd(f"hgrn{l}_dgate", o, pg, _row(w["hgrn_norm_g"][j]), dy)
    grads["hgrn_norm_g"][j] = dng[0]
    dpq, dpf, dpv, dlb = _hgrn_scan_bwd(f"hgrn{l}_dscan", pq, pf, pv, lb, ckpt, do)
    grads["hgrn_lb_logits"] = grads["hgrn_lb_logits"] + _lower_bound_bwd(f"hgrn{l}_dlb", w["hgrn_lb_logits"], dlb, l)
    dps = [dpq, dpf, dpv, dpg]
    grads["hgrn_w_in"][j] = jnp.concatenate([_matmul_tn(f"hgrn{l}_dwin{s}", h, dps[s]) for s in range(4)], axis=1)
    dh = _matmul(f"hgrn{l}_dh", [(dps[s], w_in[:, s * D:(s + 1) * D]) for s in range(4)], trans_b=True)
    dx, dg = _rmsnorm_bwd(f"mix{l}_dnorm", x, _row(w["mixer_norm"][l]), dh, dx1)
    grads["mixer_norm"][l] = dg[0]
    return dx


_MIXERS = ((_attn_fwd, _attn_bwd), (_conv_fwd, _conv_bwd), (_hgrn_fwd, _hgrn_bwd))
_PER_MIXER = {"attn": 0, "conv": 1, "hgrn": 2}


def _local_step(x, target, w):
    depth = w["mixer_norm"].shape[0]
    grads = {}
    for name, v in w.items():
        lead = v.shape[0]
        grads[name] = jnp.zeros(v.shape, F32) if name == "hgrn_lb_logits" else [None] * lead
    saved = []
    for l in range(depth):
        fwd, _ = _MIXERS[l % N_MIXERS]
        x, s_mix = fwd(l, l // N_MIXERS, x, w)
        x, s_ffn = _ffn_fwd(l, x, w)
        saved.append((s_mix, s_ffn))
    dx, loss_cols = _loss_grad("loss", x, target)
    for l in reversed(range(depth)):
        _, bwd = _MIXERS[l % N_MIXERS]
        s_mix, s_ffn = saved[l]
        dx = _ffn_bwd(l, dx, s_ffn, w, grads)
        dx = bwd(l, l // N_MIXERS, dx, s_mix, w, grads)
    grads = {k: (v if k == "hgrn_lb_logits" else jnp.stack(v)) for k, v in grads.items()}
    return jnp.sum(loss_cols), dx, grads


_HBM = pl.BlockSpec(memory_space=pltpu.HBM)


def _chip_peers():
    x, y, c = lax.axis_index("x"), lax.axis_index("y"), lax.axis_index("c")
    return 2 * x + y, (x, y, c), [(1 - x, y), (x, 1 - y), (1 - x, 1 - y)]


def _exchange_chips(name, src):
    def body(src_ref, out_ref, send_sems, recv_sems, local_sem):
        p, (x, y, c), peers = _chip_peers()
        mine = pltpu.make_async_copy(src_ref.at[p], out_ref.at[p], local_sem)
        mine.start()

        def copy(k, slab_from, slab_to, peer):
            return pltpu.make_async_remote_copy(
                src_ref=src_ref.at[slab_from], dst_ref=out_ref.at[slab_to], send_sem=send_sems.at[k],
                recv_sem=recv_sems.at[k], device_id=(peer[0], peer[1], c), device_id_type=MESH)

        sends = [copy(k, 2 * px + py, p, (px, py)) for k, (px, py) in enumerate(peers)]
        for s in sends:
            s.start()
        for k, (px, py) in enumerate(peers):
            copy(k, p, 2 * px + py, (px, py)).wait_recv()
        for s in sends:
            s.wait_send()
        mine.wait()

    return pl.pallas_call(
        body, name=name, in_specs=[_HBM], out_specs=_HBM, out_shape=jax.ShapeDtypeStruct(src.shape, src.dtype),
        scratch_shapes=[pltpu.SemaphoreType.DMA((3,)), pltpu.SemaphoreType.DMA((3,)), pltpu.SemaphoreType.DMA],
    )(src)


def _all_gather_chips(name, shard):
    R = shard.shape[0]
    half = R // 2

    def body(src_ref, out_ref, send_sems, recv_sems, local_sem):
        p, (x, y, c), peers = _chip_peers()
        mine = pltpu.make_async_copy(src_ref, out_ref.at[p], local_sem)
        mine.start()

        def rows(slab, core):
            return out_ref.at[slab, pl.ds(core * half, half), :]

        def over_ici(k, slab, peer):
            src = src_ref.at[pl.ds(c * half, half), :] if slab is None else rows(slab, c)
            return pltpu.make_async_remote_copy(
                src_ref=src, dst_ref=rows(p if slab is None else slab, c), send_sem=send_sems.at[k],
                recv_sem=recv_sems.at[k], device_id=(peer[0], peer[1], c), device_id_type=MESH)

        def to_sibling(k, slab, core):
            return pltpu.make_async_remote_copy(
                src_ref=rows(slab, core), dst_ref=rows(slab, core), send_sem=send_sems.at[3 + k],
                recv_sem=recv_sems.at[3 + k], device_id=(x, y, 1 - c), device_id_type=MESH)

        sends = [over_ici(k, None, peer) for k, peer in enumerate(peers)]
        for s in sends:
            s.start()
        passed = []
        for k, (px, py) in enumerate(peers):
            over_ici(k, 2 * px + py, (px, py)).wait_recv()
            passed.append(to_sibling(k, 2 * px + py, c))
            passed[k].start()
        for k, (px, py) in enumerate(peers):
            to_sibling(k, 2 * px + py, 1 - c).wait_recv()
        for s in sends + passed:
            s.wait_send()
        mine.wait()

    return pl.pallas_call(
        body, name=name, in_specs=[_HBM], out_specs=_HBM,
        out_shape=jax.ShapeDtypeStruct((N_CHIPS,) + shard.shape, shard.dtype),
        scratch_shapes=[pltpu.SemaphoreType.DMA((6,)), pltpu.SemaphoreType.DMA((6,)), pltpu.SemaphoreType.DMA],
    )(shard)


def _swap_cores(name, v):
    def body(v_ref, out_ref, send_sem, recv_sem):
        x, y, c = lax.axis_index("x"), lax.axis_index("y"), lax.axis_index("c")
        cp = pltpu.make_async_remote_copy(src_ref=v_ref, dst_ref=out_ref, send_sem=send_sem, recv_sem=recv_sem,
                                          device_id=(x, y, 1 - c), device_id_type=MESH)
        cp.start()
        cp.wait()

    return pl.pallas_call(
        body, name=name, in_specs=[_HBM], out_specs=_HBM, out_shape=jax.ShapeDtypeStruct(v.shape, v.dtype),
        scratch_shapes=[pltpu.SemaphoreType.DMA, pltpu.SemaphoreType.DMA],
    )(v)


_WEIGHTS = ("mixer_norm", "ffn_norm", "attn_w_in", "attn_q_gain", "attn_k_gain", "attn_w_out", "conv_w_in",
            "conv_b_in", "conv_dw_w", "conv_dw_b", "conv_ln_g", "conv_ln_b", "conv_w_out", "conv_b_out",
            "hgrn_w_in", "hgrn_lb_logits", "hgrn_norm_g", "hgrn_w_out", "ffn_w_up", "ffn_conv_w", "ffn_conv_b",
            "ffn_w_down")
_SHARD_AXIS = {"attn_w_in": 2, "attn_w_out": 2, "conv_w_in": 2, "conv_dw_w": 2, "conv_w_out": 1, "hgrn_w_in": 2,
               "hgrn_norm_g": 1, "hgrn_w_out": 1, "ffn_w_up": 2, "ffn_conv_w": 2, "ffn_w_down": 1}
_MATMUL_WEIGHTS = ("attn_w_in", "attn_w_out", "conv_w_in", "conv_w_out", "hgrn_w_in", "hgrn_w_out", "ffn_w_up",
                   "ffn_w_down")
PACK_COLS = 1024
PACK_ROWS = 512


def _pack(arrays, nlead, dtype):
    lead = arrays[0].shape[:nlead]
    flat = []
    for a in arrays:
        f = a.reshape(lead + (-1,)).astype(dtype)
        flat.append(jnp.pad(f, [(0, 0)] * nlead + [(0, (-f.shape[-1]) % PACK_COLS)]))
    buf = jnp.concatenate(flat, axis=-1)
    buf = jnp.pad(buf, [(0, 0)] * nlead + [(0, (-buf.shape[-1]) % (PACK_COLS * PACK_ROWS))])
    return buf.reshape(lead + (-1, PACK_COLS))


def _unpack(buf, shapes, nlead):
    lead = buf.shape[:nlead]
    flat = buf.reshape(lead + (-1,))
    out, off = [], 0
    for shape in shapes:
        n = 1
        for s in shape:
            n *= s
        out.append(flat[..., off:off + n].reshape(lead + tuple(shape)))
        off += n + (-n) % PACK_COLS
    return out


def _merge_shards(piece, axis):
    moved = jnp.moveaxis(piece, 0, axis)
    shape = moved.shape
    return moved.reshape(shape[:axis] + (shape[axis] * shape[axis + 1],) + shape[axis + 2:])


def _split_shards(full, axis):
    shape = full.shape
    cut = full.reshape(shape[:axis] + (N_CHIPS, shape[axis] // N_CHIPS) + shape[axis + 1:])
    return jnp.moveaxis(cut, axis, 0)


def _gather_weights(local):
    big = [n for n in _WEIGHTS if n in _MATMUL_WEIGHTS]
    small = [n for n in _WEIGHTS if n in _SHARD_AXIS and n not in _MATMUL_WEIGHTS]
    full = {n: local[n] for n in _WEIGHTS if n not in _SHARD_AXIS}
    for names, dtype, tag in ((big, BF16, "comm_gather_matmul_weights"), (small, F32, "comm_gather_small_weights")):
        gathered = _all_gather_chips(tag, _pack([local[n] for n in names], 0, dtype))
        pieces = _unpack(gathered, [local[n].shape for n in names], 1)
        for n, piece in zip(names, pieces):
            full[n] = _merge_shards(piece, _SHARD_AXIS[n])
    return full


def _reduce_gradients(grads, local):
    out = {}
    big = [n for n in _WEIGHTS if n in _MATMUL_WEIGHTS]
    rest = [n for n in _WEIGHTS if n not in _MATMUL_WEIGHTS]
    for names, dtype, tag in ((big, BF16, "matmul"), (rest, F32, "small")):
        slabs = []
        for n in names:
            g = grads[n]
            if n in _SHARD_AXIS:
                slabs.append(_split_shards(g, _SHARD_AXIS[n]))
            else:
                slabs.append(jnp.broadcast_to(g[None], (N_CHIPS,) + g.shape))
        packed = _pack(slabs, 1, dtype)
        landed = _exchange_chips(f"comm_scatter_{tag}_gradients", packed)
        partial = _sum_slabs(f"sum_chips_{tag}", landed)
        other = _swap_cores(f"comm_swap_{tag}_sums", partial)
        total = _add(f"sum_cores_{tag}", [partial, other])
        out.update(zip(names, _unpack(total, [local[n].shape for n in names], 0)))
    return out


def kernel(x, mixer_norm, ffn_norm, attn_w_in, attn_q_gain, attn_k_gain, attn_w_out, conv_w_in, conv_b_in, conv_dw_w, conv_dw_b, conv_ln_g, conv_ln_b, conv_w_out, conv_b_out, hgrn_w_in, hgrn_lb_logits, hgrn_norm_g, hgrn_w_out, ffn_w_up, ffn_conv_w, ffn_conv_b, ffn_w_down, loss_target, m_mixer_norm, m_ffn_norm, m_attn_w_in, m_attn_q_gain, m_attn_k_gain, m_attn_w_out, m_conv_w_in, m_conv_b_in, m_conv_dw_w, m_conv_dw_b, m_conv_ln_g, m_conv_ln_b, m_conv_w_out, m_conv_b_out, m_hgrn_w_in, m_hgrn_lb_logits, m_hgrn_norm_g, m_hgrn_w_out, m_ffn_w_up, m_ffn_conv_w, m_ffn_conv_b, m_ffn_w_down, v_mixer_norm, v_ffn_norm, v_attn_w_in, v_attn_q_gain, v_attn_k_gain, v_attn_w_out, v_conv_w_in, v_conv_b_in, v_conv_dw_w, v_conv_dw_b, v_conv_ln_g, v_conv_ln_b, v_conv_w_out, v_conv_b_out, v_hgrn_w_in, v_hgrn_lb_logits, v_hgrn_norm_g, v_hgrn_w_out, v_ffn_w_up, v_ffn_conv_w, v_ffn_conv_b, v_ffn_w_down):
    given = dict(locals())
    local = {n: given[n] for n in _WEIGHTS}
    full = _gather_weights(local)
    loss, dx, grads = _local_step(x[0], loss_target[0], full)
    loss = lax.psum(loss, ("x", "y", "c"))
    grad = _reduce_gradients(grads, local)
    delta, new_m, new_v = {}, {}, {}
    for n in _WEIGHTS:
        shape = local[n].shape
        as2d = lambda a: a.reshape(-1, shape[-1])
        d, m, v = _adamw(f"adamw_{n}", as2d(local[n]), as2d(grad[n]), as2d(given["m_" + n]), as2d(given["v_" + n]))
        delta[n], new_m[n], new_v[n] = d.reshape(shape), m.reshape(shape), v.reshape(shape)
    return (loss, dx[None], *[grad[n] for n in _WEIGHTS], *[delta[n] for n in _WEIGHTS],
            *[new_m[n] for n in _WEIGHTS], *[new_v[n] for n in _WEIGHTS])
```

```python
import functools

import jax
import jax.numpy as jnp
from jax import lax
from jax.experimental import pallas as pl
from jax.experimental.pallas import tpu as pltpu

F32 = jnp.float32
BF16 = jnp.bfloat16

EPS = 1e-6
N_MIXERS = 3
ATTN_DILATIONS = (1, 4, 16)
ATTN_BLOCK = 128
ATTN_HEADS = 8
ATTN_HEAD_DIM = 64
ATTN_GW = ATTN_HEADS * ATTN_HEAD_DIM
ATTN_ROWS = 32
HGRN_HEAD = 128
HGRN_CHUNK = 16
HGRN_TILE = 256
HGRN_GROUP = 4
ADAM_LR, ADAM_B1, ADAM_B2, ADAM_EPS, ADAM_WD, ADAM_STEP = 0.001, 0.9, 0.999, 1e-08, 0.01, 10

LANES = 128
SUBLANES = 8
VMEM_LIMIT = 56 * 1024 * 1024
N_CHIPS = 4
MESH = pl.DeviceIdType.MESH

HI = lax.Precision.HIGHEST


def _params(*sem):
    return pltpu.CompilerParams(dimension_semantics=sem, vmem_limit_bytes=VMEM_LIMIT)


def _pick(n, target):
    if n <= target:
        return n
    best = None
    for t in range(LANES, target + 1, LANES):
        if n % t == 0:
            best = t
    assert best is not None, (n, target)
    return best


def _pick_rows(n, target):
    if n <= target:
        return n
    for t in range(target, 15, -16):
        if n % t == 0:
            return t
    return n


def _dot(a, b, dims, precision=None):
    return lax.dot_general(a, b, (dims, ((), ())), precision=precision, preferred_element_type=F32)


def _nn(a, b, precision=None):
    return _dot(a, b, ((1,), (0,)), precision)


def _nt(a, b, precision=None):
    return _dot(a, b, ((1,), (1,)), precision)


def _tn(a, b, precision=None):
    return _dot(a, b, ((0,), (0,)), precision)


def _sigmoid(x):
    return 1.0 / (1.0 + jnp.exp(-x))


ROWWISE_UNROLL_ROWS = 64


def _rowwise(name, fn, rows, pars=(), outs=(), accs=(), *, tc=None, tm=512, rb=16):
    S = rows[0][0].shape[0]
    tm = _pick_rows(S, tm)
    rb = rb if tm % rb == 0 else tm
    width = tc if tc is not None else None
    ncol = 1
    if tc is not None:
        base = outs[0][0] if outs else accs[0][1]
        ncol = base // tc
    n_r, n_p, n_o, n_a = len(rows), len(pars), len(outs), len(accs)

    def body(*refs):
        row_refs, par_refs = refs[:n_r], refs[n_r:n_r + n_p]
        out_refs, acc_refs = refs[n_r + n_p:n_r + n_p + n_o], refs[n_r + n_p + n_o:]
        if n_a:
            @pl.when(pl.program_id(1) == 0)
            def _():
                for a in acc_refs:
                    a[...] = jnp.zeros_like(a)

        def step(s, carry):
            sl = pl.ds(pl.multiple_of(s * rb, rb), rb)
            res = fn(*[r[sl, :] for r in row_refs], *[p[...] for p in par_refs])
            res = res if isinstance(res, tuple) else (res,)
            for o, v in zip(out_refs, res[:n_o]):
                o[sl, :] = v.astype(o.dtype)
            for a, v in zip(acc_refs, res[n_o:]):
                a[...] += v
            return carry

        lax.fori_loop(0, tm // rb, step, 0, unroll=min(tm // rb, max(2, ROWWISE_UNROLL_ROWS // rb)))

    def row_spec(c, off):
        if tc is None:
            return pl.BlockSpec((tm, c), lambda j, i: (i, 0))
        return pl.BlockSpec((tm, tc), lambda j, i, o=off // tc: (i, j + o))

    def par_spec(shape, off):
        if off is None or tc is None:
            return pl.BlockSpec(shape, lambda j, i: (0, 0))
        return pl.BlockSpec((shape[0], tc), lambda j, i, o=off // tc: (0, j + o))

    in_specs = [row_spec(a.shape[1], off) for a, off in rows]
    in_specs += [par_spec(a.shape, off) for a, off in pars]
    out_specs = [row_spec(c, 0) for c, _ in outs] + [par_spec(s, 0) for s in accs]
    out_shape = [jax.ShapeDtypeStruct((S, c), d) for c, d in outs]
    out_shape += [jax.ShapeDtypeStruct(s, F32) for s in accs]
    res = pl.pallas_call(
        body, name=name, grid=(ncol, S // tm), in_specs=in_specs, out_specs=out_specs, out_shape=out_shape,
        compiler_params=_params("parallel", "arbitrary" if n_a else "parallel"),
    )(*[a for a, _ in rows], *[a for a, _ in pars])
    return res[0] if len(res) == 1 else tuple(res)


MATMUL_VMEM = 36 * 1024 * 1024


def _matmul_tiles(M, N, pairs, out_dtype, residual):
    tm = _pick_rows(M, 512)
    for tn in sorted({_pick(N, t) for t in range(LANES, 2049, LANES)}, reverse=True):
        step = sum(tm * a.shape[1] * a.dtype.itemsize + a.shape[1] * tn * b.dtype.itemsize for a, b in pairs)
        step += tm * tn * (jnp.dtype(out_dtype).itemsize + (4 if residual is not None else 0))
        if 2 * step <= MATMUL_VMEM:
            return tm, tn
    return tm, LANES


def _matmul(name, pairs, *, trans_b=False, bias=None, residual=None, out_dtype=F32):
    M = pairs[0][0].shape[0]
    N = pairs[0][1].shape[0] if trans_b else pairs[0][1].shape[1]
    tm, tn = _matmul_tiles(M, N, pairs, out_dtype, residual)
    n = len(pairs)

    def body(*refs):
        acc = None
        for i in range(n):
            a = refs[2 * i][...].astype(BF16)
            b = refs[2 * i + 1][...].astype(BF16)
            d = _nt(a, b) if trans_b else _nn(a, b)
            acc = d if acc is None else acc + d
        k = 2 * n
        if bias is not None:
            acc = acc + refs[k][...]
            k += 1
        if residual is not None:
            acc = acc + refs[k][...]
            k += 1
        refs[k][...] = acc.astype(out_dtype)

    in_specs, args = [], []
    for a, b in pairs:
        K = a.shape[1]
        in_specs.append(pl.BlockSpec((tm, K), lambda j, i: (i, 0)))
        in_specs.append(pl.BlockSpec((tn, K), lambda j, i: (j, 0)) if trans_b
                        else pl.BlockSpec((K, tn), lambda j, i: (0, j)))
        args += [a, b]
    if bias is not None:
        in_specs.append(pl.BlockSpec((1, tn), lambda j, i: (0, j)))
        args.append(bias)
    if residual is not None:
        in_specs.append(pl.BlockSpec((tm, tn), lambda j, i: (i, j)))
        args.append(residual)
    return pl.pallas_call(
        body, name=name, grid=(N // tn, M // tm), in_specs=in_specs,
        out_specs=pl.BlockSpec((tm, tn), lambda j, i: (i, j)),
        out_shape=jax.ShapeDtypeStruct((M, N), out_dtype), compiler_params=_params("parallel", "parallel"),
    )(*args)


def _matmul_tn(name, a, b, *, tm=1408, tn=1408, tk=1024):
    S, M = a.shape
    N = b.shape[1]
    tm, tn, tk = _pick(M, tm), _pick(N, tn), _pick_rows(S, tk)

    def body(a_ref, b_ref, o_ref):
        @pl.when(pl.program_id(2) == 0)
        def _():
            o_ref[...] = jnp.zeros_like(o_ref)

        o_ref[...] += _tn(a_ref[...].astype(BF16), b_ref[...].astype(BF16))

    return pl.pallas_call(
        body, name=name, grid=(M // tm, N // tn, S // tk),
        in_specs=[pl.BlockSpec((tk, tm), lambda i, j, k: (k, i)), pl.BlockSpec((tk, tn), lambda i, j, k: (k, j))],
        out_specs=pl.BlockSpec((tm, tn), lambda i, j, k: (i, j)),
        out_shape=jax.ShapeDtypeStruct((M, N), F32), compiler_params=_params("parallel", "parallel", "arbitrary"),
    )(a, b)


def _halo_rows(K):
    return 8 if K <= 9 else 32


def _shifted_copies(ext, shifted, K):
    if K <= SUBLANES:
        return
    n = shifted.shape[1]
    for s in range(1, SUBLANES):
        shifted[s, 0:n, :] = ext[s:s + n, :]


def _window(ext, shifted, K, off, rows):
    s = off % SUBLANES
    if K <= SUBLANES or s == 0:
        return ext[off:off + rows, :]
    return shifted[s, off - s:off - s + rows, :]


def _dwconv(name, x, w, b, *, reverse, out_dtype=F32):
    S, C = x.shape
    K = w.shape[0]
    H = _halo_rows(K)
    tm, tc = _pick_rows(S, 512 if K <= 4 else 256), _pick(C, 1408 if K <= 4 else 256)
    nrow = S // tm
    RB = 16 if out_dtype == BF16 else 8

    def body(x_ref, h_ref, w_ref, b_ref, o_ref, ext, shifted):
        i = pl.program_id(1)
        edge = (i == nrow - 1) if reverse else (i == 0)
        halo = jnp.where(edge, 0.0, h_ref[...].astype(F32))
        if reverse:
            ext[0:tm, :] = x_ref[...].astype(F32)
            ext[tm:tm + H, :] = halo
        else:
            ext[0:H, :] = halo
            ext[H:H + tm, :] = x_ref[...].astype(F32)
        _shifted_copies(ext, shifted, K)
        wv = w_ref[...]
        for s in range(tm // RB):
            acc = jnp.broadcast_to(b_ref[...], (RB, tc))
            for k in range(K):
                off = s * RB + ((K - 1 - k) if reverse else (H - (K - 1) + k))
                acc = acc + wv[k:k + 1, :] * _window(ext, shifted, K, off, RB)
            o_ref[s * RB:(s + 1) * RB, :] = acc.astype(out_dtype)

    r = tm // H
    if reverse:
        halo_map = lambda j, i: (jnp.minimum((i + 1) * r, S // H - 1), j)
    else:
        halo_map = lambda j, i: (jnp.maximum(i * r - 1, 0), j)
    return pl.pallas_call(
        body, name=name, grid=(C // tc, nrow),
        in_specs=[pl.BlockSpec((tm, tc), lambda j, i: (i, j)), pl.BlockSpec((H, tc), halo_map),
                  pl.BlockSpec((K, tc), lambda j, i: (0, j)), pl.BlockSpec((1, tc), lambda j, i: (0, j))],
        out_specs=pl.BlockSpec((tm, tc), lambda j, i: (i, j)),
        out_shape=jax.ShapeDtypeStruct((S, C), out_dtype),
        scratch_shapes=[pltpu.VMEM((tm + H, tc), F32), pltpu.VMEM((SUBLANES, tm + H - SUBLANES, tc), F32)],
        compiler_params=_params("parallel", "parallel"),
    )(x, x, w, b)


def _dwconv_wgrad(name, x, dy, K):
    S, C = x.shape
    H = _halo_rows(K)
    tm, tc = _pick_rows(S, 512 if K <= 4 else 256), _pick(C, 512 if K <= 4 else LANES)
    RB = 8

    def body(x_ref, h_ref, dy_ref, dw_ref, db_ref, ext, shifted):
        i = pl.program_id(1)

        @pl.when(i == 0)
        def _():
            dw_ref[...] = jnp.zeros_like(dw_ref)
            db_ref[...] = jnp.zeros_like(db_ref)

        ext[0:H, :] = jnp.where(i == 0, 0.0, h_ref[...].astype(F32))
        ext[H:H + tm, :] = x_ref[...].astype(F32)
        _shifted_copies(ext, shifted, K)
        acc = [jnp.zeros((RB, tc), F32) for _ in range(K)]
        accb = jnp.zeros((RB, tc), F32)
        for s in range(tm // RB):
            d = dy_ref[s * RB:(s + 1) * RB, :].astype(F32)
            accb = accb + d
            for k in range(K):
                off = s * RB + H - (K - 1) + k
                acc[k] = acc[k] + d * _window(ext, shifted, K, off, RB)
        for k in range(K):
            dw_ref[k:k + 1, :] += jnp.sum(acc[k], axis=0, keepdims=True)
        db_ref[...] += jnp.sum(accb, axis=0, keepdims=True)

    r = tm // H
    return pl.pallas_call(
        body, name=name, grid=(C // tc, S // tm),
        in_specs=[pl.BlockSpec((tm, tc), lambda j, i: (i, j)),
                  pl.BlockSpec((H, tc), lambda j, i: (jnp.maximum(i * r - 1, 0), j)),
                  pl.BlockSpec((tm, tc), lambda j, i: (i, j))],
        out_specs=[pl.BlockSpec((K, tc), lambda j, i: (0, j)), pl.BlockSpec((1, tc), lambda j, i: (0, j))],
        out_shape=[jax.ShapeDtypeStruct((K, C), F32), jax.ShapeDtypeStruct((1, C), F32)],
        scratch_shapes=[pltpu.VMEM((tm + H, tc), F32), pltpu.VMEM((SUBLANES, tm + H - SUBLANES, tc), F32)],
        compiler_params=_params("parallel", "arbitrary"),
    )(x, x, dy)


FFN_HALO = 16


def _row_shifts(ref, r0, rows, cs, shifts):
    n = rows // SUBLANES
    lo = -1 if max(shifts) > 0 else 0
    hi = n + (1 if min(shifts) < 0 else 0)
    v = {j: ref[r0 + j * SUBLANES:r0 + (j + 1) * SUBLANES, cs] for j in range(lo, hi)}
    row = lax.broadcasted_iota(jnp.int32, (SUBLANES, LANES), 0)
    out = {}
    for s in shifts:
        if s == 0:
            pieces = [v[j] for j in range(n)]
        elif s > 0:
            rot = {j: pltpu.roll(v[j], s, axis=0) for j in range(-1, n)}
            pieces = [jnp.where(row < s, rot[j - 1], rot[j]) for j in range(n)]
        else:
            rot = {j: pltpu.roll(v[j], SUBLANES + s, axis=0) for j in range(0, n + 1)}
            pieces = [jnp.where(row < SUBLANES + s, rot[j], rot[j + 1]) for j in range(n)]
        out[s] = jnp.concatenate(pieces, axis=0)
    return out


def _conv_taps(w, b, ext, r0, rows, cs):
    K = w.shape[0]
    win = _row_shifts(ext, r0, rows, cs, list(range(K)))
    acc = b
    for k in range(K):
        acc = acc + w[k:k + 1, :] * win[K - 1 - k]
    return acc


def _ffn_up_fused(name, h, w_up, cw, cb):
    S, D = h.shape
    F = w_up.shape[1] // 2
    tm, tn = _pick_rows(S, 512), _pick(F, 1408)
    nj, H, RB = F // tn, FFN_HALO, 64

    def body(h_ref, hh_ref, wg_ref, wu_ref, cwg_ref, cwu_ref, cbg_ref, cbu_ref,
             a_ref, u0g_ref, u0u_ref, ug_ref, uu_ref, eg, eu):
        first = pl.program_id(1) == 0
        hv, halo = h_ref[...], hh_ref[...]
        for w_ref, u0_ref, e in ((wg_ref, u0g_ref, eg), (wu_ref, u0u_ref, eu)):
            w = w_ref[...]
            u0 = _nn(hv, w)
            u0_ref[...] = u0.astype(BF16)
            e[0:H, :] = jnp.where(first, 0.0, _nn(halo, w))
            e[H:H + tm, :] = u0
        for c in range(tn // LANES):
            cs = slice(c * LANES, (c + 1) * LANES)
            wg, wu, bg, bu = cwg_ref[:, cs], cwu_ref[:, cs], cbg_ref[:, cs], cbu_ref[:, cs]
            for s in range(tm // RB):
                rows = slice(s * RB, (s + 1) * RB)
                ug = _conv_taps(wg, bg, eg, H + s * RB, RB, cs)
                uu = _conv_taps(wu, bu, eu, H + s * RB, RB, cs)
                ug_ref[rows, cs] = ug.astype(BF16)
                uu_ref[rows, cs] = uu.astype(BF16)
                a_ref[rows, cs] = (ug * _sigmoid(ug) * uu).astype(BF16)

    r = tm // H
    gate = lambda rows: pl.BlockSpec((rows, tn), lambda j, i: (0, j))
    up = lambda rows: pl.BlockSpec((rows, tn), lambda j, i: (0, j + nj))
    tile = pl.BlockSpec((tm, tn), lambda j, i: (i, j))
    K = cw.shape[0]
    return pl.pallas_call(
        body, name=name, grid=(nj, S // tm),
        in_specs=[pl.BlockSpec((tm, D), lambda j, i: (i, 0)),
                  pl.BlockSpec((H, D), lambda j, i: (jnp.maximum(i * r - 1, 0), 0)),
                  gate(D), up(D), gate(K), up(K), gate(1), up(1)],
        out_specs=[tile] * 5, out_shape=[jax.ShapeDtypeStruct((S, F), BF16)] * 5,
        scratch_shapes=[pltpu.VMEM((H + tm, tn), F32)] * 2, compiler_params=_params("parallel", "parallel"),
    )(h, h, w_up, w_up, cw, cw, cb, cb)


def _ffn_gate_bwd_fused(name, dy, w_down, u0g, u0u, ug, uu, cw):
    S, D = dy.shape
    F, K = w_down.shape[0], cw.shape[0]
    tm, tn = _pick_rows(S, 512), _pick(F, 1408)
    nj, nrow, H, RB = F // tn, S // tm, FFN_HALO, 16
    RW = 32

    def body(dy_ref, dyn_ref, wd_ref, u0g_ref, u0u_ref, g_ref, gn_ref, u_ref, un_ref, cwg_ref, cwu_ref,
             dg_ref, du_ref, dcwg_ref, dcwu_ref, dcbg_ref, dcbu_ref, dg_s, du_s, da_s):
        i = pl.program_id(1)
        last = i == nrow - 1

        @pl.when(i == 0)
        def _():
            for ref in (dcwg_ref, dcwu_ref, dcbg_ref, dcbu_ref):
                ref[...] = jnp.zeros_like(ref)

        wd = wd_ref[...]
        da_s[0:tm, :] = _nt(dy_ref[...].astype(BF16), wd)
        da_s[tm:tm + H, :] = jnp.where(last, 0.0, _nt(dyn_ref[...].astype(BF16), wd))
        for c in range(tn // LANES):
            cs = slice(c * LANES, (c + 1) * LANES)
            for s in range(tm // RB + 1):
                rows = slice(s * RB, (s + 1) * RB)
                src_g, src_u, src_rows = (g_ref, u_ref, rows) if s < tm // RB else (gn_ref, un_ref, slice(0, RB))
                gv, uv = src_g[src_rows, cs].astype(F32), src_u[src_rows, cs].astype(F32)
                da = da_s[rows, cs]
                sg = _sigmoid(gv)
                dg_s[rows, cs] = da * uv * (sg * (1.0 + gv * (1.0 - sg)))
                du_s[rows, cs] = da * (gv * sg)
            for d_s, u0_ref, cw_ref, out_ref, dcw_ref, dcb_ref in (
                    (dg_s, u0g_ref, cwg_ref, dg_ref, dcwg_ref, dcbg_ref),
                    (du_s, u0u_ref, cwu_ref, du_ref, dcwu_ref, dcbu_ref)):
                w = cw_ref[:, cs]
                acc = [jnp.zeros((RW, LANES), F32) for _ in range(K)]
                accb = jnp.zeros((RW, LANES), F32)
                for s in range(tm // RW):
                    r0 = s * RW
                    u0 = u0_ref[r0:r0 + RW, cs].astype(F32)
                    ahead = _row_shifts(d_s, r0, RW, cs, [-m for m in range(K)])
                    t = None
                    for k in range(K):
                        m = K - 1 - k
                        win = ahead[-m]
                        if m == 0:
                            accb = accb + win
                        acc[k] = acc[k] + win * u0
                        term = w[k:k + 1, :] * win
                        t = term if t is None else t + term
                    out_ref[r0:r0 + RW, cs] = t.astype(BF16)
                for k in range(K):
                    dcw_ref[k:k + 1, cs] += jnp.sum(acc[k], axis=0, keepdims=True)
                dcb_ref[:, cs] += jnp.sum(accb, axis=0, keepdims=True)

    r = tm // H
    gate = lambda rows: pl.BlockSpec((rows, tn), lambda j, i: (0, j))
    up = lambda rows: pl.BlockSpec((rows, tn), lambda j, i: (0, j + nj))
    tile = pl.BlockSpec((tm, tn), lambda j, i: (i, j))
    nxt = pl.BlockSpec((H, tn), lambda j, i: (jnp.minimum((i + 1) * r, S // H - 1), j))
    acc_w, acc_b = pl.BlockSpec((K, tn), lambda j, i: (0, j)), pl.BlockSpec((1, tn), lambda j, i: (0, j))
    return pl.pallas_call(
        body, name=name, grid=(nj, nrow),
        in_specs=[pl.BlockSpec((tm, D), lambda j, i: (i, 0)),
                  pl.BlockSpec((H, D), lambda j, i: (jnp.minimum((i + 1) * r, S // H - 1), 0)),
                  pl.BlockSpec((tn, D), lambda j, i: (j, 0)),
                  tile, tile, tile, nxt, tile, nxt, gate(K), up(K)],
        out_specs=[tile, tile, acc_w, acc_w, acc_b, acc_b],
        out_shape=[jax.ShapeDtypeStruct((S, F), BF16)] * 2 + [jax.ShapeDtypeStruct((K, F), F32)] * 2
        + [jax.ShapeDtypeStruct((1, F), F32)] * 2,
        scratch_shapes=[pltpu.VMEM((tm + H, tn), F32)] * 3, compiler_params=_params("parallel", "arbitrary"),
    )(dy, dy, w_down, u0g, u0u, ug, ug, uu, uu, cw, cw)


def _colsum(v):
    return jnp.sum(v, axis=0, keepdims=True)


def _rmsnorm_fwd(name, x, gain):
    def fn(x, g):
        r = lax.rsqrt(jnp.mean(x * x, axis=-1, keepdims=True) + EPS)
        return x * r * g
    return _rowwise(name, fn, [(x, 0)], [(gain, None)], [(x.shape[1], BF16)])


def _rmsnorm_bwd(name, x, gain, dh, dres):
    def fn(x, dh, dres, g):
        r = lax.rsqrt(jnp.mean(x * x, axis=-1, keepdims=True) + EPS)
        xh = x * r
        dxh = dh * g
        dx = r * (dxh - xh * jnp.mean(dxh * xh, axis=-1, keepdims=True))
        return dres + dx, _colsum(dh * xh)
    D = x.shape[1]
    return _rowwise(name, fn, [(x, 0), (dh, 0), (dres, 0)], [(gain, None)], [(D, F32)], [(1, D)])


def _silu_gate_fwd(name, gate, up):
    F = gate.shape[1]
    def fn(g, up):
        return g * _sigmoid(g) * up
    return _rowwise(name, fn, [(gate, 0), (up, 0)], [], [(F, BF16)], tc=_pick(F, 512))


def _silu_gate_bwd(name, gate, up, da):
    F = gate.shape[1]
    def fn(g, up, da):
        s = _sigmoid(g)
        return da * up * (s * (1.0 + g * (1.0 - s))), da * (g * s)
    return _rowwise(name, fn, [(gate, 0), (up, 0), (da, 0)], [], [(F, F32), (F, F32)], tc=_pick(F, 512))


def _glu_fwd(name, a, gate):
    C = a.shape[1]
    def fn(a, g):
        return a * _sigmoid(g)
    return _rowwise(name, fn, [(a, 0), (gate, 0)], [], [(C, F32)], tc=_pick(C, 512))


def _glu_bwd(name, a, gate, dglu):
    C = a.shape[1]
    def fn(a, g, d):
        s = _sigmoid(g)
        da, dg = d * s, d * a * s * (1.0 - s)
        return da, dg, _colsum(da), _colsum(dg)
    return _rowwise(name, fn, [(a, 0), (gate, 0), (dglu, 0)], [], [(C, BF16), (C, BF16)], [(1, C), (1, C)],
                    tc=_pick(C, 512))


def _ln_silu_fwd(name, c, g, b):
    def fn(c, g, b):
        mu = jnp.mean(c, axis=-1, keepdims=True)
        d = c - mu
        n = d * lax.rsqrt(jnp.mean(d * d, axis=-1, keepdims=True) + EPS) * g + b
        return n * _sigmoid(n)
    return _rowwise(name, fn, [(c, 0)], [(g, None), (b, None)], [(c.shape[1], BF16)])


def _ln_silu_bwd(name, c, g, b, dsw):
    def fn(c, dsw, g, b):
        mu = jnp.mean(c, axis=-1, keepdims=True)
        d = c - mu
        r = lax.rsqrt(jnp.mean(d * d, axis=-1, keepdims=True) + EPS)
        ch = d * r
        n = ch * g + b
        s = _sigmoid(n)
        dn = dsw * (s * (1.0 + n * (1.0 - s)))
        dch = dn * g
        dc = r * (dch - jnp.mean(dch, axis=-1, keepdims=True) - ch * jnp.mean(dch * ch, axis=-1, keepdims=True))
        return dc, _colsum(dn * ch), _colsum(dn)
    C = c.shape[1]
    return _rowwise(name, fn, [(c, 0), (dsw, 0)], [(g, None), (b, None)], [(C, F32)], [(1, C), (1, C)])


def _column_sums(name, x):
    return _rowwise(name, lambda x: (_colsum(x),), [(x, 0)], [], [], [(1, x.shape[1])])


def _loss_grad(name, y, target):
    D = y.shape[1]
    def fn(y, t):
        e = y - t
        return e * (1.0 / D), _colsum(e * e) * (0.5 / D)
    return _rowwise(name, fn, [(y, 0), (target, 0)], [], [(D, F32)], [(1, D)])


def _add(name, arrays):
    def fn(*xs):
        acc = xs[0]
        for x in xs[1:]:
            acc = acc + x
        return acc
    return _rowwise(name, fn, [(a, 0) for a in arrays], [], [(arrays[0].shape[1], F32)], tm=256)


def _sum_slabs(name, stacked):
    n, R, C = stacked.shape
    tm = _pick_rows(R, 256)

    def body(*refs):
        acc = refs[0][0].astype(F32)
        for r in refs[1:n]:
            acc = acc + r[0].astype(F32)
        refs[n][...] = acc

    return pl.pallas_call(
        body, name=name, grid=(R // tm,),
        in_specs=[pl.BlockSpec((1, tm, C), lambda i, q=q: (q, i, 0)) for q in range(n)],
        out_specs=pl.BlockSpec((tm, C), lambda i: (i, 0)), out_shape=jax.ShapeDtypeStruct((R, C), F32),
        compiler_params=_params("parallel"),
    )(*[stacked] * n)


def _adamw(name, w, g, m, v):
    c1 = 1.0 - ADAM_B1 ** ADAM_STEP
    c2 = 1.0 - ADAM_B2 ** ADAM_STEP
    def fn(w, g, m, v):
        m = ADAM_B1 * m + (1.0 - ADAM_B1) * g
        v = ADAM_B2 * v + (1.0 - ADAM_B2) * (g * g)
        delta = -ADAM_LR * ((m / c1) / (jnp.sqrt(v / c2) + ADAM_EPS) + ADAM_WD * w)
        return delta, m, v
    C = w.shape[1]
    return _rowwise(name, fn, [(w, 0), (g, 0), (m, 0), (v, 0)], [], [(C, F32)] * 3, tm=256)


SEG_ROWS = 128


def _segment_matrix(n, seg):
    i = jnp.arange(n) // seg
    return (i[:, None] == i[None, :]).astype(BF16)


def _seg_sum(v, B):
    hi = v.astype(BF16)
    lo = (v - hi.astype(F32)).astype(BF16)
    n = B.shape[0]
    slabs = [slice(c, c + n) for c in range(0, v.shape[1], n)]
    return jnp.concatenate([_nn(hi[:, c], B) + _nn(lo[:, c], B) for c in slabs], axis=1)


def _qknorm_fwd(name, proj, gain_full, is_norm, seg):
    def fn(x, gf, isn, B):
        ms = _seg_sum(x * x, B) * (1.0 / ATTN_HEAD_DIM)
        r = lax.rsqrt(ms + EPS)
        return x * (isn * r + (1.0 - isn)) * gf
    W = proj.shape[1]
    return _rowwise(name, fn, [(proj, 0)], [(gain_full, 0), (is_norm, 0), (seg, None)], [(W, BF16)],
                    tc=ATTN_GW, rb=SEG_ROWS)


def _qknorm_bwd(name, proj, dy, gain_full, is_norm, seg):
    def fn(x, dy, gf, isn, B):
        ms = _seg_sum(x * x, B) * (1.0 / ATTN_HEAD_DIM)
        r = lax.rsqrt(ms + EPS)
        xh = x * r
        dxh = dy * gf
        dn = r * (dxh - xh * (_seg_sum(dxh * xh, B) * (1.0 / ATTN_HEAD_DIM)))
        return isn * dn + (1.0 - isn) * dxh, _colsum(dy * xh)
    W = proj.shape[1]
    return _rowwise(name, fn, [(proj, 0), (dy, 0)], [(gain_full, 0), (is_norm, 0), (seg, None)],
                    [(W, BF16)], [(1, W)], tc=ATTN_GW, rb=SEG_ROWS)


def _attn_masks(r0=0, rows=ATTN_BLOCK):
    shape = (rows, ATTN_BLOCK)
    row = lax.broadcasted_iota(jnp.int32, shape, 0) + r0
    col = lax.broadcasted_iota(jnp.int32, shape, 1)
    return col <= row, col >= row, col < ATTN_HEAD_DIM


def _attn_group_fwd(name, qkv, d):
    S = qkv.shape[0]
    n, W, G = S // d, 3 * ATTN_GW, ATTN_GW
    nb = n // ATTN_BLOCK
    view = qkv.reshape(n, d * W)

    B, RQ = ATTN_BLOCK, ATTN_ROWS

    def body(cur, prev, o_ref, l_ref, s_scr, p_scr, lse_scr, inv_scr):
        b = pl.program_id(1)
        cur_mask, prev_mask, low = _attn_masks()
        prev_mask = jnp.logical_and(prev_mask, b > 0)
        for h in range(ATTN_HEADS):
            c0 = (h // 2) * LANES
            hm = low if h % 2 == 0 else jnp.logical_not(low)
            q2 = cur[:, c0:c0 + LANES]
            qm = jnp.where(hm, q2, jnp.zeros_like(q2))
            s_scr[h, :, 0:B] = jnp.where(cur_mask, _nt(qm, cur[:, G + c0:G + c0 + LANES]), -jnp.inf)
            s_scr[h, :, B:2 * B] = jnp.where(prev_mask, _nt(qm, prev[:, G + c0:G + c0 + LANES]), -jnp.inf)
        for h in range(ATTN_HEADS):
            for r0 in range(0, B, RQ):
                s = s_scr[h, r0:r0 + RQ, :]
                m = jnp.max(s, axis=1, keepdims=True)
                p = jnp.exp(s - m)
                l = jnp.sum(p, axis=1, keepdims=True)
                p_scr[h, r0:r0 + RQ, :] = p.astype(BF16)
                lse_scr[h, r0:r0 + RQ, :] = jnp.broadcast_to(m + jnp.log(l), (RQ, LANES))
                inv_scr[h, r0:r0 + RQ, :] = jnp.broadcast_to(1.0 / l, (RQ, LANES))
        for pr in range(G // LANES):
            c0 = pr * LANES
            vc, vp = cur[:, 2 * G + c0:2 * G + c0 + LANES], prev[:, 2 * G + c0:2 * G + c0 + LANES]
            o = [(_nn(p_scr[h, :, 0:B], vc) + _nn(p_scr[h, :, B:2 * B], vp)) * inv_scr[h] for h in (2 * pr, 2 * pr + 1)]
            o_ref[:, c0:c0 + LANES] = jnp.where(low, o[0], o[1])
            l_ref[:, c0:c0 + LANES] = jnp.where(low, lse_scr[2 * pr], lse_scr[2 * pr + 1])

    o, l = pl.pallas_call(
        body, name=name, grid=(d, nb),
        in_specs=[pl.BlockSpec((B, W), lambda r, b: (b, r)),
                  pl.BlockSpec((B, W), lambda r, b: (jnp.maximum(b - 1, 0), r))],
        out_specs=[pl.BlockSpec((B, G), lambda r, b: (b, r))] * 2,
        out_shape=[jax.ShapeDtypeStruct((n, d * G), F32)] * 2,
        scratch_shapes=[pltpu.VMEM((ATTN_HEADS, B, 2 * B), F32), pltpu.VMEM((ATTN_HEADS, B, 2 * B), BF16),
                        pltpu.VMEM((ATTN_HEADS, B, LANES), F32), pltpu.VMEM((ATTN_HEADS, B, LANES), F32)],
        compiler_params=_params("parallel", "parallel"),
    )(view, view)
    return o.reshape(S, G), l.reshape(S, G)


def _attn_combine(name, os, ls):
    def fn(o1, o2, o3, l1, l2, l3):
        m = jnp.maximum(jnp.maximum(l1, l2), l3)
        e1, e2, e3 = jnp.exp(l1 - m), jnp.exp(l2 - m), jnp.exp(l3 - m)
        den = e1 + e2 + e3
        return (e1 * o1 + e2 * o2 + e3 * o3) / den, m + jnp.log(den)
    G = os[0].shape[1]
    return _rowwise(name, fn, [(a, 0) for a in (*os, *ls)], [], [(G, F32), (G, F32)])


def _attn_delta(name, do, o, seg):
    def fn(do, o, B):
        return _seg_sum(do * o, B)
    return _rowwise(name, fn, [(do, 0), (o, 0)], [(seg, None)], [(o.shape[1], F32)], rb=SEG_ROWS)


def _attn_group_bwd(name, qkv, do, lse, delta, d):
    S = qkv.shape[0]
    n, W, G = S // d, 3 * ATTN_GW, ATTN_GW
    nb = n // ATTN_BLOCK

    B, RQ = ATTN_BLOCK, ATTN_ROWS
    S_A, DP_A, S_B, DP_B, S_C, DP_C = range(6)
    P_A, DS_A, P_B, DS_B, DS_C = range(5)

    def body(qp, qc, qn, do_c, do_n, l_c, l_n, dl_c, dl_n, out, f_scr, b_scr):
        j = pl.program_id(1)
        low = _attn_masks()[2]
        for h in range(ATTN_HEADS):
            c0 = (h // 2) * LANES
            hm = low if h % 2 == 0 else jnp.logical_not(low)
            k_c, v_c = qc[:, G + c0:G + c0 + LANES], qc[:, 2 * G + c0:2 * G + c0 + LANES]
            k_p, v_p = qp[:, G + c0:G + c0 + LANES], qp[:, 2 * G + c0:2 * G + c0 + LANES]
            zero = jnp.zeros((B, LANES), BF16)
            qmc, qmn = jnp.where(hm, qc[:, c0:c0 + LANES], zero), jnp.where(hm, qn[:, c0:c0 + LANES], zero)
            dmc = jnp.where(hm, do_c[:, c0:c0 + LANES].astype(BF16), zero)
            dmn = jnp.where(hm, do_n[:, c0:c0 + LANES].astype(BF16), zero)
            f_scr[h, S_A], f_scr[h, DP_A] = _nt(qmc, k_c), _nt(dmc, v_c)
            f_scr[h, S_B], f_scr[h, DP_B] = _nt(qmn, k_c), _nt(dmn, v_c)
            f_scr[h, S_C], f_scr[h, DP_C] = _nt(qmc, k_p), _nt(dmc, v_p)
        for h in range(ATTN_HEADS):
            h0 = h * ATTN_HEAD_DIM
            for r0 in range(0, B, RQ):
                rows = slice(r0, r0 + RQ)
                cur_mask, band, _ = _attn_masks(r0, RQ)
                next_mask, prev_mask = jnp.logical_and(band, j < nb - 1), jnp.logical_and(band, j > 0)
                lc, ln = l_c[rows, h0:h0 + 1], l_n[rows, h0:h0 + 1]
                dlc, dln = dl_c[rows, h0:h0 + 1], dl_n[rows, h0:h0 + 1]
                p_a = jnp.where(cur_mask, jnp.exp(f_scr[h, S_A, rows, :] - lc), 0.0)
                p_b = jnp.where(next_mask, jnp.exp(f_scr[h, S_B, rows, :] - ln), 0.0)
                p_c = jnp.where(prev_mask, jnp.exp(f_scr[h, S_C, rows, :] - lc), 0.0)
                b_scr[h, P_A, rows, :] = p_a.astype(BF16)
                b_scr[h, P_B, rows, :] = p_b.astype(BF16)
                b_scr[h, DS_A, rows, :] = (p_a * (f_scr[h, DP_A, rows, :] - dlc)).astype(BF16)
                b_scr[h, DS_B, rows, :] = (p_b * (f_scr[h, DP_B, rows, :] - dln)).astype(BF16)
                b_scr[h, DS_C, rows, :] = (p_c * (f_scr[h, DP_C, rows, :] - dlc)).astype(BF16)
        for pr in range(G // LANES):
            c0 = pr * LANES
            q_c, k_c, q_n = qc[:, c0:c0 + LANES], qc[:, G + c0:G + c0 + LANES], qn[:, c0:c0 + LANES]
            k_p = qp[:, G + c0:G + c0 + LANES]
            d_c, d_n = do_c[:, c0:c0 + LANES].astype(BF16), do_n[:, c0:c0 + LANES].astype(BF16)
            res = []
            for h in (2 * pr, 2 * pr + 1):
                dq = _nn(b_scr[h, DS_A], k_c) + _nn(b_scr[h, DS_C], k_p)
                dk = _tn(b_scr[h, DS_A], q_c) + _tn(b_scr[h, DS_B], q_n)
                dv = _tn(b_scr[h, P_A], d_c) + _tn(b_scr[h, P_B], d_n)
                res.append((dq, dk, dv))
            for t in range(3):
                out[:, t * G + c0:t * G + c0 + LANES] = jnp.where(low, res[0][t], res[1][t])

    prv = lambda r, j: (jnp.maximum(j - 1, 0), r)
    cur = lambda r, j: (j, r)
    nxt = lambda r, j: (jnp.minimum(j + 1, nb - 1), r)
    wide = lambda m: pl.BlockSpec((ATTN_BLOCK, W), m)
    narrow = lambda m: pl.BlockSpec((ATTN_BLOCK, G), m)
    qv, dv, lv, tv = qkv.reshape(n, d * W), do.reshape(n, d * G), lse.reshape(n, d * G), delta.reshape(n, d * G)
    out = pl.pallas_call(
        body, name=name, grid=(d, nb),
        in_specs=[wide(prv), wide(cur), wide(nxt), narrow(cur), narrow(nxt), narrow(cur), narrow(nxt),
                  narrow(cur), narrow(nxt)],
        out_specs=wide(cur), out_shape=jax.ShapeDtypeStruct((n, d * W), F32),
        scratch_shapes=[pltpu.VMEM((ATTN_HEADS, 6, B, B), F32), pltpu.VMEM((ATTN_HEADS, 5, B, B), BF16)],
        compiler_params=_params("parallel", "parallel"),
    )(qv, qv, qv, dv, dv, lv, lv, tv, tv)
    return out.reshape(S, W)


def _chunk_triangle(T, upper):
    i = jnp.arange(T)
    same = (i[:, None] // HGRN_CHUNK) == (i[None, :] // HGRN_CHUNK)
    tri = (i[None, :] >= i[:, None]) if upper else (i[None, :] <= i[:, None])
    return jnp.logical_and(same, tri).astype(F32)


def _hgrn_prologue(qr, fr, lbv, q_s, k_s, b_s, tri_ref, T):
    def pro(s, c):
        sl = pl.ds(pl.multiple_of(s * HGRN_CHUNK, HGRN_CHUNK), HGRN_CHUNK)
        sg = _sigmoid(fr[sl, :])
        qv = qr[sl, :]
        q_s[sl, :] = qv * _sigmoid(qv)
        k_s[sl, :] = (1.0 - lbv) * (1.0 - sg)
        b_s[sl, :] = jnp.log(lbv + (1.0 - lbv) * sg)
        return c
    lax.fori_loop(0, T // HGRN_CHUNK, pro, 0)
    b_s[...] = _nn(tri_ref[...], b_s[...], HI)


def _hgrn_scan_fwd(name, pq, pf, pv, lb):
    S, D = pq.shape
    T = _pick_rows(S, HGRN_TILE)
    NH, NT, C, HD, HB = D // HGRN_HEAD, S // T, HGRN_CHUNK, HGRN_HEAD, HGRN_GROUP
    W = HB * HD
    tri = _chunk_triangle(T, upper=False)

    def body(qr, fr, iv, lb_ref, tri_ref, o_ref, ck_ref, st_ref, q_s, k_s, b_s):
        @pl.when(pl.program_id(1) == 0)
        def _():
            st_ref[...] = jnp.zeros_like(st_ref)

        ck_ref[...] = st_ref[...]
        _hgrn_prologue(qr, fr, lb_ref[...], q_s, k_s, b_s, tri_ref, T)
        row = lax.broadcasted_iota(jnp.int32, (C, 1), 0)

        def chunk(c, carry):
            sl = pl.ds(pl.multiple_of(c * C, C), C)
            for hh in range(HB):
                cs = slice(hh * HD, (hh + 1) * HD)
                q, k, b, v = q_s[sl, cs], k_s[sl, cs], b_s[sl, cs], iv[sl, cs]
                b_last = b[C - 1:C, :]
                st = st_ref[cs, :]
                o = _nt((q * jnp.exp(b)).astype(BF16), st.astype(BF16))
                for s in range(C):
                    e = jnp.exp(jnp.minimum(b - b[s:s + 1, :], 0.0))
                    a = jnp.sum(q * e * k[s:s + 1, :], axis=1, keepdims=True)
                    o = o + jnp.where(row >= s, a, 0.0) * v[s:s + 1, :]
                o_ref[sl, cs] = o
                kd = k * jnp.exp(b_last - b)
                st_ref[cs, :] = st * jnp.exp(b_last) + _tn(v.astype(BF16), kd.astype(BF16))
            return carry

        lax.fori_loop(0, T // C, chunk, 0)

    NG = NH // HB
    col = pl.BlockSpec((T, W), lambda h, t: (t, h))
    return pl.pallas_call(
        body, name=name, grid=(NG, NT),
        in_specs=[col, col, col, pl.BlockSpec((1, W), lambda h, t: (0, h)), pl.BlockSpec((T, T), lambda h, t: (0, 0))],
        out_specs=[col, pl.BlockSpec((W, HD), lambda h, t: (t * NG + h, 0))],
        out_shape=[jax.ShapeDtypeStruct((S, D), F32), jax.ShapeDtypeStruct((NT * NH * HD, HD), F32)],
        scratch_shapes=[pltpu.VMEM((W, HD), F32)] + [pltpu.VMEM((T, W), F32)] * 3,
        compiler_params=_params("parallel", "arbitrary"),
    )(pq, pf, pv, lb, tri)


def _hgrn_scan_bwd(name, pq, pf, pv, lb, ckpt, do):
    S, D = pq.shape
    T = _pick_rows(S, HGRN_TILE)
    NH, NT, C, HD, HB = D // HGRN_HEAD, S // T, HGRN_CHUNK, HGRN_HEAD, HGRN_GROUP
    NC, W, NG = T // C, HB * HD, NH // HB
    tri, tri_up = _chunk_triangle(T, upper=False), _chunk_triangle(T, upper=True)

    def body(qr, fr, iv, do_ref, ck_ref, lb_ref, tri_ref, triu_ref, dq_ref, df_ref, dv_ref, dlb_ref,
             dst_ref, run, save, q_s, k_s, b_s, dq_s, dk_s, db_s):
        @pl.when(pl.program_id(1) == 0)
        def _():
            dst_ref[...] = jnp.zeros_like(dst_ref)
            dlb_ref[...] = jnp.zeros_like(dlb_ref)

        lbv = lb_ref[...]
        _hgrn_prologue(qr, fr, lbv, q_s, k_s, b_s, tri_ref, T)
        row = lax.broadcasted_iota(jnp.int32, (C, 1), 0)
        run[...] = ck_ref[...]

        def replay(c, carry):
            sl = pl.ds(pl.multiple_of(c * C, C), C)
            for hh in range(HB):
                cs = slice(hh * HD, (hh + 1) * HD)
                st = run[cs, :]
                save[pl.ds(pl.multiple_of((hh * NC + c) * HD, HD), HD), :] = st
                k, b, v = k_s[sl, cs], b_s[sl, cs], iv[sl, cs]
                b_last = b[C - 1:C, :]
                kd = k * jnp.exp(b_last - b)
                run[cs, :] = st * jnp.exp(b_last) + _tn(v.astype(BF16), kd.astype(BF16))
            return carry

        lax.fori_loop(0, NC, replay, 0)

        def chunk(ci, carry):
            c = NC - 1 - ci
            sl = pl.ds(pl.multiple_of(c * C, C), C)
            for hh in range(HB):
                cs = slice(hh * HD, (hh + 1) * HD)
                q, k, b, v, g = q_s[sl, cs], k_s[sl, cs], b_s[sl, cs], iv[sl, cs], do_ref[sl, cs]
                st0 = save[pl.ds(pl.multiple_of((hh * NC + c) * HD, HD), HD), :]
                dst1 = dst_ref[cs, :]
                b_last = b[C - 1:C, :]
                eb, ebl, ek = jnp.exp(b), jnp.exp(b_last), jnp.exp(b_last - b)
                dst1_b = dst1.astype(BF16)
                dq = _nn(g.astype(BF16), st0.astype(BF16)) * eb
                dv = _nt((k * ek).astype(BF16), dst1_b)
                dk = _nn(v.astype(BF16), dst1_b) * ek
                db_last = _colsum(dk * k) + _colsum(dst1 * st0) * ebl
                for s in range(C):
                    e = jnp.where(row >= s, jnp.exp(jnp.minimum(b - b[s:s + 1, :], 0.0)), 0.0)
                    ks, vs = k[s:s + 1, :], v[s:s + 1, :]
                    da = jnp.sum(g * vs, axis=1, keepdims=True)
                    a = jnp.sum(q * e * ks, axis=1, keepdims=True)
                    dq = dq + da * e * ks
                    dk = dk + jnp.where(row == s, _colsum(da * q * e), 0.0)
                    dv = dv + jnp.where(row == s, _colsum(a * g), 0.0)
                dq_s[sl, cs] = dq
                dk_s[sl, cs] = dk
                db_s[sl, cs] = q * dq - k * dk + jnp.where(row == C - 1, db_last, 0.0)
                dv_ref[sl, cs] = dv.astype(BF16)
                dst_ref[cs, :] = dst1 * ebl + _tn(g.astype(BF16), (q * eb).astype(BF16))
            return carry

        lax.fori_loop(0, NC, chunk, 0)
        db_s[...] = _nn(triu_ref[...], db_s[...], HI)

        def epi(s, carry):
            sl = pl.ds(pl.multiple_of(s * C, C), C)
            qv = qr[sl, :]
            sq = _sigmoid(qv)
            dq_ref[sl, :] = (dq_s[sl, :] * sq * (1.0 + qv * (1.0 - sq))).astype(BF16)
            sg = _sigmoid(fr[sl, :])
            common = db_s[sl, :] / (lbv + (1.0 - lbv) * sg) - dk_s[sl, :]
            df_ref[sl, :] = (common * (1.0 - lbv) * sg * (1.0 - sg)).astype(BF16)
            dlb_ref[...] += _colsum(common * (1.0 - sg))
            return carry

        lax.fori_loop(0, NC, epi, 0)

    col = pl.BlockSpec((T, W), lambda h, t: (NT - 1 - t, h))
    dq, df, dv, dlb = pl.pallas_call(
        body, name=name, grid=(NG, NT),
        in_specs=[col, col, col, col, pl.BlockSpec((W, HD), lambda h, t: ((NT - 1 - t) * NG + h, 0)),
                  pl.BlockSpec((1, W), lambda h, t: (0, h)),
                  pl.BlockSpec((T, T), lambda h, t: (0, 0)), pl.BlockSpec((T, T), lambda h, t: (0, 0))],
        out_specs=[col, col, col, pl.BlockSpec((1, W), lambda h, t: (0, h))],
        out_shape=[jax.ShapeDtypeStruct((S, D), BF16)] * 3 + [jax.ShapeDtypeStruct((1, D), F32)],
        scratch_shapes=[pltpu.VMEM((W, HD), F32)] * 2 + [pltpu.VMEM((HB * NC * HD, HD), F32)]
        + [pltpu.VMEM((T, W), F32)] * 6,
        compiler_params=_params("parallel", "arbitrary"),
    )(pq, pf, pv, do, ckpt, lb, tri, tri_up)
    return dq, df, dv, dlb


def _hgrn_out_fwd(name, o, gate, norm_g):
    def fn(o, g, ng):
        parts = []
        for h in range(o.shape[1] // HGRN_HEAD):
            c = slice(h * HGRN_HEAD, (h + 1) * HGRN_HEAD)
            oh, gh = o[:, c], g[:, c]
            r = lax.rsqrt(jnp.mean(oh * oh, axis=-1, keepdims=True) + EPS)
            parts.append(oh * r * ng[:, c] * (gh * _sigmoid(gh)))
        return jnp.concatenate(parts, axis=1)
    return _rowwise(name, fn, [(o, 0), (gate, 0)], [(norm_g, None)], [(o.shape[1], BF16)])


def _hgrn_out_bwd(name, o, gate, norm_g, dy):
    def fn(o, g, dy, ng):
        dos, dgs, dngs = [], [], []
        for h in range(o.shape[1] // HGRN_HEAD):
            c = slice(h * HGRN_HEAD, (h + 1) * HGRN_HEAD)
            oh, gh, dyh, ngh = o[:, c], g[:, c], dy[:, c], ng[:, c]
            r = lax.rsqrt(jnp.mean(oh * oh, axis=-1, keepdims=True) + EPS)
            xh = oh * r
            s = _sigmoid(gh)
            dn = dyh * (gh * s)
            dxh = dn * ngh
            dos.append(r * (dxh - xh * jnp.mean(dxh * xh, axis=-1, keepdims=True)))
            dgs.append(dyh * xh * ngh * (s * (1.0 + gh * (1.0 - s))))
            dngs.append(_colsum(dn * xh))
        return jnp.concatenate(dos, axis=1), jnp.concatenate(dgs, axis=1), jnp.concatenate(dngs, axis=1)
    D = o.shape[1]
    return _rowwise(name, fn, [(o, 0), (gate, 0), (dy, 0)], [(norm_g, None)], [(D, F32), (D, BF16)], [(1, D)])


def _lower_bound_fwd(name, logits, layer):
    n = logits.shape[0]

    def body(x_ref, o_ref):
        rows = [x_ref[i:i + 1, :] for i in range(n)]
        m = functools.reduce(jnp.maximum, rows)
        e = [jnp.exp(r - m) for r in rows]
        den = functools.reduce(jnp.add, e)
        o_ref[...] = functools.reduce(jnp.add, e[1:layer + 1]) / den

    return pl.pallas_call(body, name=name, out_shape=jax.ShapeDtypeStruct((1, logits.shape[1]), F32))(logits)


def _lower_bound_bwd(name, logits, dlb, layer):
    n = logits.shape[0]

    def body(x_ref, d_ref, o_ref):
        rows = [x_ref[i:i + 1, :] for i in range(n)]
        m = functools.reduce(jnp.maximum, rows)
        e = [jnp.exp(r - m) for r in rows]
        den = functools.reduce(jnp.add, e)
        s = [v / den for v in e]
        d = d_ref[...]
        inner = functools.reduce(jnp.add, s[1:layer + 1]) * d
        for i in range(n):
            o_ref[i:i + 1, :] = s[i] * ((d if 1 <= i <= layer else 0.0) - inner)

    return pl.pallas_call(body, name=name, out_shape=jax.ShapeDtypeStruct(logits.shape, F32))(logits, dlb)


def _row(v):
    return v.reshape(1, -1)


def _ffn_fwd(l, x1, w):
    h2 = _rmsnorm_fwd(f"ffn{l}_norm", x1, _row(w["ffn_norm"][l]))
    a, *u = _ffn_up_fused(f"ffn{l}_up", h2, w["ffn_w_up"][l], w["ffn_conv_w"][l], _row(w["ffn_conv_b"][l]))
    x2 = _matmul(f"ffn{l}_down", [(a, w["ffn_w_down"][l])], residual=x1)
    return x2, (x1, h2, u, a)


def _ffn_bwd(l, dx2, saved, w, grads):
    x1, h2, (u0g, u0u, ug, uu), a = saved
    w_up, w_down = w["ffn_w_up"][l], w["ffn_w_down"][l]
    F = w_down.shape[0]
    grads["ffn_w_down"][l] = _matmul_tn(f"ffn{l}_dwdown", a, dx2)
    dg, du, dcwg, dcwu, dcbg, dcbu = _ffn_gate_bwd_fused(
        f"ffn{l}_dgate", dx2, w_down, u0g, u0u, ug, uu, w["ffn_conv_w"][l])
    grads["ffn_conv_w"][l] = jnp.concatenate([dcwg, dcwu], axis=1)
    grads["ffn_conv_b"][l] = jnp.concatenate([dcbg, dcbu], axis=1)[0]
    grads["ffn_w_up"][l] = jnp.concatenate(
        [_matmul_tn(f"ffn{l}_dwup0", h2, dg), _matmul_tn(f"ffn{l}_dwup1", h2, du)], axis=1)
    dh2 = _matmul(f"ffn{l}_dh", [(dg, w_up[:, :F]), (du, w_up[:, F:])], trans_b=True)
    dx1, dgain = _rmsnorm_bwd(f"ffn{l}_dnorm", x1, _row(w["ffn_norm"][l]), dh2, dx2)
    grads["ffn_norm"][l] = dgain[0]
    return dx1


def _attn_gain_rows(w, j, g):
    scale = ATTN_HEAD_DIM ** -0.5
    qg = jnp.tile(w["attn_q_gain"][j, g] * scale, ATTN_HEADS)
    kg = jnp.tile(w["attn_k_gain"][j, g], ATTN_HEADS)
    gain = jnp.concatenate([qg, kg, jnp.ones((ATTN_GW,), F32)])
    is_norm = jnp.concatenate([jnp.ones((2 * ATTN_GW,), F32), jnp.zeros((ATTN_GW,), F32)])
    return _row(gain), _row(is_norm)


def _attn_fwd(l, j, x, w):
    h = _rmsnorm_fwd(f"mix{l}_norm", x, _row(w["mixer_norm"][l]))
    w_in = w["attn_w_in"][j]
    seg = _segment_matrix(LANES, ATTN_HEAD_DIM)
    GW3 = 3 * ATTN_GW
    proj, qkv, os, ls = [], [], [], []
    for g, d in enumerate(ATTN_DILATIONS):
        gain, is_norm = _attn_gain_rows(w, j, g)
        proj.append(_matmul(f"attn{l}_in{g}", [(h, w_in[:, g * GW3:(g + 1) * GW3])]))
        qkv.append(_qknorm_fwd(f"attn{l}_qknorm{g}", proj[g], gain, is_norm, seg))
        o, lse = _attn_group_fwd(f"attn{l}_core{g}", qkv[g], d)
        os.append(o)
        ls.append(lse)
    o, lse = _attn_combine(f"attn{l}_combine", os, ls)
    x1 = _matmul(f"attn{l}_out", [(o, w["attn_w_out"][j])], residual=x)
    return x1, (x, h, proj, qkv, o, lse)


def _attn_bwd(l, j, dx1, saved, w, grads):
    x, h, proj, qkv, o, lse = saved
    w_in, w_out = w["attn_w_in"][j], w["attn_w_out"][j]
    seg = _segment_matrix(LANES, ATTN_HEAD_DIM)
    GW3 = 3 * ATTN_GW
    grads["attn_w_out"][j] = _matmul_tn(f"attn{l}_dwout", o, dx1)
    do = _matmul(f"attn{l}_do", [(dx1, w_out)], trans_b=True)
    delta = _attn_delta(f"attn{l}_delta", do, o, seg)
    dproj, dwin, dqg, dkg = [], [], [], []
    for g, d in enumerate(ATTN_DILATIONS):
        gain, is_norm = _attn_gain_rows(w, j, g)
        dqkv = _attn_group_bwd(f"attn{l}_dcore{g}", qkv[g], do, lse, delta, d)
        dp, dgain = _qknorm_bwd(f"attn{l}_dqknorm{g}", proj[g], dqkv, gain, is_norm, seg)
        dproj.append(dp)
        dwin.append(_matmul_tn(f"attn{l}_dwin{g}", h, dp))
        per_head = dgain.reshape(3, ATTN_HEADS, ATTN_HEAD_DIM).sum(axis=1)
        dqg.append(per_head[0] * ATTN_HEAD_DIM ** -0.5)
        dkg.append(per_head[1])
    grads["attn_w_in"][j] = jnp.concatenate(dwin, axis=1)
    grads["attn_q_gain"][j] = jnp.stack(dqg)
    grads["attn_k_gain"][j] = jnp.stack(dkg)
    dh = _matmul(f"attn{l}_dh", [(dproj[g], w_in[:, g * GW3:(g + 1) * GW3]) for g in range(3)], trans_b=True)
    dx, dg = _rmsnorm_bwd(f"mix{l}_dnorm", x, _row(w["mixer_norm"][l]), dh, dx1)
    grads["mixer_norm"][l] = dg[0]
    return dx


def _conv_fwd(l, j, x, w):
    h = _rmsnorm_fwd(f"mix{l}_norm", x, _row(w["mixer_norm"][l]))
    w_in, b_in = w["conv_w_in"][j], _row(w["conv_b_in"][j])
    C = w_in.shape[1] // 2
    ua = _matmul(f"conv{l}_in0", [(h, w_in[:, :C])], bias=b_in[:, :C])
    ug = _matmul(f"conv{l}_in1", [(h, w_in[:, C:])], bias=b_in[:, C:])
    glu = _glu_fwd(f"conv{l}_glu", ua, ug)
    c = _dwconv(f"conv{l}_dw", glu, w["conv_dw_w"][j], _row(w["conv_dw_b"][j]), reverse=False)
    sw = _ln_silu_fwd(f"conv{l}_ln", c, _row(w["conv_ln_g"][j]), _row(w["conv_ln_b"][j]))
    x1 = _matmul(f"conv{l}_out", [(sw, w["conv_w_out"][j])], bias=_row(w["conv_b_out"][j]), residual=x)
    return x1, (x, h, ua, ug, glu, c, sw)


def _conv_bwd(l, j, dx1, saved, w, grads):
    x, h, ua, ug, glu, c, sw = saved
    w_in, w_out, dw_w = w["conv_w_in"][j], w["conv_w_out"][j], w["conv_dw_w"][j]
    C = w_out.shape[0]
    grads["conv_b_out"][j] = _column_sums(f"conv{l}_dbout", dx1)[0]
    grads["conv_w_out"][j] = _matmul_tn(f"conv{l}_dwout", sw, dx1)
    dsw = _matmul(f"conv{l}_dsw", [(dx1, w_out)], trans_b=True)
    dc, dlg, dlb = _ln_silu_bwd(f"conv{l}_dln", c, _row(w["conv_ln_g"][j]), _row(w["conv_ln_b"][j]), dsw)
    grads["conv_ln_g"][j], grads["conv_ln_b"][j] = dlg[0], dlb[0]
    dglu = _dwconv(f"conv{l}_ddw", dc, dw_w, jnp.zeros((1, C), F32), reverse=True)
    gw, gb = _dwconv_wgrad(f"conv{l}_ddww", glu, dc, dw_w.shape[0])
    grads["conv_dw_w"][j], grads["conv_dw_b"][j] = gw, gb[0]
    da, dgate, sa, sg = _glu_bwd(f"conv{l}_dglu", ua, ug, dglu)
    grads["conv_b_in"][j] = jnp.concatenate([sa, sg], axis=1)[0]
    grads["conv_w_in"][j] = jnp.concatenate(
        [_matmul_tn(f"conv{l}_dwin0", h, da), _matmul_tn(f"conv{l}_dwin1", h, dgate)], axis=1)
    dh = _matmul(f"conv{l}_dh", [(da, w_in[:, :C]), (dgate, w_in[:, C:])], trans_b=True)
    dx, dg = _rmsnorm_bwd(f"mix{l}_dnorm", x, _row(w["mixer_norm"][l]), dh, dx1)
    grads["mixer_norm"][l] = dg[0]
    return dx


def _hgrn_fwd(l, j, x, w):
    h = _rmsnorm_fwd(f"mix{l}_norm", x, _row(w["mixer_norm"][l]))
    w_in = w["hgrn_w_in"][j]
    D = w_in.shape[1] // 4
    pq, pf, pv, pg = [_matmul(f"hgrn{l}_in{s}", [(h, w_in[:, s * D:(s + 1) * D])]) for s in range(4)]
    lb = _lower_bound_fwd(f"hgrn{l}_lb", w["hgrn_lb_logits"], l)
    o, ckpt = _hgrn_scan_fwd(f"hgrn{l}_scan", pq, pf, pv, lb)
    y = _hgrn_out_fwd(f"hgrn{l}_gate", o, pg, _row(w["hgrn_norm_g"][j]))
    x1 = _matmul(f"hgrn{l}_out", [(y, w["hgrn_w_out"][j])], residual=x)
    return x1, (x, h, pq, pf, pv, pg, lb, o, ckpt, y)


def _hgrn_bwd(l, j, dx1, saved, w, grads):
    x, h, pq, pf, pv, pg, lb, o, ckpt, y = saved
    w_in, w_out = w["hgrn_w_in"][j], w["hgrn_w_out"][j]
    D = w_out.shape[0]
    grads["hgrn_w_out"][j] = _matmul_tn(f"hgrn{l}_dwout", y, dx1)
    dy = _matmul(f"hgrn{l}_dy", [(dx1, w_out)], trans_b=True)
    do, dpg, dng = _hgrn_out_bwd(f"hgrn{l}_dgate", o, pg, _row(w["hgrn_norm_g"][j]), dy)
    grads["hgrn_norm_g"][j] = dng[0]
    dpq, dpf, dpv, dlb = _hgrn_scan_bwd(f"hgrn{l}_dscan", pq, pf, pv, lb, ckpt, do)
    grads["hgrn_lb_logits"] = grads["hgrn_lb_logits"] + _lower_bound_bwd(f"hgrn{l}_dlb", w["hgrn_lb_logits"], dlb, l)
    dps = [dpq, dpf, dpv, dpg]
    grads["hgrn_w_in"][j] = jnp.concatenate([_matmul_tn(f"hgrn{l}_dwin{s}", h, dps[s]) for s in range(4)], axis=1)
    dh = _matmul(f"hgrn{l}_dh", [(dps[s], w_in[:, s * D:(s + 1) * D]) for s in range(4)], trans_b=True)
    dx, dg = _rmsnorm_bwd(f"mix{l}_dnorm", x, _row(w["mixer_norm"][l]), dh, dx1)
    grads["mixer_norm"][l] = dg[0]
    return dx


_MIXERS = ((_attn_fwd, _attn_bwd), (_conv_fwd, _conv_bwd), (_hgrn_fwd, _hgrn_bwd))
_PER_MIXER = {"attn": 0, "conv": 1, "hgrn": 2}


def _local_step(x, target, w):
    depth = w["mixer_norm"].shape[0]
    grads = {}
    for name, v in w.items():
        lead = v.shape[0]
        grads[name] = jnp.zeros(v.shape, F32) if name == "hgrn_lb_logits" else [None] * lead
    saved = []
    for l in range(depth):
        fwd, _ = _MIXERS[l % N_MIXERS]
        x, s_mix = fwd(l, l // N_MIXERS, x, w)
        x, s_ffn = _ffn_fwd(l, x, w)
        saved.append((s_mix, s_ffn))
    dx, loss_cols = _loss_grad("loss", x, target)
    for l in reversed(range(depth)):
        _, bwd = _MIXERS[l % N_MIXERS]
        s_mix, s_ffn = saved[l]
        dx = _ffn_bwd(l, dx, s_ffn, w, grads)
        dx = bwd(l, l // N_MIXERS, dx, s_mix, w, grads)
    grads = {k: (v if k == "hgrn_lb_logits" else jnp.stack(v)) for k, v in grads.items()}
    return jnp.sum(loss_cols), dx, grads


_HBM = pl.BlockSpec(memory_space=pltpu.HBM)


def _chip_peers():
    x, y, c = lax.axis_index("x"), lax.axis_index("y"), lax.axis_index("c")
    return 2 * x + y, (x, y, c), [(1 - x, y), (x, 1 - y), (1 - x, 1 - y)]


def _exchange_chips(name, src):
    def body(src_ref, out_ref, send_sems, recv_sems, local_sem):
        p, (x, y, c), peers = _chip_peers()
        mine = pltpu.make_async_copy(src_ref.at[p], out_ref.at[p], local_sem)
        mine.start()

        def copy(k, slab_from, slab_to, peer):
            return pltpu.make_async_remote_copy(
                src_ref=src_ref.at[slab_from], dst_ref=out_ref.at[slab_to], send_sem=send_sems.at[k],
                recv_sem=recv_sems.at[k], device_id=(peer[0], peer[1], c), device_id_type=MESH)

        sends = [copy(k, 2 * px + py, p, (px, py)) for k, (px, py) in enumerate(peers)]
        for s in sends:
            s.start()
        for k, (px, py) in enumerate(peers):
            copy(k, p, 2 * px + py, (px, py)).wait_recv()
        for s in sends:
            s.wait_send()
        mine.wait()

    return pl.pallas_call(
        body, name=name, in_specs=[_HBM], out_specs=_HBM, out_shape=jax.ShapeDtypeStruct(src.shape, src.dtype),
        scratch_shapes=[pltpu.SemaphoreType.DMA((3,)), pltpu.SemaphoreType.DMA((3,)), pltpu.SemaphoreType.DMA],
    )(src)


def _all_gather_chips(name, shard):
    R = shard.shape[0]
    half = R // 2

    def body(src_ref, out_ref, send_sems, recv_sems, local_sem):
        p, (x, y, c), peers = _chip_peers()
        mine = pltpu.make_async_copy(src_ref, out_ref.at[p], local_sem)
        mine.start()

        def rows(slab, core):
            return out_ref.at[slab, pl.ds(core * half, half), :]

        def over_ici(k, slab, peer):
            src = src_ref.at[pl.ds(c * half, half), :] if slab is None else rows(slab, c)
            return pltpu.make_async_remote_copy(
                src_ref=src, dst_ref=rows(p if slab is None else slab, c), send_sem=send_sems.at[k],
                recv_sem=recv_sems.at[k], device_id=(peer[0], peer[1], c), device_id_type=MESH)

        def to_sibling(k, slab, core):
            return pltpu.make_async_remote_copy(
                src_ref=rows(slab, core), dst_ref=rows(slab, core), send_sem=send_sems.at[3 + k],
                recv_sem=recv_sems.at[3 + k], device_id=(x, y, 1 - c), device_id_type=MESH)

        sends = [over_ici(k, None, peer) for k, peer in enumerate(peers)]
        for s in sends:
            s.start()
        passed = []
        for k, (px, py) in enumerate(peers):
            over_ici(k, 2 * px + py, (px, py)).wait_recv()
            passed.append(to_sibling(k, 2 * px + py, c))
            passed[k].start()
        for k, (px, py) in enumerate(peers):
            to_sibling(k, 2 * px + py, 1 - c).wait_recv()
        for s in sends + passed:
            s.wait_send()
        mine.wait()

    return pl.pallas_call(
        body, name=name, in_specs=[_HBM], out_specs=_HBM,
        out_shape=jax.ShapeDtypeStruct((N_CHIPS,) + shard.shape, shard.dtype),
        scratch_shapes=[pltpu.SemaphoreType.DMA((6,)), pltpu.SemaphoreType.DMA((6,)), pltpu.SemaphoreType.DMA],
    )(shard)


def _swap_cores(name, v):
    def body(v_ref, out_ref, send_sem, recv_sem):
        x, y, c = lax.axis_index("x"), lax.axis_index("y"), lax.axis_index("c")
        cp = pltpu.make_async_remote_copy(src_ref=v_ref, dst_ref=out_ref, send_sem=send_sem, recv_sem=recv_sem,
                                          device_id=(x, y, 1 - c), device_id_type=MESH)
        cp.start()
        cp.wait()

    return pl.pallas_call(
        body, name=name, in_specs=[_HBM], out_specs=_HBM, out_shape=jax.ShapeDtypeStruct(v.shape, v.dtype),
        scratch_shapes=[pltpu.SemaphoreType.DMA, pltpu.SemaphoreType.DMA],
    )(v)


_WEIGHTS = ("mixer_norm", "ffn_norm", "attn_w_in", "attn_q_gain", "attn_k_gain", "attn_w_out", "conv_w_in",
            "conv_b_in", "conv_dw_w", "conv_dw_b", "conv_ln_g", "conv_ln_b", "conv_w_out", "conv_b_out",
            "hgrn_w_in", "hgrn_lb_logits", "hgrn_norm_g", "hgrn_w_out", "ffn_w_up", "ffn_conv_w", "ffn_conv_b",
            "ffn_w_down")
_SHARD_AXIS = {"attn_w_in": 2, "attn_w_out": 2, "conv_w_in": 2, "conv_dw_w": 2, "conv_w_out": 1, "hgrn_w_in": 2,
               "hgrn_norm_g": 1, "hgrn_w_out": 1, "ffn_w_up": 2, "ffn_conv_w": 2, "ffn_w_down": 1}
_MATMUL_WEIGHTS = ("attn_w_in", "attn_w_out", "conv_w_in", "conv_w_out", "hgrn_w_in", "hgrn_w_out", "ffn_w_up",
                   "ffn_w_down")
PACK_COLS = 1024
PACK_ROWS = 512


def _pack(arrays, nlead, dtype):
    lead = arrays[0].shape[:nlead]
    flat = []
    for a in arrays:
        f = a.reshape(lead + (-1,)).astype(dtype)
        flat.append(jnp.pad(f, [(0, 0)] * nlead + [(0, (-f.shape[-1]) % PACK_COLS)]))
    buf = jnp.concatenate(flat, axis=-1)
    buf = jnp.pad(buf, [(0, 0)] * nlead + [(0, (-buf.shape[-1]) % (PACK_COLS * PACK_ROWS))])
    return buf.reshape(lead + (-1, PACK_COLS))


def _unpack(buf, shapes, nlead):
    lead = buf.shape[:nlead]
    flat = buf.reshape(lead + (-1,))
    out, off = [], 0
    for shape in shapes:
        n = 1
        for s in shape:
            n *= s
        out.append(flat[..., off:off + n].reshape(lead + tuple(shape)))
        off += n + (-n) % PACK_COLS
    return out


def _merge_shards(piece, axis):
    moved = jnp.moveaxis(piece, 0, axis)
    shape = moved.shape
    return moved.reshape(shape[:axis] + (shape[axis] * shape[axis + 1],) + shape[axis + 2:])


def _split_shards(full, axis):
    shape = full.shape
    cut = full.reshape(shape[:axis] + (N_CHIPS, shape[axis] // N_CHIPS) + shape[axis + 1:])
    return jnp.moveaxis(cut, axis, 0)


def _gather_weights(local):
    big = [n for n in _WEIGHTS if n in _MATMUL_WEIGHTS]
    small = [n for n in _WEIGHTS if n in _SHARD_AXIS and n not in _MATMUL_WEIGHTS]
    full = {n: local[n] for n in _WEIGHTS if n not in _SHARD_AXIS}
    for names, dtype, tag in ((big, BF16, "comm_gather_matmul_weights"), (small, F32, "comm_gather_small_weights")):
        gathered = _all_gather_chips(tag, _pack([local[n] for n in names], 0, dtype))
        pieces = _unpack(gathered, [local[n].shape for n in names], 1)
        for n, piece in zip(names, pieces):
            full[n] = _merge_shards(piece, _SHARD_AXIS[n])
    return full


def _reduce_gradients(grads, local):
    out = {}
    big = [n for n in _WEIGHTS if n in _MATMUL_WEIGHTS]
    rest = [n for n in _WEIGHTS if n not in _MATMUL_WEIGHTS]
    for names, dtype, tag in ((big, BF16, "matmul"), (rest, F32, "small")):
        slabs = []
        for n in names:
            g = grads[n]
            if n in _SHARD_AXIS:
                slabs.append(_split_shards(g, _SHARD_AXIS[n]))
            else:
                slabs.append(jnp.broadcast_to(g[None], (N_CHIPS,) + g.shape))
        packed = _pack(slabs, 1, dtype)
        landed = _exchange_chips(f"comm_scatter_{tag}_gradients", packed)
        partial = _sum_slabs(f"sum_chips_{tag}", landed)
        other = _swap_cores(f"comm_swap_{tag}_sums", partial)
        total = _add(f"sum_cores_{tag}", [partial, other])
        out.update(zip(names, _unpack(total, [local[n].shape for n in names], 0)))
    return out


def kernel(x, mixer_norm, ffn_norm, attn_w_in, attn_q_gain, attn_k_gain, attn_w_out, conv_w_in, conv_b_in, conv_dw_w, conv_dw_b, conv_ln_g, conv_ln_b, conv_w_out, conv_b_out, hgrn_w_in, hgrn_lb_logits, hgrn_norm_g, hgrn_w_out, ffn_w_up, ffn_conv_w, ffn_conv_b, ffn_w_down, loss_target, m_mixer_norm, m_ffn_norm, m_attn_w_in, m_attn_q_gain, m_attn_k_gain, m_attn_w_out, m_conv_w_in, m_conv_b_in, m_conv_dw_w, m_conv_dw_b, m_conv_ln_g, m_conv_ln_b, m_conv_w_out, m_conv_b_out, m_hgrn_w_in, m_hgrn_lb_logits, m_hgrn_norm_g, m_hgrn_w_out, m_ffn_w_up, m_ffn_conv_w, m_ffn_conv_b, m_ffn_w_down, v_mixer_norm, v_ffn_norm, v_attn_w_in, v_attn_q_gain, v_attn_k_gain, v_attn_w_out, v_conv_w_in, v_conv_b_in, v_conv_dw_w, v_conv_dw_b, v_conv_ln_g, v_conv_ln_b, v_conv_w_out, v_conv_b_out, v_hgrn_w_in, v_hgrn_lb_logits, v_hgrn_norm_g, v_hgrn_w_out, v_ffn_w_up, v_ffn_conv_w, v_ffn_conv_b, v_ffn_w_down):
    given = dict(locals())
    local = {n: given[n] for n in _WEIGHTS}
    full = _gather_weights(local)
    loss, dx, grads = _local_step(x[0], loss_target[0], full)
    loss = lax.psum(loss, ("x", "y", "c"))
    grad = _reduce_gradients(grads, local)
    delta, new_m, new_v = {}, {}, {}
    for n in _WEIGHTS:
        shape = local[n].shape
        as2d = lambda a: a.reshape(-1, shape[-1])
        d, m, v = _adamw(f"adamw_{n}", as2d(local[n]), as2d(grad[n]), as2d(given["m_" + n]), as2d(given["v_" + n]))
        delta[n], new_m[n], new_v[n] = d.reshape(shape), m.reshape(shape), v.reshape(shape)
    return (loss, dx[None], *[grad[n] for n in _WEIGHTS], *[delta[n] for n in _WEIGHTS],
            *[new_m[n] for n in _WEIGHTS], *[new_v[n] for n in _WEIGHTS])
```

```python
import functools

import jax
import jax.numpy as jnp
from jax import lax
from jax.experimental import pallas as pl
from jax.experimental.pallas import tpu as pltpu

F32 = jnp.float32
BF16 = jnp.bfloat16

EPS = 1e-6
N_MIXERS = 3
ATTN_DILATIONS = (1, 4, 16)
ATTN_BLOCK = 128
ATTN_HEADS = 8
ATTN_HEAD_DIM = 64
ATTN_GW = ATTN_HEADS * ATTN_HEAD_DIM
ATTN_ROWS = 32
HGRN_HEAD = 128
HGRN_CHUNK = 16
HGRN_TILE = 256
HGRN_GROUP = 4
ADAM_LR, ADAM_B1, ADAM_B2, ADAM_EPS, ADAM_WD, ADAM_STEP = 0.001, 0.9, 0.999, 1e-08, 0.01, 10

LANES = 128
SUBLANES = 8
VMEM_LIMIT = 56 * 1024 * 1024
N_CHIPS = 4
MESH = pl.DeviceIdType.MESH

HI = lax.Precision.HIGHEST


def _params(*sem):
    return pltpu.CompilerParams(dimension_semantics=sem, vmem_limit_bytes=VMEM_LIMIT)


def _pick(n, target):
    if n <= target:
        return n
    best = None
    for t in range(LANES, target + 1, LANES):
        if n % t == 0:
            best = t
    assert best is not None, (n, target)
    return best


def _pick_rows(n, target):
    if n <= target:
        return n
    for t in range(target, 15, -16):
        if n % t == 0:
            return t
    return n


def _dot(a, b, dims, precision=None):
    return lax.dot_general(a, b, (dims, ((), ())), precision=precision, preferred_element_type=F32)


def _nn(a, b, precision=None):
    return _dot(a, b, ((1,), (0,)), precision)


def _nt(a, b, precision=None):
    return _dot(a, b, ((1,), (1,)), precision)


def _tn(a, b, precision=None):
    return _dot(a, b, ((0,), (0,)), precision)


def _sigmoid(x):
    return 1.0 / (1.0 + jnp.exp(-x))


ROWWISE_UNROLL_ROWS = 64


def _rowwise(name, fn, rows, pars=(), outs=(), accs=(), *, tc=None, tm=512, rb=16):
    S = rows[0][0].shape[0]
    tm = _pick_rows(S, tm)
    rb = rb if tm % rb == 0 else tm
    width = tc if tc is not None else None
    ncol = 1
    if tc is not None:
        base = outs[0][0] if outs else accs[0][1]
        ncol = base // tc
    n_r, n_p, n_o, n_a = len(rows), len(pars), len(outs), len(accs)

    def body(*refs):
        row_refs, par_refs = refs[:n_r], refs[n_r:n_r + n_p]
        out_refs, acc_refs = refs[n_r + n_p:n_r + n_p + n_o], refs[n_r + n_p + n_o:]
        if n_a:
            @pl.when(pl.program_id(1) == 0)
            def _():
                for a in acc_refs:
                    a[...] = jnp.zeros_like(a)

        def step(s, carry):
            sl = pl.ds(pl.multiple_of(s * rb, rb), rb)
            res = fn(*[r[sl, :] for r in row_refs], *[p[...] for p in par_refs])
            res = res if isinstance(res, tuple) else (res,)
            for o, v in zip(out_refs, res[:n_o]):
                o[sl, :] = v.astype(o.dtype)
            for a, v in zip(acc_refs, res[n_o:]):
                a[...] += v
            return carry

        lax.fori_loop(0, tm // rb, step, 0, unroll=min(tm // rb, max(2, ROWWISE_UNROLL_ROWS // rb)))

    def row_spec(c, off):
        if tc is None:
            return pl.BlockSpec((tm, c), lambda j, i: (i, 0))
        return pl.BlockSpec((tm, tc), lambda j, i, o=off // tc: (i, j + o))

    def par_spec(shape, off):
        if off is None or tc is None:
            return pl.BlockSpec(shape, lambda j, i: (0, 0))
        return pl.BlockSpec((shape[0], tc), lambda j, i, o=off // tc: (0, j + o))

    in_specs = [row_spec(a.shape[1], off) for a, off in rows]
    in_specs += [par_spec(a.shape, off) for a, off in pars]
    out_specs = [row_spec(c, 0) for c, _ in outs] + [par_spec(s, 0) for s in accs]
    out_shape = [jax.ShapeDtypeStruct((S, c), d) for c, d in outs]
    out_shape += [jax.ShapeDtypeStruct(s, F32) for s in accs]
    res = pl.pallas_call(
        body, name=name, grid=(ncol, S // tm), in_specs=in_specs, out_specs=out_specs, out_shape=out_shape,
        compiler_params=_params("parallel", "arbitrary" if n_a else "parallel"),
    )(*[a for a, _ in rows], *[a for a, _ in pars])
    return res[0] if len(res) == 1 else tuple(res)


MATMUL_VMEM = 36 * 1024 * 1024


def _matmul_tiles(M, N, pairs, out_dtype, residual):
    tm = _pick_rows(M, 512)
    for tn in sorted({_pick(N, t) for t in range(LANES, 2049, LANES)}, reverse=True):
        step = sum(tm * a.shape[1] * a.dtype.itemsize + a.shape[1] * tn * b.dtype.itemsize for a, b in pairs)
        step += tm * tn * (jnp.dtype(out_dtype).itemsize + (4 if residual is not None else 0))
        if 2 * step <= MATMUL_VMEM:
            return tm, tn
    return tm, LANES


def _matmul(name, pairs, *, trans_b=False, bias=None, residual=None, out_dtype=F32):
    M = pairs[0][0].shape[0]
    N = pairs[0][1].shape[0] if trans_b else pairs[0][1].shape[1]
    tm, tn = _matmul_tiles(M, N, pairs, out_dtype, residual)
    n = len(pairs)

    def body(*refs):
        acc = None
        for i in range(n):
            a = refs[2 * i][...].astype(BF16)
            b = refs[2 * i + 1][...].astype(BF16)
            d = _nt(a, b) if trans_b else _nn(a, b)
            acc = d if acc is None else acc + d
        k = 2 * n
        if bias is not None:
            acc = acc + refs[k][...]
            k += 1
        if residual is not None:
            acc = acc + refs[k][...]
            k += 1
        refs[k][...] = acc.astype(out_dtype)

    in_specs, args = [], []
    for a, b in pairs:
        K = a.shape[1]
        in_specs.append(pl.BlockSpec((tm, K), lambda j, i: (i, 0)))
        in_specs.append(pl.BlockSpec((tn, K), lambda j, i: (j, 0)) if trans_b
                        else pl.BlockSpec((K, tn), lambda j, i: (0, j)))
        args += [a, b]
    if bias is not None:
        in_specs.append(pl.BlockSpec((1, tn), lambda j, i: (0, j)))
        args.append(bias)
    if residual is not None:
        in_specs.append(pl.BlockSpec((tm, tn), lambda j, i: (i, j)))
        args.append(residual)
    return pl.pallas_call(
        body, name=name, grid=(N // tn, M // tm), in_specs=in_specs,
        out_specs=pl.BlockSpec((tm, tn), lambda j, i: (i, j)),
        out_shape=jax.ShapeDtypeStruct((M, N), out_dtype), compiler_params=_params("parallel", "parallel"),
    )(*args)


def _matmul_tn(name, a, b, *, tm=1408, tn=1408, tk=1024):
    S, M = a.shape
    N = b.shape[1]
    tm, tn, tk = _pick(M, tm), _pick(N, tn), _pick_rows(S, tk)

    def body(a_ref, b_ref, o_ref):
        @pl.when(pl.program_id(2) == 0)
        def _():
            o_ref[...] = jnp.zeros_like(o_ref)

        o_ref[...] += _tn(a_ref[...].astype(BF16), b_ref[...].astype(BF16))

    return pl.pallas_call(
        body, name=name, grid=(M // tm, N // tn, S // tk),
        in_specs=[pl.BlockSpec((tk, tm), lambda i, j, k: (k, i)), pl.BlockSpec((tk, tn), lambda i, j, k: (k, j))],
        out_specs=pl.BlockSpec((tm, tn), lambda i, j, k: (i, j)),
        out_shape=jax.ShapeDtypeStruct((M, N), F32), compiler_params=_params("parallel", "parallel", "arbitrary"),
    )(a, b)


def _halo_rows(K):
    return 8 if K <= 9 else 32


def _shifted_copies(ext, shifted, K):
    if K <= SUBLANES:
        return
    n = shifted.shape[1]
    for s in range(1, SUBLANES):
        shifted[s, 0:n, :] = ext[s:s + n, :]


def _window(ext, shifted, K, off, rows):
    s = off % SUBLANES
    if K <= SUBLANES or s == 0:
        return ext[off:off + rows, :]
    return shifted[s, off - s:off - s + rows, :]


def _dwconv(name, x, w, b, *, reverse, out_dtype=F32):
    S, C = x.shape
    K = w.shape[0]
    H = _halo_rows(K)
    tm, tc = _pick_rows(S, 512 if K <= 4 else 256), _pick(C, 1408 if K <= 4 else 256)
    nrow = S // tm
    RB = 16 if out_dtype == BF16 else 8

    def body(x_ref, h_ref, w_ref, b_ref, o_ref, ext, shifted):
        i = pl.program_id(1)
        edge = (i == nrow - 1) if reverse else (i == 0)
        halo = jnp.where(edge, 0.0, h_ref[...].astype(F32))
        if reverse:
            ext[0:tm, :] = x_ref[...].astype(F32)
            ext[tm:tm + H, :] = halo
        else:
            ext[0:H, :] = halo
            ext[H:H + tm, :] = x_ref[...].astype(F32)
        _shifted_copies(ext, shifted, K)
        wv = w_ref[...]
        for s in range(tm // RB):
            acc = jnp.broadcast_to(b_ref[...], (RB, tc))
            for k in range(K):
                off = s * RB + ((K - 1 - k) if reverse else (H - (K - 1) + k))
                acc = acc + wv[k:k + 1, :] * _window(ext, shifted, K, off, RB)
            o_ref[s * RB:(s + 1) * RB, :] = acc.astype(out_dtype)

    r = tm // H
    if reverse:
        halo_map = lambda j, i: (jnp.minimum((i + 1) * r, S // H - 1), j)
    else:
        halo_map = lambda j, i: (jnp.maximum(i * r - 1, 0), j)
    return pl.pallas_call(
        body, name=name, grid=(C // tc, nrow),
        in_specs=[pl.BlockSpec((tm, tc), lambda j, i: (i, j)), pl.BlockSpec((H, tc), halo_map),
                  pl.BlockSpec((K, tc), lambda j, i: (0, j)), pl.BlockSpec((1, tc), lambda j, i: (0, j))],
        out_specs=pl.BlockSpec((tm, tc), lambda j, i: (i, j)),
        out_shape=jax.ShapeDtypeStruct((S, C), out_dtype),
        scratch_shapes=[pltpu.VMEM((tm + H, tc), F32), pltpu.VMEM((SUBLANES, tm + H - SUBLANES, tc), F32)],
        compiler_params=_params("parallel", "parallel"),
    )(x, x, w, b)


def _dwconv_wgrad(name, x, dy, K):
    S, C = x.shape
    H = _halo_rows(K)
    tm, tc = _pick_rows(S, 512 if K <= 4 else 256), _pick(C, 512 if K <= 4 else LANES)
    RB = 8

    def body(x_ref, h_ref, dy_ref, dw_ref, db_ref, ext, shifted):
        i = pl.program_id(1)

        @pl.when(i == 0)
        def _():
            dw_ref[...] = jnp.zeros_like(dw_ref)
            db_ref[...] = jnp.zeros_like(db_ref)

        ext[0:H, :] = jnp.where(i == 0, 0.0, h_ref[...].astype(F32))
        ext[H:H + tm, :] = x_ref[...].astype(F32)
        _shifted_copies(ext, shifted, K)
        acc = [jnp.zeros((RB, tc), F32) for _ in range(K)]
        accb = jnp.zeros((RB, tc), F32)
        for s in range(tm // RB):
            d = dy_ref[s * RB:(s + 1) * RB, :].astype(F32)
            accb = accb + d
            for k in range(K):
                off = s * RB + H - (K - 1) + k
                acc[k] = acc[k] + d * _window(ext, shifted, K, off, RB)
        for k in range(K):
            dw_ref[k:k + 1, :] += jnp.sum(acc[k], axis=0, keepdims=True)
        db_ref[...] += jnp.sum(accb, axis=0, keepdims=True)

    r = tm // H
    return pl.pallas_call(
        body, name=name, grid=(C // tc, S // tm),
        in_specs=[pl.BlockSpec((tm, tc), lambda j, i: (i, j)),
                  pl.BlockSpec((H, tc), lambda j, i: (jnp.maximum(i * r - 1, 0), j)),
                  pl.BlockSpec((tm, tc), lambda j, i: (i, j))],
        out_specs=[pl.BlockSpec((K, tc), lambda j, i: (0, j)), pl.BlockSpec((1, tc), lambda j, i: (0, j))],
        out_shape=[jax.ShapeDtypeStruct((K, C), F32), jax.ShapeDtypeStruct((1, C), F32)],
        scratch_shapes=[pltpu.VMEM((tm + H, tc), F32), pltpu.VMEM((SUBLANES, tm + H - SUBLANES, tc), F32)],
        compiler_params=_params("parallel", "arbitrary"),
    )(x, x, dy)


FFN_HALO = 16


def _row_shifts(ref, r0, rows, cs, shifts):
    n = rows // SUBLANES
    lo = -1 if max(shifts) > 0 else 0
    hi = n + (1 if min(shifts) < 0 else 0)
    v = {j: ref[r0 + j * SUBLANES:r0 + (j + 1) * SUBLANES, cs] for j in range(lo, hi)}
    row = lax.broadcasted_iota(jnp.int32, (SUBLANES, LANES), 0)
    out = {}
    for s in shifts:
        if s == 0:
            pieces = [v[j] for j in range(n)]
        elif s > 0:
            rot = {j: pltpu.roll(v[j], s, axis=0) for j in range(-1, n)}
            pieces = [jnp.where(row < s, rot[j - 1], rot[j]) for j in range(n)]
        else:
            rot = {j: pltpu.roll(v[j], SUBLANES + s, axis=0) for j in range(0, n + 1)}
            pieces = [jnp.where(row < SUBLANES + s, rot[j], rot[j + 1]) for j in range(n)]
        out[s] = jnp.concatenate(pieces, axis=0)
    return out


def _conv_taps(w, b, ext, r0, rows, cs):
    K = w.shape[0]
    win = _row_shifts(ext, r0, rows, cs, list(range(K)))
    acc = b
    for k in range(K):
        acc = acc + w[k:k + 1, :] * win[K - 1 - k]
    return acc


def _ffn_up_fused(name, h, w_up, cw, cb):
    S, D = h.shape
    F = w_up.shape[1] // 2
    tm, tn = _pick_rows(S, 512), _pick(F, 1408)
    nj, H, RB = F // tn, FFN_HALO, 64

    def body(h_ref, hh_ref, wg_ref, wu_ref, cwg_ref, cwu_ref, cbg_ref, cbu_ref,
             a_ref, u0g_ref, u0u_ref, ug_ref, uu_ref, eg, eu):
        first = pl.program_id(1) == 0
        hv, halo = h_ref[...], hh_ref[...]
        for w_ref, u0_ref, e in ((wg_ref, u0g_ref, eg), (wu_ref, u0u_ref, eu)):
            w = w_ref[...]
            u0 = _nn(hv, w)
            u0_ref[...] = u0.astype(BF16)
            e[0:H, :] = jnp.where(first, 0.0, _nn(halo, w))
            e[H:H + tm, :] = u0
        for c in range(tn // LANES):
            cs = slice(c * LANES, (c + 1) * LANES)
            wg, wu, bg, bu = cwg_ref[:, cs], cwu_ref[:, cs], cbg_ref[:, cs], cbu_ref[:, cs]
            for s in range(tm // RB):
                rows = slice(s * RB, (s + 1) * RB)
                ug = _conv_taps(wg, bg, eg, H + s * RB, RB, cs)
                uu = _conv_taps(wu, bu, eu, H + s * RB, RB, cs)
                ug_ref[rows, cs] = ug.astype(BF16)
                uu_ref[rows, cs] = uu.astype(BF16)
                a_ref[rows, cs] = (ug * _sigmoid(ug) * uu).astype(BF16)

    r = tm // H
    gate = lambda rows: pl.BlockSpec((rows, tn), lambda j, i: (0, j))
    up = lambda rows: pl.BlockSpec((rows, tn), lambda j, i: (0, j + nj))
    tile = pl.BlockSpec((tm, tn), lambda j, i: (i, j))
    K = cw.shape[0]
    return pl.pallas_call(
        body, name=name, grid=(nj, S // tm),
        in_specs=[pl.BlockSpec((tm, D), lambda j, i: (i, 0)),
                  pl.BlockSpec((H, D), lambda j, i: (jnp.maximum(i * r - 1, 0), 0)),
                  gate(D), up(D), gate(K), up(K), gate(1), up(1)],
        out_specs=[tile] * 5, out_shape=[jax.ShapeDtypeStruct((S, F), BF16)] * 5,
        scratch_shapes=[pltpu.VMEM((H + tm, tn), F32)] * 2, compiler_params=_params("parallel", "parallel"),
    )(h, h, w_up, w_up, cw, cw, cb, cb)


def _ffn_gate_bwd_fused(name, dy, w_down, u0g, u0u, ug, uu, cw):
    S, D = dy.shape
    F, K = w_down.shape[0], cw.shape[0]
    tm, tn = _pick_rows(S, 512), _pick(F, 1408)
    nj, nrow, H, RB = F // tn, S // tm, FFN_HALO, 16
    RW = 32

    def body(dy_ref, dyn_ref, wd_ref, u0g_ref, u0u_ref, g_ref, gn_ref, u_ref, un_ref, cwg_ref, cwu_ref,
             dg_ref, du_ref, dcwg_ref, dcwu_ref, dcbg_ref, dcbu_ref, dg_s, du_s, da_s):
        i = pl.program_id(1)
        last = i == nrow - 1

        @pl.when(i == 0)
        def _():
            for ref in (dcwg_ref, dcwu_ref, dcbg_ref, dcbu_ref):
                ref[...] = jnp.zeros_like(ref)

        wd = wd_ref[...]
        da_s[0:tm, :] = _nt(dy_ref[...].astype(BF16), wd)
        da_s[tm:tm + H, :] = jnp.where(last, 0.0, _nt(dyn_ref[...].astype(BF16), wd))
        for c in range(tn // LANES):
            cs = slice(c * LANES, (c + 1) * LANES)
            for s in range(tm // RB + 1):
                rows = slice(s * RB, (s + 1) * RB)
                src_g, src_u, src_rows = (g_ref, u_ref, rows) if s < tm // RB else (gn_ref, un_ref, slice(0, RB))
                gv, uv = src_g[src_rows, cs].astype(F32), src_u[src_rows, cs].astype(F32)
                da = da_s[rows, cs]
                sg = _sigmoid(gv)
                dg_s[rows, cs] = da * uv * (sg * (1.0 + gv * (1.0 - sg)))
                du_s[rows, cs] = da * (gv * sg)
            for d_s, u0_ref, cw_ref, out_ref, dcw_ref, dcb_ref in (
                    (dg_s, u0g_ref, cwg_ref, dg_ref, dcwg_ref, dcbg_ref),
                    (du_s, u0u_ref, cwu_ref, du_ref, dcwu_ref, dcbu_ref)):
                w = cw_ref[:, cs]
                acc = [jnp.zeros((RW, LANES), F32) for _ in range(K)]
                accb = jnp.zeros((RW, LANES), F32)
                for s in range(tm // RW):
                    r0 = s * RW
                    u0 = u0_ref[r0:r0 + RW, cs].astype(F32)
                    ahead = _row_shifts(d_s, r0, RW, cs, [-m for m in range(K)])
                    t = None
                    for k in range(K):
                        m = K - 1 - k
                        win = ahead[-m]
                        if m == 0:
                            accb = accb + win
                        acc[k] = acc[k] + win * u0
                        term = w[k:k + 1, :] * win
                        t = term if t is None else t + term
                    out_ref[r0:r0 + RW, cs] = t.astype(BF16)
                for k in range(K):
                    dcw_ref[k:k + 1, cs] += jnp.sum(acc[k], axis=0, keepdims=True)
                dcb_ref[:, cs] += jnp.sum(accb, axis=0, keepdims=True)

    r = tm // H
    gate = lambda rows: pl.BlockSpec((rows, tn), lambda j, i: (0, j))
    up = lambda rows: pl.BlockSpec((rows, tn), lambda j, i: (0, j + nj))
    tile = pl.BlockSpec((tm, tn), lambda j, i: (i, j))
    nxt = pl.BlockSpec((H, tn), lambda j, i: (jnp.minimum((i + 1) * r, S // H - 1), j))
    acc_w, acc_b = pl.BlockSpec((K, tn), lambda j, i: (0, j)), pl.BlockSpec((1, tn), lambda j, i: (0, j))
    return pl.pallas_call(
        body, name=name, grid=(nj, nrow),
        in_specs=[pl.BlockSpec((tm, D), lambda j, i: (i, 0)),
                  pl.BlockSpec((H, D), lambda j, i: (jnp.minimum((i + 1) * r, S // H - 1), 0)),
                  pl.BlockSpec((tn, D), lambda j, i: (j, 0)),
                  tile, tile, tile, nxt, tile, nxt, gate(K), up(K)],
        out_specs=[tile, tile, acc_w, acc_w, acc_b, acc_b],
        out_shape=[jax.ShapeDtypeStruct((S, F), BF16)] * 2 + [jax.ShapeDtypeStruct((K, F), F32)] * 2
        + [jax.ShapeDtypeStruct((1, F), F32)] * 2,
        scratch_shapes=[pltpu.VMEM((tm + H, tn), F32)] * 3, compiler_params=_params("parallel", "arbitrary"),
    )(dy, dy, w_down, u0g, u0u, ug, ug, uu, uu, cw, cw)


def _colsum(v):
    return jnp.sum(v, axis=0, keepdims=True)


def _rmsnorm_fwd(name, x, gain):
    def fn(x, g):
        r = lax.rsqrt(jnp.mean(x * x, axis=-1, keepdims=True) + EPS)
        return x * r * g
    return _rowwise(name, fn, [(x, 0)], [(gain, None)], [(x.shape[1], BF16)])


NORM_BWD_VMEM = 44 * 1024 * 1024


def _matmul_rmsnorm_bwd(name, pairs, x, gain, dres):
    M, D = x.shape
    n = len(pairs)
    weights = sum(b.shape[0] * b.shape[1] * b.dtype.itemsize for _, b in pairs)
    for tm in (512, 256, 128):
        step = sum(tm * a.shape[1] * a.dtype.itemsize for a, _ in pairs) + 3 * tm * D * 4
        if 2 * (step + weights) + tm * D * 4 <= NORM_BWD_VMEM:
            break
    RB = 16

    def body(*refs):
        x_ref, g_ref, r_ref, o_ref, dg_ref, dh_s = refs[2 * n:]

        @pl.when(pl.program_id(0) == 0)
        def _():
            dg_ref[...] = jnp.zeros_like(dg_ref)

        acc = None
        for i in range(n):
            d = _nt(refs[2 * i][...].astype(BF16), refs[2 * i + 1][...].astype(BF16))
            acc = d if acc is None else acc + d
        dh_s[...] = acc
        g = g_ref[...]

        def step(s, carry):
            sl = pl.ds(pl.multiple_of(s * RB, RB), RB)
            xv, dh = x_ref[sl, :], dh_s[sl, :]
            r = lax.rsqrt(jnp.mean(xv * xv, axis=-1, keepdims=True) + EPS)
            xh = xv * r
            dxh = dh * g
            o_ref[sl, :] = r_ref[sl, :] + r * (dxh - xh * jnp.mean(dxh * xh, axis=-1, keepdims=True))
            dg_ref[...] += _colsum(dh * xh)
            return carry

        lax.fori_loop(0, tm // RB, step, 0, unroll=ROWWISE_UNROLL_ROWS // RB)

    in_specs, args = [], []
    for a, b in pairs:
        in_specs += [pl.BlockSpec((tm, a.shape[1]), lambda i: (i, 0)), pl.BlockSpec(b.shape, lambda i: (0, 0))]
        args += [a, b]
    rows = pl.BlockSpec((tm, D), lambda i: (i, 0))
    vec = pl.BlockSpec((1, D), lambda i: (0, 0))
    return pl.pallas_call(
        body, name=name, grid=(M // tm,), in_specs=in_specs + [rows, vec, rows], out_specs=[rows, vec],
        out_shape=[jax.ShapeDtypeStruct((M, D), F32), jax.ShapeDtypeStruct((1, D), F32)],
        scratch_shapes=[pltpu.VMEM((tm, D), F32)], compiler_params=_params("arbitrary"),
    )(*args, x, gain, dres)


def _silu_gate_fwd(name, gate, up):
    F = gate.shape[1]
    def fn(g, up):
        return g * _sigmoid(g) * up
    return _rowwise(name, fn, [(gate, 0), (up, 0)], [], [(F, BF16)], tc=_pick(F, 512))


def _silu_gate_bwd(name, gate, up, da):
    F = gate.shape[1]
    def fn(g, up, da):
        s = _sigmoid(g)
        return da * up * (s * (1.0 + g * (1.0 - s))), da * (g * s)
    return _rowwise(name, fn, [(gate, 0), (up, 0), (da, 0)], [], [(F, F32), (F, F32)], tc=_pick(F, 512))


def _glu_fwd(name, a, gate):
    C = a.shape[1]
    def fn(a, g):
        return a * _sigmoid(g)
    return _rowwise(name, fn, [(a, 0), (gate, 0)], [], [(C, F32)], tc=_pick(C, 512))


def _glu_bwd(name, a, gate, dglu):
    C = a.shape[1]
    def fn(a, g, d):
        s = _sigmoid(g)
        da, dg = d * s, d * a * s * (1.0 - s)
        return da, dg, _colsum(da), _colsum(dg)
    return _rowwise(name, fn, [(a, 0), (gate, 0), (dglu, 0)], [], [(C, BF16), (C, BF16)], [(1, C), (1, C)],
                    tc=_pick(C, 512))


def _ln_silu_fwd(name, c, g, b):
    def fn(c, g, b):
        mu = jnp.mean(c, axis=-1, keepdims=True)
        d = c - mu
        n = d * lax.rsqrt(jnp.mean(d * d, axis=-1, keepdims=True) + EPS) * g + b
        return n * _sigmoid(n)
    return _rowwise(name, fn, [(c, 0)], [(g, None), (b, None)], [(c.shape[1], BF16)])


def _ln_silu_bwd(name, c, g, b, dsw):
    def fn(c, dsw, g, b):
        mu = jnp.mean(c, axis=-1, keepdims=True)
        d = c - mu
        r = lax.rsqrt(jnp.mean(d * d, axis=-1, keepdims=True) + EPS)
        ch = d * r
        n = ch * g + b
        s = _sigmoid(n)
        dn = dsw * (s * (1.0 + n * (1.0 - s)))
        dch = dn * g
        dc = r * (dch - jnp.mean(dch, axis=-1, keepdims=True) - ch * jnp.mean(dch * ch, axis=-1, keepdims=True))
        return dc, _colsum(dn * ch), _colsum(dn)
    C = c.shape[1]
    return _rowwise(name, fn, [(c, 0), (dsw, 0)], [(g, None), (b, None)], [(C, F32)], [(1, C), (1, C)])


def _column_sums(name, x):
    return _rowwise(name, lambda x: (_colsum(x),), [(x, 0)], [], [], [(1, x.shape[1])])


def _loss_grad(name, y, target):
    D = y.shape[1]
    def fn(y, t):
        e = y - t
        return e * (1.0 / D), _colsum(e * e) * (0.5 / D)
    return _rowwise(name, fn, [(y, 0), (target, 0)], [], [(D, F32)], [(1, D)])


def _add(name, arrays):
    def fn(*xs):
        acc = xs[0]
        for x in xs[1:]:
            acc = acc + x
        return acc
    return _rowwise(name, fn, [(a, 0) for a in arrays], [], [(arrays[0].shape[1], F32)], tm=256)


def _sum_slabs(name, stacked):
    n, R, C = stacked.shape
    tm = _pick_rows(R, 256)

    def body(*refs):
        acc = refs[0][0].astype(F32)
        for r in refs[1:n]:
            acc = acc + r[0].astype(F32)
        refs[n][...] = acc

    return pl.pallas_call(
        body, name=name, grid=(R // tm,),
        in_specs=[pl.BlockSpec((1, tm, C), lambda i, q=q: (q, i, 0)) for q in range(n)],
        out_specs=pl.BlockSpec((tm, C), lambda i: (i, 0)), out_shape=jax.ShapeDtypeStruct((R, C), F32),
        compiler_params=_params("parallel"),
    )(*[stacked] * n)


def _adamw(name, w, g, m, v):
    c1 = 1.0 - ADAM_B1 ** ADAM_STEP
    c2 = 1.0 - ADAM_B2 ** ADAM_STEP
    def fn(w, g, m, v):
        m = ADAM_B1 * m + (1.0 - ADAM_B1) * g
        v = ADAM_B2 * v + (1.0 - ADAM_B2) * (g * g)
        delta = -ADAM_LR * ((m / c1) / (jnp.sqrt(v / c2) + ADAM_EPS) + ADAM_WD * w)
        return delta, m, v
    C = w.shape[1]
    return _rowwise(name, fn, [(w, 0), (g, 0), (m, 0), (v, 0)], [], [(C, F32)] * 3, tm=256)


SEG_ROWS = 128


def _segment_matrix(n, seg):
    i = jnp.arange(n) // seg
    return (i[:, None] == i[None, :]).astype(BF16)


def _seg_sum(v, B):
    hi = v.astype(BF16)
    lo = (v - hi.astype(F32)).astype(BF16)
    n = B.shape[0]
    slabs = [slice(c, c + n) for c in range(0, v.shape[1], n)]
    return jnp.concatenate([_nn(hi[:, c], B) + _nn(lo[:, c], B) for c in slabs], axis=1)


def _qknorm_fwd(name, proj, gain_full, is_norm, seg):
    def fn(x, gf, isn, B):
        ms = _seg_sum(x * x, B) * (1.0 / ATTN_HEAD_DIM)
        r = lax.rsqrt(ms + EPS)
        return x * (isn * r + (1.0 - isn)) * gf
    W = proj.shape[1]
    return _rowwise(name, fn, [(proj, 0)], [(gain_full, 0), (is_norm, 0), (seg, None)], [(W, BF16)],
                    tc=ATTN_GW, rb=SEG_ROWS)


def _qknorm_bwd(name, proj, dy, gain_full, is_norm, seg):
    def fn(x, dy, gf, isn, B):
        ms = _seg_sum(x * x, B) * (1.0 / ATTN_HEAD_DIM)
        r = lax.rsqrt(ms + EPS)
        xh = x * r
        dxh = dy * gf
        dn = r * (dxh - xh * (_seg_sum(dxh * xh, B) * (1.0 / ATTN_HEAD_DIM)))
        return isn * dn + (1.0 - isn) * dxh, _colsum(dy * xh)
    W = proj.shape[1]
    return _rowwise(name, fn, [(proj, 0), (dy, 0)], [(gain_full, 0), (is_norm, 0), (seg, None)],
                    [(W, BF16)], [(1, W)], tc=ATTN_GW, rb=SEG_ROWS)


def _attn_masks(r0=0, rows=ATTN_BLOCK):
    shape = (rows, ATTN_BLOCK)
    row = lax.broadcasted_iota(jnp.int32, shape, 0) + r0
    col = lax.broadcasted_iota(jnp.int32, shape, 1)
    return col <= row, col >= row, col < ATTN_HEAD_DIM


def _attn_group_fwd(name, qkv, d):
    S = qkv.shape[0]
    n, W, G = S // d, 3 * ATTN_GW, ATTN_GW
    nb = n // ATTN_BLOCK
    view = qkv.reshape(n, d * W)

    B, RQ = ATTN_BLOCK, ATTN_ROWS

    def body(cur, prev, o_ref, l_ref, s_scr, p_scr, lse_scr, inv_scr):
        b = pl.program_id(1)
        cur_mask, prev_mask, low = _attn_masks()
        prev_mask = jnp.logical_and(prev_mask, b > 0)
        for h in range(ATTN_HEADS):
            c0 = (h // 2) * LANES
            hm = low if h % 2 == 0 else jnp.logical_not(low)
            q2 = cur[:, c0:c0 + LANES]
            qm = jnp.where(hm, q2, jnp.zeros_like(q2))
            s_scr[h, :, 0:B] = jnp.where(cur_mask, _nt(qm, cur[:, G + c0:G + c0 + LANES]), -jnp.inf)
            s_scr[h, :, B:2 * B] = jnp.where(prev_mask, _nt(qm, prev[:, G + c0:G + c0 + LANES]), -jnp.inf)
        for h in range(ATTN_HEADS):
            for r0 in range(0, B, RQ):
                s = s_scr[h, r0:r0 + RQ, :]
                m = jnp.max(s, axis=1, keepdims=True)
                p = jnp.exp(s - m)
                l = jnp.sum(p, axis=1, keepdims=True)
                p_scr[h, r0:r0 + RQ, :] = p.astype(BF16)
                lse_scr[h, r0:r0 + RQ, :] = jnp.broadcast_to(m + jnp.log(l), (RQ, LANES))
                inv_scr[h, r0:r0 + RQ, :] = jnp.broadcast_to(1.0 / l, (RQ, LANES))
        for pr in range(G // LANES):
            c0 = pr * LANES
            vc, vp = cur[:, 2 * G + c0:2 * G + c0 + LANES], prev[:, 2 * G + c0:2 * G + c0 + LANES]
            o = [(_nn(p_scr[h, :, 0:B], vc) + _nn(p_scr[h, :, B:2 * B], vp)) * inv_scr[h] for h in (2 * pr, 2 * pr + 1)]
            o_ref[:, c0:c0 + LANES] = jnp.where(low, o[0], o[1])
            l_ref[:, c0:c0 + LANES] = jnp.where(low, lse_scr[2 * pr], lse_scr[2 * pr + 1])

    o, l = pl.pallas_call(
        body, name=name, grid=(d, nb),
        in_specs=[pl.BlockSpec((B, W), lambda r, b: (b, r)),
                  pl.BlockSpec((B, W), lambda r, b: (jnp.maximum(b - 1, 0), r))],
        out_specs=[pl.BlockSpec((B, G), lambda r, b: (b, r))] * 2,
        out_shape=[jax.ShapeDtypeStruct((n, d * G), F32)] * 2,
        scratch_shapes=[pltpu.VMEM((ATTN_HEADS, B, 2 * B), F32), pltpu.VMEM((ATTN_HEADS, B, 2 * B), BF16),
                        pltpu.VMEM((ATTN_HEADS, B, LANES), F32), pltpu.VMEM((ATTN_HEADS, B, LANES), F32)],
        compiler_params=_params("parallel", "parallel"),
    )(view, view)
    return o.reshape(S, G), l.reshape(S, G)


def _attn_combine(name, os, ls):
    def fn(o1, o2, o3, l1, l2, l3):
        m = jnp.maximum(jnp.maximum(l1, l2), l3)
        e1, e2, e3 = jnp.exp(l1 - m), jnp.exp(l2 - m), jnp.exp(l3 - m)
        den = e1 + e2 + e3
        return (e1 * o1 + e2 * o2 + e3 * o3) / den, m + jnp.log(den)
    G = os[0].shape[1]
    return _rowwise(name, fn, [(a, 0) for a in (*os, *ls)], [], [(G, F32), (G, F32)])


def _attn_delta(name, do, o, seg):
    def fn(do, o, B):
        return _seg_sum(do * o, B), do
    G = o.shape[1]
    return _rowwise(name, fn, [(do, 0), (o, 0)], [(seg, None)], [(G, F32), (G, BF16)], rb=SEG_ROWS)


def _attn_group_bwd(name, qkv, do, lse, delta, d):
    S = qkv.shape[0]
    n, W, G = S // d, 3 * ATTN_GW, ATTN_GW
    nb = n // ATTN_BLOCK

    B, RQ = ATTN_BLOCK, ATTN_ROWS
    S_A, DP_A, S_B, DP_B, S_C, DP_C = range(6)
    P_A, DS_A, P_B, DS_B, DS_C = range(5)

    def body(qp, qc, qn, do_c, do_n, l_c, l_n, dl_c, dl_n, out, f_scr, b_scr):
        j = pl.program_id(1)
        low = _attn_masks()[2]
        for h in range(ATTN_HEADS):
            c0 = (h // 2) * LANES
            hm = low if h % 2 == 0 else jnp.logical_not(low)
            k_c, v_c = qc[:, G + c0:G + c0 + LANES], qc[:, 2 * G + c0:2 * G + c0 + LANES]
            k_p, v_p = qp[:, G + c0:G + c0 + LANES], qp[:, 2 * G + c0:2 * G + c0 + LANES]
            zero = jnp.zeros((B, LANES), BF16)
            qmc, qmn = jnp.where(hm, qc[:, c0:c0 + LANES], zero), jnp.where(hm, qn[:, c0:c0 + LANES], zero)
            dmc = jnp.where(hm, do_c[:, c0:c0 + LANES].astype(BF16), zero)
            dmn = jnp.where(hm, do_n[:, c0:c0 + LANES].astype(BF16), zero)
            f_scr[h, S_A], f_scr[h, DP_A] = _nt(qmc, k_c), _nt(dmc, v_c)
            f_scr[h, S_B], f_scr[h, DP_B] = _nt(qmn, k_c), _nt(dmn, v_c)
            f_scr[h, S_C], f_scr[h, DP_C] = _nt(qmc, k_p), _nt(dmc, v_p)
        for h in range(ATTN_HEADS):
            h0 = h * ATTN_HEAD_DIM
            for r0 in range(0, B, RQ):
                rows = slice(r0, r0 + RQ)
                cur_mask, band, _ = _attn_masks(r0, RQ)
                next_mask, prev_mask = jnp.logical_and(band, j < nb - 1), jnp.logical_and(band, j > 0)
                lc, ln = l_c[rows, h0:h0 + 1], l_n[rows, h0:h0 + 1]
                dlc, dln = dl_c[rows, h0:h0 + 1], dl_n[rows, h0:h0 + 1]
                p_a = jnp.where(cur_mask, jnp.exp(f_scr[h, S_A, rows, :] - lc), 0.0)
                p_b = jnp.where(next_mask, jnp.exp(f_scr[h, S_B, rows, :] - ln), 0.0)
                p_c = jnp.where(prev_mask, jnp.exp(f_scr[h, S_C, rows, :] - lc), 0.0)
                b_scr[h, P_A, rows, :] = p_a.astype(BF16)
                b_scr[h, P_B, rows, :] = p_b.astype(BF16)
                b_scr[h, DS_A, rows, :] = (p_a * (f_scr[h, DP_A, rows, :] - dlc)).astype(BF16)
                b_scr[h, DS_B, rows, :] = (p_b * (f_scr[h, DP_B, rows, :] - dln)).astype(BF16)
                b_scr[h, DS_C, rows, :] = (p_c * (f_scr[h, DP_C, rows, :] - dlc)).astype(BF16)
        for pr in range(G // LANES):
            c0 = pr * LANES
            q_c, k_c, q_n = qc[:, c0:c0 + LANES], qc[:, G + c0:G + c0 + LANES], qn[:, c0:c0 + LANES]
            k_p = qp[:, G + c0:G + c0 + LANES]
            d_c, d_n = do_c[:, c0:c0 + LANES].astype(BF16), do_n[:, c0:c0 + LANES].astype(BF16)
            res = []
            for h in (2 * pr, 2 * pr + 1):
                dq = _nn(b_scr[h, DS_A], k_c) + _nn(b_scr[h, DS_C], k_p)
                dk = _tn(b_scr[h, DS_A], q_c) + _tn(b_scr[h, DS_B], q_n)
                dv = _tn(b_scr[h, P_A], d_c) + _tn(b_scr[h, P_B], d_n)
                res.append((dq, dk, dv))
            for t in range(3):
                out[:, t * G + c0:t * G + c0 + LANES] = jnp.where(low, res[0][t], res[1][t])

    prv = lambda r, j: (jnp.maximum(j - 1, 0), r)
    cur = lambda r, j: (j, r)
    nxt = lambda r, j: (jnp.minimum(j + 1, nb - 1), r)
    wide = lambda m: pl.BlockSpec((ATTN_BLOCK, W), m)
    narrow = lambda m: pl.BlockSpec((ATTN_BLOCK, G), m)
    qv, dv, lv, tv = qkv.reshape(n, d * W), do.reshape(n, d * G), lse.reshape(n, d * G), delta.reshape(n, d * G)
    out = pl.pallas_call(
        body, name=name, grid=(d, nb),
        in_specs=[wide(prv), wide(cur), wide(nxt), narrow(cur), narrow(nxt), narrow(cur), narrow(nxt),
                  narrow(cur), narrow(nxt)],
        out_specs=wide(cur), out_shape=jax.ShapeDtypeStruct((n, d * W), F32),
        scratch_shapes=[pltpu.VMEM((ATTN_HEADS, 6, B, B), F32), pltpu.VMEM((ATTN_HEADS, 5, B, B), BF16)],
        compiler_params=_params("parallel", "parallel"),
    )(qv, qv, qv, dv, dv, lv, lv, tv, tv)
    return out.reshape(S, W)


def _chunk_triangle(T, upper):
    i = jnp.arange(T)
    same = (i[:, None] // HGRN_CHUNK) == (i[None, :] // HGRN_CHUNK)
    tri = (i[None, :] >= i[:, None]) if upper else (i[None, :] <= i[:, None])
    return jnp.logical_and(same, tri).astype(F32)


def _hgrn_prologue(qr, fr, lbv, q_s, k_s, b_s, tri_ref, T):
    def pro(s, c):
        sl = pl.ds(pl.multiple_of(s * HGRN_CHUNK, HGRN_CHUNK), HGRN_CHUNK)
        sg = _sigmoid(fr[sl, :])
        qv = qr[sl, :]
        q_s[sl, :] = qv * _sigmoid(qv)
        k_s[sl, :] = (1.0 - lbv) * (1.0 - sg)
        b_s[sl, :] = jnp.log(lbv + (1.0 - lbv) * sg)
        return c
    lax.fori_loop(0, T // HGRN_CHUNK, pro, 0)
    b_s[...] = _nn(tri_ref[...], b_s[...], HI)


def _hgrn_scan_fwd(name, pq, pf, pv, lb):
    S, D = pq.shape
    T = _pick_rows(S, HGRN_TILE)
    NH, NT, C, HD, HB = D // HGRN_HEAD, S // T, HGRN_CHUNK, HGRN_HEAD, HGRN_GROUP
    W = HB * HD
    tri = _chunk_triangle(T, upper=False)

    def body(qr, fr, iv, lb_ref, tri_ref, o_ref, ck_ref, st_ref, q_s, k_s, b_s):
        @pl.when(pl.program_id(1) == 0)
        def _():
            st_ref[...] = jnp.zeros_like(st_ref)

        ck_ref[...] = st_ref[...]
        _hgrn_prologue(qr, fr, lb_ref[...], q_s, k_s, b_s, tri_ref, T)
        row = lax.broadcasted_iota(jnp.int32, (C, 1), 0)

        def chunk(c, carry):
            sl = pl.ds(pl.multiple_of(c * C, C), C)
            for hh in range(HB):
                cs = slice(hh * HD, (hh + 1) * HD)
                q, k, b, v = q_s[sl, cs], k_s[sl, cs], b_s[sl, cs], iv[sl, cs]
                b_last = b[C - 1:C, :]
                st = st_ref[cs, :]
                o = _nt((q * jnp.exp(b)).astype(BF16), st.astype(BF16))
                for s in range(C):
                    e = jnp.exp(jnp.minimum(b - b[s:s + 1, :], 0.0))
                    a = jnp.sum(q * e * k[s:s + 1, :], axis=1, keepdims=True)
                    o = o + jnp.where(row >= s, a, 0.0) * v[s:s + 1, :]
                o_ref[sl, cs] = o
                kd = k * jnp.exp(b_last - b)
                st_ref[cs, :] = st * jnp.exp(b_last) + _tn(v.astype(BF16), kd.astype(BF16))
            return carry

        lax.fori_loop(0, T // C, chunk, 0)

    NG = NH // HB
    col = pl.BlockSpec((T, W), lambda h, t: (t, h))
    return pl.pallas_call(
        body, name=name, grid=(NG, NT),
        in_specs=[col, col, col, pl.BlockSpec((1, W), lambda h, t: (0, h)), pl.BlockSpec((T, T), lambda h, t: (0, 0))],
        out_specs=[col, pl.BlockSpec((W, HD), lambda h, t: (t * NG + h, 0))],
        out_shape=[jax.ShapeDtypeStruct((S, D), F32), jax.ShapeDtypeStruct((NT * NH * HD, HD), F32)],
        scratch_shapes=[pltpu.VMEM((W, HD), F32)] + [pltpu.VMEM((T, W), F32)] * 3,
        compiler_params=_params("parallel", "arbitrary"),
    )(pq, pf, pv, lb, tri)


def _hgrn_scan_bwd(name, pq, pf, pv, lb, ckpt, do):
    S, D = pq.shape
    T = _pick_rows(S, HGRN_TILE)
    NH, NT, C, HD, HB = D // HGRN_HEAD, S // T, HGRN_CHUNK, HGRN_HEAD, HGRN_GROUP
    NC, W, NG = T // C, HB * HD, NH // HB
    tri, tri_up = _chunk_triangle(T, upper=False), _chunk_triangle(T, upper=True)

    def body(qr, fr, iv, do_ref, ck_ref, lb_ref, tri_ref, triu_ref, dq_ref, df_ref, dv_ref, dlb_ref,
             dst_ref, run, save, q_s, k_s, b_s, dq_s, dk_s, db_s):
        @pl.when(pl.program_id(1) == 0)
        def _():
            dst_ref[...] = jnp.zeros_like(dst_ref)
            dlb_ref[...] = jnp.zeros_like(dlb_ref)

        lbv = lb_ref[...]
        _hgrn_prologue(qr, fr, lbv, q_s, k_s, b_s, tri_ref, T)
        row = lax.broadcasted_iota(jnp.int32, (C, 1), 0)
        run[...] = ck_ref[...]

        def replay(c, carry):
            sl = pl.ds(pl.multiple_of(c * C, C), C)
            for hh in range(HB):
                cs = slice(hh * HD, (hh + 1) * HD)
                st = run[cs, :]
                save[pl.ds(pl.multiple_of((hh * NC + c) * HD, HD), HD), :] = st
                k, b, v = k_s[sl, cs], b_s[sl, cs], iv[sl, cs]
                b_last = b[C - 1:C, :]
                kd = k * jnp.exp(b_last - b)
                run[cs, :] = st * jnp.exp(b_last) + _tn(v.astype(BF16), kd.astype(BF16))
            return carry

        lax.fori_loop(0, NC, replay, 0)

        def chunk(ci, carry):
            c = NC - 1 - ci
            sl = pl.ds(pl.multiple_of(c * C, C), C)
            for hh in range(HB):
                cs = slice(hh * HD, (hh + 1) * HD)
                q, k, b, v, g = q_s[sl, cs], k_s[sl, cs], b_s[sl, cs], iv[sl, cs], do_ref[sl, cs]
                st0 = save[pl.ds(pl.multiple_of((hh * NC + c) * HD, HD), HD), :]
                dst1 = dst_ref[cs, :]
                b_last = b[C - 1:C, :]
                eb, ebl, ek = jnp.exp(b), jnp.exp(b_last), jnp.exp(b_last - b)
                dst1_b = dst1.astype(BF16)
                dq = _nn(g.astype(BF16), st0.astype(BF16)) * eb
                dv = _nt((k * ek).astype(BF16), dst1_b)
                dk = _nn(v.astype(BF16), dst1_b) * ek
                db_last = _colsum(dk * k) + _colsum(dst1 * st0) * ebl
                for s in range(C):
                    e = jnp.where(row >= s, jnp.exp(jnp.minimum(b - b[s:s + 1, :], 0.0)), 0.0)
                    ks, vs = k[s:s + 1, :], v[s:s + 1, :]
                    da = jnp.sum(g * vs, axis=1, keepdims=True)
                    a = jnp.sum(q * e * ks, axis=1, keepdims=True)
                    dq = dq + da * e * ks
                    dk = dk + jnp.where(row == s, _colsum(da * q * e), 0.0)
                    dv = dv + jnp.where(row == s, _colsum(a * g), 0.0)
                dq_s[sl, cs] = dq
                dk_s[sl, cs] = dk
                db_s[sl, cs] = q * dq - k * dk + jnp.where(row == C - 1, db_last, 0.0)
                dv_ref[sl, cs] = dv.astype(BF16)
                dst_ref[cs, :] = dst1 * ebl + _tn(g.astype(BF16), (q * eb).astype(BF16))
            return carry

        lax.fori_loop(0, NC, chunk, 0)
        db_s[...] = _nn(triu_ref[...], db_s[...], HI)

        def epi(s, carry):
            sl = pl.ds(pl.multiple_of(s * C, C), C)
            qv = qr[sl, :]
            sq = _sigmoid(qv)
            dq_ref[sl, :] = (dq_s[sl, :] * sq * (1.0 + qv * (1.0 - sq))).astype(BF16)
            sg = _sigmoid(fr[sl, :])
            common = db_s[sl, :] / (lbv + (1.0 - lbv) * sg) - dk_s[sl, :]
            df_ref[sl, :] = (common * (1.0 - lbv) * sg * (1.0 - sg)).astype(BF16)
            dlb_ref[...] += _colsum(common * (1.0 - sg))
            return carry

        lax.fori_loop(0, NC, epi, 0)

    col = pl.BlockSpec((T, W), lambda h, t: (NT - 1 - t, h))
    dq, df, dv, dlb = pl.pallas_call(
        body, name=name, grid=(NG, NT),
        in_specs=[col, col, col, col, pl.BlockSpec((W, HD), lambda h, t: ((NT - 1 - t) * NG + h, 0)),
                  pl.BlockSpec((1, W), lambda h, t: (0, h)),
                  pl.BlockSpec((T, T), lambda h, t: (0, 0)), pl.BlockSpec((T, T), lambda h, t: (0, 0))],
        out_specs=[col, col, col, pl.BlockSpec((1, W), lambda h, t: (0, h))],
        out_shape=[jax.ShapeDtypeStruct((S, D), BF16)] * 3 + [jax.ShapeDtypeStruct((1, D), F32)],
        scratch_shapes=[pltpu.VMEM((W, HD), F32)] * 2 + [pltpu.VMEM((HB * NC * HD, HD), F32)]
        + [pltpu.VMEM((T, W), F32)] * 6,
        compiler_params=_params("parallel", "arbitrary"),
    )(pq, pf, pv, do, ckpt, lb, tri, tri_up)
    return dq, df, dv, dlb


def _hgrn_out_fwd(name, o, gate, norm_g):
    def fn(o, g, ng):
        parts = []
        for h in range(o.shape[1] // HGRN_HEAD):
            c = slice(h * HGRN_HEAD, (h + 1) * HGRN_HEAD)
            oh, gh = o[:, c], g[:, c]
            r = lax.rsqrt(jnp.mean(oh * oh, axis=-1, keepdims=True) + EPS)
            parts.append(oh * r * ng[:, c] * (gh * _sigmoid(gh)))
        return jnp.concatenate(parts, axis=1)
    return _rowwise(name, fn, [(o, 0), (gate, 0)], [(norm_g, None)], [(o.shape[1], BF16)])


def _hgrn_out_bwd(name, o, gate, norm_g, dy):
    def fn(o, g, dy, ng):
        dos, dgs, dngs = [], [], []
        for h in range(o.shape[1] // HGRN_HEAD):
            c = slice(h * HGRN_HEAD, (h + 1) * HGRN_HEAD)
            oh, gh, dyh, ngh = o[:, c], g[:, c], dy[:, c], ng[:, c]
            r = lax.rsqrt(jnp.mean(oh * oh, axis=-1, keepdims=True) + EPS)
            xh = oh * r
            s = _sigmoid(gh)
            dn = dyh * (gh * s)
            dxh = dn * ngh
            dos.append(r * (dxh - xh * jnp.mean(dxh * xh, axis=-1, keepdims=True)))
            dgs.append(dyh * xh * ngh * (s * (1.0 + gh * (1.0 - s))))
            dngs.append(_colsum(dn * xh))
        return jnp.concatenate(dos, axis=1), jnp.concatenate(dgs, axis=1), jnp.concatenate(dngs, axis=1)
    D = o.shape[1]
    return _rowwise(name, fn, [(o, 0), (gate, 0), (dy, 0)], [(norm_g, None)], [(D, F32), (D, BF16)], [(1, D)])


def _lower_bound_fwd(name, logits, layer):
    n = logits.shape[0]

    def body(x_ref, o_ref):
        rows = [x_ref[i:i + 1, :] for i in range(n)]
        m = functools.reduce(jnp.maximum, rows)
        e = [jnp.exp(r - m) for r in rows]
        den = functools.reduce(jnp.add, e)
        o_ref[...] = functools.reduce(jnp.add, e[1:layer + 1]) / den

    return pl.pallas_call(body, name=name, out_shape=jax.ShapeDtypeStruct((1, logits.shape[1]), F32))(logits)


def _lower_bound_bwd(name, logits, dlb, layer):
    n = logits.shape[0]

    def body(x_ref, d_ref, o_ref):
        rows = [x_ref[i:i + 1, :] for i in range(n)]
        m = functools.reduce(jnp.maximum, rows)
        e = [jnp.exp(r - m) for r in rows]
        den = functools.reduce(jnp.add, e)
        s = [v / den for v in e]
        d = d_ref[...]
        inner = functools.reduce(jnp.add, s[1:layer + 1]) * d
        for i in range(n):
            o_ref[i:i + 1, :] = s[i] * ((d if 1 <= i <= layer else 0.0) - inner)

    return pl.pallas_call(body, name=name, out_shape=jax.ShapeDtypeStruct(logits.shape, F32))(logits, dlb)


def _row(v):
    return v.reshape(1, -1)


def _ffn_fwd(l, x1, w):
    h2 = _rmsnorm_fwd(f"ffn{l}_norm", x1, _row(w["ffn_norm"][l]))
    a, *u = _ffn_up_fused(f"ffn{l}_up", h2, w["ffn_w_up"][l], w["ffn_conv_w"][l], _row(w["ffn_conv_b"][l]))
    x2 = _matmul(f"ffn{l}_down", [(a, w["ffn_w_down"][l])], residual=x1)
    return x2, (x1, h2, u, a)


def _ffn_bwd(l, dx2, saved, w, grads):
    x1, h2, (u0g, u0u, ug, uu), a = saved
    w_up, w_down = w["ffn_w_up"][l], w["ffn_w_down"][l]
    F = w_down.shape[0]
    grads["ffn_w_down"][l] = _matmul_tn(f"ffn{l}_dwdown", a, dx2)
    dg, du, dcwg, dcwu, dcbg, dcbu = _ffn_gate_bwd_fused(
        f"ffn{l}_dgate", dx2, w_down, u0g, u0u, ug, uu, w["ffn_conv_w"][l])
    grads["ffn_conv_w"][l] = jnp.concatenate([dcwg, dcwu], axis=1)
    grads["ffn_conv_b"][l] = jnp.concatenate([dcbg, dcbu], axis=1)[0]
    grads["ffn_w_up"][l] = jnp.concatenate(
        [_matmul_tn(f"ffn{l}_dwup0", h2, dg), _matmul_tn(f"ffn{l}_dwup1", h2, du)], axis=1)
    dx1, dgain = _matmul_rmsnorm_bwd(f"ffn{l}_dh", [(dg, w_up[:, :F]), (du, w_up[:, F:])], x1,
                                     _row(w["ffn_norm"][l]), dx2)
    grads["ffn_norm"][l] = dgain[0]
    return dx1


def _attn_gain_rows(w, j, g):
    scale = ATTN_HEAD_DIM ** -0.5
    qg = jnp.tile(w["attn_q_gain"][j, g] * scale, ATTN_HEADS)
    kg = jnp.tile(w["attn_k_gain"][j, g], ATTN_HEADS)
    gain = jnp.concatenate([qg, kg, jnp.ones((ATTN_GW,), F32)])
    is_norm = jnp.concatenate([jnp.ones((2 * ATTN_GW,), F32), jnp.zeros((ATTN_GW,), F32)])
    return _row(gain), _row(is_norm)


def _attn_fwd(l, j, x, w):
    h = _rmsnorm_fwd(f"mix{l}_norm", x, _row(w["mixer_norm"][l]))
    w_in = w["attn_w_in"][j]
    seg = _segment_matrix(LANES, ATTN_HEAD_DIM)
    GW3 = 3 * ATTN_GW
    proj, qkv, os, ls = [], [], [], []
    for g, d in enumerate(ATTN_DILATIONS):
        gain, is_norm = _attn_gain_rows(w, j, g)
        proj.append(_matmul(f"attn{l}_in{g}", [(h, w_in[:, g * GW3:(g + 1) * GW3])]))
        qkv.append(_qknorm_fwd(f"attn{l}_qknorm{g}", proj[g], gain, is_norm, seg))
        o, lse = _attn_group_fwd(f"attn{l}_core{g}", qkv[g], d)
        os.append(o)
        ls.append(lse)
    o, lse = _attn_combine(f"attn{l}_combine", os, ls)
    x1 = _matmul(f"attn{l}_out", [(o, w["attn_w_out"][j])], residual=x)
    return x1, (x, h, proj, qkv, o, lse)


def _attn_bwd(l, j, dx1, saved, w, grads):
    x, h, proj, qkv, o, lse = saved
    w_in, w_out = w["attn_w_in"][j], w["attn_w_out"][j]
    seg = _segment_matrix(LANES, ATTN_HEAD_DIM)
    GW3 = 3 * ATTN_GW
    grads["attn_w_out"][j] = _matmul_tn(f"attn{l}_dwout", o, dx1)
    do = _matmul(f"attn{l}_do", [(dx1, w_out)], trans_b=True)
    delta, do = _attn_delta(f"attn{l}_delta", do, o, seg)
    dproj, dwin, dqg, dkg = [], [], [], []
    for g, d in enumerate(ATTN_DILATIONS):
        gain, is_norm = _attn_gain_rows(w, j, g)
        dqkv = _attn_group_bwd(f"attn{l}_dcore{g}", qkv[g], do, lse, delta, d)
        dp, dgain = _qknorm_bwd(f"attn{l}_dqknorm{g}", proj[g], dqkv, gain, is_norm, seg)
        dproj.append(dp)
        dwin.append(_matmul_tn(f"attn{l}_dwin{g}", h, dp))
        per_head = dgain.reshape(3, ATTN_HEADS, ATTN_HEAD_DIM).sum(axis=1)
        dqg.append(per_head[0] * ATTN_HEAD_DIM ** -0.5)
        dkg.append(per_head[1])
    grads["attn_w_in"][j] = jnp.concatenate(dwin, axis=1)
    grads["attn_q_gain"][j] = jnp.stack(dqg)
    grads["attn_k_gain"][j] = jnp.stack(dkg)
    dx, dg = _matmul_rmsnorm_bwd(f"attn{l}_dh", [(dproj[g], w_in[:, g * GW3:(g + 1) * GW3]) for g in range(3)],
                                 x, _row(w["mixer_norm"][l]), dx1)
    grads["mixer_norm"][l] = dg[0]
    return dx


def _conv_fwd(l, j, x, w):
    h = _rmsnorm_fwd(f"mix{l}_norm", x, _row(w["mixer_norm"][l]))
    w_in, b_in = w["conv_w_in"][j], _row(w["conv_b_in"][j])
    C = w_in.shape[1] // 2
    ua = _matmul(f"conv{l}_in0", [(h, w_in[:, :C])], bias=b_in[:, :C])
    ug = _matmul(f"conv{l}_in1", [(h, w_in[:, C:])], bias=b_in[:, C:])
    glu = _glu_fwd(f"conv{l}_glu", ua, ug)
    c = _dwconv(f"conv{l}_dw", glu, w["conv_dw_w"][j], _row(w["conv_dw_b"][j]), reverse=False)
    sw = _ln_silu_fwd(f"conv{l}_ln", c, _row(w["conv_ln_g"][j]), _row(w["conv_ln_b"][j]))
    x1 = _matmul(f"conv{l}_out", [(sw, w["conv_w_out"][j])], bias=_row(w["conv_b_out"][j]), residual=x)
    return x1, (x, h, ua, ug, glu, c, sw)


def _conv_bwd(l, j, dx1, saved, w, grads):
    x, h, ua, ug, glu, c, sw = saved
    w_in, w_out, dw_w = w["conv_w_in"][j], w["conv_w_out"][j], w["conv_dw_w"][j]
    C = w_out.shape[0]
    grads["conv_b_out"][j] = _column_sums(f"conv{l}_dbout", dx1)[0]
    grads["conv_w_out"][j] = _matmul_tn(f"conv{l}_dwout", sw, dx1)
    dsw = _matmul(f"conv{l}_dsw", [(dx1, w_out)], trans_b=True)
    dc, dlg, dlb = _ln_silu_bwd(f"conv{l}_dln", c, _row(w["conv_ln_g"][j]), _row(w["conv_ln_b"][j]), dsw)
    grads["conv_ln_g"][j], grads["conv_ln_b"][j] = dlg[0], dlb[0]
    dglu = _dwconv(f"conv{l}_ddw", dc, dw_w, jnp.zeros((1, C), F32), reverse=True)
    gw, gb = _dwconv_wgrad(f"conv{l}_ddww", glu, dc, dw_w.shape[0])
    grads["conv_dw_w"][j], grads["conv_dw_b"][j] = gw, gb[0]
    da, dgate, sa, sg = _glu_bwd(f"conv{l}_dglu", ua, ug, dglu)
    grads["conv_b_in"][j] = jnp.concatenate([sa, sg], axis=1)[0]
    grads["conv_w_in"][j] = jnp.concatenate(
        [_matmul_tn(f"conv{l}_dwin0", h, da), _matmul_tn(f"conv{l}_dwin1", h, dgate)], axis=1)
    dx, dg = _matmul_rmsnorm_bwd(f"conv{l}_dh", [(da, w_in[:, :C]), (dgate, w_in[:, C:])], x,
                                 _row(w["mixer_norm"][l]), dx1)
    grads["mixer_norm"][l] = dg[0]
    return dx


def _hgrn_fwd(l, j, x, w):
    h = _rmsnorm_fwd(f"mix{l}_norm", x, _row(w["mixer_norm"][l]))
    w_in = w["hgrn_w_in"][j]
    D = w_in.shape[1] // 4
    pq, pf, pv, pg = [_matmul(f"hgrn{l}_in{s}", [(h, w_in[:, s * D:(s + 1) * D])]) for s in range(4)]
    lb = _lower_bound_fwd(f"hgrn{l}_lb", w["hgrn_lb_logits"], l)
    o, ckpt = _hgrn_scan_fwd(f"hgrn{l}_scan", pq, pf, pv, lb)
    y = _hgrn_out_fwd(f"hgrn{l}_gate", o, pg, _row(w["hgrn_norm_g"][j]))
    x1 = _matmul(f"hgrn{l}_out", [(y, w["hgrn_w_out"][j])], residual=x)
    return x1, (x, h, pq, pf, pv, pg, lb, o, ckpt, y)


def _hgrn_bwd(l, j, dx1, saved, w, grads):
    x, h, pq, pf, pv, pg, lb, o, ckpt, y = saved
    w_in, w_out = w["hgrn_w_in"][j], w["hgrn_w_out"][j]
    D = w_out.shape[0]
    grads["hgrn_w_out"][j] = _matmul_tn(f"hgrn{l}_dwout", y, dx1)
    dy = _matmul(f"hgrn{l}_dy", [(dx1, w_out)], trans_b=True)
    do, dpg, dng = _hgrn_out_bwd(f"hgrn{l}_dgate", o, pg, _row(w["hgrn_norm_g"][j]), dy)
    grads["hgrn_norm_g"][j] = dng[0]
    dpq, dpf, dpv, dlb = _hgrn_scan_bwd(f"hgrn{l}_dscan", pq, pf, pv, lb, ckpt, do)
    grads["hgrn_lb_logits"] = grads["hgrn_lb_logits"] + _lower_bound_bwd(f"hgrn{l}_dlb", w["hgrn_lb_logits"], dlb, l)
    dps = [dpq, dpf, dpv, dpg]
    grads["hgrn_w_in"][j] = jnp.concatenate([_matmul_tn(f"hgrn{l}_dwin{s}", h, dps[s]) for s in range(4)], axis=1)
    dx, dg = _matmul_rmsnorm_bwd(f"hgrn{l}_dh", [(dps[s], w_in[:, s * D:(s + 1) * D]) for s in range(4)], x,
                                 _row(w["mixer_norm"][l]), dx1)
    grads["mixer_norm"][l] = dg[0]
    return dx


_MIXERS = ((_attn_fwd, _attn_bwd), (_conv_fwd, _conv_bwd), (_hgrn_fwd, _hgrn_bwd))
_PER_MIXER = {"attn": 0, "conv": 1, "hgrn": 2}


def _local_step(x, target, w):
    depth = w["mixer_norm"].shape[0]
    grads = {}
    for name, v in w.items():
        lead = v.shape[0]
        grads[name] = jnp.zeros(v.shape, F32) if name == "hgrn_lb_logits" else [None] * lead
    saved = []
    for l in range(depth):
        fwd, _ = _MIXERS[l % N_MIXERS]
        x, s_mix = fwd(l, l // N_MIXERS, x, w)
        x, s_ffn = _ffn_fwd(l, x, w)
        saved.append((s_mix, s_ffn))
    dx, loss_cols = _loss_grad("loss", x, target)
    for l in reversed(range(depth)):
        _, bwd = _MIXERS[l % N_MIXERS]
        s_mix, s_ffn = saved[l]
        dx = _ffn_bwd(l, dx, s_ffn, w, grads)
        dx = bwd(l, l // N_MIXERS, dx, s_mix, w, grads)
    grads = {k: (v if k == "hgrn_lb_logits" else jnp.stack(v)) for k, v in grads.items()}
    return jnp.sum(loss_cols), dx, grads


_HBM = pl.BlockSpec(memory_space=pltpu.HBM)


def _chip_peers():
    x, y, c = lax.axis_index("x"), lax.axis_index("y"), lax.axis_index("c")
    return 2 * x + y, (x, y, c), [(1 - x, y), (x, 1 - y), (1 - x, 1 - y)]


def _exchange_chips(name, src):
    def body(src_ref, out_ref, send_sems, recv_sems, local_sem):
        p, (x, y, c), peers = _chip_peers()
        mine = pltpu.make_async_copy(src_ref.at[p], out_ref.at[p], local_sem)
        mine.start()

        def copy(k, slab_from, slab_to, peer):
            return pltpu.make_async_remote_copy(
                src_ref=src_ref.at[slab_from], dst_ref=out_ref.at[slab_to], send_sem=send_sems.at[k],
                recv_sem=recv_sems.at[k], device_id=(peer[0], peer[1], c), device_id_type=MESH)

        sends = [copy(k, 2 * px + py, p, (px, py)) for k, (px, py) in enumerate(peers)]
        for s in sends:
            s.start()
        for k, (px, py) in enumerate(peers):
            copy(k, p, 2 * px + py, (px, py)).wait_recv()
        for s in sends:
            s.wait_send()
        mine.wait()

    return pl.pallas_call(
        body, name=name, in_specs=[_HBM], out_specs=_HBM, out_shape=jax.ShapeDtypeStruct(src.shape, src.dtype),
        scratch_shapes=[pltpu.SemaphoreType.DMA((3,)), pltpu.SemaphoreType.DMA((3,)), pltpu.SemaphoreType.DMA],
    )(src)


def _all_gather_chips(name, shard):
    R = shard.shape[0]
    half = R // 2

    def body(src_ref, out_ref, send_sems, recv_sems, local_sem):
        p, (x, y, c), peers = _chip_peers()
        mine = pltpu.make_async_copy(src_ref, out_ref.at[p], local_sem)
        mine.start()

        def rows(slab, core):
            return out_ref.at[slab, pl.ds(core * half, half), :]

        def over_ici(k, slab, peer):
            src = src_ref.at[pl.ds(c * half, half), :] if slab is None else rows(slab, c)
            return pltpu.make_async_remote_copy(
                src_ref=src, dst_ref=rows(p if slab is None else slab, c), send_sem=send_sems.at[k],
                recv_sem=recv_sems.at[k], device_id=(peer[0], peer[1], c), device_id_type=MESH)

        def to_sibling(k, slab, core):
            return pltpu.make_async_remote_copy(
                src_ref=rows(slab, core), dst_ref=rows(slab, core), send_sem=send_sems.at[3 + k],
                recv_sem=recv_sems.at[3 + k], device_id=(x, y, 1 - c), device_id_type=MESH)

        sends = [over_ici(k, None, peer) for k, peer in enumerate(peers)]
        for s in sends:
            s.start()
        passed = []
        for k, (px, py) in enumerate(peers):
            over_ici(k, 2 * px + py, (px, py)).wait_recv()
            passed.append(to_sibling(k, 2 * px + py, c))
            passed[k].start()
        for k, (px, py) in enumerate(peers):
            to_sibling(k, 2 * px + py, 1 - c).wait_recv()
        for s in sends + passed:
            s.wait_send()
        mine.wait()

    return pl.pallas_call(
        body, name=name, in_specs=[_HBM], out_specs=_HBM,
        out_shape=jax.ShapeDtypeStruct((N_CHIPS,) + shard.shape, shard.dtype),
        scratch_shapes=[pltpu.SemaphoreType.DMA((6,)), pltpu.SemaphoreType.DMA((6,)), pltpu.SemaphoreType.DMA],
    )(shard)


def _swap_cores(name, v):
    def body(v_ref, out_ref, send_sem, recv_sem):
        x, y, c = lax.axis_index("x"), lax.axis_index("y"), lax.axis_index("c")
        cp = pltpu.make_async_remote_copy(src_ref=v_ref, dst_ref=out_ref, send_sem=send_sem, recv_sem=recv_sem,
                                          device_id=(x, y, 1 - c), device_id_type=MESH)
        cp.start()
        cp.wait()

    return pl.pallas_call(
        body, name=name, in_specs=[_HBM], out_specs=_HBM, out_shape=jax.ShapeDtypeStruct(v.shape, v.dtype),
        scratch_shapes=[pltpu.SemaphoreType.DMA, pltpu.SemaphoreType.DMA],
    )(v)


_WEIGHTS = ("mixer_norm", "ffn_norm", "attn_w_in", "attn_q_gain", "attn_k_gain", "attn_w_out", "conv_w_in",
            "conv_b_in", "conv_dw_w", "conv_dw_b", "conv_ln_g", "conv_ln_b", "conv_w_out", "conv_b_out",
            "hgrn_w_in", "hgrn_lb_logits", "hgrn_norm_g", "hgrn_w_out", "ffn_w_up", "ffn_conv_w", "ffn_conv_b",
            "ffn_w_down")
_SHARD_AXIS = {"attn_w_in": 2, "attn_w_out": 2, "conv_w_in": 2, "conv_dw_w": 2, "conv_w_out": 1, "hgrn_w_in": 2,
               "hgrn_norm_g": 1, "hgrn_w_out": 1, "ffn_w_up": 2, "ffn_conv_w": 2, "ffn_w_down": 1}
_MATMUL_WEIGHTS = ("attn_w_in", "attn_w_out", "conv_w_in", "conv_w_out", "hgrn_w_in", "hgrn_w_out", "ffn_w_up",
                   "ffn_w_down")
PACK_COLS = 1024
PACK_ROWS = 512


def _pack(arrays, nlead, dtype):
    lead = arrays[0].shape[:nlead]
    flat = []
    for a in arrays:
        f = a.reshape(lead + (-1,)).astype(dtype)
        flat.append(jnp.pad(f, [(0, 0)] * nlead + [(0, (-f.shape[-1]) % PACK_COLS)]))
    buf = jnp.concatenate(flat, axis=-1)
    buf = jnp.pad(buf, [(0, 0)] * nlead + [(0, (-buf.shape[-1]) % (PACK_COLS * PACK_ROWS))])
    return buf.reshape(lead + (-1, PACK_COLS))


def _unpack(buf, shapes, nlead):
    lead = buf.shape[:nlead]
    flat = buf.reshape(lead + (-1,))
    out, off = [], 0
    for shape in shapes:
        n = 1
        for s in shape:
            n *= s
        out.append(flat[..., off:off + n].reshape(lead + tuple(shape)))
        off += n + (-n) % PACK_COLS
    return out


def _merge_shards(piece, axis):
    moved = jnp.moveaxis(piece, 0, axis)
    shape = moved.shape
    return moved.reshape(shape[:axis] + (shape[axis] * shape[axis + 1],) + shape[axis + 2:])


def _split_shards(full, axis):
    shape = full.shape
    cut = full.reshape(shape[:axis] + (N_CHIPS, shape[axis] // N_CHIPS) + shape[axis + 1:])
    return jnp.moveaxis(cut, axis, 0)


def _gather_weights(local):
    big = [n for n in _WEIGHTS if n in _MATMUL_WEIGHTS]
    small = [n for n in _WEIGHTS if n in _SHARD_AXIS and n not in _MATMUL_WEIGHTS]
    full = {n: local[n] for n in _WEIGHTS if n not in _SHARD_AXIS}
    for names, dtype, tag in ((big, BF16, "comm_gather_matmul_weights"), (small, F32, "comm_gather_small_weights")):
        gathered = _all_gather_chips(tag, _pack([local[n] for n in names], 0, dtype))
        pieces = _unpack(gathered, [local[n].shape for n in names], 1)
        for n, piece in zip(names, pieces):
            full[n] = _merge_shards(piece, _SHARD_AXIS[n])
    return full


def _reduce_gradients(grads, local):
    out = {}
    big = [n for n in _WEIGHTS if n in _MATMUL_WEIGHTS]
    rest = [n for n in _WEIGHTS if n not in _MATMUL_WEIGHTS]
    for names, dtype, tag in ((big, BF16, "matmul"), (rest, F32, "small")):
        slabs = []
        for n in names:
            g = grads[n]
            if n in _SHARD_AXIS:
                slabs.append(_split_shards(g, _SHARD_AXIS[n]))
            else:
                slabs.append(jnp.broadcast_to(g[None], (N_CHIPS,) + g.shape))
        packed = _pack(slabs, 1, dtype)
        landed = _exchange_chips(f"comm_scatter_{tag}_gradients", packed)
        partial = _sum_slabs(f"sum_chips_{tag}", landed)
        other = _swap_cores(f"comm_swap_{tag}_sums", partial)
        total = _add(f"sum_cores_{tag}", [partial, other])
        out.update(zip(names, _unpack(total, [local[n].shape for n in names], 0)))
    return out


def kernel(x, mixer_norm, ffn_norm, attn_w_in, attn_q_gain, attn_k_gain, attn_w_out, conv_w_in, conv_b_in, conv_dw_w, conv_dw_b, conv_ln_g, conv_ln_b, conv_w_out, conv_b_out, hgrn_w_in, hgrn_lb_logits, hgrn_norm_g, hgrn_w_out, ffn_w_up, ffn_conv_w, ffn_conv_b, ffn_w_down, loss_target, m_mixer_norm, m_ffn_norm, m_attn_w_in, m_attn_q_gain, m_attn_k_gain, m_attn_w_out, m_conv_w_in, m_conv_b_in, m_conv_dw_w, m_conv_dw_b, m_conv_ln_g, m_conv_ln_b, m_conv_w_out, m_conv_b_out, m_hgrn_w_in, m_hgrn_lb_logits, m_hgrn_norm_g, m_hgrn_w_out, m_ffn_w_up, m_ffn_conv_w, m_ffn_conv_b, m_ffn_w_down, v_mixer_norm, v_ffn_norm, v_attn_w_in, v_attn_q_gain, v_attn_k_gain, v_attn_w_out, v_conv_w_in, v_conv_b_in, v_conv_dw_w, v_conv_dw_b, v_conv_ln_g, v_conv_ln_b, v_conv_w_out, v_conv_b_out, v_hgrn_w_in, v_hgrn_lb_logits, v_hgrn_norm_g, v_hgrn_w_out, v_ffn_w_up, v_ffn_conv_w, v_ffn_conv_b, v_ffn_w_down):
    given = dict(locals())
    local = {n: given[n] for n in _WEIGHTS}
    full = _gather_weights(local)
    loss, dx, grads = _local_step(x[0], loss_target[0], full)
    loss = lax.psum(loss, ("x", "y", "c"))
    grad = _reduce_gradients(grads, local)
    delta, new_m, new_v = {}, {}, {}
    for n in _WEIGHTS:
        shape = local[n].shape
        as2d = lambda a: a.reshape(-1, shape[-1])
        d, m, v = _adamw(f"adamw_{n}", as2d(local[n]), as2d(grad[n]), as2d(given["m_" + n]), as2d(given["v_" + n]))
        delta[n], new_m[n], new_v[n] = d.reshape(shape), m.reshape(shape), v.reshape(shape)
    return (loss, dx[None], *[grad[n] for n in _WEIGHTS], *[delta[n] for n in _WEIGHTS],
            *[new_m[n] for n in _WEIGHTS], *[new_v[n] for n in _WEIGHTS])
```

```python
import functools

import jax
import jax.numpy as jnp
from jax import lax
from jax.experimental import pallas as pl
from jax.experimental.pallas import tpu as pltpu

F32 = jnp.float32
BF16 = jnp.bfloat16

EPS = 1e-6
N_MIXERS = 3
ATTN_DILATIONS = (1, 4, 16)
ATTN_BLOCK = 128
ATTN_HEADS = 8
ATTN_HEAD_DIM = 64
ATTN_GW = ATTN_HEADS * ATTN_HEAD_DIM
ATTN_ROWS = 32
HGRN_HEAD = 128
HGRN_CHUNK = 16
HGRN_TILE = 256
HGRN_GROUP = 4
ADAM_LR, ADAM_B1, ADAM_B2, ADAM_EPS, ADAM_WD, ADAM_STEP = 0.001, 0.9, 0.999, 1e-08, 0.01, 10

LANES = 128
SUBLANES = 8
VMEM_LIMIT = 56 * 1024 * 1024
N_CHIPS = 4
MESH = pl.DeviceIdType.MESH

HI = lax.Precision.HIGHEST


def _params(*sem):
    return pltpu.CompilerParams(dimension_semantics=sem, vmem_limit_bytes=VMEM_LIMIT)


def _pick(n, target):
    if n <= target:
        return n
    best = None
    for t in range(LANES, target + 1, LANES):
        if n % t == 0:
            best = t
    assert best is not None, (n, target)
    return best


def _pick_rows(n, target):
    if n <= target:
        return n
    for t in range(target, 15, -16):
        if n % t == 0:
            return t
    return n


def _dot(a, b, dims, precision=None):
    return lax.dot_general(a, b, (dims, ((), ())), precision=precision, preferred_element_type=F32)


def _nn(a, b, precision=None):
    return _dot(a, b, ((1,), (0,)), precision)


def _nt(a, b, precision=None):
    return _dot(a, b, ((1,), (1,)), precision)


def _tn(a, b, precision=None):
    return _dot(a, b, ((0,), (0,)), precision)


def _sigmoid(x):
    return 1.0 / (1.0 + jnp.exp(-x))


ROWWISE_UNROLL_ROWS = 64


def _rowwise(name, fn, rows, pars=(), outs=(), accs=(), *, tc=None, tm=512, rb=16):
    S = rows[0][0].shape[0]
    tm = _pick_rows(S, tm)
    rb = rb if tm % rb == 0 else tm
    width = tc if tc is not None else None
    ncol = 1
    if tc is not None:
        base = outs[0][0] if outs else accs[0][1]
        ncol = base // tc
    n_r, n_p, n_o, n_a = len(rows), len(pars), len(outs), len(accs)

    def body(*refs):
        row_refs, par_refs = refs[:n_r], refs[n_r:n_r + n_p]
        out_refs, acc_refs = refs[n_r + n_p:n_r + n_p + n_o], refs[n_r + n_p + n_o:]
        if n_a:
            @pl.when(pl.program_id(1) == 0)
            def _():
                for a in acc_refs:
                    a[...] = jnp.zeros_like(a)

        def step(s, carry):
            sl = pl.ds(pl.multiple_of(s * rb, rb), rb)
            res = fn(*[r[sl, :] for r in row_refs], *[p[...] for p in par_refs])
            res = res if isinstance(res, tuple) else (res,)
            for o, v in zip(out_refs, res[:n_o]):
                o[sl, :] = v.astype(o.dtype)
            for a, v in zip(acc_refs, res[n_o:]):
                a[...] += v
            return carry

        lax.fori_loop(0, tm // rb, step, 0, unroll=min(tm // rb, max(2, ROWWISE_UNROLL_ROWS // rb)))

    def row_spec(c, off):
        if tc is None:
            return pl.BlockSpec((tm, c), lambda j, i: (i, 0))
        return pl.BlockSpec((tm, tc), lambda j, i, o=off // tc: (i, j + o))

    def par_spec(shape, off):
        if off is None or tc is None:
            return pl.BlockSpec(shape, lambda j, i: (0, 0))
        return pl.BlockSpec((shape[0], tc), lambda j, i, o=off // tc: (0, j + o))

    in_specs = [row_spec(a.shape[1], off) for a, off in rows]
    in_specs += [par_spec(a.shape, off) for a, off in pars]
    out_specs = [row_spec(c, 0) for c, _ in outs] + [par_spec(s, 0) for s in accs]
    out_shape = [jax.ShapeDtypeStruct((S, c), d) for c, d in outs]
    out_shape += [jax.ShapeDtypeStruct(s, F32) for s in accs]
    res = pl.pallas_call(
        body, name=name, grid=(ncol, S // tm), in_specs=in_specs, out_specs=out_specs, out_shape=out_shape,
        compiler_params=_params("parallel", "arbitrary" if n_a else "parallel"),
    )(*[a for a, _ in rows], *[a for a, _ in pars])
    return res[0] if len(res) == 1 else tuple(res)


MATMUL_VMEM = 36 * 1024 * 1024


def _matmul_tiles(M, N, pairs, out_dtype, residual):
    tm = _pick_rows(M, 512)
    for tn in sorted({_pick(N, t) for t in range(LANES, 2049, LANES)}, reverse=True):
        step = sum(tm * a.shape[1] * a.dtype.itemsize + a.shape[1] * tn * b.dtype.itemsize for a, b in pairs)
        step += tm * tn * (jnp.dtype(out_dtype).itemsize + (4 if residual is not None else 0))
        if 2 * step <= MATMUL_VMEM:
            return tm, tn
    return tm, LANES


def _matmul(name, pairs, *, trans_b=False, bias=None, residual=None, out_dtype=F32):
    M = pairs[0][0].shape[0]
    N = pairs[0][1].shape[0] if trans_b else pairs[0][1].shape[1]
    tm, tn = _matmul_tiles(M, N, pairs, out_dtype, residual)
    n = len(pairs)

    def body(*refs):
        acc = None
        for i in range(n):
            a = refs[2 * i][...].astype(BF16)
            b = refs[2 * i + 1][...].astype(BF16)
            d = _nt(a, b) if trans_b else _nn(a, b)
            acc = d if acc is None else acc + d
        k = 2 * n
        if bias is not None:
            acc = acc + refs[k][...]
            k += 1
        if residual is not None:
            acc = acc + refs[k][...]
            k += 1
        refs[k][...] = acc.astype(out_dtype)

    in_specs, args = [], []
    for a, b in pairs:
        K = a.shape[1]
        in_specs.append(pl.BlockSpec((tm, K), lambda j, i: (i, 0)))
        in_specs.append(pl.BlockSpec((tn, K), lambda j, i: (j, 0)) if trans_b
                        else pl.BlockSpec((K, tn), lambda j, i: (0, j)))
        args += [a, b]
    if bias is not None:
        in_specs.append(pl.BlockSpec((1, tn), lambda j, i: (0, j)))
        args.append(bias)
    if residual is not None:
        in_specs.append(pl.BlockSpec((tm, tn), lambda j, i: (i, j)))
        args.append(residual)
    return pl.pallas_call(
        body, name=name, grid=(N // tn, M // tm), in_specs=in_specs,
        out_specs=pl.BlockSpec((tm, tn), lambda j, i: (i, j)),
        out_shape=jax.ShapeDtypeStruct((M, N), out_dtype), compiler_params=_params("parallel", "parallel"),
    )(*args)


def _matmul_tn(name, a, b, *, tm=1408, tn=1408, tk=1024):
    S, M = a.shape
    N = b.shape[1]
    tm, tn, tk = _pick(M, tm), _pick(N, tn), _pick_rows(S, tk)

    def body(a_ref, b_ref, o_ref):
        @pl.when(pl.program_id(2) == 0)
        def _():
            o_ref[...] = jnp.zeros_like(o_ref)

        o_ref[...] += _tn(a_ref[...].astype(BF16), b_ref[...].astype(BF16))

    return pl.pallas_call(
        body, name=name, grid=(M // tm, N // tn, S // tk),
        in_specs=[pl.BlockSpec((tk, tm), lambda i, j, k: (k, i)), pl.BlockSpec((tk, tn), lambda i, j, k: (k, j))],
        out_specs=pl.BlockSpec((tm, tn), lambda i, j, k: (i, j)),
        out_shape=jax.ShapeDtypeStruct((M, N), F32), compiler_params=_params("parallel", "parallel", "arbitrary"),
    )(a, b)


def _halo_rows(K):
    return 8 if K <= 9 else 32


def _shifted_copies(ext, shifted, K):
    if K <= SUBLANES:
        return
    n = shifted.shape[1]
    for s in range(1, SUBLANES):
        shifted[s, 0:n, :] = ext[s:s + n, :]


def _window(ext, shifted, K, off, rows):
    s = off % SUBLANES
    if K <= SUBLANES or s == 0:
        return ext[off:off + rows, :]
    return shifted[s, off - s:off - s + rows, :]


def _dwconv(name, x, w, b, *, reverse, out_dtype=F32):
    S, C = x.shape
    K = w.shape[0]
    H = _halo_rows(K)
    tm, tc = _pick_rows(S, 512 if K <= 4 else 256), _pick(C, 1408 if K <= 4 else 256)
    nrow = S // tm
    RB = 16 if out_dtype == BF16 else 8

    def body(x_ref, h_ref, w_ref, b_ref, o_ref, ext, shifted):
        i = pl.program_id(1)
        edge = (i == nrow - 1) if reverse else (i == 0)
        halo = jnp.where(edge, 0.0, h_ref[...].astype(F32))
        if reverse:
            ext[0:tm, :] = x_ref[...].astype(F32)
            ext[tm:tm + H, :] = halo
        else:
            ext[0:H, :] = halo
            ext[H:H + tm, :] = x_ref[...].astype(F32)
        _shifted_copies(ext, shifted, K)
        wv = w_ref[...]
        for s in range(tm // RB):
            acc = jnp.broadcast_to(b_ref[...], (RB, tc))
            for k in range(K):
                off = s * RB + ((K - 1 - k) if reverse else (H - (K - 1) + k))
                acc = acc + wv[k:k + 1, :] * _window(ext, shifted, K, off, RB)
            o_ref[s * RB:(s + 1) * RB, :] = acc.astype(out_dtype)

    r = tm // H
    if reverse:
        halo_map = lambda j, i: (jnp.minimum((i + 1) * r, S // H - 1), j)
    else:
        halo_map = lambda j, i: (jnp.maximum(i * r - 1, 0), j)
    return pl.pallas_call(
        body, name=name, grid=(C // tc, nrow),
        in_specs=[pl.BlockSpec((tm, tc), lambda j, i: (i, j)), pl.BlockSpec((H, tc), halo_map),
                  pl.BlockSpec((K, tc), lambda j, i: (0, j)), pl.BlockSpec((1, tc), lambda j, i: (0, j))],
        out_specs=pl.BlockSpec((tm, tc), lambda j, i: (i, j)),
        out_shape=jax.ShapeDtypeStruct((S, C), out_dtype),
        scratch_shapes=[pltpu.VMEM((tm + H, tc), F32), pltpu.VMEM((SUBLANES, tm + H - SUBLANES, tc), F32)],
        compiler_params=_params("parallel", "parallel"),
    )(x, x, w, b)


def _dwconv_wgrad(name, x, dy, K):
    S, C = x.shape
    H = _halo_rows(K)
    tm, tc = _pick_rows(S, 512 if K <= 4 else 256), _pick(C, 512 if K <= 4 else LANES)
    RB = 8

    def body(x_ref, h_ref, dy_ref, dw_ref, db_ref, ext, shifted):
        i = pl.program_id(1)

        @pl.when(i == 0)
        def _():
            dw_ref[...] = jnp.zeros_like(dw_ref)
            db_ref[...] = jnp.zeros_like(db_ref)

        ext[0:H, :] = jnp.where(i == 0, 0.0, h_ref[...].astype(F32))
        ext[H:H + tm, :] = x_ref[...].astype(F32)
        _shifted_copies(ext, shifted, K)
        acc = [jnp.zeros((RB, tc), F32) for _ in range(K)]
        accb = jnp.zeros((RB, tc), F32)
        for s in range(tm // RB):
            d = dy_ref[s * RB:(s + 1) * RB, :].astype(F32)
            accb = accb + d
            for k in range(K):
                off = s * RB + H - (K - 1) + k
                acc[k] = acc[k] + d * _window(ext, shifted, K, off, RB)
        for k in range(K):
            dw_ref[k:k + 1, :] += jnp.sum(acc[k], axis=0, keepdims=True)
        db_ref[...] += jnp.sum(accb, axis=0, keepdims=True)

    r = tm // H
    return pl.pallas_call(
        body, name=name, grid=(C // tc, S // tm),
        in_specs=[pl.BlockSpec((tm, tc), lambda j, i: (i, j)),
                  pl.BlockSpec((H, tc), lambda j, i: (jnp.maximum(i * r - 1, 0), j)),
                  pl.BlockSpec((tm, tc), lambda j, i: (i, j))],
        out_specs=[pl.BlockSpec((K, tc), lambda j, i: (0, j)), pl.BlockSpec((1, tc), lambda j, i: (0, j))],
        out_shape=[jax.ShapeDtypeStruct((K, C), F32), jax.ShapeDtypeStruct((1, C), F32)],
        scratch_shapes=[pltpu.VMEM((tm + H, tc), F32), pltpu.VMEM((SUBLANES, tm + H - SUBLANES, tc), F32)],
        compiler_params=_params("parallel", "arbitrary"),
    )(x, x, dy)


FFN_HALO = 16


def _row_shifts(ref, r0, rows, cs, shifts):
    n = rows // SUBLANES
    lo = -1 if max(shifts) > 0 else 0
    hi = n + (1 if min(shifts) < 0 else 0)
    v = {j: ref[r0 + j * SUBLANES:r0 + (j + 1) * SUBLANES, cs] for j in range(lo, hi)}
    row = lax.broadcasted_iota(jnp.int32, (SUBLANES, LANES), 0)
    out = {}
    for s in shifts:
        if s == 0:
            pieces = [v[j] for j in range(n)]
        elif s > 0:
            rot = {j: pltpu.roll(v[j], s, axis=0) for j in range(-1, n)}
            pieces = [jnp.where(row < s, rot[j - 1], rot[j]) for j in range(n)]
        else:
            rot = {j: pltpu.roll(v[j], SUBLANES + s, axis=0) for j in range(0, n + 1)}
            pieces = [jnp.where(row < SUBLANES + s, rot[j], rot[j + 1]) for j in range(n)]
        out[s] = jnp.concatenate(pieces, axis=0)
    return out


def _conv_taps(w, b, ext, r0, rows, cs):
    K = w.shape[0]
    win = _row_shifts(ext, r0, rows, cs, list(range(K)))
    acc = b
    for k in range(K):
        acc = acc + w[k:k + 1, :] * win[K - 1 - k]
    return acc


def _ffn_up_fused(name, h, w_up, cw, cb):
    S, D = h.shape
    F = w_up.shape[1] // 2
    tm, tn = _pick_rows(S, 512), _pick(F, 1408)
    nj, H, RB = F // tn, FFN_HALO, 64

    def body(h_ref, hh_ref, wg_ref, wu_ref, cwg_ref, cwu_ref, cbg_ref, cbu_ref,
             a_ref, u0g_ref, u0u_ref, ug_ref, uu_ref, eg, eu):
        first = pl.program_id(1) == 0
        hv, halo = h_ref[...], hh_ref[...]
        for w_ref, u0_ref, e in ((wg_ref, u0g_ref, eg), (wu_ref, u0u_ref, eu)):
            w = w_ref[...]
            u0 = _nn(hv, w)
            u0_ref[...] = u0.astype(BF16)
            e[0:H, :] = jnp.where(first, 0.0, _nn(halo, w))
            e[H:H + tm, :] = u0
        for c in range(tn // LANES):
            cs = slice(c * LANES, (c + 1) * LANES)
            wg, wu, bg, bu = cwg_ref[:, cs], cwu_ref[:, cs], cbg_ref[:, cs], cbu_ref[:, cs]
            for s in range(tm // RB):
                rows = slice(s * RB, (s + 1) * RB)
                ug = _conv_taps(wg, bg, eg, H + s * RB, RB, cs)
                uu = _conv_taps(wu, bu, eu, H + s * RB, RB, cs)
                ug_ref[rows, cs] = ug.astype(BF16)
                uu_ref[rows, cs] = uu.astype(BF16)
                a_ref[rows, cs] = (ug * _sigmoid(ug) * uu).astype(BF16)

    r = tm // H
    gate = lambda rows: pl.BlockSpec((rows, tn), lambda j, i: (0, j))
    up = lambda rows: pl.BlockSpec((rows, tn), lambda j, i: (0, j + nj))
    tile = pl.BlockSpec((tm, tn), lambda j, i: (i, j))
    K = cw.shape[0]
    return pl.pallas_call(
        body, name=name, grid=(nj, S // tm),
        in_specs=[pl.BlockSpec((tm, D), lambda j, i: (i, 0)),
                  pl.BlockSpec((H, D), lambda j, i: (jnp.maximum(i * r - 1, 0), 0)),
                  gate(D), up(D), gate(K), up(K), gate(1), up(1)],
        out_specs=[tile] * 5, out_shape=[jax.ShapeDtypeStruct((S, F), BF16)] * 5,
        scratch_shapes=[pltpu.VMEM((H + tm, tn), F32)] * 2, compiler_params=_params("parallel", "parallel"),
    )(h, h, w_up, w_up, cw, cw, cb, cb)


def _ffn_gate_bwd_fused(name, dy, w_down, u0g, u0u, ug, uu, cw):
    S, D = dy.shape
    F, K = w_down.shape[0], cw.shape[0]
    tm, tn = _pick_rows(S, 512), _pick(F, 1408)
    nj, nrow, H, RB = F // tn, S // tm, FFN_HALO, 16
    RW = 32

    def body(dy_ref, dyn_ref, wd_ref, u0g_ref, u0u_ref, g_ref, gn_ref, u_ref, un_ref, cwg_ref, cwu_ref,
             dg_ref, du_ref, dcwg_ref, dcwu_ref, dcbg_ref, dcbu_ref, dg_s, du_s, da_s):
        i = pl.program_id(1)
        last = i == nrow - 1

        @pl.when(i == 0)
        def _():
            for ref in (dcwg_ref, dcwu_ref, dcbg_ref, dcbu_ref):
                ref[...] = jnp.zeros_like(ref)

        wd = wd_ref[...]
        da_s[0:tm, :] = _nt(dy_ref[...].astype(BF16), wd)
        da_s[tm:tm + H, :] = jnp.where(last, 0.0, _nt(dyn_ref[...].astype(BF16), wd))
        for c in range(tn // LANES):
            cs = slice(c * LANES, (c + 1) * LANES)
            for s in range(tm // RB + 1):
                rows = slice(s * RB, (s + 1) * RB)
                src_g, src_u, src_rows = (g_ref, u_ref, rows) if s < tm // RB else (gn_ref, un_ref, slice(0, RB))
                gv, uv = src_g[src_rows, cs].astype(F32), src_u[src_rows, cs].astype(F32)
                da = da_s[rows, cs]
                sg = _sigmoid(gv)
                dg_s[rows, cs] = da * uv * (sg * (1.0 + gv * (1.0 - sg)))
                du_s[rows, cs] = da * (gv * sg)
            for d_s, u0_ref, cw_ref, out_ref, dcw_ref, dcb_ref in (
                    (dg_s, u0g_ref, cwg_ref, dg_ref, dcwg_ref, dcbg_ref),
                    (du_s, u0u_ref, cwu_ref, du_ref, dcwu_ref, dcbu_ref)):
                w = cw_ref[:, cs]
                acc = [jnp.zeros((RW, LANES), F32) for _ in range(K)]
                accb = jnp.zeros((RW, LANES), F32)
                for s in range(tm // RW):
                    r0 = s * RW
                    u0 = u0_ref[r0:r0 + RW, cs].astype(F32)
                    ahead = _row_shifts(d_s, r0, RW, cs, [-m for m in range(K)])
                    t = None
                    for k in range(K):
                        m = K - 1 - k
                        win = ahead[-m]
                        if m == 0:
                            accb = accb + win
                        acc[k] = acc[k] + win * u0
                        term = w[k:k + 1, :] * win
                        t = term if t is None else t + term
                    out_ref[r0:r0 + RW, cs] = t.astype(BF16)
                for k in range(K):
                    dcw_ref[k:k + 1, cs] += jnp.sum(acc[k], axis=0, keepdims=True)
                dcb_ref[:, cs] += jnp.sum(accb, axis=0, keepdims=True)

    r = tm // H
    gate = lambda rows: pl.BlockSpec((rows, tn), lambda j, i: (0, j))
    up = lambda rows: pl.BlockSpec((rows, tn), lambda j, i: (0, j + nj))
    tile = pl.BlockSpec((tm, tn), lambda j, i: (i, j))
    nxt = pl.BlockSpec((H, tn), lambda j, i: (jnp.minimum((i + 1) * r, S // H - 1), j))
    acc_w, acc_b = pl.BlockSpec((K, tn), lambda j, i: (0, j)), pl.BlockSpec((1, tn), lambda j, i: (0, j))
    return pl.pallas_call(
        body, name=name, grid=(nj, nrow),
        in_specs=[pl.BlockSpec((tm, D), lambda j, i: (i, 0)),
                  pl.BlockSpec((H, D), lambda j, i: (jnp.minimum((i + 1) * r, S // H - 1), 0)),
                  pl.BlockSpec((tn, D), lambda j, i: (j, 0)),
                  tile, tile, tile, nxt, tile, nxt, gate(K), up(K)],
        out_specs=[tile, tile, acc_w, acc_w, acc_b, acc_b],
        out_shape=[jax.ShapeDtypeStruct((S, F), BF16)] * 2 + [jax.ShapeDtypeStruct((K, F), F32)] * 2
        + [jax.ShapeDtypeStruct((1, F), F32)] * 2,
        scratch_shapes=[pltpu.VMEM((tm + H, tn), F32)] * 3, compiler_params=_params("parallel", "arbitrary"),
    )(dy, dy, w_down, u0g, u0u, ug, ug, uu, uu, cw, cw)


def _colsum(v):
    return jnp.sum(v, axis=0, keepdims=True)


def _rmsnorm_fwd(name, x, gain):
    def fn(x, g):
        r = lax.rsqrt(jnp.mean(x * x, axis=-1, keepdims=True) + EPS)
        return x * r * g
    return _rowwise(name, fn, [(x, 0)], [(gain, None)], [(x.shape[1], BF16)])


NORM_BWD_VMEM = 44 * 1024 * 1024


def _matmul_rmsnorm_bwd(name, pairs, x, gain, dres):
    M, D = x.shape
    n = len(pairs)
    weights = sum(b.shape[0] * b.shape[1] * b.dtype.itemsize for _, b in pairs)
    for tm in (512, 256, 128):
        step = sum(tm * a.shape[1] * a.dtype.itemsize for a, _ in pairs) + 3 * tm * D * 4
        if 2 * (step + weights) + tm * D * 4 <= NORM_BWD_VMEM:
            break
    RB = 16

    def body(*refs):
        x_ref, g_ref, r_ref, o_ref, dg_ref, dh_s = refs[2 * n:]

        @pl.when(pl.program_id(0) == 0)
        def _():
            dg_ref[...] = jnp.zeros_like(dg_ref)

        acc = None
        for i in range(n):
            d = _nt(refs[2 * i][...].astype(BF16), refs[2 * i + 1][...].astype(BF16))
            acc = d if acc is None else acc + d
        dh_s[...] = acc
        g = g_ref[...]

        def step(s, carry):
            sl = pl.ds(pl.multiple_of(s * RB, RB), RB)
            xv, dh = x_ref[sl, :], dh_s[sl, :]
            r = lax.rsqrt(jnp.mean(xv * xv, axis=-1, keepdims=True) + EPS)
            xh = xv * r
            dxh = dh * g
            o_ref[sl, :] = r_ref[sl, :] + r * (dxh - xh * jnp.mean(dxh * xh, axis=-1, keepdims=True))
            dg_ref[...] += _colsum(dh * xh)
            return carry

        lax.fori_loop(0, tm // RB, step, 0, unroll=ROWWISE_UNROLL_ROWS // RB)

    in_specs, args = [], []
    for a, b in pairs:
        in_specs += [pl.BlockSpec((tm, a.shape[1]), lambda i: (i, 0)), pl.BlockSpec(b.shape, lambda i: (0, 0))]
        args += [a, b]
    rows = pl.BlockSpec((tm, D), lambda i: (i, 0))
    vec = pl.BlockSpec((1, D), lambda i: (0, 0))
    return pl.pallas_call(
        body, name=name, grid=(M // tm,), in_specs=in_specs + [rows, vec, rows], out_specs=[rows, vec],
        out_shape=[jax.ShapeDtypeStruct((M, D), F32), jax.ShapeDtypeStruct((1, D), F32)],
        scratch_shapes=[pltpu.VMEM((tm, D), F32)], compiler_params=_params("arbitrary"),
    )(*args, x, gain, dres)


def _silu_gate_fwd(name, gate, up):
    F = gate.shape[1]
    def fn(g, up):
        return g * _sigmoid(g) * up
    return _rowwise(name, fn, [(gate, 0), (up, 0)], [], [(F, BF16)], tc=_pick(F, 512))


def _silu_gate_bwd(name, gate, up, da):
    F = gate.shape[1]
    def fn(g, up, da):
        s = _sigmoid(g)
        return da * up * (s * (1.0 + g * (1.0 - s))), da * (g * s)
    return _rowwise(name, fn, [(gate, 0), (up, 0), (da, 0)], [], [(F, F32), (F, F32)], tc=_pick(F, 512))


def _glu_fwd(name, a, gate):
    C = a.shape[1]
    def fn(a, g):
        return a * _sigmoid(g)
    return _rowwise(name, fn, [(a, 0), (gate, 0)], [], [(C, F32)], tc=_pick(C, 512))


def _glu_bwd(name, a, gate, dglu):
    C = a.shape[1]
    def fn(a, g, d):
        s = _sigmoid(g)
        da, dg = d * s, d * a * s * (1.0 - s)
        return da, dg, _colsum(da), _colsum(dg)
    return _rowwise(name, fn, [(a, 0), (gate, 0), (dglu, 0)], [], [(C, BF16), (C, BF16)], [(1, C), (1, C)],
                    tc=_pick(C, 512))


def _ln_silu_fwd(name, c, g, b):
    def fn(c, g, b):
        mu = jnp.mean(c, axis=-1, keepdims=True)
        d = c - mu
        n = d * lax.rsqrt(jnp.mean(d * d, axis=-1, keepdims=True) + EPS) * g + b
        return n * _sigmoid(n)
    return _rowwise(name, fn, [(c, 0)], [(g, None), (b, None)], [(c.shape[1], BF16)])


def _ln_silu_bwd(name, c, g, b, dsw):
    def fn(c, dsw, g, b):
        mu = jnp.mean(c, axis=-1, keepdims=True)
        d = c - mu
        r = lax.rsqrt(jnp.mean(d * d, axis=-1, keepdims=True) + EPS)
        ch = d * r
        n = ch * g + b
        s = _sigmoid(n)
        dn = dsw * (s * (1.0 + n * (1.0 - s)))
        dch = dn * g
        dc = r * (dch - jnp.mean(dch, axis=-1, keepdims=True) - ch * jnp.mean(dch * ch, axis=-1, keepdims=True))
        return dc, _colsum(dn * ch), _colsum(dn)
    C = c.shape[1]
    return _rowwise(name, fn, [(c, 0), (dsw, 0)], [(g, None), (b, None)], [(C, F32)], [(1, C), (1, C)])


def _column_sums(name, x):
    return _rowwise(name, lambda x: (_colsum(x),), [(x, 0)], [], [], [(1, x.shape[1])])


def _loss_grad(name, y, target):
    D = y.shape[1]
    def fn(y, t):
        e = y - t
        return e * (1.0 / D), _colsum(e * e) * (0.5 / D)
    return _rowwise(name, fn, [(y, 0), (target, 0)], [], [(D, F32)], [(1, D)])


def _add(name, arrays):
    def fn(*xs):
        acc = xs[0]
        for x in xs[1:]:
            acc = acc + x
        return acc
    return _rowwise(name, fn, [(a, 0) for a in arrays], [], [(arrays[0].shape[1], F32)], tm=256)


def _sum_slabs(name, stacked):
    n, R, C = stacked.shape
    tm = _pick_rows(R, 256)

    def body(*refs):
        acc = refs[0][0].astype(F32)
        for r in refs[1:n]:
            acc = acc + r[0].astype(F32)
        refs[n][...] = acc

    return pl.pallas_call(
        body, name=name, grid=(R // tm,),
        in_specs=[pl.BlockSpec((1, tm, C), lambda i, q=q: (q, i, 0)) for q in range(n)],
        out_specs=pl.BlockSpec((tm, C), lambda i: (i, 0)), out_shape=jax.ShapeDtypeStruct((R, C), F32),
        compiler_params=_params("parallel"),
    )(*[stacked] * n)


def _adamw(name, w, g, m, v):
    c1 = 1.0 - ADAM_B1 ** ADAM_STEP
    c2 = 1.0 - ADAM_B2 ** ADAM_STEP
    def fn(w, g, m, v):
        m = ADAM_B1 * m + (1.0 - ADAM_B1) * g
        v = ADAM_B2 * v + (1.0 - ADAM_B2) * (g * g)
        delta = -ADAM_LR * ((m / c1) / (jnp.sqrt(v / c2) + ADAM_EPS) + ADAM_WD * w)
        return delta, m, v
    C = w.shape[1]
    return _rowwise(name, fn, [(w, 0), (g, 0), (m, 0), (v, 0)], [], [(C, F32)] * 3, tm=256)


SEG_ROWS = 128


def _segment_matrix(n, seg):
    i = jnp.arange(n) // seg
    return (i[:, None] == i[None, :]).astype(BF16)


def _seg_sum(v, B):
    hi = v.astype(BF16)
    lo = (v - hi.astype(F32)).astype(BF16)
    n = B.shape[0]
    slabs = [slice(c, c + n) for c in range(0, v.shape[1], n)]
    return jnp.concatenate([_nn(hi[:, c], B) + _nn(lo[:, c], B) for c in slabs], axis=1)


def _qknorm_fwd(name, proj, gain_full, is_norm, seg):
    def fn(x, gf, isn, B):
        ms = _seg_sum(x * x, B) * (1.0 / ATTN_HEAD_DIM)
        r = lax.rsqrt(ms + EPS)
        return x * (isn * r + (1.0 - isn)) * gf
    W = proj.shape[1]
    return _rowwise(name, fn, [(proj, 0)], [(gain_full, 0), (is_norm, 0), (seg, None)], [(W, BF16)],
                    tc=ATTN_GW, rb=SEG_ROWS)


def _qknorm_bwd(name, proj, dy, gain_full, is_norm, seg):
    def fn(x, dy, gf, isn, B):
        ms = _seg_sum(x * x, B) * (1.0 / ATTN_HEAD_DIM)
        r = lax.rsqrt(ms + EPS)
        xh = x * r
        dxh = dy * gf
        dn = r * (dxh - xh * (_seg_sum(dxh * xh, B) * (1.0 / ATTN_HEAD_DIM)))
        return isn * dn + (1.0 - isn) * dxh, _colsum(dy * xh)
    W = proj.shape[1]
    return _rowwise(name, fn, [(proj, 0), (dy, 0)], [(gain_full, 0), (is_norm, 0), (seg, None)],
                    [(W, BF16)], [(1, W)], tc=ATTN_GW, rb=SEG_ROWS)


def _attn_masks(r0=0, rows=ATTN_BLOCK):
    shape = (rows, ATTN_BLOCK)
    row = lax.broadcasted_iota(jnp.int32, shape, 0) + r0
    col = lax.broadcasted_iota(jnp.int32, shape, 1)
    return col <= row, col >= row, col < ATTN_HEAD_DIM


def _attn_group_fwd(name, qkv, d):
    S = qkv.shape[0]
    n, W, G = S // d, 3 * ATTN_GW, ATTN_GW
    nb = n // ATTN_BLOCK
    view = qkv.reshape(n, d * W)

    B, RQ = ATTN_BLOCK, ATTN_ROWS

    def body(cur, prev, o_ref, l_ref, s_scr, p_scr, lse_scr, inv_scr):
        b = pl.program_id(1)
        cur_mask, prev_mask, low = _attn_masks()
        prev_mask = jnp.logical_and(prev_mask, b > 0)
        for h in range(ATTN_HEADS):
            c0 = (h // 2) * LANES
            hm = low if h % 2 == 0 else jnp.logical_not(low)
            q2 = cur[:, c0:c0 + LANES]
            qm = jnp.where(hm, q2, jnp.zeros_like(q2))
            s_scr[h, :, 0:B] = jnp.where(cur_mask, _nt(qm, cur[:, G + c0:G + c0 + LANES]), -jnp.inf)
            s_scr[h, :, B:2 * B] = jnp.where(prev_mask, _nt(qm, prev[:, G + c0:G + c0 + LANES]), -jnp.inf)
        for h in range(ATTN_HEADS):
            for r0 in range(0, B, RQ):
                s = s_scr[h, r0:r0 + RQ, :]
                m = jnp.max(s, axis=1, keepdims=True)
                p = jnp.exp(s - m)
                l = jnp.sum(p, axis=1, keepdims=True)
                p_scr[h, r0:r0 + RQ, :] = p.astype(BF16)
                lse_scr[h, r0:r0 + RQ, :] = jnp.broadcast_to(m + jnp.log(l), (RQ, LANES))
                inv_scr[h, r0:r0 + RQ, :] = jnp.broadcast_to(1.0 / l, (RQ, LANES))
        for pr in range(G // LANES):
            c0 = pr * LANES
            vc, vp = cur[:, 2 * G + c0:2 * G + c0 + LANES], prev[:, 2 * G + c0:2 * G + c0 + LANES]
            o = [(_nn(p_scr[h, :, 0:B], vc) + _nn(p_scr[h, :, B:2 * B], vp)) * inv_scr[h] for h in (2 * pr, 2 * pr + 1)]
            o_ref[:, c0:c0 + LANES] = jnp.where(low, o[0], o[1])
            l_ref[:, c0:c0 + LANES] = jnp.where(low, lse_scr[2 * pr], lse_scr[2 * pr + 1])

    o, l = pl.pallas_call(
        body, name=name, grid=(d, nb),
        in_specs=[pl.BlockSpec((B, W), lambda r, b: (b, r)),
                  pl.BlockSpec((B, W), lambda r, b: (jnp.maximum(b - 1, 0), r))],
        out_specs=[pl.BlockSpec((B, G), lambda r, b: (b, r))] * 2,
        out_shape=[jax.ShapeDtypeStruct((n, d * G), F32)] * 2,
        scratch_shapes=[pltpu.VMEM((ATTN_HEADS, B, 2 * B), F32), pltpu.VMEM((ATTN_HEADS, B, 2 * B), BF16),
                        pltpu.VMEM((ATTN_HEADS, B, LANES), F32), pltpu.VMEM((ATTN_HEADS, B, LANES), F32)],
        compiler_params=_params("parallel", "parallel"),
    )(view, view)
    return o.reshape(S, G), l.reshape(S, G)


def _attn_combine(name, os, ls):
    def fn(o1, o2, o3, l1, l2, l3):
        m = jnp.maximum(jnp.maximum(l1, l2), l3)
        e1, e2, e3 = jnp.exp(l1 - m), jnp.exp(l2 - m), jnp.exp(l3 - m)
        den = e1 + e2 + e3
        return (e1 * o1 + e2 * o2 + e3 * o3) / den, m + jnp.log(den)
    G = os[0].shape[1]
    return _rowwise(name, fn, [(a, 0) for a in (*os, *ls)], [], [(G, F32), (G, F32)])


def _attn_delta(name, do, o, lse, seg):
    def fn(do, o, lse, B):
        lane = lax.broadcasted_iota(jnp.int32, lse.shape, 1)
        first = (lane & (ATTN_HEAD_DIM - 1)) < ATTN_HEAD_DIM // 2
        return jnp.where(first, lse, _seg_sum(do * o, B)), do
    G = o.shape[1]
    return _rowwise(name, fn, [(do, 0), (o, 0), (lse, 0)], [(seg, None)], [(G, F32), (G, BF16)], rb=SEG_ROWS)


def _attn_group_bwd(name, qkv, do, stats, d):
    S = qkv.shape[0]
    n, W, G = S // d, 3 * ATTN_GW, ATTN_GW
    nb = n // ATTN_BLOCK

    B, RQ = ATTN_BLOCK, ATTN_ROWS
    S_A, DP_A, S_B, DP_B, S_C, DP_C = range(6)
    P_A, DS_A, P_B, DS_B, DS_C = range(5)

    def body(qp, qc, qn, do_c, do_n, st_c, st_n, out, f_scr, b_scr):
        j = pl.program_id(1)
        low = _attn_masks()[2]
        for h in range(ATTN_HEADS):
            c0 = (h // 2) * LANES
            hm = low if h % 2 == 0 else jnp.logical_not(low)
            k_c, v_c = qc[:, G + c0:G + c0 + LANES], qc[:, 2 * G + c0:2 * G + c0 + LANES]
            k_p, v_p = qp[:, G + c0:G + c0 + LANES], qp[:, 2 * G + c0:2 * G + c0 + LANES]
            zero = jnp.zeros((B, LANES), BF16)
            qmc, qmn = jnp.where(hm, qc[:, c0:c0 + LANES], zero), jnp.where(hm, qn[:, c0:c0 + LANES], zero)
            dmc = jnp.where(hm, do_c[:, c0:c0 + LANES].astype(BF16), zero)
            dmn = jnp.where(hm, do_n[:, c0:c0 + LANES].astype(BF16), zero)
            f_scr[h, S_A], f_scr[h, DP_A] = _nt(qmc, k_c), _nt(dmc, v_c)
            f_scr[h, S_B], f_scr[h, DP_B] = _nt(qmn, k_c), _nt(dmn, v_c)
            f_scr[h, S_C], f_scr[h, DP_C] = _nt(qmc, k_p), _nt(dmc, v_p)
        for h in range(ATTN_HEADS):
            h0 = h * ATTN_HEAD_DIM
            for r0 in range(0, B, RQ):
                rows = slice(r0, r0 + RQ)
                cur_mask, band, _ = _attn_masks(r0, RQ)
                next_mask, prev_mask = jnp.logical_and(band, j < nb - 1), jnp.logical_and(band, j > 0)
                h1 = h0 + ATTN_HEAD_DIM // 2
                lc, ln = st_c[rows, h0:h0 + 1], st_n[rows, h0:h0 + 1]
                dlc, dln = st_c[rows, h1:h1 + 1], st_n[rows, h1:h1 + 1]
                p_a = jnp.where(cur_mask, jnp.exp(f_scr[h, S_A, rows, :] - lc), 0.0)
                p_b = jnp.where(next_mask, jnp.exp(f_scr[h, S_B, rows, :] - ln), 0.0)
                p_c = jnp.where(prev_mask, jnp.exp(f_scr[h, S_C, rows, :] - lc), 0.0)
                b_scr[h, P_A, rows, :] = p_a.astype(BF16)
                b_scr[h, P_B, rows, :] = p_b.astype(BF16)
                b_scr[h, DS_A, rows, :] = (p_a * (f_scr[h, DP_A, rows, :] - dlc)).astype(BF16)
                b_scr[h, DS_B, rows, :] = (p_b * (f_scr[h, DP_B, rows, :] - dln)).astype(BF16)
                b_scr[h, DS_C, rows, :] = (p_c * (f_scr[h, DP_C, rows, :] - dlc)).astype(BF16)
        for pr in range(G // LANES):
            c0 = pr * LANES
            q_c, k_c, q_n = qc[:, c0:c0 + LANES], qc[:, G + c0:G + c0 + LANES], qn[:, c0:c0 + LANES]
            k_p = qp[:, G + c0:G + c0 + LANES]
            d_c, d_n = do_c[:, c0:c0 + LANES].astype(BF16), do_n[:, c0:c0 + LANES].astype(BF16)
            res = []
            for h in (2 * pr, 2 * pr + 1):
                dq = _nn(b_scr[h, DS_A], k_c) + _nn(b_scr[h, DS_C], k_p)
                dk = _tn(b_scr[h, DS_A], q_c) + _tn(b_scr[h, DS_B], q_n)
                dv = _tn(b_scr[h, P_A], d_c) + _tn(b_scr[h, P_B], d_n)
                res.append((dq, dk, dv))
            for t in range(3):
                out[:, t * G + c0:t * G + c0 + LANES] = jnp.where(low, res[0][t], res[1][t]).astype(out.dtype)

    prv = lambda r, j: (jnp.maximum(j - 1, 0), r)
    cur = lambda r, j: (j, r)
    nxt = lambda r, j: (jnp.minimum(j + 1, nb - 1), r)
    wide = lambda m: pl.BlockSpec((ATTN_BLOCK, W), m)
    narrow = lambda m: pl.BlockSpec((ATTN_BLOCK, G), m)
    qv, dv, sv = qkv.reshape(n, d * W), do.reshape(n, d * G), stats.reshape(n, d * G)
    out = pl.pallas_call(
        body, name=name, grid=(d, nb),
        in_specs=[wide(prv), wide(cur), wide(nxt), narrow(cur), narrow(nxt), narrow(cur), narrow(nxt)],
        out_specs=wide(cur), out_shape=jax.ShapeDtypeStruct((n, d * W), BF16),
        scratch_shapes=[pltpu.VMEM((ATTN_HEADS, 6, B, B), F32), pltpu.VMEM((ATTN_HEADS, 5, B, B), BF16)],
        compiler_params=_params("parallel", "parallel"),
    )(qv, qv, qv, dv, dv, sv, sv)
    return out.reshape(S, W)


def _chunk_triangle(T, upper):
    i = jnp.arange(T)
    same = (i[:, None] // HGRN_CHUNK) == (i[None, :] // HGRN_CHUNK)
    tri = (i[None, :] >= i[:, None]) if upper else (i[None, :] <= i[:, None])
    return jnp.logical_and(same, tri).astype(F32)


def _hgrn_prologue(qr, fr, lbv, q_s, k_s, b_s, tri_ref, T):
    def pro(s, c):
        sl = pl.ds(pl.multiple_of(s * HGRN_CHUNK, HGRN_CHUNK), HGRN_CHUNK)
        sg = _sigmoid(fr[sl, :])
        qv = qr[sl, :]
        q_s[sl, :] = qv * _sigmoid(qv)
        k_s[sl, :] = (1.0 - lbv) * (1.0 - sg)
        b_s[sl, :] = jnp.log(lbv + (1.0 - lbv) * sg)
        return c
    lax.fori_loop(0, T // HGRN_CHUNK, pro, 0)
    b_s[...] = _nn(tri_ref[...], b_s[...], HI)


def _hgrn_scan_fwd(name, pq, pf, pv, lb):
    S, D = pq.shape
    T = _pick_rows(S, HGRN_TILE)
    NH, NT, C, HD, HB = D // HGRN_HEAD, S // T, HGRN_CHUNK, HGRN_HEAD, HGRN_GROUP
    W = HB * HD
    tri = _chunk_triangle(T, upper=False)

    def body(qr, fr, iv, lb_ref, tri_ref, o_ref, ck_ref, st_ref, q_s, k_s, b_s):
        @pl.when(pl.program_id(1) == 0)
        def _():
            st_ref[...] = jnp.zeros_like(st_ref)

        ck_ref[...] = st_ref[...]
        _hgrn_prologue(qr, fr, lb_ref[...], q_s, k_s, b_s, tri_ref, T)
        row = lax.broadcasted_iota(jnp.int32, (C, 1), 0)

        def chunk(c, carry):
            sl = pl.ds(pl.multiple_of(c * C, C), C)
            for hh in range(HB):
                cs = slice(hh * HD, (hh + 1) * HD)
                q, k, b, v = q_s[sl, cs], k_s[sl, cs], b_s[sl, cs], iv[sl, cs]
                b_last = b[C - 1:C, :]
                st = st_ref[cs, :]
                o = _nt((q * jnp.exp(b)).astype(BF16), st.astype(BF16))
                for s in range(C):
                    e = jnp.exp(jnp.minimum(b - b[s:s + 1, :], 0.0))
                    a = jnp.sum(q * e * k[s:s + 1, :], axis=1, keepdims=True)
                    o = o + jnp.where(row >= s, a, 0.0) * v[s:s + 1, :]
                o_ref[sl, cs] = o
                kd = k * jnp.exp(b_last - b)
                st_ref[cs, :] = st * jnp.exp(b_last) + _tn(v.astype(BF16), kd.astype(BF16))
            return carry

        lax.fori_loop(0, T // C, chunk, 0)

    NG = NH // HB
    col = pl.BlockSpec((T, W), lambda h, t: (t, h))
    return pl.pallas_call(
        body, name=name, grid=(NG, NT),
        in_specs=[col, col, col, pl.BlockSpec((1, W), lambda h, t: (0, h)), pl.BlockSpec((T, T), lambda h, t: (0, 0))],
        out_specs=[col, pl.BlockSpec((W, HD), lambda h, t: (t * NG + h, 0))],
        out_shape=[jax.ShapeDtypeStruct((S, D), F32), jax.ShapeDtypeStruct((NT * NH * HD, HD), F32)],
        scratch_shapes=[pltpu.VMEM((W, HD), F32)] + [pltpu.VMEM((T, W), F32)] * 3,
        compiler_params=_params("parallel", "arbitrary"),
    )(pq, pf, pv, lb, tri)


def _hgrn_scan_bwd(name, pq, pf, pv, lb, ckpt, do):
    S, D = pq.shape
    T = _pick_rows(S, HGRN_TILE)
    NH, NT, C, HD, HB = D // HGRN_HEAD, S // T, HGRN_CHUNK, HGRN_HEAD, HGRN_GROUP
    NC, W, NG = T // C, HB * HD, NH // HB
    tri, tri_up = _chunk_triangle(T, upper=False), _chunk_triangle(T, upper=True)

    def body(qr, fr, iv, do_ref, ck_ref, lb_ref, tri_ref, triu_ref, dq_ref, df_ref, dv_ref, dlb_ref,
             dst_ref, run, save, q_s, k_s, b_s, dq_s, dk_s, db_s):
        @pl.when(pl.program_id(1) == 0)
        def _():
            dst_ref[...] = jnp.zeros_like(dst_ref)
            dlb_ref[...] = jnp.zeros_like(dlb_ref)

        lbv = lb_ref[...]
        _hgrn_prologue(qr, fr, lbv, q_s, k_s, b_s, tri_ref, T)
        row = lax.broadcasted_iota(jnp.int32, (C, 1), 0)
        run[...] = ck_ref[...]

        def replay(c, carry):
            sl = pl.ds(pl.multiple_of(c * C, C), C)
            for hh in range(HB):
                cs = slice(hh * HD, (hh + 1) * HD)
                st = run[cs, :]
                save[pl.ds(pl.multiple_of((hh * NC + c) * HD, HD), HD), :] = st
                k, b, v = k_s[sl, cs], b_s[sl, cs], iv[sl, cs]
                b_last = b[C - 1:C, :]
                kd = k * jnp.exp(b_last - b)
                run[cs, :] = st * jnp.exp(b_last) + _tn(v.astype(BF16), kd.astype(BF16))
            return carry

        lax.fori_loop(0, NC, replay, 0)

        def chunk(ci, carry):
            c = NC - 1 - ci
            sl = pl.ds(pl.multiple_of(c * C, C), C)
            for hh in range(HB):
                cs = slice(hh * HD, (hh + 1) * HD)
                q, k, b, v, g = q_s[sl, cs], k_s[sl, cs], b_s[sl, cs], iv[sl, cs], do_ref[sl, cs]
                st0 = save[pl.ds(pl.multiple_of((hh * NC + c) * HD, HD), HD), :]
                dst1 = dst_ref[cs, :]
                b_last = b[C - 1:C, :]
                eb, ebl, ek = jnp.exp(b), jnp.exp(b_last), jnp.exp(b_last - b)
                dst1_b = dst1.astype(BF16)
                dq = _nn(g.astype(BF16), st0.astype(BF16)) * eb
                dv = _nt((k * ek).astype(BF16), dst1_b)
                dk = _nn(v.astype(BF16), dst1_b) * ek
                db_last = _colsum(dk * k) + _colsum(dst1 * st0) * ebl
                for s in range(C):
                    e = jnp.where(row >= s, jnp.exp(jnp.minimum(b - b[s:s + 1, :], 0.0)), 0.0)
                    ks, vs = k[s:s + 1, :], v[s:s + 1, :]
                    da = jnp.sum(g * vs, axis=1, keepdims=True)
                    a = jnp.sum(q * e * ks, axis=1, keepdims=True)
                    dq = dq + da * e * ks
                    dk = dk + jnp.where(row == s, _colsum(da * q * e), 0.0)
                    dv = dv + jnp.where(row == s, _colsum(a * g), 0.0)
                dq_s[sl, cs] = dq
                dk_s[sl, cs] = dk
                db_s[sl, cs] = q * dq - k * dk + jnp.where(row == C - 1, db_last, 0.0)
                dv_ref[sl, cs] = dv.astype(BF16)
                dst_ref[cs, :] = dst1 * ebl + _tn(g.astype(BF16), (q * eb).astype(BF16))
            return carry

        lax.fori_loop(0, NC, chunk, 0)
        db_s[...] = _nn(triu_ref[...], db_s[...], HI)

        def epi(s, carry):
            sl = pl.ds(pl.multiple_of(s * C, C), C)
            qv = qr[sl, :]
            sq = _sigmoid(qv)
            dq_ref[sl, :] = (dq_s[sl, :] * sq * (1.0 + qv * (1.0 - sq))).astype(BF16)
            sg = _sigmoid(fr[sl, :])
            common = db_s[sl, :] / (lbv + (1.0 - lbv) * sg) - dk_s[sl, :]
            df_ref[sl, :] = (common * (1.0 - lbv) * sg * (1.0 - sg)).astype(BF16)
            dlb_ref[...] += _colsum(common * (1.0 - sg))
            return carry

        lax.fori_loop(0, NC, epi, 0)

    col = pl.BlockSpec((T, W), lambda h, t: (NT - 1 - t, h))
    dq, df, dv, dlb = pl.pallas_call(
        body, name=name, grid=(NG, NT),
        in_specs=[col, col, col, col, pl.BlockSpec((W, HD), lambda h, t: ((NT - 1 - t) * NG + h, 0)),
                  pl.BlockSpec((1, W), lambda h, t: (0, h)),
                  pl.BlockSpec((T, T), lambda h, t: (0, 0)), pl.BlockSpec((T, T), lambda h, t: (0, 0))],
        out_specs=[col, col, col, pl.BlockSpec((1, W), lambda h, t: (0, h))],
        out_shape=[jax.ShapeDtypeStruct((S, D), BF16)] * 3 + [jax.ShapeDtypeStruct((1, D), F32)],
        scratch_shapes=[pltpu.VMEM((W, HD), F32)] * 2 + [pltpu.VMEM((HB * NC * HD, HD), F32)]
        + [pltpu.VMEM((T, W), F32)] * 6,
        compiler_params=_params("parallel", "arbitrary"),
    )(pq, pf, pv, do, ckpt, lb, tri, tri_up)
    return dq, df, dv, dlb


def _hgrn_out_fwd(name, o, gate, norm_g):
    def fn(o, g, ng):
        parts = []
        for h in range(o.shape[1] // HGRN_HEAD):
            c = slice(h * HGRN_HEAD, (h + 1) * HGRN_HEAD)
            oh, gh = o[:, c], g[:, c]
            r = lax.rsqrt(jnp.mean(oh * oh, axis=-1, keepdims=True) + EPS)
            parts.append(oh * r * ng[:, c] * (gh * _sigmoid(gh)))
        return jnp.concatenate(parts, axis=1)
    return _rowwise(name, fn, [(o, 0), (gate, 0)], [(norm_g, None)], [(o.shape[1], BF16)])


def _hgrn_out_bwd(name, o, gate, norm_g, dy):
    def fn(o, g, dy, ng):
        dos, dgs, dngs = [], [], []
        for h in range(o.shape[1] // HGRN_HEAD):
            c = slice(h * HGRN_HEAD, (h + 1) * HGRN_HEAD)
            oh, gh, dyh, ngh = o[:, c], g[:, c], dy[:, c], ng[:, c]
            r = lax.rsqrt(jnp.mean(oh * oh, axis=-1, keepdims=True) + EPS)
            xh = oh * r
            s = _sigmoid(gh)
            dn = dyh * (gh * s)
            dxh = dn * ngh
            dos.append(r * (dxh - xh * jnp.mean(dxh * xh, axis=-1, keepdims=True)))
            dgs.append(dyh * xh * ngh * (s * (1.0 + gh * (1.0 - s))))
            dngs.append(_colsum(dn * xh))
        return jnp.concatenate(dos, axis=1), jnp.concatenate(dgs, axis=1), jnp.concatenate(dngs, axis=1)
    D = o.shape[1]
    return _rowwise(name, fn, [(o, 0), (gate, 0), (dy, 0)], [(norm_g, None)], [(D, F32), (D, BF16)], [(1, D)])


def _lower_bound_fwd(name, logits, layer):
    n = logits.shape[0]

    def body(x_ref, o_ref):
        rows = [x_ref[i:i + 1, :] for i in range(n)]
        m = functools.reduce(jnp.maximum, rows)
        e = [jnp.exp(r - m) for r in rows]
        den = functools.reduce(jnp.add, e)
        o_ref[...] = functools.reduce(jnp.add, e[1:layer + 1]) / den

    return pl.pallas_call(body, name=name, out_shape=jax.ShapeDtypeStruct((1, logits.shape[1]), F32))(logits)


def _lower_bound_bwd(name, logits, dlb, layer):
    n = logits.shape[0]

    def body(x_ref, d_ref, o_ref):
        rows = [x_ref[i:i + 1, :] for i in range(n)]
        m = functools.reduce(jnp.maximum, rows)
        e = [jnp.exp(r - m) for r in rows]
        den = functools.reduce(jnp.add, e)
        s = [v / den for v in e]
        d = d_ref[...]
        inner = functools.reduce(jnp.add, s[1:layer + 1]) * d
        for i in range(n):
            o_ref[i:i + 1, :] = s[i] * ((d if 1 <= i <= layer else 0.0) - inner)

    return pl.pallas_call(body, name=name, out_shape=jax.ShapeDtypeStruct(logits.shape, F32))(logits, dlb)


def _row(v):
    return v.reshape(1, -1)


def _ffn_fwd(l, x1, w):
    h2 = _rmsnorm_fwd(f"ffn{l}_norm", x1, _row(w["ffn_norm"][l]))
    a, *u = _ffn_up_fused(f"ffn{l}_up", h2, w["ffn_w_up"][l], w["ffn_conv_w"][l], _row(w["ffn_conv_b"][l]))
    x2 = _matmul(f"ffn{l}_down", [(a, w["ffn_w_down"][l])], residual=x1)
    return x2, (x1, h2, u, a)


def _ffn_bwd(l, dx2, saved, w, grads):
    x1, h2, (u0g, u0u, ug, uu), a = saved
    w_up, w_down = w["ffn_w_up"][l], w["ffn_w_down"][l]
    F = w_down.shape[0]
    grads["ffn_w_down"][l] = _matmul_tn(f"ffn{l}_dwdown", a, dx2)
    dg, du, dcwg, dcwu, dcbg, dcbu = _ffn_gate_bwd_fused(
        f"ffn{l}_dgate", dx2, w_down, u0g, u0u, ug, uu, w["ffn_conv_w"][l])
    grads["ffn_conv_w"][l] = jnp.concatenate([dcwg, dcwu], axis=1)
    grads["ffn_conv_b"][l] = jnp.concatenate([dcbg, dcbu], axis=1)[0]
    grads["ffn_w_up"][l] = jnp.concatenate(
        [_matmul_tn(f"ffn{l}_dwup0", h2, dg), _matmul_tn(f"ffn{l}_dwup1", h2, du)], axis=1)
    dx1, dgain = _matmul_rmsnorm_bwd(f"ffn{l}_dh", [(dg, w_up[:, :F]), (du, w_up[:, F:])], x1,
                                     _row(w["ffn_norm"][l]), dx2)
    grads["ffn_norm"][l] = dgain[0]
    return dx1


def _attn_gain_rows(w, j, g):
    scale = ATTN_HEAD_DIM ** -0.5
    qg = jnp.tile(w["attn_q_gain"][j, g] * scale, ATTN_HEADS)
    kg = jnp.tile(w["attn_k_gain"][j, g], ATTN_HEADS)
    gain = jnp.concatenate([qg, kg, jnp.ones((ATTN_GW,), F32)])
    is_norm = jnp.concatenate([jnp.ones((2 * ATTN_GW,), F32), jnp.zeros((ATTN_GW,), F32)])
    return _row(gain), _row(is_norm)


def _attn_fwd(l, j, x, w):
    h = _rmsnorm_fwd(f"mix{l}_norm", x, _row(w["mixer_norm"][l]))
    w_in = w["attn_w_in"][j]
    seg = _segment_matrix(LANES, ATTN_HEAD_DIM)
    GW3 = 3 * ATTN_GW
    proj, qkv, os, ls = [], [], [], []
    for g, d in enumerate(ATTN_DILATIONS):
        gain, is_norm = _attn_gain_rows(w, j, g)
        proj.append(_matmul(f"attn{l}_in{g}", [(h, w_in[:, g * GW3:(g + 1) * GW3])]))
        qkv.append(_qknorm_fwd(f"attn{l}_qknorm{g}", proj[g], gain, is_norm, seg))
        o, lse = _attn_group_fwd(f"attn{l}_core{g}", qkv[g], d)
        os.append(o)
        ls.append(lse)
    o, lse = _attn_combine(f"attn{l}_combine", os, ls)
    x1 = _matmul(f"attn{l}_out", [(o, w["attn_w_out"][j])], residual=x)
    return x1, (x, h, proj, qkv, o, lse)


def _attn_bwd(l, j, dx1, saved, w, grads):
    x, h, proj, qkv, o, lse = saved
    w_in, w_out = w["attn_w_in"][j], w["attn_w_out"][j]
    seg = _segment_matrix(LANES, ATTN_HEAD_DIM)
    GW3 = 3 * ATTN_GW
    grads["attn_w_out"][j] = _matmul_tn(f"attn{l}_dwout", o, dx1)
    do = _matmul(f"attn{l}_do", [(dx1, w_out)], trans_b=True)
    stats, do = _attn_delta(f"attn{l}_delta", do, o, lse, seg)
    dproj, dwin, dqg, dkg = [], [], [], []
    for g, d in enumerate(ATTN_DILATIONS):
        gain, is_norm = _attn_gain_rows(w, j, g)
        dqkv = _attn_group_bwd(f"attn{l}_dcore{g}", qkv[g], do, stats, d)
        dp, dgain = _qknorm_bwd(f"attn{l}_dqknorm{g}", proj[g], dqkv, gain, is_norm, seg)
        dproj.append(dp)
        dwin.append(_matmul_tn(f"attn{l}_dwin{g}", h, dp))
        per_head = dgain.reshape(3, ATTN_HEADS, ATTN_HEAD_DIM).sum(axis=1)
        dqg.append(per_head[0] * ATTN_HEAD_DIM ** -0.5)
        dkg.append(per_head[1])
    grads["attn_w_in"][j] = jnp.concatenate(dwin, axis=1)
    grads["attn_q_gain"][j] = jnp.stack(dqg)
    grads["attn_k_gain"][j] = jnp.stack(dkg)
    dx, dg = _matmul_rmsnorm_bwd(f"attn{l}_dh", [(dproj[g], w_in[:, g * GW3:(g + 1) * GW3]) for g in range(3)],
                                 x, _row(w["mixer_norm"][l]), dx1)
    grads["mixer_norm"][l] = dg[0]
    return dx


def _conv_fwd(l, j, x, w):
    h = _rmsnorm_fwd(f"mix{l}_norm", x, _row(w["mixer_norm"][l]))
    w_in, b_in = w["conv_w_in"][j], _row(w["conv_b_in"][j])
    C = w_in.shape[1] // 2
    ua = _matmul(f"conv{l}_in0", [(h, w_in[:, :C])], bias=b_in[:, :C])
    ug = _matmul(f"conv{l}_in1", [(h, w_in[:, C:])], bias=b_in[:, C:])
    glu = _glu_fwd(f"conv{l}_glu", ua, ug)
    c = _dwconv(f"conv{l}_dw", glu, w["conv_dw_w"][j], _row(w["conv_dw_b"][j]), reverse=False)
    sw = _ln_silu_fwd(f"conv{l}_ln", c, _row(w["conv_ln_g"][j]), _row(w["conv_ln_b"][j]))
    x1 = _matmul(f"conv{l}_out", [(sw, w["conv_w_out"][j])], bias=_row(w["conv_b_out"][j]), residual=x)
    return x1, (x, h, ua, ug, glu, c, sw)


def _conv_bwd(l, j, dx1, saved, w, grads):
    x, h, ua, ug, glu, c, sw = saved
    w_in, w_out, dw_w = w["conv_w_in"][j], w["conv_w_out"][j], w["conv_dw_w"][j]
    C = w_out.shape[0]
    grads["conv_b_out"][j] = _column_sums(f"conv{l}_dbout", dx1)[0]
    grads["conv_w_out"][j] = _matmul_tn(f"conv{l}_dwout", sw, dx1)
    dsw = _matmul(f"conv{l}_dsw", [(dx1, w_out)], trans_b=True)
    dc, dlg, dlb = _ln_silu_bwd(f"conv{l}_dln", c, _row(w["conv_ln_g"][j]), _row(w["conv_ln_b"][j]), dsw)
    grads["conv_ln_g"][j], grads["conv_ln_b"][j] = dlg[0], dlb[0]
    dglu = _dwconv(f"conv{l}_ddw", dc, dw_w, jnp.zeros((1, C), F32), reverse=True)
    gw, gb = _dwconv_wgrad(f"conv{l}_ddww", glu, dc, dw_w.shape[0])
    grads["conv_dw_w"][j], grads["conv_dw_b"][j] = gw, gb[0]
    da, dgate, sa, sg = _glu_bwd(f"conv{l}_dglu", ua, ug, dglu)
    grads["conv_b_in"][j] = jnp.concatenate([sa, sg], axis=1)[0]
    grads["conv_w_in"][j] = jnp.concatenate(
        [_matmul_tn(f"conv{l}_dwin0", h, da), _matmul_tn(f"conv{l}_dwin1", h, dgate)], axis=1)
    dx, dg = _matmul_rmsnorm_bwd(f"conv{l}_dh", [(da, w_in[:, :C]), (dgate, w_in[:, C:])], x,
                                 _row(w["mixer_norm"][l]), dx1)
    grads["mixer_norm"][l] = dg[0]
    return dx


def _hgrn_fwd(l, j, x, w):
    h = _rmsnorm_fwd(f"mix{l}_norm", x, _row(w["mixer_norm"][l]))
    w_in = w["hgrn_w_in"][j]
    D = w_in.shape[1] // 4
    pq, pf, pv, pg = [_matmul(f"hgrn{l}_in{s}", [(h, w_in[:, s * D:(s + 1) * D])]) for s in range(4)]
    lb = _lower_bound_fwd(f"hgrn{l}_lb", w["hgrn_lb_logits"], l)
    o, ckpt = _hgrn_scan_fwd(f"hgrn{l}_scan", pq, pf, pv, lb)
    y = _hgrn_out_fwd(f"hgrn{l}_gate", o, pg, _row(w["hgrn_norm_g"][j]))
    x1 = _matmul(f"hgrn{l}_out", [(y, w["hgrn_w_out"][j])], residual=x)
    return x1, (x, h, pq, pf, pv, pg, lb, o, ckpt, y)


def _hgrn_bwd(l, j, dx1, saved, w, grads):
    x, h, pq, pf, pv, pg, lb, o, ckpt, y = saved
    w_in, w_out = w["hgrn_w_in"][j], w["hgrn_w_out"][j]
    D = w_out.shape[0]
    grads["hgrn_w_out"][j] = _matmul_tn(f"hgrn{l}_dwout", y, dx1)
    dy = _matmul(f"hgrn{l}_dy", [(dx1, w_out)], trans_b=True)
    do, dpg, dng = _hgrn_out_bwd(f"hgrn{l}_dgate", o, pg, _row(w["hgrn_norm_g"][j]), dy)
    grads["hgrn_norm_g"][j] = dng[0]
    dpq, dpf, dpv, dlb = _hgrn_scan_bwd(f"hgrn{l}_dscan", pq, pf, pv, lb, ckpt, do)
    grads["hgrn_lb_logits"] = grads["hgrn_lb_logits"] + _lower_bound_bwd(f"hgrn{l}_dlb", w["hgrn_lb_logits"], dlb, l)
    dps = [dpq, dpf, dpv, dpg]
    grads["hgrn_w_in"][j] = jnp.concatenate([_matmul_tn(f"hgrn{l}_dwin{s}", h, dps[s]) for s in range(4)], axis=1)
    dx, dg = _matmul_rmsnorm_bwd(f"hgrn{l}_dh", [(dps[s], w_in[:, s * D:(s + 1) * D]) for s in range(4)], x,
                                 _row(w["mixer_norm"][l]), dx1)
    grads["mixer_norm"][l] = dg[0]
    return dx


_MIXERS = ((_attn_fwd, _attn_bwd), (_conv_fwd, _conv_bwd), (_hgrn_fwd, _hgrn_bwd))
_PER_MIXER = {"attn": 0, "conv": 1, "hgrn": 2}


def _local_step(x, target, w):
    depth = w["mixer_norm"].shape[0]
    grads = {}
    for name, v in w.items():
        lead = v.shape[0]
        grads[name] = jnp.zeros(v.shape, F32) if name == "hgrn_lb_logits" else [None] * lead
    saved = []
    for l in range(depth):
        fwd, _ = _MIXERS[l % N_MIXERS]
        x, s_mix = fwd(l, l // N_MIXERS, x, w)
        x, s_ffn = _ffn_fwd(l, x, w)
        saved.append((s_mix, s_ffn))
    dx, loss_cols = _loss_grad("loss", x, target)
    for l in reversed(range(depth)):
        _, bwd = _MIXERS[l % N_MIXERS]
        s_mix, s_ffn = saved[l]
        dx = _ffn_bwd(l, dx, s_ffn, w, grads)
        dx = bwd(l, l // N_MIXERS, dx, s_mix, w, grads)
    grads = {k: (v if k == "hgrn_lb_logits" else jnp.stack(v)) for k, v in grads.items()}
    return jnp.sum(loss_cols), dx, grads


_HBM = pl.BlockSpec(memory_space=pltpu.HBM)


def _chip_peers():
    x, y, c = lax.axis_index("x"), lax.axis_index("y"), lax.axis_index("c")
    return 2 * x + y, (x, y, c), [(1 - x, y), (x, 1 - y), (1 - x, 1 - y)]


def _exchange_chips(name, src):
    def body(src_ref, out_ref, send_sems, recv_sems, local_sem):
        p, (x, y, c), peers = _chip_peers()
        mine = pltpu.make_async_copy(src_ref.at[p], out_ref.at[p], local_sem)
        mine.start()

        def copy(k, slab_from, slab_to, peer):
            return pltpu.make_async_remote_copy(
                src_ref=src_ref.at[slab_from], dst_ref=out_ref.at[slab_to], send_sem=send_sems.at[k],
                recv_sem=recv_sems.at[k], device_id=(peer[0], peer[1], c), device_id_type=MESH)

        sends = [copy(k, 2 * px + py, p, (px, py)) for k, (px, py) in enumerate(peers)]
        for s in sends:
            s.start()
        for k, (px, py) in enumerate(peers):
            copy(k, p, 2 * px + py, (px, py)).wait_recv()
        for s in sends:
            s.wait_send()
        mine.wait()

    return pl.pallas_call(
        body, name=name, in_specs=[_HBM], out_specs=_HBM, out_shape=jax.ShapeDtypeStruct(src.shape, src.dtype),
        scratch_shapes=[pltpu.SemaphoreType.DMA((3,)), pltpu.SemaphoreType.DMA((3,)), pltpu.SemaphoreType.DMA],
    )(src)


def _all_gather_chips(name, shard):
    R = shard.shape[0]
    half = R // 2

    def body(src_ref, out_ref, send_sems, recv_sems, local_sem):
        p, (x, y, c), peers = _chip_peers()
        mine = pltpu.make_async_copy(src_ref, out_ref.at[p], local_sem)
        mine.start()

        def rows(slab, core):
            return out_ref.at[slab, pl.ds(core * half, half), :]

        def over_ici(k, slab, peer):
            src = src_ref.at[pl.ds(c * half, half), :] if slab is None else rows(slab, c)
            return pltpu.make_async_remote_copy(
                src_ref=src, dst_ref=rows(p if slab is None else slab, c), send_sem=send_sems.at[k],
                recv_sem=recv_sems.at[k], device_id=(peer[0], peer[1], c), device_id_type=MESH)

        def to_sibling(k, slab, core):
            return pltpu.make_async_remote_copy(
                src_ref=rows(slab, core), dst_ref=rows(slab, core), send_sem=send_sems.at[3 + k],
                recv_sem=recv_sems.at[3 + k], device_id=(x, y, 1 - c), device_id_type=MESH)

        sends = [over_ici(k, None, peer) for k, peer in enumerate(peers)]
        for s in sends:
            s.start()
        passed = []
        for k, (px, py) in enumerate(peers):
            over_ici(k, 2 * px + py, (px, py)).wait_recv()
            passed.append(to_sibling(k, 2 * px + py, c))
            passed[k].start()
        for k, (px, py) in enumerate(peers):
            to_sibling(k, 2 * px + py, 1 - c).wait_recv()
        for s in sends + passed:
            s.wait_send()
        mine.wait()

    return pl.pallas_call(
        body, name=name, in_specs=[_HBM], out_specs=_HBM,
        out_shape=jax.ShapeDtypeStruct((N_CHIPS,) + shard.shape, shard.dtype),
        scratch_shapes=[pltpu.SemaphoreType.DMA((6,)), pltpu.SemaphoreType.DMA((6,)), pltpu.SemaphoreType.DMA],
    )(shard)


def _swap_cores(name, v):
    def body(v_ref, out_ref, send_sem, recv_sem):
        x, y, c = lax.axis_index("x"), lax.axis_index("y"), lax.axis_index("c")
        cp = pltpu.make_async_remote_copy(src_ref=v_ref, dst_ref=out_ref, send_sem=send_sem, recv_sem=recv_sem,
                                          device_id=(x, y, 1 - c), device_id_type=MESH)
        cp.start()
        cp.wait()

    return pl.pallas_call(
        body, name=name, in_specs=[_HBM], out_specs=_HBM, out_shape=jax.ShapeDtypeStruct(v.shape, v.dtype),
        scratch_shapes=[pltpu.SemaphoreType.DMA, pltpu.SemaphoreType.DMA],
    )(v)


_WEIGHTS = ("mixer_norm", "ffn_norm", "attn_w_in", "attn_q_gain", "attn_k_gain", "attn_w_out", "conv_w_in",
            "conv_b_in", "conv_dw_w", "conv_dw_b", "conv_ln_g", "conv_ln_b", "conv_w_out", "conv_b_out",
            "hgrn_w_in", "hgrn_lb_logits", "hgrn_norm_g", "hgrn_w_out", "ffn_w_up", "ffn_conv_w", "ffn_conv_b",
            "ffn_w_down")
_SHARD_AXIS = {"attn_w_in": 2, "attn_w_out": 2, "conv_w_in": 2, "conv_dw_w": 2, "conv_w_out": 1, "hgrn_w_in": 2,
               "hgrn_norm_g": 1, "hgrn_w_out": 1, "ffn_w_up": 2, "ffn_conv_w": 2, "ffn_w_down": 1}
_MATMUL_WEIGHTS = ("attn_w_in", "attn_w_out", "conv_w_in", "conv_w_out", "hgrn_w_in", "hgrn_w_out", "ffn_w_up",
                   "ffn_w_down")
PACK_COLS = 1024
PACK_ROWS = 512


def _pack(arrays, nlead, dtype):
    lead = arrays[0].shape[:nlead]
    flat = []
    for a in arrays:
        f = a.reshape(lead + (-1,)).astype(dtype)
        flat.append(jnp.pad(f, [(0, 0)] * nlead + [(0, (-f.shape[-1]) % PACK_COLS)]))
    buf = jnp.concatenate(flat, axis=-1)
    buf = jnp.pad(buf, [(0, 0)] * nlead + [(0, (-buf.shape[-1]) % (PACK_COLS * PACK_ROWS))])
    return buf.reshape(lead + (-1, PACK_COLS))


def _unpack(buf, shapes, nlead):
    lead = buf.shape[:nlead]
    flat = buf.reshape(lead + (-1,))
    out, off = [], 0
    for shape in shapes:
        n = 1
        for s in shape:
            n *= s
        out.append(flat[..., off:off + n].reshape(lead + tuple(shape)))
        off += n + (-n) % PACK_COLS
    return out


def _merge_shards(piece, axis):
    moved = jnp.moveaxis(piece, 0, axis)
    shape = moved.shape
    return moved.reshape(shape[:axis] + (shape[axis] * shape[axis + 1],) + shape[axis + 2:])


def _split_shards(full, axis):
    shape = full.shape
    cut = full.reshape(shape[:axis] + (N_CHIPS, shape[axis] // N_CHIPS) + shape[axis + 1:])
    return jnp.moveaxis(cut, axis, 0)


def _gather_weights(local):
    big = [n for n in _WEIGHTS if n in _MATMUL_WEIGHTS]
    small = [n for n in _WEIGHTS if n in _SHARD_AXIS and n not in _MATMUL_WEIGHTS]
    full = {n: local[n] for n in _WEIGHTS if n not in _SHARD_AXIS}
    for names, dtype, tag in ((big, BF16, "comm_gather_matmul_weights"), (small, F32, "comm_gather_small_weights")):
        gathered = _all_gather_chips(tag, _pack([local[n] for n in names], 0, dtype))
        pieces = _unpack(gathered, [local[n].shape for n in names], 1)
        for n, piece in zip(names, pieces):
            full[n] = _merge_shards(piece, _SHARD_AXIS[n])
    return full


def _reduce_gradients(grads, local):
    out = {}
    big = [n for n in _WEIGHTS if n in _MATMUL_WEIGHTS]
    rest = [n for n in _WEIGHTS if n not in _MATMUL_WEIGHTS]
    for names, dtype, tag in ((big, BF16, "matmul"), (rest, F32, "small")):
        slabs = []
        for n in names:
            g = grads[n]
            if n in _SHARD_AXIS:
                slabs.append(_split_shards(g, _SHARD_AXIS[n]))
            else:
                slabs.append(jnp.broadcast_to(g[None], (N_CHIPS,) + g.shape))
        packed = _pack(slabs, 1, dtype)
        landed = _exchange_chips(f"comm_scatter_{tag}_gradients", packed)
        partial = _sum_slabs(f"sum_chips_{tag}", landed)
        other = _swap_cores(f"comm_swap_{tag}_sums", partial)
        total = _add(f"sum_cores_{tag}", [partial, other])
        out.update(zip(names, _unpack(total, [local[n].shape for n in names], 0)))
    return out


def kernel(x, mixer_norm, ffn_norm, attn_w_in, attn_q_gain, attn_k_gain, attn_w_out, conv_w_in, conv_b_in, conv_dw_w, conv_dw_b, conv_ln_g, conv_ln_b, conv_w_out, conv_b_out, hgrn_w_in, hgrn_lb_logits, hgrn_norm_g, hgrn_w_out, ffn_w_up, ffn_conv_w, ffn_conv_b, ffn_w_down, loss_target, m_mixer_norm, m_ffn_norm, m_attn_w_in, m_attn_q_gain, m_attn_k_gain, m_attn_w_out, m_conv_w_in, m_conv_b_in, m_conv_dw_w, m_conv_dw_b, m_conv_ln_g, m_conv_ln_b, m_conv_w_out, m_conv_b_out, m_hgrn_w_in, m_hgrn_lb_logits, m_hgrn_norm_g, m_hgrn_w_out, m_ffn_w_up, m_ffn_conv_w, m_ffn_conv_b, m_ffn_w_down, v_mixer_norm, v_ffn_norm, v_attn_w_in, v_attn_q_gain, v_attn_k_gain, v_attn_w_out, v_conv_w_in, v_conv_b_in, v_conv_dw_w, v_conv_dw_b, v_conv_ln_g, v_conv_ln_b, v_conv_w_out, v_conv_b_out, v_hgrn_w_in, v_hgrn_lb_logits, v_hgrn_norm_g, v_hgrn_w_out, v_ffn_w_up, v_ffn_conv_w, v_ffn_conv_b, v_ffn_w_down):
    given = dict(locals())
    local = {n: given[n] for n in _WEIGHTS}
    full = _gather_weights(local)
    loss, dx, grads = _local_step(x[0], loss_target[0], full)
    loss = lax.psum(loss, ("x", "y", "c"))
    grad = _reduce_gradients(grads, local)
    delta, new_m, new_v = {}, {}, {}
    for n in _WEIGHTS:
        shape = local[n].shape
        as2d = lambda a: a.reshape(-1, shape[-1])
        d, m, v = _adamw(f"adamw_{n}", as2d(local[n]), as2d(grad[n]), as2d(given["m_" + n]), as2d(given["v_" + n]))
        delta[n], new_m[n], new_v[n] = d.reshape(shape), m.reshape(shape), v.reshape(shape)
    return (loss, dx[None], *[grad[n] for n in _WEIGHTS], *[delta[n] for n in _WEIGHTS],
            *[new_m[n] for n in _WEIGHTS], *[new_v[n] for n in _WEIGHTS])
```

```python
import functools

import jax
import jax.numpy as jnp
from jax import lax
from jax.experimental import pallas as pl
from jax.experimental.pallas import tpu as pltpu

F32 = jnp.float32
BF16 = jnp.bfloat16

EPS = 1e-6
N_MIXERS = 3
ATTN_DILATIONS = (1, 4, 16)
ATTN_BLOCK = 128
ATTN_HEADS = 8
ATTN_HEAD_DIM = 64
ATTN_GW = ATTN_HEADS * ATTN_HEAD_DIM
ATTN_ROWS = 32
HGRN_HEAD = 128
HGRN_CHUNK = 16
HGRN_TILE = 256
HGRN_GROUP = 8
ADAM_LR, ADAM_B1, ADAM_B2, ADAM_EPS, ADAM_WD, ADAM_STEP = 0.001, 0.9, 0.999, 1e-08, 0.01, 10

LANES = 128
SUBLANES = 8
VMEM_LIMIT = 56 * 1024 * 1024
N_CHIPS = 4
MESH = pl.DeviceIdType.MESH

HI = lax.Precision.HIGHEST


def _params(*sem):
    return pltpu.CompilerParams(dimension_semantics=sem, vmem_limit_bytes=VMEM_LIMIT)


def _pick(n, target):
    if n <= target:
        return n
    best = None
    for t in range(LANES, target + 1, LANES):
        if n % t == 0:
            best = t
    assert best is not None, (n, target)
    return best


def _pick_rows(n, target):
    if n <= target:
        return n
    for t in range(target, 15, -16):
        if n % t == 0:
            return t
    return n


def _dot(a, b, dims, precision=None):
    return lax.dot_general(a, b, (dims, ((), ())), precision=precision, preferred_element_type=F32)


def _nn(a, b, precision=None):
    return _dot(a, b, ((1,), (0,)), precision)


def _nt(a, b, precision=None):
    return _dot(a, b, ((1,), (1,)), precision)


def _tn(a, b, precision=None):
    return _dot(a, b, ((0,), (0,)), precision)


def _sigmoid(x):
    return 1.0 / (1.0 + jnp.exp(-x))


ROWWISE_UNROLL_ROWS = 64


def _rowwise(name, fn, rows, pars=(), outs=(), accs=(), *, tc=None, tm=512, rb=16):
    S = rows[0][0].shape[0]
    tm = _pick_rows(S, tm)
    rb = rb if tm % rb == 0 else tm
    width = tc if tc is not None else None
    ncol = 1
    if tc is not None:
        base = outs[0][0] if outs else accs[0][1]
        ncol = base // tc
    n_r, n_p, n_o, n_a = len(rows), len(pars), len(outs), len(accs)

    def body(*refs):
        row_refs, par_refs = refs[:n_r], refs[n_r:n_r + n_p]
        out_refs, acc_refs = refs[n_r + n_p:n_r + n_p + n_o], refs[n_r + n_p + n_o:]
        if n_a:
            @pl.when(pl.program_id(1) == 0)
            def _():
                for a in acc_refs:
                    a[...] = jnp.zeros_like(a)

        def step(s, carry):
            sl = pl.ds(pl.multiple_of(s * rb, rb), rb)
            res = fn(*[r[sl, :] for r in row_refs], *[p[...] for p in par_refs])
            res = res if isinstance(res, tuple) else (res,)
            for o, v in zip(out_refs, res[:n_o]):
                o[sl, :] = v.astype(o.dtype)
            for a, v in zip(acc_refs, res[n_o:]):
                a[...] += v
            return carry

        lax.fori_loop(0, tm // rb, step, 0, unroll=min(tm // rb, max(2, ROWWISE_UNROLL_ROWS // rb)))

    def row_spec(c, off):
        if tc is None:
            return pl.BlockSpec((tm, c), lambda j, i: (i, 0))
        return pl.BlockSpec((tm, tc), lambda j, i, o=off // tc: (i, j + o))

    def par_spec(shape, off):
        if off is None or tc is None:
            return pl.BlockSpec(shape, lambda j, i: (0, 0))
        return pl.BlockSpec((shape[0], tc), lambda j, i, o=off // tc: (0, j + o))

    in_specs = [row_spec(a.shape[1], off) for a, off in rows]
    in_specs += [par_spec(a.shape, off) for a, off in pars]
    out_specs = [row_spec(c, 0) for c, _ in outs] + [par_spec(s, 0) for s in accs]
    out_shape = [jax.ShapeDtypeStruct((S, c), d) for c, d in outs]
    out_shape += [jax.ShapeDtypeStruct(s, F32) for s in accs]
    res = pl.pallas_call(
        body, name=name, grid=(ncol, S // tm), in_specs=in_specs, out_specs=out_specs, out_shape=out_shape,
        compiler_params=_params("parallel", "arbitrary" if n_a else "parallel"),
    )(*[a for a, _ in rows], *[a for a, _ in pars])
    return res[0] if len(res) == 1 else tuple(res)


MATMUL_VMEM = 36 * 1024 * 1024


def _matmul_tiles(M, N, pairs, out_dtype, residual):
    tm = _pick_rows(M, 512)
    for tn in sorted({_pick(N, t) for t in range(LANES, 2049, LANES)}, reverse=True):
        step = sum(tm * a.shape[1] * a.dtype.itemsize + a.shape[1] * tn * b.dtype.itemsize for a, b in pairs)
        step += tm * tn * (jnp.dtype(out_dtype).itemsize + (4 if residual is not None else 0))
        if 2 * step <= MATMUL_VMEM:
            return tm, tn
    return tm, LANES


def _matmul(name, pairs, *, trans_b=False, bias=None, residual=None, out_dtype=F32):
    M = pairs[0][0].shape[0]
    N = pairs[0][1].shape[0] if trans_b else pairs[0][1].shape[1]
    tm, tn = _matmul_tiles(M, N, pairs, out_dtype, residual)
    n = len(pairs)

    def body(*refs):
        acc = None
        for i in range(n):
            a = refs[2 * i][...].astype(BF16)
            b = refs[2 * i + 1][...].astype(BF16)
            d = _nt(a, b) if trans_b else _nn(a, b)
            acc = d if acc is None else acc + d
        k = 2 * n
        if bias is not None:
            acc = acc + refs[k][...]
            k += 1
        if residual is not None:
            acc = acc + refs[k][...]
            k += 1
        refs[k][...] = acc.astype(out_dtype)

    in_specs, args = [], []
    for a, b in pairs:
        K = a.shape[1]
        in_specs.append(pl.BlockSpec((tm, K), lambda j, i: (i, 0)))
        in_specs.append(pl.BlockSpec((tn, K), lambda j, i: (j, 0)) if trans_b
                        else pl.BlockSpec((K, tn), lambda j, i: (0, j)))
        args += [a, b]
    if bias is not None:
        in_specs.append(pl.BlockSpec((1, tn), lambda j, i: (0, j)))
        args.append(bias)
    if residual is not None:
        in_specs.append(pl.BlockSpec((tm, tn), lambda j, i: (i, j)))
        args.append(residual)
    return pl.pallas_call(
        body, name=name, grid=(N // tn, M // tm), in_specs=in_specs,
        out_specs=pl.BlockSpec((tm, tn), lambda j, i: (i, j)),
        out_shape=jax.ShapeDtypeStruct((M, N), out_dtype), compiler_params=_params("parallel", "parallel"),
    )(*args)


def _matmul_tn(name, a, b, *, tm=1408, tn=1408, tk=1024):
    S, M = a.shape
    N = b.shape[1]
    tm, tn, tk = _pick(M, tm), _pick(N, tn), _pick_rows(S, tk)

    def body(a_ref, b_ref, o_ref):
        @pl.when(pl.program_id(2) == 0)
        def _():
            o_ref[...] = jnp.zeros_like(o_ref)

        o_ref[...] += _tn(a_ref[...].astype(BF16), b_ref[...].astype(BF16))

    return pl.pallas_call(
        body, name=name, grid=(M // tm, N // tn, S // tk),
        in_specs=[pl.BlockSpec((tk, tm), lambda i, j, k: (k, i)), pl.BlockSpec((tk, tn), lambda i, j, k: (k, j))],
        out_specs=pl.BlockSpec((tm, tn), lambda i, j, k: (i, j)),
        out_shape=jax.ShapeDtypeStruct((M, N), F32), compiler_params=_params("parallel", "parallel", "arbitrary"),
    )(a, b)


def _halo_rows(K):
    return 8 if K <= 9 else 32


def _shifted_copies(ext, shifted, K):
    if K <= SUBLANES:
        return
    n = shifted.shape[1]
    for s in range(1, SUBLANES):
        shifted[s, 0:n, :] = ext[s:s + n, :]


def _window(ext, shifted, K, off, rows):
    s = off % SUBLANES
    if K <= SUBLANES or s == 0:
        return ext[off:off + rows, :]
    return shifted[s, off - s:off - s + rows, :]


def _dwconv(name, x, w, b, *, reverse, out_dtype=F32):
    S, C = x.shape
    K = w.shape[0]
    H = _halo_rows(K)
    tm, tc = _pick_rows(S, 512 if K <= 4 else 256), _pick(C, 1408 if K <= 4 else 256)
    nrow = S // tm
    RB = 16 if out_dtype == BF16 else 8

    def body(x_ref, h_ref, w_ref, b_ref, o_ref, ext, shifted):
        i = pl.program_id(1)
        edge = (i == nrow - 1) if reverse else (i == 0)
        halo = jnp.where(edge, 0.0, h_ref[...].astype(F32))
        if reverse:
            ext[0:tm, :] = x_ref[...].astype(F32)
            ext[tm:tm + H, :] = halo
        else:
            ext[0:H, :] = halo
            ext[H:H + tm, :] = x_ref[...].astype(F32)
        _shifted_copies(ext, shifted, K)
        wv = w_ref[...]
        for s in range(tm // RB):
            acc = jnp.broadcast_to(b_ref[...], (RB, tc))
            for k in range(K):
                off = s * RB + ((K - 1 - k) if reverse else (H - (K - 1) + k))
                acc = acc + wv[k:k + 1, :] * _window(ext, shifted, K, off, RB)
            o_ref[s * RB:(s + 1) * RB, :] = acc.astype(out_dtype)

    r = tm // H
    if reverse:
        halo_map = lambda j, i: (jnp.minimum((i + 1) * r, S // H - 1), j)
    else:
        halo_map = lambda j, i: (jnp.maximum(i * r - 1, 0), j)
    return pl.pallas_call(
        body, name=name, grid=(C // tc, nrow),
        in_specs=[pl.BlockSpec((tm, tc), lambda j, i: (i, j)), pl.BlockSpec((H, tc), halo_map),
                  pl.BlockSpec((K, tc), lambda j, i: (0, j)), pl.BlockSpec((1, tc), lambda j, i: (0, j))],
        out_specs=pl.BlockSpec((tm, tc), lambda j, i: (i, j)),
        out_shape=jax.ShapeDtypeStruct((S, C), out_dtype),
        scratch_shapes=[pltpu.VMEM((tm + H, tc), F32), pltpu.VMEM((SUBLANES, tm + H - SUBLANES, tc), F32)],
        compiler_params=_params("parallel", "parallel"),
    )(x, x, w, b)


def _dwconv_wgrad(name, x, dy, K):
    S, C = x.shape
    H = _halo_rows(K)
    tm, tc = _pick_rows(S, 512 if K <= 4 else 256), _pick(C, 512 if K <= 4 else LANES)
    RB = 8

    def body(x_ref, h_ref, dy_ref, dw_ref, db_ref, ext, shifted):
        i = pl.program_id(1)

        @pl.when(i == 0)
        def _():
            dw_ref[...] = jnp.zeros_like(dw_ref)
            db_ref[...] = jnp.zeros_like(db_ref)

        ext[0:H, :] = jnp.where(i == 0, 0.0, h_ref[...].astype(F32))
        ext[H:H + tm, :] = x_ref[...].astype(F32)
        _shifted_copies(ext, shifted, K)
        acc = [jnp.zeros((RB, tc), F32) for _ in range(K)]
        accb = jnp.zeros((RB, tc), F32)
        for s in range(tm // RB):
            d = dy_ref[s * RB:(s + 1) * RB, :].astype(F32)
            accb = accb + d
            for k in range(K):
                off = s * RB + H - (K - 1) + k
                acc[k] = acc[k] + d * _window(ext, shifted, K, off, RB)
        for k in range(K):
            dw_ref[k:k + 1, :] += jnp.sum(acc[k], axis=0, keepdims=True)
        db_ref[...] += jnp.sum(accb, axis=0, keepdims=True)

    r = tm // H
    return pl.pallas_call(
        body, name=name, grid=(C // tc, S // tm),
        in_specs=[pl.BlockSpec((tm, tc), lambda j, i: (i, j)),
                  pl.BlockSpec((H, tc), lambda j, i: (jnp.maximum(i * r - 1, 0), j)),
                  pl.BlockSpec((tm, tc), lambda j, i: (i, j))],
        out_specs=[pl.BlockSpec((K, tc), lambda j, i: (0, j)), pl.BlockSpec((1, tc), lambda j, i: (0, j))],
        out_shape=[jax.ShapeDtypeStruct((K, C), F32), jax.ShapeDtypeStruct((1, C), F32)],
        scratch_shapes=[pltpu.VMEM((tm + H, tc), F32), pltpu.VMEM((SUBLANES, tm + H - SUBLANES, tc), F32)],
        compiler_params=_params("parallel", "arbitrary"),
    )(x, x, dy)


FFN_HALO = 16


def _row_shifts(ref, r0, rows, cs, shifts):
    n = rows // SUBLANES
    lo = -1 if max(shifts) > 0 else 0
    hi = n + (1 if min(shifts) < 0 else 0)
    v = {j: ref[r0 + j * SUBLANES:r0 + (j + 1) * SUBLANES, cs] for j in range(lo, hi)}
    row = lax.broadcasted_iota(jnp.int32, (SUBLANES, LANES), 0)
    out = {}
    for s in shifts:
        if s == 0:
            pieces = [v[j] for j in range(n)]
        elif s > 0:
            rot = {j: pltpu.roll(v[j], s, axis=0) for j in range(-1, n)}
            pieces = [jnp.where(row < s, rot[j - 1], rot[j]) for j in range(n)]
        else:
            rot = {j: pltpu.roll(v[j], SUBLANES + s, axis=0) for j in range(0, n + 1)}
            pieces = [jnp.where(row < SUBLANES + s, rot[j], rot[j + 1]) for j in range(n)]
        out[s] = jnp.concatenate(pieces, axis=0)
    return out


def _conv_taps(w, b, ext, r0, rows, cs):
    K = w.shape[0]
    win = _row_shifts(ext, r0, rows, cs, list(range(K)))
    acc = b
    for k in range(K):
        acc = acc + w[k:k + 1, :] * win[K - 1 - k]
    return acc


def _ffn_up_fused(name, h, w_up, cw, cb):
    S, D = h.shape
    F = w_up.shape[1] // 2
    tm, tn = _pick_rows(S, 512), _pick(F, 1408)
    nj, H, RB = F // tn, FFN_HALO, 64

    def body(h_ref, hh_ref, wg_ref, wu_ref, cwg_ref, cwu_ref, cbg_ref, cbu_ref,
             a_ref, u0g_ref, u0u_ref, ug_ref, uu_ref, eg, eu):
        first = pl.program_id(1) == 0
        hv, halo = h_ref[...], hh_ref[...]
        for w_ref, u0_ref, e in ((wg_ref, u0g_ref, eg), (wu_ref, u0u_ref, eu)):
            w = w_ref[...]
            u0 = _nn(hv, w)
            u0_ref[...] = u0.astype(BF16)
            e[0:H, :] = jnp.where(first, 0.0, _nn(halo, w))
            e[H:H + tm, :] = u0
        for c in range(tn // LANES):
            cs = slice(c * LANES, (c + 1) * LANES)
            wg, wu, bg, bu = cwg_ref[:, cs], cwu_ref[:, cs], cbg_ref[:, cs], cbu_ref[:, cs]
            for s in range(tm // RB):
                rows = slice(s * RB, (s + 1) * RB)
                ug = _conv_taps(wg, bg, eg, H + s * RB, RB, cs)
                uu = _conv_taps(wu, bu, eu, H + s * RB, RB, cs)
                ug_ref[rows, cs] = ug.astype(BF16)
                uu_ref[rows, cs] = uu.astype(BF16)
                a_ref[rows, cs] = (ug * _sigmoid(ug) * uu).astype(BF16)

    r = tm // H
    gate = lambda rows: pl.BlockSpec((rows, tn), lambda j, i: (0, j))
    up = lambda rows: pl.BlockSpec((rows, tn), lambda j, i: (0, j + nj))
    tile = pl.BlockSpec((tm, tn), lambda j, i: (i, j))
    K = cw.shape[0]
    return pl.pallas_call(
        body, name=name, grid=(nj, S // tm),
        in_specs=[pl.BlockSpec((tm, D), lambda j, i: (i, 0)),
                  pl.BlockSpec((H, D), lambda j, i: (jnp.maximum(i * r - 1, 0), 0)),
                  gate(D), up(D), gate(K), up(K), gate(1), up(1)],
        out_specs=[tile] * 5, out_shape=[jax.ShapeDtypeStruct((S, F), BF16)] * 5,
        scratch_shapes=[pltpu.VMEM((H + tm, tn), F32)] * 2, compiler_params=_params("parallel", "parallel"),
    )(h, h, w_up, w_up, cw, cw, cb, cb)


def _ffn_gate_bwd_fused(name, dy, w_down, u0g, u0u, ug, uu, cw):
    S, D = dy.shape
    F, K = w_down.shape[0], cw.shape[0]
    tm, tn = _pick_rows(S, 512), _pick(F, 1408)
    nj, nrow, H, RB = F // tn, S // tm, FFN_HALO, 16
    RW = 32

    def body(dy_ref, dyn_ref, wd_ref, u0g_ref, u0u_ref, g_ref, gn_ref, u_ref, un_ref, cwg_ref, cwu_ref,
             dg_ref, du_ref, dcwg_ref, dcwu_ref, dcbg_ref, dcbu_ref, dg_s, du_s, da_s):
        i = pl.program_id(1)
        last = i == nrow - 1

        @pl.when(i == 0)
        def _():
            for ref in (dcwg_ref, dcwu_ref, dcbg_ref, dcbu_ref):
                ref[...] = jnp.zeros_like(ref)

        wd = wd_ref[...]
        da_s[0:tm, :] = _nt(dy_ref[...].astype(BF16), wd)
        da_s[tm:tm + H, :] = jnp.where(last, 0.0, _nt(dyn_ref[...].astype(BF16), wd))
        for c in range(tn // LANES):
            cs = slice(c * LANES, (c + 1) * LANES)
            for s in range(tm // RB + 1):
                rows = slice(s * RB, (s + 1) * RB)
                src_g, src_u, src_rows = (g_ref, u_ref, rows) if s < tm // RB else (gn_ref, un_ref, slice(0, RB))
                gv, uv = src_g[src_rows, cs].astype(F32), src_u[src_rows, cs].astype(F32)
                da = da_s[rows, cs]
                sg = _sigmoid(gv)
                dg_s[rows, cs] = da * uv * (sg * (1.0 + gv * (1.0 - sg)))
                du_s[rows, cs] = da * (gv * sg)
            for d_s, u0_ref, cw_ref, out_ref, dcw_ref, dcb_ref in (
                    (dg_s, u0g_ref, cwg_ref, dg_ref, dcwg_ref, dcbg_ref),
                    (du_s, u0u_ref, cwu_ref, du_ref, dcwu_ref, dcbu_ref)):
                w = cw_ref[:, cs]
                acc = [jnp.zeros((RW, LANES), F32) for _ in range(K)]
                accb = jnp.zeros((RW, LANES), F32)
                for s in range(tm // RW):
                    r0 = s * RW
                    u0 = u0_ref[r0:r0 + RW, cs].astype(F32)
                    ahead = _row_shifts(d_s, r0, RW, cs, [-m for m in range(K)])
                    t = None
                    for k in range(K):
                        m = K - 1 - k
                        win = ahead[-m]
                        if m == 0:
                            accb = accb + win
                        acc[k] = acc[k] + win * u0
                        term = w[k:k + 1, :] * win
                        t = term if t is None else t + term
                    out_ref[r0:r0 + RW, cs] = t.astype(BF16)
                for k in range(K):
                    dcw_ref[k:k + 1, cs] += jnp.sum(acc[k], axis=0, keepdims=True)
                dcb_ref[:, cs] += jnp.sum(accb, axis=0, keepdims=True)

    r = tm // H
    gate = lambda rows: pl.BlockSpec((rows, tn), lambda j, i: (0, j))
    up = lambda rows: pl.BlockSpec((rows, tn), lambda j, i: (0, j + nj))
    tile = pl.BlockSpec((tm, tn), lambda j, i: (i, j))
    nxt = pl.BlockSpec((H, tn), lambda j, i: (jnp.minimum((i + 1) * r, S // H - 1), j))
    acc_w, acc_b = pl.BlockSpec((K, tn), lambda j, i: (0, j)), pl.BlockSpec((1, tn), lambda j, i: (0, j))
    return pl.pallas_call(
        body, name=name, grid=(nj, nrow),
        in_specs=[pl.BlockSpec((tm, D), lambda j, i: (i, 0)),
                  pl.BlockSpec((H, D), lambda j, i: (jnp.minimum((i + 1) * r, S // H - 1), 0)),
                  pl.BlockSpec((tn, D), lambda j, i: (j, 0)),
                  tile, tile, tile, nxt, tile, nxt, gate(K), up(K)],
        out_specs=[tile, tile, acc_w, acc_w, acc_b, acc_b],
        out_shape=[jax.ShapeDtypeStruct((S, F), BF16)] * 2 + [jax.ShapeDtypeStruct((K, F), F32)] * 2
        + [jax.ShapeDtypeStruct((1, F), F32)] * 2,
        scratch_shapes=[pltpu.VMEM((tm + H, tn), F32)] * 3, compiler_params=_params("parallel", "arbitrary"),
    )(dy, dy, w_down, u0g, u0u, ug, ug, uu, uu, cw, cw)


def _colsum(v):
    return jnp.sum(v, axis=0, keepdims=True)


def _rmsnorm_fwd(name, x, gain):
    def fn(x, g):
        r = lax.rsqrt(jnp.mean(x * x, axis=-1, keepdims=True) + EPS)
        return x * r * g
    return _rowwise(name, fn, [(x, 0)], [(gain, None)], [(x.shape[1], BF16)])


NORM_BWD_VMEM = 44 * 1024 * 1024


def _matmul_rmsnorm_bwd(name, pairs, x, gain, dres):
    M, D = x.shape
    n = len(pairs)
    weights = sum(b.shape[0] * b.shape[1] * b.dtype.itemsize for _, b in pairs)
    for tm in (512, 256, 128):
        step = sum(tm * a.shape[1] * a.dtype.itemsize for a, _ in pairs) + 3 * tm * D * 4
        if 2 * (step + weights) + tm * D * 4 <= NORM_BWD_VMEM:
            break
    RB = 16

    def body(*refs):
        x_ref, g_ref, r_ref, o_ref, dg_ref, dh_s = refs[2 * n:]

        @pl.when(pl.program_id(0) == 0)
        def _():
            dg_ref[...] = jnp.zeros_like(dg_ref)

        acc = None
        for i in range(n):
            d = _nt(refs[2 * i][...].astype(BF16), refs[2 * i + 1][...].astype(BF16))
            acc = d if acc is None else acc + d
        dh_s[...] = acc
        g = g_ref[...]

        def step(s, carry):
            sl = pl.ds(pl.multiple_of(s * RB, RB), RB)
            xv, dh = x_ref[sl, :], dh_s[sl, :]
            r = lax.rsqrt(jnp.mean(xv * xv, axis=-1, keepdims=True) + EPS)
            xh = xv * r
            dxh = dh * g
            o_ref[sl, :] = r_ref[sl, :] + r * (dxh - xh * jnp.mean(dxh * xh, axis=-1, keepdims=True))
            dg_ref[...] += _colsum(dh * xh)
            return carry

        lax.fori_loop(0, tm // RB, step, 0, unroll=ROWWISE_UNROLL_ROWS // RB)

    in_specs, args = [], []
    for a, b in pairs:
        in_specs += [pl.BlockSpec((tm, a.shape[1]), lambda i: (i, 0)), pl.BlockSpec(b.shape, lambda i: (0, 0))]
        args += [a, b]
    rows = pl.BlockSpec((tm, D), lambda i: (i, 0))
    vec = pl.BlockSpec((1, D), lambda i: (0, 0))
    return pl.pallas_call(
        body, name=name, grid=(M // tm,), in_specs=in_specs + [rows, vec, rows], out_specs=[rows, vec],
        out_shape=[jax.ShapeDtypeStruct((M, D), F32), jax.ShapeDtypeStruct((1, D), F32)],
        scratch_shapes=[pltpu.VMEM((tm, D), F32)], compiler_params=_params("arbitrary"),
    )(*args, x, gain, dres)


def _silu_gate_fwd(name, gate, up):
    F = gate.shape[1]
    def fn(g, up):
        return g * _sigmoid(g) * up
    return _rowwise(name, fn, [(gate, 0), (up, 0)], [], [(F, BF16)], tc=_pick(F, 512))


def _silu_gate_bwd(name, gate, up, da):
    F = gate.shape[1]
    def fn(g, up, da):
        s = _sigmoid(g)
        return da * up * (s * (1.0 + g * (1.0 - s))), da * (g * s)
    return _rowwise(name, fn, [(gate, 0), (up, 0), (da, 0)], [], [(F, F32), (F, F32)], tc=_pick(F, 512))


def _glu_fwd(name, a, gate):
    C = a.shape[1]
    def fn(a, g):
        return a * _sigmoid(g)
    return _rowwise(name, fn, [(a, 0), (gate, 0)], [], [(C, F32)], tc=_pick(C, 512))


def _glu_bwd(name, a, gate, dglu):
    C = a.shape[1]
    def fn(a, g, d):
        s = _sigmoid(g)
        da, dg = d * s, d * a * s * (1.0 - s)
        return da, dg, _colsum(da), _colsum(dg)
    return _rowwise(name, fn, [(a, 0), (gate, 0), (dglu, 0)], [], [(C, BF16), (C, BF16)], [(1, C), (1, C)],
                    tc=_pick(C, 512))


def _ln_silu_fwd(name, c, g, b):
    def fn(c, g, b):
        mu = jnp.mean(c, axis=-1, keepdims=True)
        d = c - mu
        n = d * lax.rsqrt(jnp.mean(d * d, axis=-1, keepdims=True) + EPS) * g + b
        return n * _sigmoid(n)
    return _rowwise(name, fn, [(c, 0)], [(g, None), (b, None)], [(c.shape[1], BF16)])


def _ln_silu_bwd(name, c, g, b, dsw):
    def fn(c, dsw, g, b):
        mu = jnp.mean(c, axis=-1, keepdims=True)
        d = c - mu
        r = lax.rsqrt(jnp.mean(d * d, axis=-1, keepdims=True) + EPS)
        ch = d * r
        n = ch * g + b
        s = _sigmoid(n)
        dn = dsw * (s * (1.0 + n * (1.0 - s)))
        dch = dn * g
        dc = r * (dch - jnp.mean(dch, axis=-1, keepdims=True) - ch * jnp.mean(dch * ch, axis=-1, keepdims=True))
        return dc, _colsum(dn * ch), _colsum(dn)
    C = c.shape[1]
    return _rowwise(name, fn, [(c, 0), (dsw, 0)], [(g, None), (b, None)], [(C, F32)], [(1, C), (1, C)])


def _column_sums(name, x):
    return _rowwise(name, lambda x: (_colsum(x),), [(x, 0)], [], [], [(1, x.shape[1])])


def _loss_grad(name, y, target):
    D = y.shape[1]
    def fn(y, t):
        e = y - t
        return e * (1.0 / D), _colsum(e * e) * (0.5 / D)
    return _rowwise(name, fn, [(y, 0), (target, 0)], [], [(D, F32)], [(1, D)])


def _add(name, arrays):
    def fn(*xs):
        acc = xs[0]
        for x in xs[1:]:
            acc = acc + x
        return acc
    return _rowwise(name, fn, [(a, 0) for a in arrays], [], [(arrays[0].shape[1], F32)], tm=256)


def _sum_slabs(name, stacked):
    n, R, C = stacked.shape
    tm = _pick_rows(R, 256)

    def body(*refs):
        acc = refs[0][0].astype(F32)
        for r in refs[1:n]:
            acc = acc + r[0].astype(F32)
        refs[n][...] = acc

    return pl.pallas_call(
        body, name=name, grid=(R // tm,),
        in_specs=[pl.BlockSpec((1, tm, C), lambda i, q=q: (q, i, 0)) for q in range(n)],
        out_specs=pl.BlockSpec((tm, C), lambda i: (i, 0)), out_shape=jax.ShapeDtypeStruct((R, C), F32),
        compiler_params=_params("parallel"),
    )(*[stacked] * n)


def _adamw(name, w, g, m, v):
    c1 = 1.0 - ADAM_B1 ** ADAM_STEP
    c2 = 1.0 - ADAM_B2 ** ADAM_STEP
    def fn(w, g, m, v):
        m = ADAM_B1 * m + (1.0 - ADAM_B1) * g
        v = ADAM_B2 * v + (1.0 - ADAM_B2) * (g * g)
        delta = -ADAM_LR * ((m / c1) / (jnp.sqrt(v / c2) + ADAM_EPS) + ADAM_WD * w)
        return delta, m, v
    C = w.shape[1]
    return _rowwise(name, fn, [(w, 0), (g, 0), (m, 0), (v, 0)], [], [(C, F32)] * 3, tm=256)


SEG_ROWS = 128


def _segment_matrix(n, seg):
    i = jnp.arange(n) // seg
    return (i[:, None] == i[None, :]).astype(BF16)


def _seg_sum(v, B):
    hi = v.astype(BF16)
    lo = (v - hi.astype(F32)).astype(BF16)
    n = B.shape[0]
    slabs = [slice(c, c + n) for c in range(0, v.shape[1], n)]
    return jnp.concatenate([_nn(hi[:, c], B) + _nn(lo[:, c], B) for c in slabs], axis=1)


def _qknorm_fwd(name, proj, gain_full, is_norm, seg):
    def fn(x, gf, isn, B):
        ms = _seg_sum(x * x, B) * (1.0 / ATTN_HEAD_DIM)
        r = lax.rsqrt(ms + EPS)
        return x * (isn * r + (1.0 - isn)) * gf
    W = proj.shape[1]
    return _rowwise(name, fn, [(proj, 0)], [(gain_full, 0), (is_norm, 0), (seg, None)], [(W, BF16)],
                    tc=ATTN_GW, rb=SEG_ROWS)


def _qknorm_bwd(name, proj, dy, gain_full, is_norm, seg):
    def fn(x, dy, gf, isn, B):
        ms = _seg_sum(x * x, B) * (1.0 / ATTN_HEAD_DIM)
        r = lax.rsqrt(ms + EPS)
        xh = x * r
        dxh = dy * gf
        dn = r * (dxh - xh * (_seg_sum(dxh * xh, B) * (1.0 / ATTN_HEAD_DIM)))
        return isn * dn + (1.0 - isn) * dxh, _colsum(dy * xh)
    W = proj.shape[1]
    return _rowwise(name, fn, [(proj, 0), (dy, 0)], [(gain_full, 0), (is_norm, 0), (seg, None)],
                    [(W, BF16)], [(1, W)], tc=ATTN_GW, rb=SEG_ROWS)


def _attn_masks(r0=0, rows=ATTN_BLOCK):
    shape = (rows, ATTN_BLOCK)
    row = lax.broadcasted_iota(jnp.int32, shape, 0) + r0
    col = lax.broadcasted_iota(jnp.int32, shape, 1)
    return col <= row, col >= row, col < ATTN_HEAD_DIM


def _attn_group_fwd(name, qkv, d):
    S = qkv.shape[0]
    n, W, G = S // d, 3 * ATTN_GW, ATTN_GW
    nb = n // ATTN_BLOCK
    view = qkv.reshape(n, d * W)

    B, RQ = ATTN_BLOCK, ATTN_ROWS

    def body(cur, prev, o_ref, l_ref, s_scr, p_scr, lse_scr, inv_scr):
        b = pl.program_id(1)
        cur_mask, prev_mask, low = _attn_masks()
        prev_mask = jnp.logical_and(prev_mask, b > 0)
        for h in range(ATTN_HEADS):
            c0 = (h // 2) * LANES
            hm = low if h % 2 == 0 else jnp.logical_not(low)
            q2 = cur[:, c0:c0 + LANES]
            qm = jnp.where(hm, q2, jnp.zeros_like(q2))
            s_scr[h, :, 0:B] = jnp.where(cur_mask, _nt(qm, cur[:, G + c0:G + c0 + LANES]), -jnp.inf)
            s_scr[h, :, B:2 * B] = jnp.where(prev_mask, _nt(qm, prev[:, G + c0:G + c0 + LANES]), -jnp.inf)
        for h in range(ATTN_HEADS):
            for r0 in range(0, B, RQ):
                s = s_scr[h, r0:r0 + RQ, :]
                m = jnp.max(s, axis=1, keepdims=True)
                p = jnp.exp(s - m)
                l = jnp.sum(p, axis=1, keepdims=True)
                p_scr[h, r0:r0 + RQ, :] = p.astype(BF16)
                lse_scr[h, r0:r0 + RQ, :] = jnp.broadcast_to(m + jnp.log(l), (RQ, LANES))
                inv_scr[h, r0:r0 + RQ, :] = jnp.broadcast_to(1.0 / l, (RQ, LANES))
        for pr in range(G // LANES):
            c0 = pr * LANES
            vc, vp = cur[:, 2 * G + c0:2 * G + c0 + LANES], prev[:, 2 * G + c0:2 * G + c0 + LANES]
            o = [(_nn(p_scr[h, :, 0:B], vc) + _nn(p_scr[h, :, B:2 * B], vp)) * inv_scr[h] for h in (2 * pr, 2 * pr + 1)]
            o_ref[:, c0:c0 + LANES] = jnp.where(low, o[0], o[1])
            l_ref[:, c0:c0 + LANES] = jnp.where(low, lse_scr[2 * pr], lse_scr[2 * pr + 1])

    o, l = pl.pallas_call(
        body, name=name, grid=(d, nb),
        in_specs=[pl.BlockSpec((B, W), lambda r, b: (b, r)),
                  pl.BlockSpec((B, W), lambda r, b: (jnp.maximum(b - 1, 0), r))],
        out_specs=[pl.BlockSpec((B, G), lambda r, b: (b, r))] * 2,
        out_shape=[jax.ShapeDtypeStruct((n, d * G), F32)] * 2,
        scratch_shapes=[pltpu.VMEM((ATTN_HEADS, B, 2 * B), F32), pltpu.VMEM((ATTN_HEADS, B, 2 * B), BF16),
                        pltpu.VMEM((ATTN_HEADS, B, LANES), F32), pltpu.VMEM((ATTN_HEADS, B, LANES), F32)],
        compiler_params=_params("parallel", "parallel"),
    )(view, view)
    return o.reshape(S, G), l.reshape(S, G)


def _attn_combine(name, os, ls):
    def fn(o1, o2, o3, l1, l2, l3):
        m = jnp.maximum(jnp.maximum(l1, l2), l3)
        e1, e2, e3 = jnp.exp(l1 - m), jnp.exp(l2 - m), jnp.exp(l3 - m)
        den = e1 + e2 + e3
        return (e1 * o1 + e2 * o2 + e3 * o3) / den, m + jnp.log(den)
    G = os[0].shape[1]
    return _rowwise(name, fn, [(a, 0) for a in (*os, *ls)], [], [(G, F32), (G, F32)])


def _attn_delta(name, do, o, lse, seg):
    def fn(do, o, lse, B):
        lane = lax.broadcasted_iota(jnp.int32, lse.shape, 1)
        first = (lane & (ATTN_HEAD_DIM - 1)) < ATTN_HEAD_DIM // 2
        return jnp.where(first, lse, _seg_sum(do * o, B)), do
    G = o.shape[1]
    return _rowwise(name, fn, [(do, 0), (o, 0), (lse, 0)], [(seg, None)], [(G, F32), (G, BF16)], rb=SEG_ROWS)


def _attn_group_bwd(name, qkv, do, stats, d):
    S = qkv.shape[0]
    n, W, G = S // d, 3 * ATTN_GW, ATTN_GW
    nb = n // ATTN_BLOCK

    B, RQ = ATTN_BLOCK, ATTN_ROWS
    S_A, DP_A, S_B, DP_B, S_C, DP_C = range(6)
    P_A, DS_A, P_B, DS_B, DS_C = range(5)

    def body(qp, qc, qn, do_c, do_n, st_c, st_n, out, f_scr, b_scr):
        j = pl.program_id(1)
        low = _attn_masks()[2]
        for h in range(ATTN_HEADS):
            c0 = (h // 2) * LANES
            hm = low if h % 2 == 0 else jnp.logical_not(low)
            k_c, v_c = qc[:, G + c0:G + c0 + LANES], qc[:, 2 * G + c0:2 * G + c0 + LANES]
            k_p, v_p = qp[:, G + c0:G + c0 + LANES], qp[:, 2 * G + c0:2 * G + c0 + LANES]
            zero = jnp.zeros((B, LANES), BF16)
            qmc, qmn = jnp.where(hm, qc[:, c0:c0 + LANES], zero), jnp.where(hm, qn[:, c0:c0 + LANES], zero)
            dmc = jnp.where(hm, do_c[:, c0:c0 + LANES].astype(BF16), zero)
            dmn = jnp.where(hm, do_n[:, c0:c0 + LANES].astype(BF16), zero)
            f_scr[h, S_A], f_scr[h, DP_A] = _nt(qmc, k_c), _nt(dmc, v_c)
            f_scr[h, S_B], f_scr[h, DP_B] = _nt(qmn, k_c), _nt(dmn, v_c)
            f_scr[h, S_C], f_scr[h, DP_C] = _nt(qmc, k_p), _nt(dmc, v_p)
        for h in range(ATTN_HEADS):
            h0 = h * ATTN_HEAD_DIM
            for r0 in range(0, B, RQ):
                rows = slice(r0, r0 + RQ)
                cur_mask, band, _ = _attn_masks(r0, RQ)
                next_mask, prev_mask = jnp.logical_and(band, j < nb - 1), jnp.logical_and(band, j > 0)
                h1 = h0 + ATTN_HEAD_DIM // 2
                lc, ln = st_c[rows, h0:h0 + 1], st_n[rows, h0:h0 + 1]
                dlc, dln = st_c[rows, h1:h1 + 1], st_n[rows, h1:h1 + 1]
                p_a = jnp.where(cur_mask, jnp.exp(f_scr[h, S_A, rows, :] - lc), 0.0)
                p_b = jnp.where(next_mask, jnp.exp(f_scr[h, S_B, rows, :] - ln), 0.0)
                p_c = jnp.where(prev_mask, jnp.exp(f_scr[h, S_C, rows, :] - lc), 0.0)
                b_scr[h, P_A, rows, :] = p_a.astype(BF16)
                b_scr[h, P_B, rows, :] = p_b.astype(BF16)
                b_scr[h, DS_A, rows, :] = (p_a * (f_scr[h, DP_A, rows, :] - dlc)).astype(BF16)
                b_scr[h, DS_B, rows, :] = (p_b * (f_scr[h, DP_B, rows, :] - dln)).astype(BF16)
                b_scr[h, DS_C, rows, :] = (p_c * (f_scr[h, DP_C, rows, :] - dlc)).astype(BF16)
        for pr in range(G // LANES):
            c0 = pr * LANES
            q_c, k_c, q_n = qc[:, c0:c0 + LANES], qc[:, G + c0:G + c0 + LANES], qn[:, c0:c0 + LANES]
            k_p = qp[:, G + c0:G + c0 + LANES]
            d_c, d_n = do_c[:, c0:c0 + LANES].astype(BF16), do_n[:, c0:c0 + LANES].astype(BF16)
            res = []
            for h in (2 * pr, 2 * pr + 1):
                dq = _nn(b_scr[h, DS_A], k_c) + _nn(b_scr[h, DS_C], k_p)
                dk = _tn(b_scr[h, DS_A], q_c) + _tn(b_scr[h, DS_B], q_n)
                dv = _tn(b_scr[h, P_A], d_c) + _tn(b_scr[h, P_B], d_n)
                res.append((dq, dk, dv))
            for t in range(3):
                out[:, t * G + c0:t * G + c0 + LANES] = jnp.where(low, res[0][t], res[1][t]).astype(out.dtype)

    prv = lambda r, j: (jnp.maximum(j - 1, 0), r)
    cur = lambda r, j: (j, r)
    nxt = lambda r, j: (jnp.minimum(j + 1, nb - 1), r)
    wide = lambda m: pl.BlockSpec((ATTN_BLOCK, W), m)
    narrow = lambda m: pl.BlockSpec((ATTN_BLOCK, G), m)
    qv, dv, sv = qkv.reshape(n, d * W), do.reshape(n, d * G), stats.reshape(n, d * G)
    out = pl.pallas_call(
        body, name=name, grid=(d, nb),
        in_specs=[wide(prv), wide(cur), wide(nxt), narrow(cur), narrow(nxt), narrow(cur), narrow(nxt)],
        out_specs=wide(cur), out_shape=jax.ShapeDtypeStruct((n, d * W), BF16),
        scratch_shapes=[pltpu.VMEM((ATTN_HEADS, 6, B, B), F32), pltpu.VMEM((ATTN_HEADS, 5, B, B), BF16)],
        compiler_params=_params("parallel", "parallel"),
    )(qv, qv, qv, dv, dv, sv, sv)
    return out.reshape(S, W)


def _chunk_triangle(T, upper):
    i = jnp.arange(T)
    same = (i[:, None] // HGRN_CHUNK) == (i[None, :] // HGRN_CHUNK)
    tri = (i[None, :] >= i[:, None]) if upper else (i[None, :] <= i[:, None])
    return jnp.logical_and(same, tri).astype(F32)


def _hgrn_prologue(qr, fr, lbv, q_s, k_s, b_s, tri_ref, T):
    def pro(s, c):
        sl = pl.ds(pl.multiple_of(s * HGRN_CHUNK, HGRN_CHUNK), HGRN_CHUNK)
        sg = _sigmoid(fr[sl, :])
        qv = qr[sl, :]
        q_s[sl, :] = qv * _sigmoid(qv)
        k_s[sl, :] = (1.0 - lbv) * (1.0 - sg)
        b_s[sl, :] = jnp.log(lbv + (1.0 - lbv) * sg)
        return c
    lax.fori_loop(0, T // HGRN_CHUNK, pro, 0)
    b_s[...] = _nn(tri_ref[...], b_s[...], HI)


def _hgrn_scan_fwd(name, pq, pf, pv, lb):
    S, D = pq.shape
    T = _pick_rows(S, HGRN_TILE)
    NH, NT, C, HD, HB = D // HGRN_HEAD, S // T, HGRN_CHUNK, HGRN_HEAD, HGRN_GROUP
    W = HB * HD
    tri = _chunk_triangle(T, upper=False)

    def body(qr, fr, iv, lb_ref, tri_ref, o_ref, ck_ref, st_ref, q_s, k_s, b_s):
        @pl.when(pl.program_id(1) == 0)
        def _():
            st_ref[...] = jnp.zeros_like(st_ref)

        ck_ref[...] = st_ref[...]
        _hgrn_prologue(qr, fr, lb_ref[...], q_s, k_s, b_s, tri_ref, T)
        row = lax.broadcasted_iota(jnp.int32, (C, 1), 0)

        def chunk(c, carry):
            sl = pl.ds(pl.multiple_of(c * C, C), C)
            for hh in range(HB):
                cs = slice(hh * HD, (hh + 1) * HD)
                q, k, b, v = q_s[sl, cs], k_s[sl, cs], b_s[sl, cs], iv[sl, cs]
                b_last = b[C - 1:C, :]
                st = st_ref[cs, :]
                o = _nt((q * jnp.exp(b)).astype(BF16), st.astype(BF16))
                for s in range(C):
                    e = jnp.exp(jnp.minimum(b - b[s:s + 1, :], 0.0))
                    a = jnp.sum(q * e * k[s:s + 1, :], axis=1, keepdims=True)
                    o = o + jnp.where(row >= s, a, 0.0) * v[s:s + 1, :]
                o_ref[sl, cs] = o
                kd = k * jnp.exp(b_last - b)
                st_ref[cs, :] = st * jnp.exp(b_last) + _tn(v.astype(BF16), kd.astype(BF16))
            return carry

        lax.fori_loop(0, T // C, chunk, 0)

    NG = NH // HB
    col = pl.BlockSpec((T, W), lambda h, t: (t, h))
    return pl.pallas_call(
        body, name=name, grid=(NG, NT),
        in_specs=[col, col, col, pl.BlockSpec((1, W), lambda h, t: (0, h)), pl.BlockSpec((T, T), lambda h, t: (0, 0))],
        out_specs=[col, pl.BlockSpec((W, HD), lambda h, t: (t * NG + h, 0))],
        out_shape=[jax.ShapeDtypeStruct((S, D), F32), jax.ShapeDtypeStruct((NT * NH * HD, HD), F32)],
        scratch_shapes=[pltpu.VMEM((W, HD), F32)] + [pltpu.VMEM((T, W), F32)] * 3,
        compiler_params=_params("parallel", "arbitrary"),
    )(pq, pf, pv, lb, tri)


def _hgrn_scan_bwd(name, pq, pf, pv, lb, ckpt, do):
    S, D = pq.shape
    T = _pick_rows(S, HGRN_TILE)
    NH, NT, C, HD, HB = D // HGRN_HEAD, S // T, HGRN_CHUNK, HGRN_HEAD, HGRN_GROUP
    NC, W, NG = T // C, HB * HD, NH // HB
    tri, tri_up = _chunk_triangle(T, upper=False), _chunk_triangle(T, upper=True)

    def body(qr, fr, iv, do_ref, ck_ref, lb_ref, tri_ref, triu_ref, dq_ref, df_ref, dv_ref, dlb_ref,
             dst_ref, run, save, q_s, k_s, b_s, dq_s, dk_s, db_s):
        @pl.when(pl.program_id(1) == 0)
        def _():
            dst_ref[...] = jnp.zeros_like(dst_ref)
            dlb_ref[...] = jnp.zeros_like(dlb_ref)

        lbv = lb_ref[...]
        _hgrn_prologue(qr, fr, lbv, q_s, k_s, b_s, tri_ref, T)
        row = lax.broadcasted_iota(jnp.int32, (C, 1), 0)
        run[...] = ck_ref[...]

        def replay(c, carry):
            sl = pl.ds(pl.multiple_of(c * C, C), C)
            for hh in range(HB):
                cs = slice(hh * HD, (hh + 1) * HD)
                st = run[cs, :]
                save[pl.ds(pl.multiple_of((hh * NC + c) * HD, HD), HD), :] = st
                k, b, v = k_s[sl, cs], b_s[sl, cs], iv[sl, cs]
                b_last = b[C - 1:C, :]
                kd = k * jnp.exp(b_last - b)
                run[cs, :] = st * jnp.exp(b_last) + _tn(v.astype(BF16), kd.astype(BF16))
            return carry

        lax.fori_loop(0, NC, replay, 0)

        def chunk(ci, carry):
            c = NC - 1 - ci
            sl = pl.ds(pl.multiple_of(c * C, C), C)
            for hh in range(HB):
                cs = slice(hh * HD, (hh + 1) * HD)
                q, k, b, v, g = q_s[sl, cs], k_s[sl, cs], b_s[sl, cs], iv[sl, cs], do_ref[sl, cs]
                st0 = save[pl.ds(pl.multiple_of((hh * NC + c) * HD, HD), HD), :]
                dst1 = dst_ref[cs, :]
                b_last = b[C - 1:C, :]
                eb, ebl, ek = jnp.exp(b), jnp.exp(b_last), jnp.exp(b_last - b)
                dst1_b = dst1.astype(BF16)
                dq = _nn(g.astype(BF16), st0.astype(BF16)) * eb
                dv = _nt((k * ek).astype(BF16), dst1_b)
                dk = _nn(v.astype(BF16), dst1_b) * ek
                db_last = _colsum(dk * k) + _colsum(dst1 * st0) * ebl
                for s in range(C):
                    e = jnp.where(row >= s, jnp.exp(jnp.minimum(b - b[s:s + 1, :], 0.0)), 0.0)
                    ks, vs = k[s:s + 1, :], v[s:s + 1, :]
                    da = jnp.sum(g * vs, axis=1, keepdims=True)
                    a = jnp.sum(q * e * ks, axis=1, keepdims=True)
                    dq = dq + da * e * ks
                    dk = dk + jnp.where(row == s, _colsum(da * q * e), 0.0)
                    dv = dv + jnp.where(row == s, _colsum(a * g), 0.0)
                dq_s[sl, cs] = dq
                dk_s[sl, cs] = dk
                db_s[sl, cs] = q * dq - k * dk + jnp.where(row == C - 1, db_last, 0.0)
                dv_ref[sl, cs] = dv.astype(BF16)
                dst_ref[cs, :] = dst1 * ebl + _tn(g.astype(BF16), (q * eb).astype(BF16))
            return carry

        lax.fori_loop(0, NC, chunk, 0)
        db_s[...] = _nn(triu_ref[...], db_s[...], HI)

        def epi(s, carry):
            sl = pl.ds(pl.multiple_of(s * C, C), C)
            qv = qr[sl, :]
            sq = _sigmoid(qv)
            dq_ref[sl, :] = (dq_s[sl, :] * sq * (1.0 + qv * (1.0 - sq))).astype(BF16)
            sg = _sigmoid(fr[sl, :])
            common = db_s[sl, :] / (lbv + (1.0 - lbv) * sg) - dk_s[sl, :]
            df_ref[sl, :] = (common * (1.0 - lbv) * sg * (1.0 - sg)).astype(BF16)
            dlb_ref[...] += _colsum(common * (1.0 - sg))
            return carry

        lax.fori_loop(0, NC, epi, 0)

    col = pl.BlockSpec((T, W), lambda h, t: (NT - 1 - t, h))
    dq, df, dv, dlb = pl.pallas_call(
        body, name=name, grid=(NG, NT),
        in_specs=[col, col, col, col, pl.BlockSpec((W, HD), lambda h, t: ((NT - 1 - t) * NG + h, 0)),
                  pl.BlockSpec((1, W), lambda h, t: (0, h)),
                  pl.BlockSpec((T, T), lambda h, t: (0, 0)), pl.BlockSpec((T, T), lambda h, t: (0, 0))],
        out_specs=[col, col, col, pl.BlockSpec((1, W), lambda h, t: (0, h))],
        out_shape=[jax.ShapeDtypeStruct((S, D), BF16)] * 3 + [jax.ShapeDtypeStruct((1, D), F32)],
        scratch_shapes=[pltpu.VMEM((W, HD), F32)] * 2 + [pltpu.VMEM((HB * NC * HD, HD), F32)]
        + [pltpu.VMEM((T, W), F32)] * 6,
        compiler_params=_params("parallel", "arbitrary"),
    )(pq, pf, pv, do, ckpt, lb, tri, tri_up)
    return dq, df, dv, dlb


def _hgrn_out_fwd(name, o, gate, norm_g):
    def fn(o, g, ng):
        parts = []
        for h in range(o.shape[1] // HGRN_HEAD):
            c = slice(h * HGRN_HEAD, (h + 1) * HGRN_HEAD)
            oh, gh = o[:, c], g[:, c]
            r = lax.rsqrt(jnp.mean(oh * oh, axis=-1, keepdims=True) + EPS)
            parts.append(oh * r * ng[:, c] * (gh * _sigmoid(gh)))
        return jnp.concatenate(parts, axis=1)
    return _rowwise(name, fn, [(o, 0), (gate, 0)], [(norm_g, None)], [(o.shape[1], BF16)])


def _hgrn_out_bwd(name, o, gate, norm_g, dy):
    def fn(o, g, dy, ng):
        dos, dgs, dngs = [], [], []
        for h in range(o.shape[1] // HGRN_HEAD):
            c = slice(h * HGRN_HEAD, (h + 1) * HGRN_HEAD)
            oh, gh, dyh, ngh = o[:, c], g[:, c], dy[:, c], ng[:, c]
            r = lax.rsqrt(jnp.mean(oh * oh, axis=-1, keepdims=True) + EPS)
            xh = oh * r
            s = _sigmoid(gh)
            dn = dyh * (gh * s)
            dxh = dn * ngh
            dos.append(r * (dxh - xh * jnp.mean(dxh * xh, axis=-1, keepdims=True)))
            dgs.append(dyh * xh * ngh * (s * (1.0 + gh * (1.0 - s))))
            dngs.append(_colsum(dn * xh))
        return jnp.concatenate(dos, axis=1), jnp.concatenate(dgs, axis=1), jnp.concatenate(dngs, axis=1)
    D = o.shape[1]
    return _rowwise(name, fn, [(o, 0), (gate, 0), (dy, 0)], [(norm_g, None)], [(D, F32), (D, BF16)], [(1, D)])


def _lower_bound_fwd(name, logits, layer):
    n = logits.shape[0]

    def body(x_ref, o_ref):
        rows = [x_ref[i:i + 1, :] for i in range(n)]
        m = functools.reduce(jnp.maximum, rows)
        e = [jnp.exp(r - m) for r in rows]
        den = functools.reduce(jnp.add, e)
        o_ref[...] = functools.reduce(jnp.add, e[1:layer + 1]) / den

    return pl.pallas_call(body, name=name, out_shape=jax.ShapeDtypeStruct((1, logits.shape[1]), F32))(logits)


def _lower_bound_bwd(name, logits, dlb, layer):
    n = logits.shape[0]

    def body(x_ref, d_ref, o_ref):
        rows = [x_ref[i:i + 1, :] for i in range(n)]
        m = functools.reduce(jnp.maximum, rows)
        e = [jnp.exp(r - m) for r in rows]
        den = functools.reduce(jnp.add, e)
        s = [v / den for v in e]
        d = d_ref[...]
        inner = functools.reduce(jnp.add, s[1:layer + 1]) * d
        for i in range(n):
            o_ref[i:i + 1, :] = s[i] * ((d if 1 <= i <= layer else 0.0) - inner)

    return pl.pallas_call(body, name=name, out_shape=jax.ShapeDtypeStruct(logits.shape, F32))(logits, dlb)


def _row(v):
    return v.reshape(1, -1)


def _ffn_fwd(l, x1, w):
    h2 = _rmsnorm_fwd(f"ffn{l}_norm", x1, _row(w["ffn_norm"][l]))
    a, *u = _ffn_up_fused(f"ffn{l}_up", h2, w["ffn_w_up"][l], w["ffn_conv_w"][l], _row(w["ffn_conv_b"][l]))
    x2 = _matmul(f"ffn{l}_down", [(a, w["ffn_w_down"][l])], residual=x1)
    return x2, (x1, h2, u, a)


def _ffn_bwd(l, dx2, saved, w, grads):
    x1, h2, (u0g, u0u, ug, uu), a = saved
    w_up, w_down = w["ffn_w_up"][l], w["ffn_w_down"][l]
    F = w_down.shape[0]
    grads["ffn_w_down"][l] = _matmul_tn(f"ffn{l}_dwdown", a, dx2)
    dg, du, dcwg, dcwu, dcbg, dcbu = _ffn_gate_bwd_fused(
        f"ffn{l}_dgate", dx2, w_down, u0g, u0u, ug, uu, w["ffn_conv_w"][l])
    grads["ffn_conv_w"][l] = jnp.concatenate([dcwg, dcwu], axis=1)
    grads["ffn_conv_b"][l] = jnp.concatenate([dcbg, dcbu], axis=1)[0]
    grads["ffn_w_up"][l] = jnp.concatenate(
        [_matmul_tn(f"ffn{l}_dwup0", h2, dg), _matmul_tn(f"ffn{l}_dwup1", h2, du)], axis=1)
    dx1, dgain = _matmul_rmsnorm_bwd(f"ffn{l}_dh", [(dg, w_up[:, :F]), (du, w_up[:, F:])], x1,
                                     _row(w["ffn_norm"][l]), dx2)
    grads["ffn_norm"][l] = dgain[0]
    return dx1


def _attn_gain_rows(w, j, g):
    scale = ATTN_HEAD_DIM ** -0.5
    qg = jnp.tile(w["attn_q_gain"][j, g] * scale, ATTN_HEADS)
    kg = jnp.tile(w["attn_k_gain"][j, g], ATTN_HEADS)
    gain = jnp.concatenate([qg, kg, jnp.ones((ATTN_GW,), F32)])
    is_norm = jnp.concatenate([jnp.ones((2 * ATTN_GW,), F32), jnp.zeros((ATTN_GW,), F32)])
    return _row(gain), _row(is_norm)


def _attn_fwd(l, j, x, w):
    h = _rmsnorm_fwd(f"mix{l}_norm", x, _row(w["mixer_norm"][l]))
    w_in = w["attn_w_in"][j]
    seg = _segment_matrix(LANES, ATTN_HEAD_DIM)
    GW3 = 3 * ATTN_GW
    proj, qkv, os, ls = [], [], [], []
    for g, d in enumerate(ATTN_DILATIONS):
        gain, is_norm = _attn_gain_rows(w, j, g)
        proj.append(_matmul(f"attn{l}_in{g}", [(h, w_in[:, g * GW3:(g + 1) * GW3])]))
        qkv.append(_qknorm_fwd(f"attn{l}_qknorm{g}", proj[g], gain, is_norm, seg))
        o, lse = _attn_group_fwd(f"attn{l}_core{g}", qkv[g], d)
        os.append(o)
        ls.append(lse)
    o, lse = _attn_combine(f"attn{l}_combine", os, ls)
    x1 = _matmul(f"attn{l}_out", [(o, w["attn_w_out"][j])], residual=x)
    return x1, (x, h, proj, qkv, o, lse)


def _attn_bwd(l, j, dx1, saved, w, grads):
    x, h, proj, qkv, o, lse = saved
    w_in, w_out = w["attn_w_in"][j], w["attn_w_out"][j]
    seg = _segment_matrix(LANES, ATTN_HEAD_DIM)
    GW3 = 3 * ATTN_GW
    grads["attn_w_out"][j] = _matmul_tn(f"attn{l}_dwout", o, dx1)
    do = _matmul(f"attn{l}_do", [(dx1, w_out)], trans_b=True)
    stats, do = _attn_delta(f"attn{l}_delta", do, o, lse, seg)
    dproj, dwin, dqg, dkg = [], [], [], []
    for g, d in enumerate(ATTN_DILATIONS):
        gain, is_norm = _attn_gain_rows(w, j, g)
        dqkv = _attn_group_bwd(f"attn{l}_dcore{g}", qkv[g], do, stats, d)
        dp, dgain = _qknorm_bwd(f"attn{l}_dqknorm{g}", proj[g], dqkv, gain, is_norm, seg)
        dproj.append(dp)
        dwin.append(_matmul_tn(f"attn{l}_dwin{g}", h, dp))
        per_head = dgain.reshape(3, ATTN_HEADS, ATTN_HEAD_DIM).sum(axis=1)
        dqg.append(per_head[0] * ATTN_HEAD_DIM ** -0.5)
        dkg.append(per_head[1])
    grads["attn_w_in"][j] = jnp.concatenate(dwin, axis=1)
    grads["attn_q_gain"][j] = jnp.stack(dqg)
    grads["attn_k_gain"][j] = jnp.stack(dkg)
    dx, dg = _matmul_rmsnorm_bwd(f"attn{l}_dh", [(dproj[g], w_in[:, g * GW3:(g + 1) * GW3]) for g in range(3)],
                                 x, _row(w["mixer_norm"][l]), dx1)
    grads["mixer_norm"][l] = dg[0]
    return dx


def _conv_fwd(l, j, x, w):
    h = _rmsnorm_fwd(f"mix{l}_norm", x, _row(w["mixer_norm"][l]))
    w_in, b_in = w["conv_w_in"][j], _row(w["conv_b_in"][j])
    C = w_in.shape[1] // 2
    ua = _matmul(f"conv{l}_in0", [(h, w_in[:, :C])], bias=b_in[:, :C])
    ug = _matmul(f"conv{l}_in1", [(h, w_in[:, C:])], bias=b_in[:, C:])
    glu = _glu_fwd(f"conv{l}_glu", ua, ug)
    c = _dwconv(f"conv{l}_dw", glu, w["conv_dw_w"][j], _row(w["conv_dw_b"][j]), reverse=False)
    sw = _ln_silu_fwd(f"conv{l}_ln", c, _row(w["conv_ln_g"][j]), _row(w["conv_ln_b"][j]))
    x1 = _matmul(f"conv{l}_out", [(sw, w["conv_w_out"][j])], bias=_row(w["conv_b_out"][j]), residual=x)
    return x1, (x, h, ua, ug, glu, c, sw)


def _conv_bwd(l, j, dx1, saved, w, grads):
    x, h, ua, ug, glu, c, sw = saved
    w_in, w_out, dw_w = w["conv_w_in"][j], w["conv_w_out"][j], w["conv_dw_w"][j]
    C = w_out.shape[0]
    grads["conv_b_out"][j] = _column_sums(f"conv{l}_dbout", dx1)[0]
    grads["conv_w_out"][j] = _matmul_tn(f"conv{l}_dwout", sw, dx1)
    dsw = _matmul(f"conv{l}_dsw", [(dx1, w_out)], trans_b=True)
    dc, dlg, dlb = _ln_silu_bwd(f"conv{l}_dln", c, _row(w["conv_ln_g"][j]), _row(w["conv_ln_b"][j]), dsw)
    grads["conv_ln_g"][j], grads["conv_ln_b"][j] = dlg[0], dlb[0]
    dglu = _dwconv(f"conv{l}_ddw", dc, dw_w, jnp.zeros((1, C), F32), reverse=True)
    gw, gb = _dwconv_wgrad(f"conv{l}_ddww", glu, dc, dw_w.shape[0])
    grads["conv_dw_w"][j], grads["conv_dw_b"][j] = gw, gb[0]
    da, dgate, sa, sg = _glu_bwd(f"conv{l}_dglu", ua, ug, dglu)
    grads["conv_b_in"][j] = jnp.concatenate([sa, sg], axis=1)[0]
    grads["conv_w_in"][j] = jnp.concatenate(
        [_matmul_tn(f"conv{l}_dwin0", h, da), _matmul_tn(f"conv{l}_dwin1", h, dgate)], axis=1)
    dx, dg = _matmul_rmsnorm_bwd(f"conv{l}_dh", [(da, w_in[:, :C]), (dgate, w_in[:, C:])], x,
                                 _row(w["mixer_norm"][l]), dx1)
    grads["mixer_norm"][l] = dg[0]
    return dx


def _hgrn_fwd(l, j, x, w):
    h = _rmsnorm_fwd(f"mix{l}_norm", x, _row(w["mixer_norm"][l]))
    w_in = w["hgrn_w_in"][j]
    D = w_in.shape[1] // 4
    pq, pf, pv, pg = [_matmul(f"hgrn{l}_in{s}", [(h, w_in[:, s * D:(s + 1) * D])]) for s in range(4)]
    lb = _lower_bound_fwd(f"hgrn{l}_lb", w["hgrn_lb_logits"], l)
    o, ckpt = _hgrn_scan_fwd(f"hgrn{l}_scan", pq, pf, pv, lb)
    y = _hgrn_out_fwd(f"hgrn{l}_gate", o, pg, _row(w["hgrn_norm_g"][j]))
    x1 = _matmul(f"hgrn{l}_out", [(y, w["hgrn_w_out"][j])], residual=x)
    return x1, (x, h, pq, pf, pv, pg, lb, o, ckpt, y)


def _hgrn_bwd(l, j, dx1, saved, w, grads):
    x, h, pq, pf, pv, pg, lb, o, ckpt, y = saved
    w_in, w_out = w["hgrn_w_in"][j], w["hgrn_w_out"][j]
    D = w_out.shape[0]
    grads["hgrn_w_out"][j] = _matmul_tn(f"hgrn{l}_dwout", y, dx1)
    dy = _matmul(f"hgrn{l}_dy", [(dx1, w_out)], trans_b=True)
    do, dpg, dng = _hgrn_out_bwd(f"hgrn{l}_dgate", o, pg, _row(w["hgrn_norm_g"][j]), dy)
    grads["hgrn_norm_g"][j] = dng[0]
    dpq, dpf, dpv, dlb = _hgrn_scan_bwd(f"hgrn{l}_dscan", pq, pf, pv, lb, ckpt, do)
    grads["hgrn_lb_logits"] = grads["hgrn_lb_logits"] + _lower_bound_bwd(f"hgrn{l}_dlb", w["hgrn_lb_logits"], dlb, l)
    dps = [dpq, dpf, dpv, dpg]
    grads["hgrn_w_in"][j] = jnp.concatenate([_matmul_tn(f"hgrn{l}_dwin{s}", h, dps[s]) for s in range(4)], axis=1)
    dx, dg = _matmul_rmsnorm_bwd(f"hgrn{l}_dh", [(dps[s], w_in[:, s * D:(s + 1) * D]) for s in range(4)], x,
                                 _row(w["mixer_norm"][l]), dx1)
    grads["mixer_norm"][l] = dg[0]
    return dx


_MIXERS = ((_attn_fwd, _attn_bwd), (_conv_fwd, _conv_bwd), (_hgrn_fwd, _hgrn_bwd))
_PER_MIXER = {"attn": 0, "conv": 1, "hgrn": 2}


def _local_step(x, target, w):
    depth = w["mixer_norm"].shape[0]
    grads = {}
    for name, v in w.items():
        lead = v.shape[0]
        grads[name] = jnp.zeros(v.shape, F32) if name == "hgrn_lb_logits" else [None] * lead
    saved = []
    for l in range(depth):
        fwd, _ = _MIXERS[l % N_MIXERS]
        x, s_mix = fwd(l, l // N_MIXERS, x, w)
        x, s_ffn = _ffn_fwd(l, x, w)
        saved.append((s_mix, s_ffn))
    dx, loss_cols = _loss_grad("loss", x, target)
    for l in reversed(range(depth)):
        _, bwd = _MIXERS[l % N_MIXERS]
        s_mix, s_ffn = saved[l]
        dx = _ffn_bwd(l, dx, s_ffn, w, grads)
        dx = bwd(l, l // N_MIXERS, dx, s_mix, w, grads)
    grads = {k: (v if k == "hgrn_lb_logits" else jnp.stack(v)) for k, v in grads.items()}
    return jnp.sum(loss_cols), dx, grads


_HBM = pl.BlockSpec(memory_space=pltpu.HBM)


def _chip_peers():
    x, y, c = lax.axis_index("x"), lax.axis_index("y"), lax.axis_index("c")
    return 2 * x + y, (x, y, c), [(1 - x, y), (x, 1 - y), (1 - x, 1 - y)]


def _exchange_chips(name, src):
    def body(src_ref, out_ref, send_sems, recv_sems, local_sem):
        p, (x, y, c), peers = _chip_peers()
        mine = pltpu.make_async_copy(src_ref.at[p], out_ref.at[p], local_sem)
        mine.start()

        def copy(k, slab_from, slab_to, peer):
            return pltpu.make_async_remote_copy(
                src_ref=src_ref.at[slab_from], dst_ref=out_ref.at[slab_to], send_sem=send_sems.at[k],
                recv_sem=recv_sems.at[k], device_id=(peer[0], peer[1], c), device_id_type=MESH)

        sends = [copy(k, 2 * px + py, p, (px, py)) for k, (px, py) in enumerate(peers)]
        for s in sends:
            s.start()
        for k, (px, py) in enumerate(peers):
            copy(k, p, 2 * px + py, (px, py)).wait_recv()
        for s in sends:
            s.wait_send()
        mine.wait()

    return pl.pallas_call(
        body, name=name, in_specs=[_HBM], out_specs=_HBM, out_shape=jax.ShapeDtypeStruct(src.shape, src.dtype),
        scratch_shapes=[pltpu.SemaphoreType.DMA((3,)), pltpu.SemaphoreType.DMA((3,)), pltpu.SemaphoreType.DMA],
    )(src)


def _all_gather_chips(name, shard):
    R = shard.shape[0]
    half = R // 2

    def body(src_ref, out_ref, send_sems, recv_sems, local_sem):
        p, (x, y, c), peers = _chip_peers()
        mine = pltpu.make_async_copy(src_ref, out_ref.at[p], local_sem)
        mine.start()

        def rows(slab, core):
            return out_ref.at[slab, pl.ds(core * half, half), :]

        def over_ici(k, slab, peer):
            src = src_ref.at[pl.ds(c * half, half), :] if slab is None else rows(slab, c)
            return pltpu.make_async_remote_copy(
                src_ref=src, dst_ref=rows(p if slab is None else slab, c), send_sem=send_sems.at[k],
                recv_sem=recv_sems.at[k], device_id=(peer[0], peer[1], c), device_id_type=MESH)

        def to_sibling(k, slab, core):
            return pltpu.make_async_remote_copy(
                src_ref=rows(slab, core), dst_ref=rows(slab, core), send_sem=send_sems.at[3 + k],
                recv_sem=recv_sems.at[3 + k], device_id=(x, y, 1 - c), device_id_type=MESH)

        sends = [over_ici(k, None, peer) for k, peer in enumerate(peers)]
        for s in sends:
            s.start()
        passed = []
        for k, (px, py) in enumerate(peers):
            over_ici(k, 2 * px + py, (px, py)).wait_recv()
            passed.append(to_sibling(k, 2 * px + py, c))
            passed[k].start()
        for k, (px, py) in enumerate(peers):
            to_sibling(k, 2 * px + py, 1 - c).wait_recv()
        for s in sends + passed:
            s.wait_send()
        mine.wait()

    return pl.pallas_call(
        body, name=name, in_specs=[_HBM], out_specs=_HBM,
        out_shape=jax.ShapeDtypeStruct((N_CHIPS,) + shard.shape, shard.dtype),
        scratch_shapes=[pltpu.SemaphoreType.DMA((6,)), pltpu.SemaphoreType.DMA((6,)), pltpu.SemaphoreType.DMA],
    )(shard)


def _swap_cores(name, v):
    def body(v_ref, out_ref, send_sem, recv_sem):
        x, y, c = lax.axis_index("x"), lax.axis_index("y"), lax.axis_index("c")
        cp = pltpu.make_async_remote_copy(src_ref=v_ref, dst_ref=out_ref, send_sem=send_sem, recv_sem=recv_sem,
                                          device_id=(x, y, 1 - c), device_id_type=MESH)
        cp.start()
        cp.wait()

    return pl.pallas_call(
        body, name=name, in_specs=[_HBM], out_specs=_HBM, out_shape=jax.ShapeDtypeStruct(v.shape, v.dtype),
        scratch_shapes=[pltpu.SemaphoreType.DMA, pltpu.SemaphoreType.DMA],
    )(v)


_WEIGHTS = ("mixer_norm", "ffn_norm", "attn_w_in", "attn_q_gain", "attn_k_gain", "attn_w_out", "conv_w_in",
            "conv_b_in", "conv_dw_w", "conv_dw_b", "conv_ln_g", "conv_ln_b", "conv_w_out", "conv_b_out",
            "hgrn_w_in", "hgrn_lb_logits", "hgrn_norm_g", "hgrn_w_out", "ffn_w_up", "ffn_conv_w", "ffn_conv_b",
            "ffn_w_down")
_SHARD_AXIS = {"attn_w_in": 2, "attn_w_out": 2, "conv_w_in": 2, "conv_dw_w": 2, "conv_w_out": 1, "hgrn_w_in": 2,
               "hgrn_norm_g": 1, "hgrn_w_out": 1, "ffn_w_up": 2, "ffn_conv_w": 2, "ffn_w_down": 1}
_MATMUL_WEIGHTS = ("attn_w_in", "attn_w_out", "conv_w_in", "conv_w_out", "hgrn_w_in", "hgrn_w_out", "ffn_w_up",
                   "ffn_w_down")
PACK_COLS = 1024
PACK_ROWS = 512


def _pack(arrays, nlead, dtype):
    lead = arrays[0].shape[:nlead]
    flat = []
    for a in arrays:
        f = a.reshape(lead + (-1,)).astype(dtype)
        flat.append(jnp.pad(f, [(0, 0)] * nlead + [(0, (-f.shape[-1]) % PACK_COLS)]))
    buf = jnp.concatenate(flat, axis=-1)
    buf = jnp.pad(buf, [(0, 0)] * nlead + [(0, (-buf.shape[-1]) % (PACK_COLS * PACK_ROWS))])
    return buf.reshape(lead + (-1, PACK_COLS))


def _unpack(buf, shapes, nlead):
    lead = buf.shape[:nlead]
    flat = buf.reshape(lead + (-1,))
    out, off = [], 0
    for shape in shapes:
        n = 1
        for s in shape:
            n *= s
        out.append(flat[..., off:off + n].reshape(lead + tuple(shape)))
        off += n + (-n) % PACK_COLS
    return out


def _merge_shards(piece, axis):
    moved = jnp.moveaxis(piece, 0, axis)
    shape = moved.shape
    return moved.reshape(shape[:axis] + (shape[axis] * shape[axis + 1],) + shape[axis + 2:])


def _split_shards(full, axis):
    shape = full.shape
    cut = full.reshape(shape[:axis] + (N_CHIPS, shape[axis] // N_CHIPS) + shape[axis + 1:])
    return jnp.moveaxis(cut, axis, 0)


def _gather_weights(local):
    big = [n for n in _WEIGHTS if n in _MATMUL_WEIGHTS]
    small = [n for n in _WEIGHTS if n in _SHARD_AXIS and n not in _MATMUL_WEIGHTS]
    full = {n: local[n] for n in _WEIGHTS if n not in _SHARD_AXIS}
    for names, dtype, tag in ((big, BF16, "comm_gather_matmul_weights"), (small, F32, "comm_gather_small_weights")):
        gathered = _all_gather_chips(tag, _pack([local[n] for n in names], 0, dtype))
        pieces = _unpack(gathered, [local[n].shape for n in names], 1)
        for n, piece in zip(names, pieces):
            full[n] = _merge_shards(piece, _SHARD_AXIS[n])
    return full


def _reduce_gradients(grads, local):
    out = {}
    big = [n for n in _WEIGHTS if n in _MATMUL_WEIGHTS]
    rest = [n for n in _WEIGHTS if n not in _MATMUL_WEIGHTS]
    for names, dtype, tag in ((big, BF16, "matmul"), (rest, F32, "small")):
        slabs = []
        for n in names:
            g = grads[n]
            if n in _SHARD_AXIS:
                slabs.append(_split_shards(g, _SHARD_AXIS[n]))
            else:
                slabs.append(jnp.broadcast_to(g[None], (N_CHIPS,) + g.shape))
        packed = _pack(slabs, 1, dtype)
        landed = _exchange_chips(f"comm_scatter_{tag}_gradients", packed)
        partial = _sum_slabs(f"sum_chips_{tag}", landed)
        other = _swap_cores(f"comm_swap_{tag}_sums", partial)
        total = _add(f"sum_cores_{tag}", [partial, other])
        out.update(zip(names, _unpack(total, [local[n].shape for n in names], 0)))
    return out


def kernel(x, mixer_norm, ffn_norm, attn_w_in, attn_q_gain, attn_k_gain, attn_w_out, conv_w_in, conv_b_in, conv_dw_w, conv_dw_b, conv_ln_g, conv_ln_b, conv_w_out, conv_b_out, hgrn_w_in, hgrn_lb_logits, hgrn_norm_g, hgrn_w_out, ffn_w_up, ffn_conv_w, ffn_conv_b, ffn_w_down, loss_target, m_mixer_norm, m_ffn_norm, m_attn_w_in, m_attn_q_gain, m_attn_k_gain, m_attn_w_out, m_conv_w_in, m_conv_b_in, m_conv_dw_w, m_conv_dw_b, m_conv_ln_g, m_conv_ln_b, m_conv_w_out, m_conv_b_out, m_hgrn_w_in, m_hgrn_lb_logits, m_hgrn_norm_g, m_hgrn_w_out, m_ffn_w_up, m_ffn_conv_w, m_ffn_conv_b, m_ffn_w_down, v_mixer_norm, v_ffn_norm, v_attn_w_in, v_attn_q_gain, v_attn_k_gain, v_attn_w_out, v_conv_w_in, v_conv_b_in, v_conv_dw_w, v_conv_dw_b, v_conv_ln_g, v_conv_ln_b, v_conv_w_out, v_conv_b_out, v_hgrn_w_in, v_hgrn_lb_logits, v_hgrn_norm_g, v_hgrn_w_out, v_ffn_w_up, v_ffn_conv_w, v_ffn_conv_b, v_ffn_w_down):
    given = dict(locals())
    local = {n: given[n] for n in _WEIGHTS}
    full = _gather_weights(local)
    loss, dx, grads = _local_step(x[0], loss_target[0], full)
    loss = lax.psum(loss, ("x", "y", "c"))
    grad = _reduce_gradients(grads, local)
    delta, new_m, new_v = {}, {}, {}
    for n in _WEIGHTS:
        shape = local[n].shape
        as2d = lambda a: a.reshape(-1, shape[-1])
        d, m, v = _adamw(f"adamw_{n}", as2d(local[n]), as2d(grad[n]), as2d(given["m_" + n]), as2d(given["v_" + n]))
        delta[n], new_m[n], new_v[n] = d.reshape(shape), m.reshape(shape), v.reshape(shape)
    return (loss, dx[None], *[grad[n] for n in _WEIGHTS], *[delta[n] for n in _WEIGHTS],
            *[new_m[n] for n in _WEIGHTS], *[new_v[n] for n in _WEIGHTS])
```

```python
import functools

import jax
import jax.numpy as jnp
from jax import lax
from jax.experimental import pallas as pl
from jax.experimental.pallas import tpu as pltpu

F32 = jnp.float32
BF16 = jnp.bfloat16

EPS = 1e-6
N_MIXERS = 3
ATTN_DILATIONS = (1, 4, 16)
ATTN_BLOCK = 128
ATTN_HEADS = 8
ATTN_HEAD_DIM = 64
ATTN_GW = ATTN_HEADS * ATTN_HEAD_DIM
ATTN_ROWS = 32
HGRN_HEAD = 128
HGRN_CHUNK = 16
HGRN_TILE = 256
HGRN_GROUP = 8
ADAM_LR, ADAM_B1, ADAM_B2, ADAM_EPS, ADAM_WD, ADAM_STEP = 0.001, 0.9, 0.999, 1e-08, 0.01, 10

LANES = 128
SUBLANES = 8
VMEM_LIMIT = 56 * 1024 * 1024
N_CHIPS = 4
MESH = pl.DeviceIdType.MESH

HI = lax.Precision.HIGHEST


def _params(*sem):
    return pltpu.CompilerParams(dimension_semantics=sem, vmem_limit_bytes=VMEM_LIMIT)


def _pick(n, target):
    if n <= target:
        return n
    best = None
    for t in range(LANES, target + 1, LANES):
        if n % t == 0:
            best = t
    assert best is not None, (n, target)
    return best


def _pick_rows(n, target):
    if n <= target:
        return n
    for t in range(target, 15, -16):
        if n % t == 0:
            return t
    return n


def _dot(a, b, dims, precision=None):
    return lax.dot_general(a, b, (dims, ((), ())), precision=precision, preferred_element_type=F32)


def _nn(a, b, precision=None):
    return _dot(a, b, ((1,), (0,)), precision)


def _nt(a, b, precision=None):
    return _dot(a, b, ((1,), (1,)), precision)


def _tn(a, b, precision=None):
    return _dot(a, b, ((0,), (0,)), precision)


def _sigmoid(x):
    return 1.0 / (1.0 + jnp.exp(-x))


ROWWISE_UNROLL_ROWS = 64


def _rowwise(name, fn, rows, pars=(), outs=(), accs=(), *, tc=None, tm=512, rb=16):
    S = rows[0][0].shape[0]
    tm = _pick_rows(S, tm)
    rb = rb if tm % rb == 0 else tm
    width = tc if tc is not None else None
    ncol = 1
    if tc is not None:
        base = outs[0][0] if outs else accs[0][1]
        ncol = base // tc
    n_r, n_p, n_o, n_a = len(rows), len(pars), len(outs), len(accs)

    def body(*refs):
        row_refs, par_refs = refs[:n_r], refs[n_r:n_r + n_p]
        out_refs, acc_refs = refs[n_r + n_p:n_r + n_p + n_o], refs[n_r + n_p + n_o:]
        if n_a:
            @pl.when(pl.program_id(1) == 0)
            def _():
                for a in acc_refs:
                    a[...] = jnp.zeros_like(a)

        def step(s, carry):
            sl = pl.ds(pl.multiple_of(s * rb, rb), rb)
            res = fn(*[r[sl, :] for r in row_refs], *[p[...] for p in par_refs])
            res = res if isinstance(res, tuple) else (res,)
            for o, v in zip(out_refs, res[:n_o]):
                o[sl, :] = v.astype(o.dtype)
            for a, v in zip(acc_refs, res[n_o:]):
                a[...] += v
            return carry

        lax.fori_loop(0, tm // rb, step, 0, unroll=min(tm // rb, max(2, ROWWISE_UNROLL_ROWS // rb)))

    def row_spec(c, off):
        if tc is None:
            return pl.BlockSpec((tm, c), lambda j, i: (i, 0))
        return pl.BlockSpec((tm, tc), lambda j, i, o=off // tc: (i, j + o))

    def par_spec(shape, off):
        if off is None or tc is None:
            return pl.BlockSpec(shape, lambda j, i: (0, 0))
        return pl.BlockSpec((shape[0], tc), lambda j, i, o=off // tc: (0, j + o))

    in_specs = [row_spec(a.shape[1], off) for a, off in rows]
    in_specs += [par_spec(a.shape, off) for a, off in pars]
    out_specs = [row_spec(c, 0) for c, _ in outs] + [par_spec(s, 0) for s in accs]
    out_shape = [jax.ShapeDtypeStruct((S, c), d) for c, d in outs]
    out_shape += [jax.ShapeDtypeStruct(s, F32) for s in accs]
    res = pl.pallas_call(
        body, name=name, grid=(ncol, S // tm), in_specs=in_specs, out_specs=out_specs, out_shape=out_shape,
        compiler_params=_params("parallel", "arbitrary" if n_a else "parallel"),
    )(*[a for a, _ in rows], *[a for a, _ in pars])
    return res[0] if len(res) == 1 else tuple(res)


MATMUL_VMEM = 36 * 1024 * 1024


def _matmul_tiles(M, N, pairs, out_dtype, residual):
    tm = _pick_rows(M, 512)
    for tn in sorted({_pick(N, t) for t in range(LANES, 2049, LANES)}, reverse=True):
        step = sum(tm * a.shape[1] * a.dtype.itemsize + a.shape[1] * tn * b.dtype.itemsize for a, b in pairs)
        step += tm * tn * (jnp.dtype(out_dtype).itemsize + (4 if residual is not None else 0))
        if 2 * step <= MATMUL_VMEM:
            return tm, tn
    return tm, LANES


def _matmul(name, pairs, *, trans_b=False, bias=None, residual=None, out_dtype=F32):
    M = pairs[0][0].shape[0]
    N = pairs[0][1].shape[0] if trans_b else pairs[0][1].shape[1]
    tm, tn = _matmul_tiles(M, N, pairs, out_dtype, residual)
    n = len(pairs)

    def body(*refs):
        acc = None
        for i in range(n):
            a = refs[2 * i][...].astype(BF16)
            b = refs[2 * i + 1][...].astype(BF16)
            d = _nt(a, b) if trans_b else _nn(a, b)
            acc = d if acc is None else acc + d
        k = 2 * n
        if bias is not None:
            acc = acc + refs[k][...]
            k += 1
        if residual is not None:
            acc = acc + refs[k][...]
            k += 1
        refs[k][...] = acc.astype(out_dtype)

    in_specs, args = [], []
    for a, b in pairs:
        K = a.shape[1]
        in_specs.append(pl.BlockSpec((tm, K), lambda j, i: (i, 0)))
        in_specs.append(pl.BlockSpec((tn, K), lambda j, i: (j, 0)) if trans_b
                        else pl.BlockSpec((K, tn), lambda j, i: (0, j)))
        args += [a, b]
    if bias is not None:
        in_specs.append(pl.BlockSpec((1, tn), lambda j, i: (0, j)))
        args.append(bias)
    if residual is not None:
        in_specs.append(pl.BlockSpec((tm, tn), lambda j, i: (i, j)))
        args.append(residual)
    return pl.pallas_call(
        body, name=name, grid=(N // tn, M // tm), in_specs=in_specs,
        out_specs=pl.BlockSpec((tm, tn), lambda j, i: (i, j)),
        out_shape=jax.ShapeDtypeStruct((M, N), out_dtype), compiler_params=_params("parallel", "parallel"),
    )(*args)


def _matmul_tn(name, a, b, *, tm=1408, tn=1408, tk=1024):
    S, M = a.shape
    N = b.shape[1]
    tm, tn, tk = _pick(M, tm), _pick(N, tn), _pick_rows(S, tk)

    def body(a_ref, b_ref, o_ref):
        @pl.when(pl.program_id(2) == 0)
        def _():
            o_ref[...] = jnp.zeros_like(o_ref)

        o_ref[...] += _tn(a_ref[...].astype(BF16), b_ref[...].astype(BF16))

    return pl.pallas_call(
        body, name=name, grid=(M // tm, N // tn, S // tk),
        in_specs=[pl.BlockSpec((tk, tm), lambda i, j, k: (k, i)), pl.BlockSpec((tk, tn), lambda i, j, k: (k, j))],
        out_specs=pl.BlockSpec((tm, tn), lambda i, j, k: (i, j)),
        out_shape=jax.ShapeDtypeStruct((M, N), F32), compiler_params=_params("parallel", "parallel", "arbitrary"),
    )(a, b)


def _halo_rows(K):
    return 8 if K <= 9 else 32


def _shifted_copies(ext, shifted, K):
    if K <= SUBLANES:
        return
    n = shifted.shape[1]
    for s in range(1, SUBLANES):
        shifted[s, 0:n, :] = ext[s:s + n, :]


def _window(ext, shifted, K, off, rows):
    s = off % SUBLANES
    if K <= SUBLANES or s == 0:
        return ext[off:off + rows, :]
    return shifted[s, off - s:off - s + rows, :]


def _dwconv(name, x, w, b, *, reverse, out_dtype=F32):
    S, C = x.shape
    K = w.shape[0]
    H = _halo_rows(K)
    tm, tc = _pick_rows(S, 512 if K <= 4 else 256), _pick(C, 1408 if K <= 4 else 256)
    nrow = S // tm
    RB = 16 if out_dtype == BF16 else 8

    def body(x_ref, h_ref, w_ref, b_ref, o_ref, ext, shifted):
        i = pl.program_id(1)
        edge = (i == nrow - 1) if reverse else (i == 0)
        halo = jnp.where(edge, 0.0, h_ref[...].astype(F32))
        if reverse:
            ext[0:tm, :] = x_ref[...].astype(F32)
            ext[tm:tm + H, :] = halo
        else:
            ext[0:H, :] = halo
            ext[H:H + tm, :] = x_ref[...].astype(F32)
        _shifted_copies(ext, shifted, K)
        wv = w_ref[...]
        for s in range(tm // RB):
            acc = jnp.broadcast_to(b_ref[...], (RB, tc))
            for k in range(K):
                off = s * RB + ((K - 1 - k) if reverse else (H - (K - 1) + k))
                acc = acc + wv[k:k + 1, :] * _window(ext, shifted, K, off, RB)
            o_ref[s * RB:(s + 1) * RB, :] = acc.astype(out_dtype)

    r = tm // H
    if reverse:
        halo_map = lambda j, i: (jnp.minimum((i + 1) * r, S // H - 1), j)
    else:
        halo_map = lambda j, i: (jnp.maximum(i * r - 1, 0), j)
    return pl.pallas_call(
        body, name=name, grid=(C // tc, nrow),
        in_specs=[pl.BlockSpec((tm, tc), lambda j, i: (i, j)), pl.BlockSpec((H, tc), halo_map),
                  pl.BlockSpec((K, tc), lambda j, i: (0, j)), pl.BlockSpec((1, tc), lambda j, i: (0, j))],
        out_specs=pl.BlockSpec((tm, tc), lambda j, i: (i, j)),
        out_shape=jax.ShapeDtypeStruct((S, C), out_dtype),
        scratch_shapes=[pltpu.VMEM((tm + H, tc), F32), pltpu.VMEM((SUBLANES, tm + H - SUBLANES, tc), F32)],
        compiler_params=_params("parallel", "parallel"),
    )(x, x, w, b)


def _dwconv_wgrad(name, x, dy, K):
    S, C = x.shape
    H = _halo_rows(K)
    tm, tc = _pick_rows(S, 512 if K <= 4 else 256), _pick(C, 512 if K <= 4 else LANES)
    RB = 8

    def body(x_ref, h_ref, dy_ref, dw_ref, db_ref, ext, shifted):
        i = pl.program_id(1)

        @pl.when(i == 0)
        def _():
            dw_ref[...] = jnp.zeros_like(dw_ref)
            db_ref[...] = jnp.zeros_like(db_ref)

        ext[0:H, :] = jnp.where(i == 0, 0.0, h_ref[...].astype(F32))
        ext[H:H + tm, :] = x_ref[...].astype(F32)
        _shifted_copies(ext, shifted, K)
        acc = [jnp.zeros((RB, tc), F32) for _ in range(K)]
        accb = jnp.zeros((RB, tc), F32)
        for s in range(tm // RB):
            d = dy_ref[s * RB:(s + 1) * RB, :].astype(F32)
            accb = accb + d
            for k in range(K):
                off = s * RB + H - (K - 1) + k
                acc[k] = acc[k] + d * _window(ext, shifted, K, off, RB)
        for k in range(K):
            dw_ref[k:k + 1, :] += jnp.sum(acc[k], axis=0, keepdims=True)
        db_ref[...] += jnp.sum(accb, axis=0, keepdims=True)

    r = tm // H
    return pl.pallas_call(
        body, name=name, grid=(C // tc, S // tm),
        in_specs=[pl.BlockSpec((tm, tc), lambda j, i: (i, j)),
                  pl.BlockSpec((H, tc), lambda j, i: (jnp.maximum(i * r - 1, 0), j)),
                  pl.BlockSpec((tm, tc), lambda j, i: (i, j))],
        out_specs=[pl.BlockSpec((K, tc), lambda j, i: (0, j)), pl.BlockSpec((1, tc), lambda j, i: (0, j))],
        out_shape=[jax.ShapeDtypeStruct((K, C), F32), jax.ShapeDtypeStruct((1, C), F32)],
        scratch_shapes=[pltpu.VMEM((tm + H, tc), F32), pltpu.VMEM((SUBLANES, tm + H - SUBLANES, tc), F32)],
        compiler_params=_params("parallel", "arbitrary"),
    )(x, x, dy)


FFN_HALO = 16


def _row_shifts(ref, r0, rows, cs, shifts):
    n = rows // SUBLANES
    lo = -1 if max(shifts) > 0 else 0
    hi = n + (1 if min(shifts) < 0 else 0)
    v = {j: ref[r0 + j * SUBLANES:r0 + (j + 1) * SUBLANES, cs] for j in range(lo, hi)}
    row = lax.broadcasted_iota(jnp.int32, (SUBLANES, LANES), 0)
    out = {}
    for s in shifts:
        if s == 0:
            pieces = [v[j] for j in range(n)]
        elif s > 0:
            rot = {j: pltpu.roll(v[j], s, axis=0) for j in range(-1, n)}
            pieces = [jnp.where(row < s, rot[j - 1], rot[j]) for j in range(n)]
        else:
            rot = {j: pltpu.roll(v[j], SUBLANES + s, axis=0) for j in range(0, n + 1)}
            pieces = [jnp.where(row < SUBLANES + s, rot[j], rot[j + 1]) for j in range(n)]
        out[s] = jnp.concatenate(pieces, axis=0)
    return out


def _conv_taps(w, b, ext, r0, rows, cs):
    K = w.shape[0]
    win = _row_shifts(ext, r0, rows, cs, list(range(K)))
    acc = b
    for k in range(K):
        acc = acc + w[k:k + 1, :] * win[K - 1 - k]
    return acc


def _ffn_up_fused(name, h, w_up, cw, cb):
    S, D = h.shape
    F = w_up.shape[1] // 2
    tm, tn = _pick_rows(S, 512), _pick(F, 1408)
    nj, H, RB = F // tn, FFN_HALO, 64

    def body(h_ref, hh_ref, wg_ref, wu_ref, cwg_ref, cwu_ref, cbg_ref, cbu_ref,
             a_ref, u0g_ref, u0u_ref, ug_ref, uu_ref, eg, eu):
        first = pl.program_id(1) == 0
        hv, halo = h_ref[...], hh_ref[...]
        for w_ref, u0_ref, e in ((wg_ref, u0g_ref, eg), (wu_ref, u0u_ref, eu)):
            w = w_ref[...]
            u0 = _nn(hv, w)
            u0_ref[...] = u0.astype(BF16)
            e[0:H, :] = jnp.where(first, 0.0, _nn(halo, w))
            e[H:H + tm, :] = u0
        for c in range(tn // LANES):
            cs = slice(c * LANES, (c + 1) * LANES)
            wg, wu, bg, bu = cwg_ref[:, cs], cwu_ref[:, cs], cbg_ref[:, cs], cbu_ref[:, cs]
            for s in range(tm // RB):
                rows = slice(s * RB, (s + 1) * RB)
                ug = _conv_taps(wg, bg, eg, H + s * RB, RB, cs)
                uu = _conv_taps(wu, bu, eu, H + s * RB, RB, cs)
                ug_ref[rows, cs] = ug.astype(BF16)
                uu_ref[rows, cs] = uu.astype(BF16)
                a_ref[rows, cs] = (ug * _sigmoid(ug) * uu).astype(BF16)

    r = tm // H
    gate = lambda rows: pl.BlockSpec((rows, tn), lambda j, i: (0, j))
    up = lambda rows: pl.BlockSpec((rows, tn), lambda j, i: (0, j + nj))
    tile = pl.BlockSpec((tm, tn), lambda j, i: (i, j))
    K = cw.shape[0]
    return pl.pallas_call(
        body, name=name, grid=(nj, S // tm),
        in_specs=[pl.BlockSpec((tm, D), lambda j, i: (i, 0)),
                  pl.BlockSpec((H, D), lambda j, i: (jnp.maximum(i * r - 1, 0), 0)),
                  gate(D), up(D), gate(K), up(K), gate(1), up(1)],
        out_specs=[tile] * 5, out_shape=[jax.ShapeDtypeStruct((S, F), BF16)] * 5,
        scratch_shapes=[pltpu.VMEM((H + tm, tn), F32)] * 2, compiler_params=_params("parallel", "parallel"),
    )(h, h, w_up, w_up, cw, cw, cb, cb)


def _ffn_gate_bwd_fused(name, dy, w_down, u0g, u0u, ug, uu, cw):
    S, D = dy.shape
    F, K = w_down.shape[0], cw.shape[0]
    tm, tn = _pick_rows(S, 512), _pick(F, 1408)
    nj, nrow, H, RB = F // tn, S // tm, FFN_HALO, 16
    RW = 32

    def body(dy_ref, dyn_ref, wd_ref, u0g_ref, u0u_ref, g_ref, gn_ref, u_ref, un_ref, cwg_ref, cwu_ref,
             dg_ref, du_ref, dcwg_ref, dcwu_ref, dcbg_ref, dcbu_ref, dg_s, du_s, da_s):
        i = pl.program_id(1)
        last = i == nrow - 1

        @pl.when(i == 0)
        def _():
            for ref in (dcwg_ref, dcwu_ref, dcbg_ref, dcbu_ref):
                ref[...] = jnp.zeros_like(ref)

        wd = wd_ref[...]
        da_s[0:tm, :] = _nt(dy_ref[...].astype(BF16), wd)
        da_s[tm:tm + H, :] = jnp.where(last, 0.0, _nt(dyn_ref[...].astype(BF16), wd))
        for c in range(tn // LANES):
            cs = slice(c * LANES, (c + 1) * LANES)
            for s in range(tm // RB + 1):
                rows = slice(s * RB, (s + 1) * RB)
                src_g, src_u, src_rows = (g_ref, u_ref, rows) if s < tm // RB else (gn_ref, un_ref, slice(0, RB))
                gv, uv = src_g[src_rows, cs].astype(F32), src_u[src_rows, cs].astype(F32)
                da = da_s[rows, cs]
                sg = _sigmoid(gv)
                dg_s[rows, cs] = da * uv * (sg * (1.0 + gv * (1.0 - sg)))
                du_s[rows, cs] = da * (gv * sg)
            for d_s, u0_ref, cw_ref, out_ref, dcw_ref, dcb_ref in (
                    (dg_s, u0g_ref, cwg_ref, dg_ref, dcwg_ref, dcbg_ref),
                    (du_s, u0u_ref, cwu_ref, du_ref, dcwu_ref, dcbu_ref)):
                w = cw_ref[:, cs]
                acc = [jnp.zeros((RW, LANES), F32) for _ in range(K)]
                accb = jnp.zeros((RW, LANES), F32)
                for s in range(tm // RW):
                    r0 = s * RW
                    u0 = u0_ref[r0:r0 + RW, cs].astype(F32)
                    ahead = _row_shifts(d_s, r0, RW, cs, [-m for m in range(K)])
                    t = None
                    for k in range(K):
                        m = K - 1 - k
                        win = ahead[-m]
                        if m == 0:
                            accb = accb + win
                        acc[k] = acc[k] + win * u0
                        term = w[k:k + 1, :] * win
                        t = term if t is None else t + term
                    out_ref[r0:r0 + RW, cs] = t.astype(BF16)
                for k in range(K):
                    dcw_ref[k:k + 1, cs] += jnp.sum(acc[k], axis=0, keepdims=True)
                dcb_ref[:, cs] += jnp.sum(accb, axis=0, keepdims=True)

    r = tm // H
    gate = lambda rows: pl.BlockSpec((rows, tn), lambda j, i: (0, j))
    up = lambda rows: pl.BlockSpec((rows, tn), lambda j, i: (0, j + nj))
    tile = pl.BlockSpec((tm, tn), lambda j, i: (i, j))
    nxt = pl.BlockSpec((H, tn), lambda j, i: (jnp.minimum((i + 1) * r, S // H - 1), j))
    acc_w, acc_b = pl.BlockSpec((K, tn), lambda j, i: (0, j)), pl.BlockSpec((1, tn), lambda j, i: (0, j))
    return pl.pallas_call(
        body, name=name, grid=(nj, nrow),
        in_specs=[pl.BlockSpec((tm, D), lambda j, i: (i, 0)),
                  pl.BlockSpec((H, D), lambda j, i: (jnp.minimum((i + 1) * r, S // H - 1), 0)),
                  pl.BlockSpec((tn, D), lambda j, i: (j, 0)),
                  tile, tile, tile, nxt, tile, nxt, gate(K), up(K)],
        out_specs=[tile, tile, acc_w, acc_w, acc_b, acc_b],
        out_shape=[jax.ShapeDtypeStruct((S, F), BF16)] * 2 + [jax.ShapeDtypeStruct((K, F), F32)] * 2
        + [jax.ShapeDtypeStruct((1, F), F32)] * 2,
        scratch_shapes=[pltpu.VMEM((tm + H, tn), F32)] * 3, compiler_params=_params("parallel", "arbitrary"),
    )(dy, dy, w_down, u0g, u0u, ug, ug, uu, uu, cw, cw)


def _colsum(v):
    return jnp.sum(v, axis=0, keepdims=True)


def _rmsnorm_fwd(name, x, gain):
    def fn(x, g):
        r = lax.rsqrt(jnp.mean(x * x, axis=-1, keepdims=True) + EPS)
        return x * r * g
    return _rowwise(name, fn, [(x, 0)], [(gain, None)], [(x.shape[1], BF16)])


NORM_BWD_VMEM = 50 * 1024 * 1024


def _matmul_rmsnorm_bwd(name, pairs, x, gain, dres):
    M, D = x.shape
    n = len(pairs)
    weights = sum(b.shape[0] * b.shape[1] * b.dtype.itemsize for _, b in pairs)
    for tm in (512, 256, 128):
        step = sum(tm * a.shape[1] * a.dtype.itemsize for a, _ in pairs) + 3 * tm * D * 4
        if 2 * (step + weights) + tm * D * 4 <= NORM_BWD_VMEM:
            break
    RB = 16

    def body(*refs):
        x_ref, g_ref, r_ref, o_ref, dg_ref, dh_s = refs[2 * n:]

        @pl.when(pl.program_id(0) == 0)
        def _():
            dg_ref[...] = jnp.zeros_like(dg_ref)

        acc = None
        for i in range(n):
            d = _nt(refs[2 * i][...].astype(BF16), refs[2 * i + 1][...].astype(BF16))
            acc = d if acc is None else acc + d
        dh_s[...] = acc
        g = g_ref[...]

        def step(s, carry):
            sl = pl.ds(pl.multiple_of(s * RB, RB), RB)
            xv, dh = x_ref[sl, :], dh_s[sl, :]
            r = lax.rsqrt(jnp.mean(xv * xv, axis=-1, keepdims=True) + EPS)
            xh = xv * r
            dxh = dh * g
            o_ref[sl, :] = r_ref[sl, :] + r * (dxh - xh * jnp.mean(dxh * xh, axis=-1, keepdims=True))
            dg_ref[...] += _colsum(dh * xh)
            return carry

        lax.fori_loop(0, tm // RB, step, 0, unroll=ROWWISE_UNROLL_ROWS // RB)

    in_specs, args = [], []
    for a, b in pairs:
        in_specs += [pl.BlockSpec((tm, a.shape[1]), lambda i: (i, 0)), pl.BlockSpec(b.shape, lambda i: (0, 0))]
        args += [a, b]
    rows = pl.BlockSpec((tm, D), lambda i: (i, 0))
    vec = pl.BlockSpec((1, D), lambda i: (0, 0))
    return pl.pallas_call(
        body, name=name, grid=(M // tm,), in_specs=in_specs + [rows, vec, rows], out_specs=[rows, vec],
        out_shape=[jax.ShapeDtypeStruct((M, D), F32), jax.ShapeDtypeStruct((1, D), F32)],
        scratch_shapes=[pltpu.VMEM((tm, D), F32)], compiler_params=_params("arbitrary"),
    )(*args, x, gain, dres)


def _silu_gate_fwd(name, gate, up):
    F = gate.shape[1]
    def fn(g, up):
        return g * _sigmoid(g) * up
    return _rowwise(name, fn, [(gate, 0), (up, 0)], [], [(F, BF16)], tc=_pick(F, 512))


def _silu_gate_bwd(name, gate, up, da):
    F = gate.shape[1]
    def fn(g, up, da):
        s = _sigmoid(g)
        return da * up * (s * (1.0 + g * (1.0 - s))), da * (g * s)
    return _rowwise(name, fn, [(gate, 0), (up, 0), (da, 0)], [], [(F, F32), (F, F32)], tc=_pick(F, 512))


def _glu_fwd(name, a, gate):
    C = a.shape[1]
    def fn(a, g):
        return a * _sigmoid(g)
    return _rowwise(name, fn, [(a, 0), (gate, 0)], [], [(C, F32)], tc=_pick(C, 512))


def _glu_bwd(name, a, gate, dglu):
    C = a.shape[1]
    def fn(a, g, d):
        s = _sigmoid(g)
        da, dg = d * s, d * a * s * (1.0 - s)
        return da, dg, _colsum(da), _colsum(dg)
    return _rowwise(name, fn, [(a, 0), (gate, 0), (dglu, 0)], [], [(C, BF16), (C, BF16)], [(1, C), (1, C)],
                    tc=_pick(C, 512))


def _ln_silu_fwd(name, c, g, b):
    def fn(c, g, b):
        mu = jnp.mean(c, axis=-1, keepdims=True)
        d = c - mu
        n = d * lax.rsqrt(jnp.mean(d * d, axis=-1, keepdims=True) + EPS) * g + b
        return n * _sigmoid(n)
    return _rowwise(name, fn, [(c, 0)], [(g, None), (b, None)], [(c.shape[1], BF16)])


def _ln_silu_bwd(name, c, g, b, dsw):
    def fn(c, dsw, g, b):
        mu = jnp.mean(c, axis=-1, keepdims=True)
        d = c - mu
        r = lax.rsqrt(jnp.mean(d * d, axis=-1, keepdims=True) + EPS)
        ch = d * r
        n = ch * g + b
        s = _sigmoid(n)
        dn = dsw * (s * (1.0 + n * (1.0 - s)))
        dch = dn * g
        dc = r * (dch - jnp.mean(dch, axis=-1, keepdims=True) - ch * jnp.mean(dch * ch, axis=-1, keepdims=True))
        return dc, _colsum(dn * ch), _colsum(dn)
    C = c.shape[1]
    return _rowwise(name, fn, [(c, 0), (dsw, 0)], [(g, None), (b, None)], [(C, F32)], [(1, C), (1, C)])


def _column_sums(name, x):
    return _rowwise(name, lambda x: (_colsum(x),), [(x, 0)], [], [], [(1, x.shape[1])])


def _loss_grad(name, y, target):
    D = y.shape[1]
    def fn(y, t):
        e = y - t
        return e * (1.0 / D), _colsum(e * e) * (0.5 / D)
    return _rowwise(name, fn, [(y, 0), (target, 0)], [], [(D, F32)], [(1, D)])


def _add(name, arrays):
    def fn(*xs):
        acc = xs[0]
        for x in xs[1:]:
            acc = acc + x
        return acc
    return _rowwise(name, fn, [(a, 0) for a in arrays], [], [(arrays[0].shape[1], F32)], tm=256)


def _sum_slabs(name, stacked):
    n, R, C = stacked.shape
    tm = _pick_rows(R, 256)

    def body(*refs):
        acc = refs[0][0].astype(F32)
        for r in refs[1:n]:
            acc = acc + r[0].astype(F32)
        refs[n][...] = acc

    return pl.pallas_call(
        body, name=name, grid=(R // tm,),
        in_specs=[pl.BlockSpec((1, tm, C), lambda i, q=q: (q, i, 0)) for q in range(n)],
        out_specs=pl.BlockSpec((tm, C), lambda i: (i, 0)), out_shape=jax.ShapeDtypeStruct((R, C), F32),
        compiler_params=_params("parallel"),
    )(*[stacked] * n)


def _adamw(name, w, g, m, v):
    c1 = 1.0 - ADAM_B1 ** ADAM_STEP
    c2 = 1.0 - ADAM_B2 ** ADAM_STEP
    def fn(w, g, m, v):
        m = ADAM_B1 * m + (1.0 - ADAM_B1) * g
        v = ADAM_B2 * v + (1.0 - ADAM_B2) * (g * g)
        delta = -ADAM_LR * ((m / c1) / (jnp.sqrt(v / c2) + ADAM_EPS) + ADAM_WD * w)
        return delta, m, v
    C = w.shape[1]
    return _rowwise(name, fn, [(w, 0), (g, 0), (m, 0), (v, 0)], [], [(C, F32)] * 3, tm=256)


SEG_ROWS = 128


def _segment_matrix(n, seg):
    i = jnp.arange(n) // seg
    return (i[:, None] == i[None, :]).astype(BF16)


def _seg_sum(v, B):
    hi = v.astype(BF16)
    lo = (v - hi.astype(F32)).astype(BF16)
    n = B.shape[0]
    slabs = [slice(c, c + n) for c in range(0, v.shape[1], n)]
    return jnp.concatenate([_nn(hi[:, c], B) + _nn(lo[:, c], B) for c in slabs], axis=1)


def _qknorm_fwd(name, proj, gain_full, is_norm, seg):
    def fn(x, gf, isn, B):
        ms = _seg_sum(x * x, B) * (1.0 / ATTN_HEAD_DIM)
        r = lax.rsqrt(ms + EPS)
        return x * (isn * r + (1.0 - isn)) * gf
    W = proj.shape[1]
    return _rowwise(name, fn, [(proj, 0)], [(gain_full, 0), (is_norm, 0), (seg, None)], [(W, BF16)],
                    tc=ATTN_GW, rb=SEG_ROWS)


def _qknorm_bwd(name, proj, dy, gain_full, is_norm, seg):
    def fn(x, dy, gf, isn, B):
        ms = _seg_sum(x * x, B) * (1.0 / ATTN_HEAD_DIM)
        r = lax.rsqrt(ms + EPS)
        xh = x * r
        dxh = dy * gf
        dn = r * (dxh - xh * (_seg_sum(dxh * xh, B) * (1.0 / ATTN_HEAD_DIM)))
        return isn * dn + (1.0 - isn) * dxh, _colsum(dy * xh)
    W = proj.shape[1]
    return _rowwise(name, fn, [(proj, 0), (dy, 0)], [(gain_full, 0), (is_norm, 0), (seg, None)],
                    [(W, BF16)], [(1, W)], tc=ATTN_GW, rb=SEG_ROWS)


def _attn_masks(r0=0, rows=ATTN_BLOCK):
    shape = (rows, ATTN_BLOCK)
    row = lax.broadcasted_iota(jnp.int32, shape, 0) + r0
    col = lax.broadcasted_iota(jnp.int32, shape, 1)
    return col <= row, col >= row, col < ATTN_HEAD_DIM


def _attn_group_fwd(name, qkv, d):
    S = qkv.shape[0]
    n, W, G = S // d, 3 * ATTN_GW, ATTN_GW
    nb = n // ATTN_BLOCK
    view = qkv.reshape(n, d * W)

    B, RQ = ATTN_BLOCK, ATTN_ROWS

    def body(cur, prev, o_ref, l_ref, s_scr, p_scr, lse_scr, inv_scr):
        b = pl.program_id(1)
        cur_mask, prev_mask, low = _attn_masks()
        prev_mask = jnp.logical_and(prev_mask, b > 0)
        for h in range(ATTN_HEADS):
            c0 = (h // 2) * LANES
            hm = low if h % 2 == 0 else jnp.logical_not(low)
            q2 = cur[:, c0:c0 + LANES]
            qm = jnp.where(hm, q2, jnp.zeros_like(q2))
            s_scr[h, :, 0:B] = jnp.where(cur_mask, _nt(qm, cur[:, G + c0:G + c0 + LANES]), -jnp.inf)
            s_scr[h, :, B:2 * B] = jnp.where(prev_mask, _nt(qm, prev[:, G + c0:G + c0 + LANES]), -jnp.inf)
        for h in range(ATTN_HEADS):
            for r0 in range(0, B, RQ):
                s = s_scr[h, r0:r0 + RQ, :]
                m = jnp.max(s, axis=1, keepdims=True)
                p = jnp.exp(s - m)
                l = jnp.sum(p, axis=1, keepdims=True)
                p_scr[h, r0:r0 + RQ, :] = p.astype(BF16)
                lse_scr[h, r0:r0 + RQ, :] = jnp.broadcast_to(m + jnp.log(l), (RQ, LANES))
                inv_scr[h, r0:r0 + RQ, :] = jnp.broadcast_to(1.0 / l, (RQ, LANES))
        for pr in range(G // LANES):
            c0 = pr * LANES
            vc, vp = cur[:, 2 * G + c0:2 * G + c0 + LANES], prev[:, 2 * G + c0:2 * G + c0 + LANES]
            o = [(_nn(p_scr[h, :, 0:B], vc) + _nn(p_scr[h, :, B:2 * B], vp)) * inv_scr[h] for h in (2 * pr, 2 * pr + 1)]
            o_ref[:, c0:c0 + LANES] = jnp.where(low, o[0], o[1])
            l_ref[:, c0:c0 + LANES] = jnp.where(low, lse_scr[2 * pr], lse_scr[2 * pr + 1])

    o, l = pl.pallas_call(
        body, name=name, grid=(d, nb),
        in_specs=[pl.BlockSpec((B, W), lambda r, b: (b, r)),
                  pl.BlockSpec((B, W), lambda r, b: (jnp.maximum(b - 1, 0), r))],
        out_specs=[pl.BlockSpec((B, G), lambda r, b: (b, r))] * 2,
        out_shape=[jax.ShapeDtypeStruct((n, d * G), F32)] * 2,
        scratch_shapes=[pltpu.VMEM((ATTN_HEADS, B, 2 * B), F32), pltpu.VMEM((ATTN_HEADS, B, 2 * B), BF16),
                        pltpu.VMEM((ATTN_HEADS, B, LANES), F32), pltpu.VMEM((ATTN_HEADS, B, LANES), F32)],
        compiler_params=_params("parallel", "parallel"),
    )(view, view)
    return o.reshape(S, G), l.reshape(S, G)


def _attn_combine(name, os, ls):
    def fn(o1, o2, o3, l1, l2, l3):
        m = jnp.maximum(jnp.maximum(l1, l2), l3)
        e1, e2, e3 = jnp.exp(l1 - m), jnp.exp(l2 - m), jnp.exp(l3 - m)
        den = e1 + e2 + e3
        return (e1 * o1 + e2 * o2 + e3 * o3) / den, m + jnp.log(den)
    G = os[0].shape[1]
    return _rowwise(name, fn, [(a, 0) for a in (*os, *ls)], [], [(G, F32), (G, F32)])


def _attn_delta(name, do, o, lse, seg):
    def fn(do, o, lse, B):
        lane = lax.broadcasted_iota(jnp.int32, lse.shape, 1)
        first = (lane & (ATTN_HEAD_DIM - 1)) < ATTN_HEAD_DIM // 2
        return jnp.where(first, lse, _seg_sum(do * o, B)), do
    G = o.shape[1]
    return _rowwise(name, fn, [(do, 0), (o, 0), (lse, 0)], [(seg, None)], [(G, F32), (G, BF16)], rb=SEG_ROWS)


def _attn_group_bwd(name, qkv, do, stats, d):
    S = qkv.shape[0]
    n, W, G = S // d, 3 * ATTN_GW, ATTN_GW
    nb = n // ATTN_BLOCK

    B, RQ = ATTN_BLOCK, ATTN_ROWS
    S_A, DP_A, S_B, DP_B, S_C, DP_C = range(6)
    P_A, DS_A, P_B, DS_B, DS_C = range(5)

    def body(qp, qc, qn, do_c, do_n, st_c, st_n, out, f_scr, b_scr):
        j = pl.program_id(1)
        low = _attn_masks()[2]
        for h in range(ATTN_HEADS):
            c0 = (h // 2) * LANES
            hm = low if h % 2 == 0 else jnp.logical_not(low)
            k_c, v_c = qc[:, G + c0:G + c0 + LANES], qc[:, 2 * G + c0:2 * G + c0 + LANES]
            k_p, v_p = qp[:, G + c0:G + c0 + LANES], qp[:, 2 * G + c0:2 * G + c0 + LANES]
            zero = jnp.zeros((B, LANES), BF16)
            qmc, qmn = jnp.where(hm, qc[:, c0:c0 + LANES], zero), jnp.where(hm, qn[:, c0:c0 + LANES], zero)
            dmc = jnp.where(hm, do_c[:, c0:c0 + LANES].astype(BF16), zero)
            dmn = jnp.where(hm, do_n[:, c0:c0 + LANES].astype(BF16), zero)
            f_scr[h, S_A], f_scr[h, DP_A] = _nt(qmc, k_c), _nt(dmc, v_c)
            f_scr[h, S_B], f_scr[h, DP_B] = _nt(qmn, k_c), _nt(dmn, v_c)
            f_scr[h, S_C], f_scr[h, DP_C] = _nt(qmc, k_p), _nt(dmc, v_p)
        for h in range(ATTN_HEADS):
            h0 = h * ATTN_HEAD_DIM
            for r0 in range(0, B, RQ):
                rows = slice(r0, r0 + RQ)
                cur_mask, band, _ = _attn_masks(r0, RQ)
                next_mask, prev_mask = jnp.logical_and(band, j < nb - 1), jnp.logical_and(band, j > 0)
                h1 = h0 + ATTN_HEAD_DIM // 2
                lc, ln = st_c[rows, h0:h0 + 1], st_n[rows, h0:h0 + 1]
                dlc, dln = st_c[rows, h1:h1 + 1], st_n[rows, h1:h1 + 1]
                p_a = jnp.where(cur_mask, jnp.exp(f_scr[h, S_A, rows, :] - lc), 0.0)
                p_b = jnp.where(next_mask, jnp.exp(f_scr[h, S_B, rows, :] - ln), 0.0)
                p_c = jnp.where(prev_mask, jnp.exp(f_scr[h, S_C, rows, :] - lc), 0.0)
                b_scr[h, P_A, rows, :] = p_a.astype(BF16)
                b_scr[h, P_B, rows, :] = p_b.astype(BF16)
                b_scr[h, DS_A, rows, :] = (p_a * (f_scr[h, DP_A, rows, :] - dlc)).astype(BF16)
                b_scr[h, DS_B, rows, :] = (p_b * (f_scr[h, DP_B, rows, :] - dln)).astype(BF16)
                b_scr[h, DS_C, rows, :] = (p_c * (f_scr[h, DP_C, rows, :] - dlc)).astype(BF16)
        for pr in range(G // LANES):
            c0 = pr * LANES
            q_c, k_c, q_n = qc[:, c0:c0 + LANES], qc[:, G + c0:G + c0 + LANES], qn[:, c0:c0 + LANES]
            k_p = qp[:, G + c0:G + c0 + LANES]
            d_c, d_n = do_c[:, c0:c0 + LANES].astype(BF16), do_n[:, c0:c0 + LANES].astype(BF16)
            res = []
            for h in (2 * pr, 2 * pr + 1):
                dq = _nn(b_scr[h, DS_A], k_c) + _nn(b_scr[h, DS_C], k_p)
                dk = _tn(b_scr[h, DS_A], q_c) + _tn(b_scr[h, DS_B], q_n)
                dv = _tn(b_scr[h, P_A], d_c) + _tn(b_scr[h, P_B], d_n)
                res.append((dq, dk, dv))
            for t in range(3):
                out[:, t * G + c0:t * G + c0 + LANES] = jnp.where(low, res[0][t], res[1][t]).astype(out.dtype)

    prv = lambda r, j: (jnp.maximum(j - 1, 0), r)
    cur = lambda r, j: (j, r)
    nxt = lambda r, j: (jnp.minimum(j + 1, nb - 1), r)
    wide = lambda m: pl.BlockSpec((ATTN_BLOCK, W), m)
    narrow = lambda m: pl.BlockSpec((ATTN_BLOCK, G), m)
    qv, dv, sv = qkv.reshape(n, d * W), do.reshape(n, d * G), stats.reshape(n, d * G)
    out = pl.pallas_call(
        body, name=name, grid=(d, nb),
        in_specs=[wide(prv), wide(cur), wide(nxt), narrow(cur), narrow(nxt), narrow(cur), narrow(nxt)],
        out_specs=wide(cur), out_shape=jax.ShapeDtypeStruct((n, d * W), BF16),
        scratch_shapes=[pltpu.VMEM((ATTN_HEADS, 6, B, B), F32), pltpu.VMEM((ATTN_HEADS, 5, B, B), BF16)],
        compiler_params=_params("parallel", "parallel"),
    )(qv, qv, qv, dv, dv, sv, sv)
    return out.reshape(S, W)


def _chunk_triangle(T, upper):
    i = jnp.arange(T)
    same = (i[:, None] // HGRN_CHUNK) == (i[None, :] // HGRN_CHUNK)
    tri = (i[None, :] >= i[:, None]) if upper else (i[None, :] <= i[:, None])
    return jnp.logical_and(same, tri).astype(F32)


def _hgrn_prologue(qr, fr, lbv, q_s, k_s, b_s, tri_ref, T):
    def pro(s, c):
        sl = pl.ds(pl.multiple_of(s * HGRN_CHUNK, HGRN_CHUNK), HGRN_CHUNK)
        sg = _sigmoid(fr[sl, :])
        qv = qr[sl, :]
        q_s[sl, :] = qv * _sigmoid(qv)
        k_s[sl, :] = (1.0 - lbv) * (1.0 - sg)
        b_s[sl, :] = jnp.log(lbv + (1.0 - lbv) * sg)
        return c
    lax.fori_loop(0, T // HGRN_CHUNK, pro, 0)
    b_s[...] = _nn(tri_ref[...], b_s[...], HI)


def _hgrn_scan_fwd(name, pq, pf, pv, lb):
    S, D = pq.shape
    T = _pick_rows(S, HGRN_TILE)
    NH, NT, C, HD, HB = D // HGRN_HEAD, S // T, HGRN_CHUNK, HGRN_HEAD, HGRN_GROUP
    W = HB * HD
    tri = _chunk_triangle(T, upper=False)

    def body(qr, fr, iv, lb_ref, tri_ref, o_ref, ck_ref, st_ref, q_s, k_s, b_s):
        @pl.when(pl.program_id(1) == 0)
        def _():
            st_ref[...] = jnp.zeros_like(st_ref)

        ck_ref[...] = st_ref[...]
        _hgrn_prologue(qr, fr, lb_ref[...], q_s, k_s, b_s, tri_ref, T)
        row = lax.broadcasted_iota(jnp.int32, (C, 1), 0)

        def chunk(c, carry):
            sl = pl.ds(pl.multiple_of(c * C, C), C)
            for hh in range(HB):
                cs = slice(hh * HD, (hh + 1) * HD)
                q, k, b, v = q_s[sl, cs], k_s[sl, cs], b_s[sl, cs], iv[sl, cs]
                b_last = b[C - 1:C, :]
                st = st_ref[cs, :]
                o = _nt((q * jnp.exp(b)).astype(BF16), st.astype(BF16))
                for s in range(C):
                    e = jnp.exp(jnp.minimum(b - b[s:s + 1, :], 0.0))
                    a = jnp.sum(q * e * k[s:s + 1, :], axis=1, keepdims=True)
                    o = o + jnp.where(row >= s, a, 0.0) * v[s:s + 1, :]
                o_ref[sl, cs] = o
                kd = k * jnp.exp(b_last - b)
                st_ref[cs, :] = st * jnp.exp(b_last) + _tn(v.astype(BF16), kd.astype(BF16))
            return carry

        lax.fori_loop(0, T // C, chunk, 0)

    NG = NH // HB
    col = pl.BlockSpec((T, W), lambda h, t: (t, h))
    return pl.pallas_call(
        body, name=name, grid=(NG, NT),
        in_specs=[col, col, col, pl.BlockSpec((1, W), lambda h, t: (0, h)), pl.BlockSpec((T, T), lambda h, t: (0, 0))],
        out_specs=[col, pl.BlockSpec((W, HD), lambda h, t: (t * NG + h, 0))],
        out_shape=[jax.ShapeDtypeStruct((S, D), F32), jax.ShapeDtypeStruct((NT * NH * HD, HD), F32)],
        scratch_shapes=[pltpu.VMEM((W, HD), F32)] + [pltpu.VMEM((T, W), F32)] * 3,
        compiler_params=_params("parallel", "arbitrary"),
    )(pq, pf, pv, lb, tri)


def _hgrn_scan_bwd(name, pq, pf, pv, lb, ckpt, do):
    S, D = pq.shape
    T = _pick_rows(S, HGRN_TILE)
    NH, NT, C, HD, HB = D // HGRN_HEAD, S // T, HGRN_CHUNK, HGRN_HEAD, HGRN_GROUP
    NC, W, NG = T // C, HB * HD, NH // HB
    tri, tri_up = _chunk_triangle(T, upper=False), _chunk_triangle(T, upper=True)

    def body(qr, fr, iv, do_ref, ck_ref, lb_ref, tri_ref, triu_ref, dq_ref, df_ref, dv_ref, dlb_ref,
             dst_ref, run, save, q_s, k_s, b_s, dq_s, dk_s, db_s):
        @pl.when(pl.program_id(1) == 0)
        def _():
            dst_ref[...] = jnp.zeros_like(dst_ref)
            dlb_ref[...] = jnp.zeros_like(dlb_ref)

        lbv = lb_ref[...]
        _hgrn_prologue(qr, fr, lbv, q_s, k_s, b_s, tri_ref, T)
        row = lax.broadcasted_iota(jnp.int32, (C, 1), 0)
        run[...] = ck_ref[...]

        def replay(c, carry):
            sl = pl.ds(pl.multiple_of(c * C, C), C)
            for hh in range(HB):
                cs = slice(hh * HD, (hh + 1) * HD)
                st = run[cs, :]
                save[pl.ds(pl.multiple_of((hh * NC + c) * HD, HD), HD), :] = st
                k, b, v = k_s[sl, cs], b_s[sl, cs], iv[sl, cs]
                b_last = b[C - 1:C, :]
                kd = k * jnp.exp(b_last - b)
                run[cs, :] = st * jnp.exp(b_last) + _tn(v.astype(BF16), kd.astype(BF16))
            return carry

        lax.fori_loop(0, NC, replay, 0)

        def chunk(ci, carry):
            c = NC - 1 - ci
            sl = pl.ds(pl.multiple_of(c * C, C), C)
            for hh in range(HB):
                cs = slice(hh * HD, (hh + 1) * HD)
                q, k, b, v, g = q_s[sl, cs], k_s[sl, cs], b_s[sl, cs], iv[sl, cs], do_ref[sl, cs]
                st0 = save[pl.ds(pl.multiple_of((hh * NC + c) * HD, HD), HD), :]
                dst1 = dst_ref[cs, :]
                b_last = b[C - 1:C, :]
                eb, ebl, ek = jnp.exp(b), jnp.exp(b_last), jnp.exp(b_last - b)
                dst1_b = dst1.astype(BF16)
                dq = _nn(g.astype(BF16), st0.astype(BF16)) * eb
                dv = _nt((k * ek).astype(BF16), dst1_b)
                dk = _nn(v.astype(BF16), dst1_b) * ek
                db_last = _colsum(dk * k) + _colsum(dst1 * st0) * ebl
                for s in range(C):
                    e = jnp.where(row >= s, jnp.exp(jnp.minimum(b - b[s:s + 1, :], 0.0)), 0.0)
                    ks, vs = k[s:s + 1, :], v[s:s + 1, :]
                    da = jnp.sum(g * vs, axis=1, keepdims=True)
                    a = jnp.sum(q * e * ks, axis=1, keepdims=True)
                    dq = dq + da * e * ks
                    dk = dk + jnp.where(row == s, _colsum(da * q * e), 0.0)
                    dv = dv + jnp.where(row == s, _colsum(a * g), 0.0)
                dq_s[sl, cs] = dq
                dk_s[sl, cs] = dk
                db_s[sl, cs] = q * dq - k * dk + jnp.where(row == C - 1, db_last, 0.0)
                dv_ref[sl, cs] = dv.astype(BF16)
                dst_ref[cs, :] = dst1 * ebl + _tn(g.astype(BF16), (q * eb).astype(BF16))
            return carry

        lax.fori_loop(0, NC, chunk, 0)
        db_s[...] = _nn(triu_ref[...], db_s[...], HI)

        def epi(s, carry):
            sl = pl.ds(pl.multiple_of(s * C, C), C)
            qv = qr[sl, :]
            sq = _sigmoid(qv)
            dq_ref[sl, :] = (dq_s[sl, :] * sq * (1.0 + qv * (1.0 - sq))).astype(BF16)
            sg = _sigmoid(fr[sl, :])
            common = db_s[sl, :] / (lbv + (1.0 - lbv) * sg) - dk_s[sl, :]
            df_ref[sl, :] = (common * (1.0 - lbv) * sg * (1.0 - sg)).astype(BF16)
            dlb_ref[...] += _colsum(common * (1.0 - sg))
            return carry

        lax.fori_loop(0, NC, epi, 0)

    col = pl.BlockSpec((T, W), lambda h, t: (NT - 1 - t, h))
    dq, df, dv, dlb = pl.pallas_call(
        body, name=name, grid=(NG, NT),
        in_specs=[col, col, col, col, pl.BlockSpec((W, HD), lambda h, t: ((NT - 1 - t) * NG + h, 0)),
                  pl.BlockSpec((1, W), lambda h, t: (0, h)),
                  pl.BlockSpec((T, T), lambda h, t: (0, 0)), pl.BlockSpec((T, T), lambda h, t: (0, 0))],
        out_specs=[col, col, col, pl.BlockSpec((1, W), lambda h, t: (0, h))],
        out_shape=[jax.ShapeDtypeStruct((S, D), BF16)] * 3 + [jax.ShapeDtypeStruct((1, D), F32)],
        scratch_shapes=[pltpu.VMEM((W, HD), F32)] * 2 + [pltpu.VMEM((HB * NC * HD, HD), F32)]
        + [pltpu.VMEM((T, W), F32)] * 6,
        compiler_params=_params("parallel", "arbitrary"),
    )(pq, pf, pv, do, ckpt, lb, tri, tri_up)
    return dq, df, dv, dlb


def _hgrn_out_fwd(name, o, gate, norm_g):
    def fn(o, g, ng):
        parts = []
        for h in range(o.shape[1] // HGRN_HEAD):
            c = slice(h * HGRN_HEAD, (h + 1) * HGRN_HEAD)
            oh, gh = o[:, c], g[:, c]
            r = lax.rsqrt(jnp.mean(oh * oh, axis=-1, keepdims=True) + EPS)
            parts.append(oh * r * ng[:, c] * (gh * _sigmoid(gh)))
        return jnp.concatenate(parts, axis=1)
    return _rowwise(name, fn, [(o, 0), (gate, 0)], [(norm_g, None)], [(o.shape[1], BF16)])


def _hgrn_out_bwd(name, o, gate, norm_g, dy):
    def fn(o, g, dy, ng):
        dos, dgs, dngs = [], [], []
        for h in range(o.shape[1] // HGRN_HEAD):
            c = slice(h * HGRN_HEAD, (h + 1) * HGRN_HEAD)
            oh, gh, dyh, ngh = o[:, c], g[:, c], dy[:, c], ng[:, c]
            r = lax.rsqrt(jnp.mean(oh * oh, axis=-1, keepdims=True) + EPS)
            xh = oh * r
            s = _sigmoid(gh)
            dn = dyh * (gh * s)
            dxh = dn * ngh
            dos.append(r * (dxh - xh * jnp.mean(dxh * xh, axis=-1, keepdims=True)))
            dgs.append(dyh * xh * ngh * (s * (1.0 + gh * (1.0 - s))))
            dngs.append(_colsum(dn * xh))
        return jnp.concatenate(dos, axis=1), jnp.concatenate(dgs, axis=1), jnp.concatenate(dngs, axis=1)
    D = o.shape[1]
    return _rowwise(name, fn, [(o, 0), (gate, 0), (dy, 0)], [(norm_g, None)], [(D, F32), (D, BF16)], [(1, D)])


def _lower_bound_fwd(name, logits, layer):
    n = logits.shape[0]

    def body(x_ref, o_ref):
        rows = [x_ref[i:i + 1, :] for i in range(n)]
        m = functools.reduce(jnp.maximum, rows)
        e = [jnp.exp(r - m) for r in rows]
        den = functools.reduce(jnp.add, e)
        o_ref[...] = functools.reduce(jnp.add, e[1:layer + 1]) / den

    return pl.pallas_call(body, name=name, out_shape=jax.ShapeDtypeStruct((1, logits.shape[1]), F32))(logits)


def _lower_bound_bwd(name, logits, dlb, layer):
    n = logits.shape[0]

    def body(x_ref, d_ref, o_ref):
        rows = [x_ref[i:i + 1, :] for i in range(n)]
        m = functools.reduce(jnp.maximum, rows)
        e = [jnp.exp(r - m) for r in rows]
        den = functools.reduce(jnp.add, e)
        s = [v / den for v in e]
        d = d_ref[...]
        inner = functools.reduce(jnp.add, s[1:layer + 1]) * d
        for i in range(n):
            o_ref[i:i + 1, :] = s[i] * ((d if 1 <= i <= layer else 0.0) - inner)

    return pl.pallas_call(body, name=name, out_shape=jax.ShapeDtypeStruct(logits.shape, F32))(logits, dlb)


def _row(v):
    return v.reshape(1, -1)


def _ffn_fwd(l, x1, w):
    h2 = _rmsnorm_fwd(f"ffn{l}_norm", x1, _row(w["ffn_norm"][l]))
    a, *u = _ffn_up_fused(f"ffn{l}_up", h2, w["ffn_w_up"][l], w["ffn_conv_w"][l], _row(w["ffn_conv_b"][l]))
    x2 = _matmul(f"ffn{l}_down", [(a, w["ffn_w_down"][l])], residual=x1)
    return x2, (x1, h2, u, a)


def _ffn_bwd(l, dx2, saved, w, grads):
    x1, h2, (u0g, u0u, ug, uu), a = saved
    w_up, w_down = w["ffn_w_up"][l], w["ffn_w_down"][l]
    F = w_down.shape[0]
    grads["ffn_w_down"][l] = _matmul_tn(f"ffn{l}_dwdown", a, dx2)
    dg, du, dcwg, dcwu, dcbg, dcbu = _ffn_gate_bwd_fused(
        f"ffn{l}_dgate", dx2, w_down, u0g, u0u, ug, uu, w["ffn_conv_w"][l])
    grads["ffn_conv_w"][l] = jnp.concatenate([dcwg, dcwu], axis=1)
    grads["ffn_conv_b"][l] = jnp.concatenate([dcbg, dcbu], axis=1)[0]
    grads["ffn_w_up"][l] = jnp.concatenate(
        [_matmul_tn(f"ffn{l}_dwup0", h2, dg), _matmul_tn(f"ffn{l}_dwup1", h2, du)], axis=1)
    dx1, dgain = _matmul_rmsnorm_bwd(f"ffn{l}_dh", [(dg, w_up[:, :F]), (du, w_up[:, F:])], x1,
                                     _row(w["ffn_norm"][l]), dx2)
    grads["ffn_norm"][l] = dgain[0]
    return dx1


def _attn_gain_rows(w, j, g):
    scale = ATTN_HEAD_DIM ** -0.5
    qg = jnp.tile(w["attn_q_gain"][j, g] * scale, ATTN_HEADS)
    kg = jnp.tile(w["attn_k_gain"][j, g], ATTN_HEADS)
    gain = jnp.concatenate([qg, kg, jnp.ones((ATTN_GW,), F32)])
    is_norm = jnp.concatenate([jnp.ones((2 * ATTN_GW,), F32), jnp.zeros((ATTN_GW,), F32)])
    return _row(gain), _row(is_norm)


def _attn_fwd(l, j, x, w):
    h = _rmsnorm_fwd(f"mix{l}_norm", x, _row(w["mixer_norm"][l]))
    w_in = w["attn_w_in"][j]
    seg = _segment_matrix(LANES, ATTN_HEAD_DIM)
    GW3 = 3 * ATTN_GW
    proj, qkv, os, ls = [], [], [], []
    for g, d in enumerate(ATTN_DILATIONS):
        gain, is_norm = _attn_gain_rows(w, j, g)
        proj.append(_matmul(f"attn{l}_in{g}", [(h, w_in[:, g * GW3:(g + 1) * GW3])]))
        qkv.append(_qknorm_fwd(f"attn{l}_qknorm{g}", proj[g], gain, is_norm, seg))
        o, lse = _attn_group_fwd(f"attn{l}_core{g}", qkv[g], d)
        os.append(o)
        ls.append(lse)
    o, lse = _attn_combine(f"attn{l}_combine", os, ls)
    x1 = _matmul(f"attn{l}_out", [(o, w["attn_w_out"][j])], residual=x)
    return x1, (x, h, proj, qkv, o, lse)


def _attn_bwd(l, j, dx1, saved, w, grads):
    x, h, proj, qkv, o, lse = saved
    w_in, w_out = w["attn_w_in"][j], w["attn_w_out"][j]
    seg = _segment_matrix(LANES, ATTN_HEAD_DIM)
    GW3 = 3 * ATTN_GW
    grads["attn_w_out"][j] = _matmul_tn(f"attn{l}_dwout", o, dx1)
    do = _matmul(f"attn{l}_do", [(dx1, w_out)], trans_b=True)
    stats, do = _attn_delta(f"attn{l}_delta", do, o, lse, seg)
    dproj, dwin, dqg, dkg = [], [], [], []
    for g, d in enumerate(ATTN_DILATIONS):
        gain, is_norm = _attn_gain_rows(w, j, g)
        dqkv = _attn_group_bwd(f"attn{l}_dcore{g}", qkv[g], do, stats, d)
        dp, dgain = _qknorm_bwd(f"attn{l}_dqknorm{g}", proj[g], dqkv, gain, is_norm, seg)
        dproj.append(dp)
        dwin.append(_matmul_tn(f"attn{l}_dwin{g}", h, dp))
        per_head = dgain.reshape(3, ATTN_HEADS, ATTN_HEAD_DIM).sum(axis=1)
        dqg.append(per_head[0] * ATTN_HEAD_DIM ** -0.5)
        dkg.append(per_head[1])
    grads["attn_w_in"][j] = jnp.concatenate(dwin, axis=1)
    grads["attn_q_gain"][j] = jnp.stack(dqg)
    grads["attn_k_gain"][j] = jnp.stack(dkg)
    dx, dg = _matmul_rmsnorm_bwd(f"attn{l}_dh", [(dproj[g], w_in[:, g * GW3:(g + 1) * GW3]) for g in range(3)],
                                 x, _row(w["mixer_norm"][l]), dx1)
    grads["mixer_norm"][l] = dg[0]
    return dx


def _conv_fwd(l, j, x, w):
    h = _rmsnorm_fwd(f"mix{l}_norm", x, _row(w["mixer_norm"][l]))
    w_in, b_in = w["conv_w_in"][j], _row(w["conv_b_in"][j])
    C = w_in.shape[1] // 2
    ua = _matmul(f"conv{l}_in0", [(h, w_in[:, :C])], bias=b_in[:, :C])
    ug = _matmul(f"conv{l}_in1", [(h, w_in[:, C:])], bias=b_in[:, C:])
    glu = _glu_fwd(f"conv{l}_glu", ua, ug)
    c = _dwconv(f"conv{l}_dw", glu, w["conv_dw_w"][j], _row(w["conv_dw_b"][j]), reverse=False)
    sw = _ln_silu_fwd(f"conv{l}_ln", c, _row(w["conv_ln_g"][j]), _row(w["conv_ln_b"][j]))
    x1 = _matmul(f"conv{l}_out", [(sw, w["conv_w_out"][j])], bias=_row(w["conv_b_out"][j]), residual=x)
    return x1, (x, h, ua, ug, glu, c, sw)


def _conv_bwd(l, j, dx1, saved, w, grads):
    x, h, ua, ug, glu, c, sw = saved
    w_in, w_out, dw_w = w["conv_w_in"][j], w["conv_w_out"][j], w["conv_dw_w"][j]
    C = w_out.shape[0]
    grads["conv_b_out"][j] = _column_sums(f"conv{l}_dbout", dx1)[0]
    grads["conv_w_out"][j] = _matmul_tn(f"conv{l}_dwout", sw, dx1)
    dsw = _matmul(f"conv{l}_dsw", [(dx1, w_out)], trans_b=True)
    dc, dlg, dlb = _ln_silu_bwd(f"conv{l}_dln", c, _row(w["conv_ln_g"][j]), _row(w["conv_ln_b"][j]), dsw)
    grads["conv_ln_g"][j], grads["conv_ln_b"][j] = dlg[0], dlb[0]
    dglu = _dwconv(f"conv{l}_ddw", dc, dw_w, jnp.zeros((1, C), F32), reverse=True)
    gw, gb = _dwconv_wgrad(f"conv{l}_ddww", glu, dc, dw_w.shape[0])
    grads["conv_dw_w"][j], grads["conv_dw_b"][j] = gw, gb[0]
    da, dgate, sa, sg = _glu_bwd(f"conv{l}_dglu", ua, ug, dglu)
    grads["conv_b_in"][j] = jnp.concatenate([sa, sg], axis=1)[0]
    grads["conv_w_in"][j] = jnp.concatenate(
        [_matmul_tn(f"conv{l}_dwin0", h, da), _matmul_tn(f"conv{l}_dwin1", h, dgate)], axis=1)
    dx, dg = _matmul_rmsnorm_bwd(f"conv{l}_dh", [(da, w_in[:, :C]), (dgate, w_in[:, C:])], x,
                                 _row(w["mixer_norm"][l]), dx1)
    grads["mixer_norm"][l] = dg[0]
    return dx


def _hgrn_fwd(l, j, x, w):
    h = _rmsnorm_fwd(f"mix{l}_norm", x, _row(w["mixer_norm"][l]))
    w_in = w["hgrn_w_in"][j]
    D = w_in.shape[1] // 4
    pq, pf, pv, pg = [_matmul(f"hgrn{l}_in{s}", [(h, w_in[:, s * D:(s + 1) * D])]) for s in range(4)]
    lb = _lower_bound_fwd(f"hgrn{l}_lb", w["hgrn_lb_logits"], l)
    o, ckpt = _hgrn_scan_fwd(f"hgrn{l}_scan", pq, pf, pv, lb)
    y = _hgrn_out_fwd(f"hgrn{l}_gate", o, pg, _row(w["hgrn_norm_g"][j]))
    x1 = _matmul(f"hgrn{l}_out", [(y, w["hgrn_w_out"][j])], residual=x)
    return x1, (x, h, pq, pf, pv, pg, lb, o, ckpt, y)


def _hgrn_bwd(l, j, dx1, saved, w, grads):
    x, h, pq, pf, pv, pg, lb, o, ckpt, y = saved
    w_in, w_out = w["hgrn_w_in"][j], w["hgrn_w_out"][j]
    D = w_out.shape[0]
    grads["hgrn_w_out"][j] = _matmul_tn(f"hgrn{l}_dwout", y, dx1)
    dy = _matmul(f"hgrn{l}_dy", [(dx1, w_out)], trans_b=True)
    do, dpg, dng = _hgrn_out_bwd(f"hgrn{l}_dgate", o, pg, _row(w["hgrn_norm_g"][j]), dy)
    grads["hgrn_norm_g"][j] = dng[0]
    dpq, dpf, dpv, dlb = _hgrn_scan_bwd(f"hgrn{l}_dscan", pq, pf, pv, lb, ckpt, do)
    grads["hgrn_lb_logits"] = grads["hgrn_lb_logits"] + _lower_bound_bwd(f"hgrn{l}_dlb", w["hgrn_lb_logits"], dlb, l)
    dps = [dpq, dpf, dpv, dpg]
    grads["hgrn_w_in"][j] = jnp.concatenate([_matmul_tn(f"hgrn{l}_dwin{s}", h, dps[s]) for s in range(4)], axis=1)
    dx, dg = _matmul_rmsnorm_bwd(f"hgrn{l}_dh", [(dps[s], w_in[:, s * D:(s + 1) * D]) for s in range(4)], x,
                                 _row(w["mixer_norm"][l]), dx1)
    grads["mixer_norm"][l] = dg[0]
    return dx


_MIXERS = ((_attn_fwd, _attn_bwd), (_conv_fwd, _conv_bwd), (_hgrn_fwd, _hgrn_bwd))
_PER_MIXER = {"attn": 0, "conv": 1, "hgrn": 2}


def _local_step(x, target, w):
    depth = w["mixer_norm"].shape[0]
    grads = {}
    for name, v in w.items():
        lead = v.shape[0]
        grads[name] = jnp.zeros(v.shape, F32) if name == "hgrn_lb_logits" else [None] * lead
    saved = []
    for l in range(depth):
        fwd, _ = _MIXERS[l % N_MIXERS]
        x, s_mix = fwd(l, l // N_MIXERS, x, w)
        x, s_ffn = _ffn_fwd(l, x, w)
        saved.append((s_mix, s_ffn))
    dx, loss_cols = _loss_grad("loss", x, target)
    for l in reversed(range(depth)):
        _, bwd = _MIXERS[l % N_MIXERS]
        s_mix, s_ffn = saved[l]
        dx = _ffn_bwd(l, dx, s_ffn, w, grads)
        dx = bwd(l, l // N_MIXERS, dx, s_mix, w, grads)
    grads = {k: (v if k == "hgrn_lb_logits" else jnp.stack(v)) for k, v in grads.items()}
    return jnp.sum(loss_cols), dx, grads


_HBM = pl.BlockSpec(memory_space=pltpu.HBM)


def _chip_peers():
    x, y, c = lax.axis_index("x"), lax.axis_index("y"), lax.axis_index("c")
    return 2 * x + y, (x, y, c), [(1 - x, y), (x, 1 - y), (1 - x, 1 - y)]


def _exchange_chips(name, src):
    def body(src_ref, out_ref, send_sems, recv_sems, local_sem):
        p, (x, y, c), peers = _chip_peers()
        mine = pltpu.make_async_copy(src_ref.at[p], out_ref.at[p], local_sem)
        mine.start()

        def copy(k, slab_from, slab_to, peer):
            return pltpu.make_async_remote_copy(
                src_ref=src_ref.at[slab_from], dst_ref=out_ref.at[slab_to], send_sem=send_sems.at[k],
                recv_sem=recv_sems.at[k], device_id=(peer[0], peer[1], c), device_id_type=MESH)

        sends = [copy(k, 2 * px + py, p, (px, py)) for k, (px, py) in enumerate(peers)]
        for s in sends:
            s.start()
        for k, (px, py) in enumerate(peers):
            copy(k, p, 2 * px + py, (px, py)).wait_recv()
        for s in sends:
            s.wait_send()
        mine.wait()

    return pl.pallas_call(
        body, name=name, in_specs=[_HBM], out_specs=_HBM, out_shape=jax.ShapeDtypeStruct(src.shape, src.dtype),
        scratch_shapes=[pltpu.SemaphoreType.DMA((3,)), pltpu.SemaphoreType.DMA((3,)), pltpu.SemaphoreType.DMA],
    )(src)


def _all_gather_chips(name, shard):
    R = shard.shape[0]
    half = R // 2

    def body(src_ref, out_ref, send_sems, recv_sems, local_sem):
        p, (x, y, c), peers = _chip_peers()
        mine = pltpu.make_async_copy(src_ref, out_ref.at[p], local_sem)
        mine.start()

        def rows(slab, core):
            return out_ref.at[slab, pl.ds(core * half, half), :]

        def over_ici(k, slab, peer):
            src = src_ref.at[pl.ds(c * half, half), :] if slab is None else rows(slab, c)
            return pltpu.make_async_remote_copy(
                src_ref=src, dst_ref=rows(p if slab is None else slab, c), send_sem=send_sems.at[k],
                recv_sem=recv_sems.at[k], device_id=(peer[0], peer[1], c), device_id_type=MESH)

        def to_sibling(k, slab, core):
            return pltpu.make_async_remote_copy(
                src_ref=rows(slab, core), dst_ref=rows(slab, core), send_sem=send_sems.at[3 + k],
                recv_sem=recv_sems.at[3 + k], device_id=(x, y, 1 - c), device_id_type=MESH)

        sends = [over_ici(k, None, peer) for k, peer in enumerate(peers)]
        for s in sends:
            s.start()
        passed = []
        for k, (px, py) in enumerate(peers):
            over_ici(k, 2 * px + py, (px, py)).wait_recv()
            passed.append(to_sibling(k, 2 * px + py, c))
            passed[k].start()
        for k, (px, py) in enumerate(peers):
            to_sibling(k, 2 * px + py, 1 - c).wait_recv()
        for s in sends + passed:
            s.wait_send()
        mine.wait()

    return pl.pallas_call(
        body, name=name, in_specs=[_HBM], out_specs=_HBM,
        out_shape=jax.ShapeDtypeStruct((N_CHIPS,) + shard.shape, shard.dtype),
        scratch_shapes=[pltpu.SemaphoreType.DMA((6,)), pltpu.SemaphoreType.DMA((6,)), pltpu.SemaphoreType.DMA],
    )(shard)


def _swap_cores(name, v):
    def body(v_ref, out_ref, send_sem, recv_sem):
        x, y, c = lax.axis_index("x"), lax.axis_index("y"), lax.axis_index("c")
        cp = pltpu.make_async_remote_copy(src_ref=v_ref, dst_ref=out_ref, send_sem=send_sem, recv_sem=recv_sem,
                                          device_id=(x, y, 1 - c), device_id_type=MESH)
        cp.start()
        cp.wait()

    return pl.pallas_call(
        body, name=name, in_specs=[_HBM], out_specs=_HBM, out_shape=jax.ShapeDtypeStruct(v.shape, v.dtype),
        scratch_shapes=[pltpu.SemaphoreType.DMA, pltpu.SemaphoreType.DMA],
    )(v)


_WEIGHTS = ("mixer_norm", "ffn_norm", "attn_w_in", "attn_q_gain", "attn_k_gain", "attn_w_out", "conv_w_in",
            "conv_b_in", "conv_dw_w", "conv_dw_b", "conv_ln_g", "conv_ln_b", "conv_w_out", "conv_b_out",
            "hgrn_w_in", "hgrn_lb_logits", "hgrn_norm_g", "hgrn_w_out", "ffn_w_up", "ffn_conv_w", "ffn_conv_b",
            "ffn_w_down")
_SHARD_AXIS = {"attn_w_in": 2, "attn_w_out": 2, "conv_w_in": 2, "conv_dw_w": 2, "conv_w_out": 1, "hgrn_w_in": 2,
               "hgrn_norm_g": 1, "hgrn_w_out": 1, "ffn_w_up": 2, "ffn_conv_w": 2, "ffn_w_down": 1}
_MATMUL_WEIGHTS = ("attn_w_in", "attn_w_out", "conv_w_in", "conv_w_out", "hgrn_w_in", "hgrn_w_out", "ffn_w_up",
                   "ffn_w_down")
PACK_COLS = 1024
PACK_ROWS = 512


def _pack(arrays, nlead, dtype):
    lead = arrays[0].shape[:nlead]
    flat = []
    for a in arrays:
        f = a.reshape(lead + (-1,)).astype(dtype)
        flat.append(jnp.pad(f, [(0, 0)] * nlead + [(0, (-f.shape[-1]) % PACK_COLS)]))
    buf = jnp.concatenate(flat, axis=-1)
    buf = jnp.pad(buf, [(0, 0)] * nlead + [(0, (-buf.shape[-1]) % (PACK_COLS * PACK_ROWS))])
    return buf.reshape(lead + (-1, PACK_COLS))


def _unpack(buf, shapes, nlead):
    lead = buf.shape[:nlead]
    flat = buf.reshape(lead + (-1,))
    out, off = [], 0
    for shape in shapes:
        n = 1
        for s in shape:
            n *= s
        out.append(flat[..., off:off + n].reshape(lead + tuple(shape)))
        off += n + (-n) % PACK_COLS
    return out


def _merge_shards(piece, axis):
    moved = jnp.moveaxis(piece, 0, axis)
    shape = moved.shape
    return moved.reshape(shape[:axis] + (shape[axis] * shape[axis + 1],) + shape[axis + 2:])


def _split_shards(full, axis):
    shape = full.shape
    cut = full.reshape(shape[:axis] + (N_CHIPS, shape[axis] // N_CHIPS) + shape[axis + 1:])
    return jnp.moveaxis(cut, axis, 0)


def _gather_weights(local):
    big = [n for n in _WEIGHTS if n in _MATMUL_WEIGHTS]
    small = [n for n in _WEIGHTS if n in _SHARD_AXIS and n not in _MATMUL_WEIGHTS]
    full = {n: local[n] for n in _WEIGHTS if n not in _SHARD_AXIS}
    for names, dtype, tag in ((big, BF16, "comm_gather_matmul_weights"), (small, F32, "comm_gather_small_weights")):
        gathered = _all_gather_chips(tag, _pack([local[n] for n in names], 0, dtype))
        pieces = _unpack(gathered, [local[n].shape for n in names], 1)
        for n, piece in zip(names, pieces):
            full[n] = _merge_shards(piece, _SHARD_AXIS[n])
    return full


def _reduce_gradients(grads, local):
    out = {}
    big = [n for n in _WEIGHTS if n in _MATMUL_WEIGHTS]
    rest = [n for n in _WEIGHTS if n not in _MATMUL_WEIGHTS]
    for names, dtype, tag in ((big, BF16, "matmul"), (rest, F32, "small")):
        slabs = []
        for n in names:
            g = grads[n]
            if n in _SHARD_AXIS:
                slabs.append(_split_shards(g, _SHARD_AXIS[n]))
            else:
                slabs.append(jnp.broadcast_to(g[None], (N_CHIPS,) + g.shape))
        packed = _pack(slabs, 1, dtype)
        landed = _exchange_chips(f"comm_scatter_{tag}_gradients", packed)
        partial = _sum_slabs(f"sum_chips_{tag}", landed)
        other = _swap_cores(f"comm_swap_{tag}_sums", partial)
        total = _add(f"sum_cores_{tag}", [partial, other])
        out.update(zip(names, _unpack(total, [local[n].shape for n in names], 0)))
    return out


def kernel(x, mixer_norm, ffn_norm, attn_w_in, attn_q_gain, attn_k_gain, attn_w_out, conv_w_in, conv_b_in, conv_dw_w, conv_dw_b, conv_ln_g, conv_ln_b, conv_w_out, conv_b_out, hgrn_w_in, hgrn_lb_logits, hgrn_norm_g, hgrn_w_out, ffn_w_up, ffn_conv_w, ffn_conv_b, ffn_w_down, loss_target, m_mixer_norm, m_ffn_norm, m_attn_w_in, m_attn_q_gain, m_attn_k_gain, m_attn_w_out, m_conv_w_in, m_conv_b_in, m_conv_dw_w, m_conv_dw_b, m_conv_ln_g, m_conv_ln_b, m_conv_w_out, m_conv_b_out, m_hgrn_w_in, m_hgrn_lb_logits, m_hgrn_norm_g, m_hgrn_w_out, m_ffn_w_up, m_ffn_conv_w, m_ffn_conv_b, m_ffn_w_down, v_mixer_norm, v_ffn_norm, v_attn_w_in, v_attn_q_gain, v_attn_k_gain, v_attn_w_out, v_conv_w_in, v_conv_b_in, v_conv_dw_w, v_conv_dw_b, v_conv_ln_g, v_conv_ln_b, v_conv_w_out, v_conv_b_out, v_hgrn_w_in, v_hgrn_lb_logits, v_hgrn_norm_g, v_hgrn_w_out, v_ffn_w_up, v_ffn_conv_w, v_ffn_conv_b, v_ffn_w_down):
    given = dict(locals())
    local = {n: given[n] for n in _WEIGHTS}
    full = _gather_weights(local)
    loss, dx, grads = _local_step(x[0], loss_target[0], full)
    loss = lax.psum(loss, ("x", "y", "c"))
    grad = _reduce_gradients(grads, local)
    delta, new_m, new_v = {}, {}, {}
    for n in _WEIGHTS:
        shape = local[n].shape
        as2d = lambda a: a.reshape(-1, shape[-1])
        d, m, v = _adamw(f"adamw_{n}", as2d(local[n]), as2d(grad[n]), as2d(given["m_" + n]), as2d(given["v_" + n]))
        delta[n], new_m[n], new_v[n] = d.reshape(shape), m.reshape(shape), v.reshape(shape)
    return (loss, dx[None], *[grad[n] for n in _WEIGHTS], *[delta[n] for n in _WEIGHTS],
            *[new_m[n] for n in _WEIGHTS], *[new_v[n] for n in _WEIGHTS])
```
